```python
import math
import jax, jax.numpy as jnp
from jax import lax
import numpy as np

D_MODEL = 1024
BATCH = 2
SEQ = 8192
DEPTH = 1

D_MIX = D_MODEL
ATT_HEADS = 8
HEAD_DIM = 64
D_ATT = ATT_HEADS * HEAD_DIM
D_CONV = D_MIX - D_ATT
DILATED_PATTERNS = ((128, 1), (512, 4), (2048, 16))
ROPE_THETA = 500000.0
ROT_DIM = HEAD_DIM // 4
CONV_WIDTH = 31
N_MEM = 256
XATT_HEADS = 4
XATT_HEAD_DIM = D_MODEL // XATT_HEADS
D_FF = 4 * D_MODEL
D_IN = 3 * D_ATT + 2 * D_CONV
EPS = 1e-6
NEG_INF = -1e30

kernel_name = "hybrid_dilated_swa_conformer_encoder"


def rmsnorm(x, g):
    xf = x.astype(jnp.float32)
    var = jnp.mean(xf * xf, axis=-1, keepdims=True)
    return (xf * lax.rsqrt(var + EPS) * g.astype(jnp.float32)).astype(x.dtype)


def layernorm(x, g, b):
    xf = x.astype(jnp.float32)
    mu = jnp.mean(xf, axis=-1, keepdims=True)
    var = jnp.mean(jnp.square(xf - mu), axis=-1, keepdims=True)
    y = (xf - mu) * lax.rsqrt(var + EPS) * g.astype(jnp.float32) + b.astype(jnp.float32)
    return y.astype(x.dtype)


def partial_rotary(t):
    S = t.shape[1]
    half = ROT_DIM // 2
    freqs = ROPE_THETA ** (-jnp.arange(0, ROT_DIM, 2, dtype=jnp.float32) / ROT_DIM)
    ang = jnp.arange(S, dtype=jnp.float32)[:, None] * freqs[None, :]
    cos = jnp.cos(ang)[None, :, None, :]
    sin = jnp.sin(ang)[None, :, None, :]
    tf = t.astype(jnp.float32)
    x1, x2, rest = tf[..., :half], tf[..., half:ROT_DIM], tf[..., ROT_DIM:]
    rot = jnp.concatenate([x1 * cos - x2 * sin, x2 * cos + x1 * sin, rest], axis=-1)
    return rot.astype(t.dtype)


def to_strided(t, d):
    B, S = t.shape[:2]
    rest = t.shape[2:]
    t = jnp.moveaxis(t.reshape(B, S // d, d, *rest), 2, 1)
    return t.reshape(B * d, S // d, *rest)


def from_strided(t, d, B):
    N, L = t.shape[:2]
    rest = t.shape[2:]
    t = jnp.moveaxis(t.reshape(B, d, L, *rest), 1, 2)
    return t.reshape(B, L * d, *rest)


def banded_attention(q, k, v, half):
    N, L, H, Dh = q.shape
    blk = half
    nb = -(-L // blk)
    Lp = nb * blk
    pad = Lp - L
    qb = jnp.pad(q, ((0, 0), (0, pad), (0, 0), (0, 0))).reshape(N, nb, blk, H, Dh)
    kp = jnp.pad(k, ((0, 0), (blk, blk + pad), (0, 0), (0, 0)))
    vp = jnp.pad(v, ((0, 0), (blk, blk + pad), (0, 0), (0, 0)))

    def window(t):
        return jnp.concatenate(
            [t[:, i * blk:i * blk + Lp].reshape(N, nb, blk, H, Dh) for i in range(3)], axis=2)

    kb, vb = window(kp), window(vp)
    s = jnp.einsum('nbqhd,nbkhd->nbhqk', qb, kb).astype(jnp.float32) * (Dh ** -0.5)
    qpos = jnp.arange(nb)[:, None] * blk + jnp.arange(blk)[None, :]
    kpos = jnp.arange(nb)[:, None] * blk - blk + jnp.arange(3 * blk)[None, :]
    valid = ((jnp.abs(kpos[:, None, :] - qpos[:, :, None]) <= half)
             & (kpos >= 0)[:, None, :] & (kpos < L)[:, None, :])
    s = jnp.where(valid[None, :, None], s, NEG_INF)
    m = jnp.max(s, axis=-1, keepdims=True)
    p = jnp.exp(s - m)
    den = jnp.sum(p, axis=-1, keepdims=True)
    o = jnp.einsum('nbhqk,nbkhd->nbqhd', (p / den).astype(v.dtype), vb)
    lse = (m + jnp.log(den))[..., 0]
    o = o.reshape(N, Lp, H, Dh)[:, :L]
    lse = jnp.transpose(lse, (0, 1, 3, 2)).reshape(N, Lp, H)[:, :L]
    return o, lse


def dilated_sliding_attention(q, k, v):
    B = q.shape[0]
    outs, lses = [], []
    for window, d in DILATED_PATTERNS:
        half = window // (2 * d)
        o, lse = banded_attention(to_strided(q, d), to_strided(k, d), to_strided(v, d), half)
        outs.append(from_strided(o, d, B))
        lses.append(from_strided(lse, d, B))
    w = jax.nn.softmax(jnp.stack(lses, axis=0), axis=0)
    o = jnp.sum(w[..., None] * jnp.stack(outs, axis=0).astype(jnp.float32), axis=0)
    return o.astype(q.dtype)


def conformer_conv(a, g, conv_w, conv_b, ln_g, ln_b):
    u = a * jax.nn.sigmoid(g)
    C = u.shape[-1]
    u = lax.conv_general_dilated(
        u, conv_w.reshape(CONV_WIDTH, 1, C).astype(u.dtype),
        window_strides=(1,), padding=[((CONV_WIDTH - 1) // 2, (CONV_WIDTH - 1) // 2)],
        dimension_numbers=('NWC', 'WIO', 'NWC'), feature_group_count=C) + conv_b
    u = layernorm(u, ln_g, ln_b)
    return jax.nn.silu(u)


def setup_inputs(seed: int = 0) -> dict:
    key = jax.random.key(seed)
    ks = jax.random.split(key, 20)
    f32 = jnp.float32

    def w(k, shape, fan_in):
        return jax.random.normal(k, shape, f32) * (fan_in ** -0.5)

    def gain(k, shape):
        return 1.0 + 0.02 * jax.random.normal(k, shape, f32)

    return {
        "x": jax.random.normal(ks[0], (BATCH, SEQ, D_MODEL), f32),
        "mem": jax.random.normal(ks[1], (BATCH, N_MEM, D_MODEL), f32),
        "norm_mix_g": gain(ks[2], (DEPTH, D_MODEL)),
        "w_in": w(ks[3], (DEPTH, D_MODEL, D_IN), D_MODEL),
        "conv_w": w(ks[4], (DEPTH, CONV_WIDTH, D_CONV), CONV_WIDTH),
        "conv_b": 0.02 * jax.random.normal(ks[5], (DEPTH, D_CONV), f32),
        "conv_ln_g": gain(ks[6], (DEPTH, D_CONV)),
        "conv_ln_b": 0.02 * jax.random.normal(ks[7], (DEPTH, D_CONV), f32),
        "w_out": w(ks[8], (DEPTH, D_MIX, D_MODEL), D_MIX),
        "norm_x_g": gain(ks[9], (DEPTH, D_MODEL)),
        "norm_mem_g": gain(ks[10], (DEPTH, D_MODEL)),
        "w_xq": w(ks[11], (DEPTH, D_MODEL, D_MODEL), D_MODEL),
        "w_xk": w(ks[12], (DEPTH, D_MODEL, D_MODEL), D_MODEL),
        "w_xv": w(ks[13], (DEPTH, D_MODEL, D_MODEL), D_MODEL),
        "w_xo": w(ks[14], (DEPTH, D_MODEL, D_MODEL), D_MODEL),
        "norm_mlp_g": gain(ks[15], (DEPTH, D_MODEL)),
        "w_up": w(ks[16], (DEPTH, D_MODEL, D_FF), D_MODEL),
        "w_down": w(ks[17], (DEPTH, D_FF, D_MODEL), D_FF),
        "norm_final_g": gain(ks[18], (D_MODEL,)),
    }


def reference(x, mem, norm_mix_g, w_in, conv_w, conv_b, conv_ln_g, conv_ln_b, w_out,
              norm_x_g, norm_mem_g, w_xq, w_xk, w_xv, w_xo, norm_mlp_g, w_up, w_down,
              norm_final_g):
    B, S, _ = x.shape
    M = mem.shape[1]
    h = x
    for l in range(DEPTH):
        y = rmsnorm(h, norm_mix_g[l]) @ w_in[l]
        q = partial_rotary(y[..., 0:D_ATT].reshape(B, S, ATT_HEADS, HEAD_DIM))
        k = partial_rotary(y[..., D_ATT:2 * D_ATT].reshape(B, S, ATT_HEADS, HEAD_DIM))
        v = y[..., 2 * D_ATT:3 * D_ATT].reshape(B, S, ATT_HEADS, HEAD_DIM)
        att = dilated_sliding_attention(q, k, v).reshape(B, S, D_ATT)
        c0 = 3 * D_ATT
        conv = conformer_conv(y[..., c0:c0 + D_CONV], y[..., c0 + D_CONV:c0 + 2 * D_CONV],
                              conv_w[l], conv_b[l], conv_ln_g[l], conv_ln_b[l])
        h = h + jnp.concatenate([att, conv], axis=-1) @ w_out[l]

        xq = (rmsnorm(h, norm_x_g[l]) @ w_xq[l]).reshape(B, S, XATT_HEADS, XATT_HEAD_DIM)
        mn = rmsnorm(mem, norm_mem_g[l])
        xk = (mn @ w_xk[l]).reshape(B, M, XATT_HEADS, XATT_HEAD_DIM)
        xv = (mn @ w_xv[l]).reshape(B, M, XATT_HEADS, XATT_HEAD_DIM)
        sc = jnp.einsum('bshd,bmhd->bhsm', xq, xk).astype(jnp.float32) * (XATT_HEAD_DIM ** -0.5)
        pr = jax.nn.softmax(sc, axis=-1).astype(xv.dtype)
        xo = jnp.einsum('bhsm,bmhd->bshd', pr, xv).reshape(B, S, D_MODEL)
        h = h + xo @ w_xo[l]

        u = rmsnorm(h, norm_mlp_g[l]) @ w_up[l]
        h = h + jnp.square(jax.nn.relu(u)) @ w_down[l]
    return rmsnorm(h, norm_final_g)
```

```python
import functools
import math

import numpy as np
import jax
import jax.numpy as jnp
from jax import lax
from jax.experimental import pallas as pl
from jax.experimental.pallas import tpu as pltpu

F32 = jnp.float32
BF16 = jnp.bfloat16

D_MODEL = 1024
ATT_HEADS = 8
HEAD_DIM = 64
D_ATT = ATT_HEADS * HEAD_DIM
D_CONV = D_MODEL - D_ATT
DILATED_PATTERNS = ((128, 1), (512, 4), (2048, 16))
ROPE_THETA = 500000.0
ROT_DIM = HEAD_DIM // 4
CONV_WIDTH = 31
CONV_PAD = (CONV_WIDTH - 1) // 2
XATT_HEADS = 4
XATT_HEAD_DIM = D_MODEL // XATT_HEADS
D_FF = 4 * D_MODEL
EPS = 1e-6
NEG_INF = -1e30

LANES = 128
ATT_HALF = 64
ATT_QB = 2 * ATT_HALF
ATT_WIN = ATT_QB + 2 * ATT_HALF
CONV_HALO = 16
VMEM_LIMIT = 56 * 1024 * 1024


def _dot(a, b):
    return jnp.dot(a, b, preferred_element_type=F32)


def _dot_nt(a, b):
    return lax.dot_general(a, b, (((1,), (1,)), ((), ())), preferred_element_type=F32)


def _rms(x, g):
    var = jnp.mean(x * x, axis=-1, keepdims=True)
    return x * lax.rsqrt(var + EPS) * g


def _in_proj_kernel(x_ref, g_ref, w_ref, cos_ref, sa_ref, sb_ref,
                    q_ref, k_ref, v_ref, u_ref):
    xn = _rms(x_ref[...], g_ref[...]).astype(BF16)
    cos, sa, sb = cos_ref[...], sa_ref[...], sb_ref[...]

    def rot(y):
        outs = []
        for j in range(D_ATT // LANES):
            yj = y[:, LANES * j:LANES * (j + 1)]
            outs.append(yj * cos + pltpu.roll(yj, LANES - ROT_DIM // 2, 1) * sa
                        + pltpu.roll(yj, ROT_DIM // 2, 1) * sb)
        return jnp.concatenate(outs, axis=1)

    yq = _dot(xn, w_ref[:, 0:D_ATT])
    q_ref[...] = (rot(yq) * (HEAD_DIM ** -0.5)).astype(BF16)
    yk = _dot(xn, w_ref[:, D_ATT:2 * D_ATT])
    k_ref[...] = rot(yk).astype(BF16)
    v_ref[...] = _dot(xn, w_ref[:, 2 * D_ATT:3 * D_ATT]).astype(BF16)
    c0 = 3 * D_ATT
    a = _dot(xn, w_ref[:, c0:c0 + D_CONV])
    gt = _dot(xn, w_ref[:, c0 + D_CONV:c0 + 2 * D_CONV])
    u_ref[...] = a * (1.0 / (1.0 + jnp.exp(-gt)))


def _in_proj(x2, g, w_in, cos_t, sa_t, sb_t, seq, tm=512):
    T = x2.shape[0]
    d_in = w_in.shape[1]
    n_s = seq // tm
    row = lambda i: (i, 0)
    tab = lambda i: (i % n_s, 0)
    const = lambda i: (0, 0)
    return pl.pallas_call(
        _in_proj_kernel,
        grid=(T // tm,),
        in_specs=[pl.BlockSpec((tm, D_MODEL), row),
                  pl.BlockSpec((1, D_MODEL), const),
                  pl.BlockSpec((D_MODEL, d_in), const),
                  pl.BlockSpec((tm, LANES), tab),
                  pl.BlockSpec((tm, LANES), tab),
                  pl.BlockSpec((tm, LANES), tab)],
        out_specs=[pl.BlockSpec((tm, D_ATT), row)] * 3 + [pl.BlockSpec((tm, D_CONV), row)],
        out_shape=[jax.ShapeDtypeStruct((T, D_ATT), BF16)] * 3
                  + [jax.ShapeDtypeStruct((T, D_CONV), F32)],
        compiler_params=pltpu.CompilerParams(dimension_semantics=("arbitrary",),
                                             vmem_limit_bytes=VMEM_LIMIT),
        name="in_proj",
    )(x2, g, w_in, cos_t, sa_t, sb_t)


def _attn_kernel(q_ref, kc_ref, kp_ref, kn_ref, vc_ref, vp_ref, vn_ref, bias_ref, hm_ref,
                 o_ref, lse_ref, kext, vext, *, lb, nblk_total):
    i = pl.program_id(2)
    kext[0:ATT_HALF] = kp_ref[0]
    kext[ATT_HALF:ATT_HALF + lb] = kc_ref[0]
    kext[ATT_HALF + lb:] = kn_ref[0]
    vext[0:ATT_HALF] = vp_ref[0]
    vext[ATT_HALF:ATT_HALF + lb] = vc_ref[0]
    vext[ATT_HALF + lb:] = vn_ref[0]

    lane = lax.broadcasted_iota(jnp.int32, (ATT_QB, LANES), 1)
    first_head = lane < HEAD_DIM
    mask_a = hm_ref[0:1, :]
    mask_b = hm_ref[1:2, :]
    nblk = lb // ATT_QB

    def body(n, carry):
        r0 = pl.multiple_of(n * ATT_QB, ATT_QB)
        gblk = i * nblk + n
        bidx = jnp.where(gblk == 0, 0, jnp.where(gblk == nblk_total - 1, 2, 1))
        bias = bias_ref[bidx]
        lse_tile = jnp.zeros((ATT_QB, LANES), F32)
        for hp in range(D_ATT // LANES):
            c0 = LANES * hp
            q2 = q_ref[0, pl.ds(r0, ATT_QB), c0:c0 + LANES]
            kw = kext[pl.ds(r0, ATT_WIN), c0:c0 + LANES]
            vw = vext[pl.ds(r0, ATT_WIN), c0:c0 + LANES]
            halves = []
            for hm in (mask_a, mask_b):
                s = _dot_nt(q2 * hm, kw) + bias
                m = jnp.max(s, axis=-1, keepdims=True)
                p = jnp.exp(s - m)
                l = jnp.sum(p, axis=-1, keepdims=True)
                pv = _dot(p.astype(BF16), vw)
                halves.append((pv * (1.0 / l), m + jnp.log(l)))
            o2 = jnp.where(first_head, halves[0][0], halves[1][0])
            o_ref[0, pl.ds(r0, ATT_QB), c0:c0 + LANES] = o2.astype(o_ref.dtype)
            lse_tile = jnp.where(lane == 2 * hp, halves[0][1], lse_tile)
            lse_tile = jnp.where(lane == 2 * hp + 1, halves[1][1], lse_tile)
        lse_ref[0, pl.ds(r0, ATT_QB), :] = lse_tile
        return carry

    lax.fori_loop(0, nblk, body, 0)


def _attn_bias():
    i = np.arange(ATT_QB)[:, None]
    j = np.arange(ATT_WIN)[None, :]
    band = (j >= i) & (j <= i + 2 * ATT_HALF)
    first = band & (j >= ATT_HALF)
    last = band & (j < ATT_HALF + ATT_QB)
    tabs = np.stack([first, band, last]).astype(np.float32)
    return jnp.asarray((1.0 - tabs) * NEG_INF, dtype=F32)


def _head_masks():
    lane = np.arange(LANES)
    m = np.zeros((16, LANES), np.float32)
    m[0] = lane < HEAD_DIM
    m[1] = lane >= HEAD_DIM
    return jnp.asarray(m, dtype=BF16)


def _attn(q, k, v, bias, hmask, batch, seq, d, lb=512):
    L = seq // d
    lb = min(lb, L)
    nblk_total = L // ATT_QB
    assert nblk_total >= 2 and L % lb == 0 and lb % ATT_QB == 0
    view = lambda t: t.reshape(batch, L, d * D_ATT)
    hb = lb // ATT_HALF
    cur = lambda b, r, i: (b, i, r)
    prev = lambda b, r, i: (b, jnp.maximum(i * hb - 1, 0), r)
    nxt = lambda b, r, i: (b, jnp.minimum((i + 1) * hb, L // ATT_HALF - 1), r)
    blk = pl.BlockSpec((1, lb, D_ATT), cur)
    halo_p = pl.BlockSpec((1, ATT_HALF, D_ATT), prev)
    halo_n = pl.BlockSpec((1, ATT_HALF, D_ATT), nxt)
    o, lse = pl.pallas_call(
        functools.partial(_attn_kernel, lb=lb, nblk_total=nblk_total),
        grid=(batch, d, L // lb),
        in_specs=[blk, blk, halo_p, halo_n, blk, halo_p, halo_n,
                  pl.BlockSpec((3, ATT_QB, ATT_WIN), lambda b, r, i: (0, 0, 0)),
                  pl.BlockSpec((16, LANES), lambda b, r, i: (0, 0))],
        out_specs=[blk, pl.BlockSpec((1, lb, LANES), cur)],
        out_shape=[jax.ShapeDtypeStruct((batch, L, d * D_ATT), BF16),
                   jax.ShapeDtypeStruct((batch, L, d * LANES), F32)],
        scratch_shapes=[pltpu.VMEM((lb + 2 * ATT_HALF, D_ATT), BF16),
                        pltpu.VMEM((lb + 2 * ATT_HALF, D_ATT), BF16)],
        compiler_params=pltpu.CompilerParams(
            dimension_semantics=("arbitrary", "arbitrary", "arbitrary"),
            vmem_limit_bytes=VMEM_LIMIT),
        name=f"attn_d{d}",
    )(view(q), view(k), view(k), view(k), view(v), view(v), view(v), bias, hmask)
    return o.reshape(batch * seq, D_ATT), lse.reshape(batch * seq, LANES)


def _conv_kernel(uc_ref, up_ref, un_ref, w_ref, b_ref, lg_ref, lb_ref, c_ref, ext, *, ts, rows):
    i = pl.program_id(1)
    n = pl.num_programs(1)
    ext[0:CONV_HALO] = jnp.where(i > 0, up_ref[0], 0.0)
    ext[CONV_HALO:CONV_HALO + ts] = uc_ref[0]
    ext[CONV_HALO + ts:] = jnp.where(i < n - 1, un_ref[0], 0.0)
    bias = b_ref[...]
    lg = lg_ref[...]
    lb = lb_ref[...]
    off = CONV_HALO - CONV_PAD
    for c in range(ts // rows):
        r0 = c * rows
        acc = jnp.broadcast_to(bias, (rows, D_CONV))
        for k in range(CONV_WIDTH):
            acc = acc + ext[r0 + off + k:r0 + off + k + rows, :] * w_ref[k:k + 1, :]
        mu = jnp.mean(acc, axis=-1, keepdims=True)
        cen = acc - mu
        var = jnp.mean(cen * cen, axis=-1, keepdims=True)
        y = cen * lax.rsqrt(var + EPS) * lg + lb
        c_ref[0, r0:r0 + rows, :] = (y * (1.0 / (1.0 + jnp.exp(-y)))).astype(c_ref.dtype)


def _conv(u3, conv_w, conv_b, ln_g, ln_b, ts=256, rows=32):
    B, S, C = u3.shape
    hb = ts // CONV_HALO
    cur = lambda b, i: (b, i, 0)
    prev = lambda b, i: (b, jnp.maximum(i * hb - 1, 0), 0)
    nxt = lambda b, i: (b, jnp.minimum((i + 1) * hb, S // CONV_HALO - 1), 0)
    const = lambda b, i: (0, 0)
    return pl.pallas_call(
        functools.partial(_conv_kernel, ts=ts, rows=rows),
        grid=(B, S // ts),
        in_specs=[pl.BlockSpec((1, ts, C), cur),
                  pl.BlockSpec((1, CONV_HALO, C), prev),
                  pl.BlockSpec((1, CONV_HALO, C), nxt),
                  pl.BlockSpec((CONV_WIDTH, C), const),
                  pl.BlockSpec((1, C), const),
                  pl.BlockSpec((1, C), const),
                  pl.BlockSpec((1, C), const)],
        out_specs=pl.BlockSpec((1, ts, C), cur),
        out_shape=jax.ShapeDtypeStruct((B, S, C), BF16),
        scratch_shapes=[pltpu.VMEM((ts + 2 * CONV_HALO, C), F32)],
        compiler_params=pltpu.CompilerParams(dimension_semantics=("arbitrary", "arbitrary"),
                                             vmem_limit_bytes=VMEM_LIMIT),
        name="conv",
    )(u3, u3, u3, conv_w, conv_b, ln_g, ln_b)


def _mem_kv_kernel(mem_ref, g_ref, wk_ref, wv_ref, k_ref, v_ref):
    mn = _rms(mem_ref[...], g_ref[...]).astype(BF16)
    k_ref[...] = _dot(mn, wk_ref[...]).astype(BF16)
    v_ref[...] = _dot(mn, wv_ref[...]).astype(BF16)


def _mem_kv(mem2, g, wk, wv):
    R = mem2.shape[0]
    full = lambda shape: pl.BlockSpec(shape, lambda i: (0, 0))
    return pl.pallas_call(
        _mem_kv_kernel,
        grid=(1,),
        in_specs=[full((R, D_MODEL)), full((1, D_MODEL)),
                  full((D_MODEL, D_MODEL)), full((D_MODEL, D_MODEL))],
        out_specs=[full((R, D_MODEL))] * 2,
        out_shape=[jax.ShapeDtypeStruct((R, D_MODEL), BF16)] * 2,
        compiler_params=pltpu.CompilerParams(dimension_semantics=("arbitrary",),
                                             vmem_limit_bytes=VMEM_LIMIT),
        name="mem_kv",
    )(mem2, g, wk, wv)


def _mix_xattn_kernel(x_ref, o1_ref, o2_ref, o3_ref, l1_ref, l2_ref, l3_ref, c_ref, e_ref,
                      wo_ref, gx_ref, wq_ref, xk_ref, xv_ref, wxo_ref, h_ref):
    l1, l2, l3 = l1_ref[...], l2_ref[...], l3_ref[...]
    m = jnp.maximum(jnp.maximum(l1, l2), l3)
    e1, e2, e3 = jnp.exp(l1 - m), jnp.exp(l2 - m), jnp.exp(l3 - m)
    inv = 1.0 / (e1 + e2 + e3)
    expand = e_ref[...]

    def widen(w):
        hi = w.astype(BF16)
        lo = (w - hi.astype(F32)).astype(BF16)
        return _dot(hi, expand) + _dot(lo, expand)

    att = (widen(e1 * inv) * o1_ref[...].astype(F32)
           + widen(e2 * inv) * o2_ref[...].astype(F32)
           + widen(e3 * inv) * o3_ref[...].astype(F32))
    h1 = (x_ref[...] + _dot(att.astype(BF16), wo_ref[0:D_ATT, :])
          + _dot(c_ref[...], wo_ref[D_ATT:, :]))

    xq = (_dot(_rms(h1, gx_ref[...]).astype(BF16), wq_ref[...])
          * (XATT_HEAD_DIM ** -0.5)).astype(BF16)
    heads = []
    for h in range(XATT_HEADS):
        sl = slice(h * XATT_HEAD_DIM, (h + 1) * XATT_HEAD_DIM)
        s = _dot_nt(xq[:, sl], xk_ref[0, :, sl])
        mx = jnp.max(s, axis=-1, keepdims=True)
        p = jnp.exp(s - mx)
        den = jnp.sum(p, axis=-1, keepdims=True)
        heads.append((_dot(p.astype(BF16), xv_ref[0, :, sl]) * (1.0 / den)).astype(BF16))
    xo = jnp.concatenate(heads, axis=1)
    h_ref[...] = h1 + _dot(xo, wxo_ref[...])


def _mix_xattn(x2, os_, lses, c2, expand, w_out, gx, w_xq, xk, xv, w_xo, seq, tm=256):
    T = x2.shape[0]
    n_mem = xk.shape[1]
    per_b = seq // tm
    row = lambda i: (i, 0)
    const = lambda i: (0, 0)
    memb = lambda i: (i // per_b, 0, 0)
    sq = pl.BlockSpec((D_MODEL, D_MODEL), const)
    return pl.pallas_call(
        _mix_xattn_kernel,
        grid=(T // tm,),
        in_specs=[pl.BlockSpec((tm, D_MODEL), row)]
                 + [pl.BlockSpec((tm, D_ATT), row)] * 3
                 + [pl.BlockSpec((tm, LANES), row)] * 3
                 + [pl.BlockSpec((tm, D_CONV), row),
                    pl.BlockSpec((LANES, D_ATT), const),
                    sq, pl.BlockSpec((1, D_MODEL), const), sq,
                    pl.BlockSpec((1, n_mem, D_MODEL), memb),
                    pl.BlockSpec((1, n_mem, D_MODEL), memb),
                    sq],
        out_specs=pl.BlockSpec((tm, D_MODEL), row),
        out_shape=jax.ShapeDtypeStruct((T, D_MODEL), F32),
        compiler_params=pltpu.CompilerParams(dimension_semantics=("arbitrary",),
                                             vmem_limit_bytes=VMEM_LIMIT),
        name="mix_xattn",
    )(x2, *os_, *lses, c2, expand, w_out, gx, w_xq, xk, xv, w_xo)


def _mlp_kernel(h_ref, g_ref, wu_ref, wd_ref, gf_ref, out_ref, *, chunk):
    h = h_ref[...]
    hn = _rms(h, g_ref[...]).astype(BF16)
    acc = h
    for j in range(D_FF // chunk):
        u = jnp.maximum(_dot(hn, wu_ref[:, j * chunk:(j + 1) * chunk]), 0.0)
        acc = acc + _dot((u * u).astype(BF16), wd_ref[j * chunk:(j + 1) * chunk, :])
    out_ref[...] = _rms(acc, gf_ref[...])


def _mlp(h2, g, w_up, w_down, gf, tm=512, chunk=1024):
    T = h2.shape[0]
    row = lambda i: (i, 0)
    const = lambda i: (0, 0)
    return pl.pallas_call(
        functools.partial(_mlp_kernel, chunk=chunk),
        grid=(T // tm,),
        in_specs=[pl.BlockSpec((tm, D_MODEL), row),
                  pl.BlockSpec((1, D_MODEL), const),
                  pl.BlockSpec((D_MODEL, D_FF), const),
                  pl.BlockSpec((D_FF, D_MODEL), const),
                  pl.BlockSpec((1, D_MODEL), const)],
        out_specs=pl.BlockSpec((tm, D_MODEL), row),
        out_shape=jax.ShapeDtypeStruct((T, D_MODEL), F32),
        compiler_params=pltpu.CompilerParams(dimension_semantics=("arbitrary",),
                                             vmem_limit_bytes=VMEM_LIMIT),
        name="mlp",
    )(h2, g, w_up, w_down, gf)


def _rotary_tables(seq):
    half = ROT_DIM // 2
    freqs = ROPE_THETA ** (-jnp.arange(0, ROT_DIM, 2, dtype=F32) / ROT_DIM)
    ang = jnp.arange(seq, dtype=F32)[:, None] * freqs[None, :]
    cos, sin = jnp.cos(ang), jnp.sin(ang)
    zeros = jnp.zeros((seq, HEAD_DIM - ROT_DIM), F32)
    ones = jnp.ones((seq, HEAD_DIM - ROT_DIM), F32)
    z8 = jnp.zeros((seq, half), F32)
    cos_h = jnp.concatenate([cos, cos, ones], axis=1)
    sa_h = jnp.concatenate([-sin, z8, zeros], axis=1)
    sb_h = jnp.concatenate([z8, sin, zeros], axis=1)
    rep = LANES // HEAD_DIM
    return tuple(jnp.tile(t, (1, rep)) for t in (cos_h, sa_h, sb_h))


def _expand_matrix():
    e = np.zeros((LANES, D_ATT), np.float32)
    for h in range(ATT_HEADS):
        e[h, h * HEAD_DIM:(h + 1) * HEAD_DIM] = 1.0
    return jnp.asarray(e, dtype=BF16)


def kernel(x, mem, norm_mix_g, w_in, conv_w, conv_b, conv_ln_g, conv_ln_b, w_out, norm_x_g,
           norm_mem_g, w_xq, w_xk, w_xv, w_xo, norm_mlp_g, w_up, w_down, norm_final_g):
    B, S, D = x.shape
    n_mem = mem.shape[1]
    depth = w_in.shape[0]
    T = B * S
    cos_t, sa_t, sb_t = _rotary_tables(S)
    bias = _attn_bias()
    hmask = _head_masks()
    expand = _expand_matrix()
    row = lambda g: g.reshape(1, -1)

    h = x.reshape(T, D)
    for l in range(depth):
        q, k, v, u = _in_proj(h, row(norm_mix_g[l]), w_in[l].astype(BF16), cos_t, sa_t, sb_t, S)
        os_, lses = [], []
        for _, d in DILATED_PATTERNS:
            o, lse = _attn(q, k, v, bias, hmask, B, S, d)
            os_.append(o)
            lses.append(lse)
        c = _conv(u.reshape(B, S, D_CONV), conv_w[l], row(conv_b[l]), row(conv_ln_g[l]),
                  row(conv_ln_b[l])).reshape(T, D_CONV)
        xk, xv = _mem_kv(mem.reshape(B * n_mem, D), row(norm_mem_g[l]),
                         w_xk[l].astype(BF16), w_xv[l].astype(BF16))
        h = _mix_xattn(h, os_, lses, c, expand, w_out[l].astype(BF16), row(norm_x_g[l]),
                       w_xq[l].astype(BF16), xk.reshape(B, n_mem, D), xv.reshape(B, n_mem, D),
                       w_xo[l].astype(BF16), S)
        last = l == depth - 1
        gf = row(norm_final_g) if last else None
        assert last, "final norm is fused into the last layer's MLP kernel"
        h = _mlp(h, row(norm_mlp_g[l]), w_up[l].astype(BF16), w_down[l].astype(BF16), gf)
    return h.reshape(B, S, D)
```

```python
import functools
import math

import numpy as np
import jax
import jax.numpy as jnp
from jax import lax
from jax.experimental import pallas as pl
from jax.experimental.pallas import tpu as pltpu

F32 = jnp.float32
BF16 = jnp.bfloat16

D_MODEL = 1024
ATT_HEADS = 8
HEAD_DIM = 64
D_ATT = ATT_HEADS * HEAD_DIM
D_CONV = D_MODEL - D_ATT
DILATED_PATTERNS = ((128, 1), (512, 4), (2048, 16))
ROPE_THETA = 500000.0
ROT_DIM = HEAD_DIM // 4
CONV_WIDTH = 31
CONV_PAD = (CONV_WIDTH - 1) // 2
XATT_HEADS = 4
XATT_HEAD_DIM = D_MODEL // XATT_HEADS
D_FF = 4 * D_MODEL
EPS = 1e-6
NEG_INF = -1e30

LANES = 128
ATT_HALF = 64
ATT_QB = 2 * ATT_HALF
ATT_WIN = ATT_QB + 2 * ATT_HALF
CONV_HALO = 16
VMEM_LIMIT = 56 * 1024 * 1024


def _dot(a, b):
    return jnp.dot(a, b, preferred_element_type=F32)


def _dot_nt(a, b):
    return lax.dot_general(a, b, (((1,), (1,)), ((), ())), preferred_element_type=F32)


def _rms(x, g):
    var = jnp.mean(x * x, axis=-1, keepdims=True)
    return x * lax.rsqrt(var + EPS) * g


def _in_proj_kernel(x_ref, g_ref, w_ref, cos_ref, sa_ref, sb_ref, *refs, tm):
    n_pat = len(DILATED_PATTERNS)
    qkv_refs = [refs[a * n_pat:(a + 1) * n_pat] for a in range(3)]
    u_ref = refs[3 * n_pat]
    ybuf = refs[3 * n_pat + 1]
    xn = _rms(x_ref[...], g_ref[...]).astype(BF16)
    cos, sa, sb = cos_ref[...], sa_ref[...], sb_ref[...]
    n_slab = D_ATT // LANES

    def plain(y, j):
        return y[:, LANES * j:LANES * (j + 1)]

    def rot(y, j):
        yj = plain(y, j)
        return (yj * cos + pltpu.roll(yj, LANES - ROT_DIM // 2, 1) * sa
                + pltpu.roll(yj, ROT_DIM // 2, 1) * sb)

    def emit(y, outs, transform):
        for j in range(n_slab):
            ybuf[j] = transform(y, j)
        for (_, d), o_ref in zip(DILATED_PATTERNS, outs):
            for r in range(d):
                for j in range(n_slab):
                    if d == 1:
                        piece = ybuf[j]
                    else:
                        piece = ybuf[j, pl.ds(r, tm // d, stride=d), :]
                    c0 = r * D_ATT + j * LANES
                    o_ref[:, c0:c0 + LANES] = piece.astype(BF16)

    emit(_dot(xn, w_ref[:, 0:D_ATT]), qkv_refs[0], lambda y, j: rot(y, j) * (HEAD_DIM ** -0.5))
    emit(_dot(xn, w_ref[:, D_ATT:2 * D_ATT]), qkv_refs[1], rot)
    emit(_dot(xn, w_ref[:, 2 * D_ATT:3 * D_ATT]), qkv_refs[2], plain)
    c0 = 3 * D_ATT
    a = _dot(xn, w_ref[:, c0:c0 + D_CONV])
    gt = _dot(xn, w_ref[:, c0 + D_CONV:c0 + 2 * D_CONV])
    u_ref[...] = a * (1.0 / (1.0 + jnp.exp(-gt)))


def _in_proj(x2, g, w_in, cos_t, sa_t, sb_t, seq, tm=512):
    T = x2.shape[0]
    d_in = w_in.shape[1]
    n_s = seq // tm
    n_pat = len(DILATED_PATTERNS)
    row = lambda i: (i, 0)
    tab = lambda i: (i % n_s, 0)
    const = lambda i: (0, 0)
    qkv_specs, qkv_shapes = [], []
    for _ in range(3):
        for _, d in DILATED_PATTERNS:
            qkv_specs.append(pl.BlockSpec((tm // d, d * D_ATT), row))
            qkv_shapes.append(jax.ShapeDtypeStruct((T // d, d * D_ATT), BF16))
    outs = pl.pallas_call(
        functools.partial(_in_proj_kernel, tm=tm),
        grid=(T // tm,),
        in_specs=[pl.BlockSpec((tm, D_MODEL), row),
                  pl.BlockSpec((1, D_MODEL), const),
                  pl.BlockSpec((D_MODEL, d_in), const),
                  pl.BlockSpec((tm, LANES), tab),
                  pl.BlockSpec((tm, LANES), tab),
                  pl.BlockSpec((tm, LANES), tab)],
        out_specs=qkv_specs + [pl.BlockSpec((tm, D_CONV), row)],
        out_shape=qkv_shapes + [jax.ShapeDtypeStruct((T, D_CONV), F32)],
        scratch_shapes=[pltpu.VMEM((D_ATT // LANES, tm, LANES), F32)],
        compiler_params=pltpu.CompilerParams(dimension_semantics=("arbitrary",),
                                             vmem_limit_bytes=VMEM_LIMIT),
        name="in_proj",
    )(x2, g, w_in, cos_t, sa_t, sb_t)
    return outs[0:n_pat], outs[n_pat:2 * n_pat], outs[2 * n_pat:3 * n_pat], outs[3 * n_pat]


def _attn_kernel(q_ref, kc_ref, kp_ref, kn_ref, vc_ref, vp_ref, vn_ref, bias_ref, hm_ref,
                 o_ref, lse_ref, kext, vext, *, lb, nblk_total):
    i = pl.program_id(2)
    kext[0:ATT_HALF] = kp_ref[0]
    kext[ATT_HALF:ATT_HALF + lb] = kc_ref[0]
    kext[ATT_HALF + lb:] = kn_ref[0]
    vext[0:ATT_HALF] = vp_ref[0]
    vext[ATT_HALF:ATT_HALF + lb] = vc_ref[0]
    vext[ATT_HALF + lb:] = vn_ref[0]

    lane = lax.broadcasted_iota(jnp.int32, (ATT_QB, LANES), 1)
    first_head = lane < HEAD_DIM
    mask_a = hm_ref[0:1, :]
    mask_b = hm_ref[1:2, :]
    nblk = lb // ATT_QB

    def body(n, carry):
        r0 = pl.multiple_of(n * ATT_QB, ATT_QB)
        gblk = i * nblk + n
        bidx = jnp.where(gblk == 0, 0, jnp.where(gblk == nblk_total - 1, 2, 1))
        bias = bias_ref[bidx]
        lse_tile = jnp.zeros((ATT_QB, LANES), F32)
        for hp in range(D_ATT // LANES):
            c0 = LANES * hp
            q2 = q_ref[0, pl.ds(r0, ATT_QB), c0:c0 + LANES]
            kw = kext[pl.ds(r0, ATT_WIN), c0:c0 + LANES]
            vw = vext[pl.ds(r0, ATT_WIN), c0:c0 + LANES]
            halves = []
            for hm in (mask_a, mask_b):
                s = _dot_nt(q2 * hm, kw) + bias
                m = jnp.max(s, axis=-1, keepdims=True)
                p = jnp.exp(s - m)
                l = jnp.sum(p, axis=-1, keepdims=True)
                pv = _dot(p.astype(BF16), vw)
                halves.append((pv * (1.0 / l), m + jnp.log(l)))
            o2 = jnp.where(first_head, halves[0][0], halves[1][0])
            o_ref[0, pl.ds(r0, ATT_QB), c0:c0 + LANES] = o2.astype(o_ref.dtype)
            lse_tile = jnp.where(lane == 2 * hp, halves[0][1], lse_tile)
            lse_tile = jnp.where(lane == 2 * hp + 1, halves[1][1], lse_tile)
        lse_ref[0, pl.ds(r0, ATT_QB), :] = lse_tile
        return carry

    lax.fori_loop(0, nblk, body, 0)


def _attn_bias():
    i = np.arange(ATT_QB)[:, None]
    j = np.arange(ATT_WIN)[None, :]
    band = (j >= i) & (j <= i + 2 * ATT_HALF)
    first = band & (j >= ATT_HALF)
    last = band & (j < ATT_HALF + ATT_QB)
    tabs = np.stack([first, band, last]).astype(np.float32)
    return jnp.asarray((1.0 - tabs) * NEG_INF, dtype=F32)


def _head_masks():
    lane = np.arange(LANES)
    m = np.zeros((16, LANES), np.float32)
    m[0] = lane < HEAD_DIM
    m[1] = lane >= HEAD_DIM
    return jnp.asarray(m, dtype=BF16)


def _attn(q, k, v, bias, hmask, batch, seq, d, lb=512):
    L = seq // d
    lb = min(lb, L)
    nblk_total = L // ATT_QB
    assert nblk_total >= 2 and L % lb == 0 and lb % ATT_QB == 0
    view = lambda t: t.reshape(batch, L, d * D_ATT)
    hb = lb // ATT_HALF
    cur = lambda b, r, i: (b, i, r)
    prev = lambda b, r, i: (b, jnp.maximum(i * hb - 1, 0), r)
    nxt = lambda b, r, i: (b, jnp.minimum((i + 1) * hb, L // ATT_HALF - 1), r)
    blk = pl.BlockSpec((1, lb, D_ATT), cur)
    halo_p = pl.BlockSpec((1, ATT_HALF, D_ATT), prev)
    halo_n = pl.BlockSpec((1, ATT_HALF, D_ATT), nxt)
    o, lse = pl.pallas_call(
        functools.partial(_attn_kernel, lb=lb, nblk_total=nblk_total),
        grid=(batch, d, L // lb),
        in_specs=[blk, blk, halo_p, halo_n, blk, halo_p, halo_n,
                  pl.BlockSpec((3, ATT_QB, ATT_WIN), lambda b, r, i: (0, 0, 0)),
                  pl.BlockSpec((16, LANES), lambda b, r, i: (0, 0))],
        out_specs=[blk, pl.BlockSpec((1, lb, LANES), cur)],
        out_shape=[jax.ShapeDtypeStruct((batch, L, d * D_ATT), BF16),
                   jax.ShapeDtypeStruct((batch, L, d * LANES), F32)],
        scratch_shapes=[pltpu.VMEM((lb + 2 * ATT_HALF, D_ATT), BF16),
                        pltpu.VMEM((lb + 2 * ATT_HALF, D_ATT), BF16)],
        compiler_params=pltpu.CompilerParams(
            dimension_semantics=("arbitrary", "arbitrary", "arbitrary"),
            vmem_limit_bytes=VMEM_LIMIT),
        name=f"attn_d{d}",
    )(view(q), view(k), view(k), view(k), view(v), view(v), view(v), bias, hmask)
    return o.reshape(batch * L, d * D_ATT), lse.reshape(batch * L, d * LANES)


def _conv_kernel(uc_ref, up_ref, un_ref, w_ref, b_ref, lg_ref, lb_ref, c_ref, ext, *, ts, rows):
    i = pl.program_id(1)
    n = pl.num_programs(1)
    ext[0:CONV_HALO] = jnp.where(i > 0, up_ref[0], 0.0)
    ext[CONV_HALO:CONV_HALO + ts] = uc_ref[0]
    ext[CONV_HALO + ts:] = jnp.where(i < n - 1, un_ref[0], 0.0)
    bias = b_ref[...]
    lg = lg_ref[...]
    lb = lb_ref[...]
    off = CONV_HALO - CONV_PAD
    for c in range(ts // rows):
        r0 = c * rows
        acc = jnp.broadcast_to(bias, (rows, D_CONV))
        for k in range(CONV_WIDTH):
            acc = acc + ext[r0 + off + k:r0 + off + k + rows, :] * w_ref[k:k + 1, :]
        mu = jnp.mean(acc, axis=-1, keepdims=True)
        cen = acc - mu
        var = jnp.mean(cen * cen, axis=-1, keepdims=True)
        y = cen * lax.rsqrt(var + EPS) * lg + lb
        c_ref[0, r0:r0 + rows, :] = (y * (1.0 / (1.0 + jnp.exp(-y)))).astype(c_ref.dtype)


def _conv(u3, conv_w, conv_b, ln_g, ln_b, ts=256, rows=32):
    B, S, C = u3.shape
    hb = ts // CONV_HALO
    cur = lambda b, i: (b, i, 0)
    prev = lambda b, i: (b, jnp.maximum(i * hb - 1, 0), 0)
    nxt = lambda b, i: (b, jnp.minimum((i + 1) * hb, S // CONV_HALO - 1), 0)
    const = lambda b, i: (0, 0)
    return pl.pallas_call(
        functools.partial(_conv_kernel, ts=ts, rows=rows),
        grid=(B, S // ts),
        in_specs=[pl.BlockSpec((1, ts, C), cur),
                  pl.BlockSpec((1, CONV_HALO, C), prev),
                  pl.BlockSpec((1, CONV_HALO, C), nxt),
                  pl.BlockSpec((CONV_WIDTH, C), const),
                  pl.BlockSpec((1, C), const),
                  pl.BlockSpec((1, C), const),
                  pl.BlockSpec((1, C), const)],
        out_specs=pl.BlockSpec((1, ts, C), cur),
        out_shape=jax.ShapeDtypeStruct((B, S, C), BF16),
        scratch_shapes=[pltpu.VMEM((ts + 2 * CONV_HALO, C), F32)],
        compiler_params=pltpu.CompilerParams(dimension_semantics=("arbitrary", "arbitrary"),
                                             vmem_limit_bytes=VMEM_LIMIT),
        name="conv",
    )(u3, u3, u3, conv_w, conv_b, ln_g, ln_b)


def _mem_kv_kernel(mem_ref, g_ref, wk_ref, wv_ref, k_ref, v_ref):
    mn = _rms(mem_ref[...], g_ref[...]).astype(BF16)
    k_ref[...] = _dot(mn, wk_ref[...]).astype(BF16)
    v_ref[...] = _dot(mn, wv_ref[...]).astype(BF16)


def _mem_kv(mem2, g, wk, wv):
    R = mem2.shape[0]
    full = lambda shape: pl.BlockSpec(shape, lambda i: (0, 0))
    return pl.pallas_call(
        _mem_kv_kernel,
        grid=(1,),
        in_specs=[full((R, D_MODEL)), full((1, D_MODEL)),
                  full((D_MODEL, D_MODEL)), full((D_MODEL, D_MODEL))],
        out_specs=[full((R, D_MODEL))] * 2,
        out_shape=[jax.ShapeDtypeStruct((R, D_MODEL), BF16)] * 2,
        compiler_params=pltpu.CompilerParams(dimension_semantics=("arbitrary",),
                                             vmem_limit_bytes=VMEM_LIMIT),
        name="mem_kv",
    )(mem2, g, wk, wv)


def _mix_xattn_kernel(x_ref, o1_ref, o2_ref, o3_ref, l1_ref, l2_ref, l3_ref, c_ref, e_ref,
                      wo_ref, gx_ref, wq_ref, xk_ref, xv_ref, wxo_ref, h_ref, obuf, lbuf, *, tm):
    n_slab = D_ATT // LANES
    o_pos, l_pos = [], []
    for p, ((_, d), o_ref, l_ref) in enumerate(zip(DILATED_PATTERNS, (o1_ref, o2_ref, o3_ref),
                                                   (l1_ref, l2_ref, l3_ref))):
        if d == 1:
            o_pos.append(o_ref[...].astype(F32))
            l_pos.append(l_ref[...])
            continue
        for r in range(d):
            rows = pl.ds(r, tm // d, stride=d)
            lbuf[p, rows, :] = l_ref[:, r * LANES:(r + 1) * LANES]
            for j in range(n_slab):
                c0 = r * D_ATT + j * LANES
                obuf[p, j, rows, :] = o_ref[:, c0:c0 + LANES].astype(F32)
        o_pos.append(jnp.concatenate([obuf[p, j] for j in range(n_slab)], axis=1))
        l_pos.append(lbuf[p])

    l1, l2, l3 = l_pos
    m = jnp.maximum(jnp.maximum(l1, l2), l3)
    e1, e2, e3 = jnp.exp(l1 - m), jnp.exp(l2 - m), jnp.exp(l3 - m)
    inv = 1.0 / (e1 + e2 + e3)
    expand = e_ref[...]

    def widen(w):
        hi = w.astype(BF16)
        lo = (w - hi.astype(F32)).astype(BF16)
        return _dot(hi, expand) + _dot(lo, expand)

    att = (widen(e1 * inv) * o_pos[0] + widen(e2 * inv) * o_pos[1] + widen(e3 * inv) * o_pos[2])
    h1 = (x_ref[...] + _dot(att.astype(BF16), wo_ref[0:D_ATT, :])
          + _dot(c_ref[...], wo_ref[D_ATT:, :]))

    xq = (_dot(_rms(h1, gx_ref[...]).astype(BF16), wq_ref[...])
          * (XATT_HEAD_DIM ** -0.5)).astype(BF16)
    heads = []
    for h in range(XATT_HEADS):
        sl = slice(h * XATT_HEAD_DIM, (h + 1) * XATT_HEAD_DIM)
        s = _dot_nt(xq[:, sl], xk_ref[0, :, sl])
        mx = jnp.max(s, axis=-1, keepdims=True)
        p = jnp.exp(s - mx)
        den = jnp.sum(p, axis=-1, keepdims=True)
        heads.append((_dot(p.astype(BF16), xv_ref[0, :, sl]) * (1.0 / den)).astype(BF16))
    xo = jnp.concatenate(heads, axis=1)
    h_ref[...] = h1 + _dot(xo, wxo_ref[...])


def _mix_xattn(x2, os_, lses, c2, expand, w_out, gx, w_xq, xk, xv, w_xo, seq, tm=256):
    T = x2.shape[0]
    n_mem = xk.shape[1]
    per_b = seq // tm
    row = lambda i: (i, 0)
    const = lambda i: (0, 0)
    memb = lambda i: (i // per_b, 0, 0)
    sq = pl.BlockSpec((D_MODEL, D_MODEL), const)
    n_pat = len(DILATED_PATTERNS)
    return pl.pallas_call(
        functools.partial(_mix_xattn_kernel, tm=tm),
        grid=(T // tm,),
        in_specs=[pl.BlockSpec((tm, D_MODEL), row)]
                 + [pl.BlockSpec((tm // d, d * D_ATT), row) for _, d in DILATED_PATTERNS]
                 + [pl.BlockSpec((tm // d, d * LANES), row) for _, d in DILATED_PATTERNS]
                 + [pl.BlockSpec((tm, D_CONV), row),
                    pl.BlockSpec((LANES, D_ATT), const),
                    sq, pl.BlockSpec((1, D_MODEL), const), sq,
                    pl.BlockSpec((1, n_mem, D_MODEL), memb),
                    pl.BlockSpec((1, n_mem, D_MODEL), memb),
                    sq],
        out_specs=pl.BlockSpec((tm, D_MODEL), row),
        out_shape=jax.ShapeDtypeStruct((T, D_MODEL), F32),
        scratch_shapes=[pltpu.VMEM((n_pat, D_ATT // LANES, tm, LANES), F32),
                        pltpu.VMEM((n_pat, tm, LANES), F32)],
        compiler_params=pltpu.CompilerParams(dimension_semantics=("arbitrary",),
                                             vmem_limit_bytes=VMEM_LIMIT),
        name="mix_xattn",
    )(x2, *os_, *lses, c2, expand, w_out, gx, w_xq, xk, xv, w_xo)


def _mlp_kernel(h_ref, g_ref, wu_ref, wd_ref, gf_ref, out_ref, *, chunk):
    h = h_ref[...]
    hn = _rms(h, g_ref[...]).astype(BF16)
    acc = h
    for j in range(D_FF // chunk):
        u = jnp.maximum(_dot(hn, wu_ref[:, j * chunk:(j + 1) * chunk]), 0.0)
        acc = acc + _dot((u * u).astype(BF16), wd_ref[j * chunk:(j + 1) * chunk, :])
    out_ref[...] = _rms(acc, gf_ref[...])


def _mlp(h2, g, w_up, w_down, gf, tm=512, chunk=1024):
    T = h2.shape[0]
    row = lambda i: (i, 0)
    const = lambda i: (0, 0)
    return pl.pallas_call(
        functools.partial(_mlp_kernel, chunk=chunk),
        grid=(T // tm,),
        in_specs=[pl.BlockSpec((tm, D_MODEL), row),
                  pl.BlockSpec((1, D_MODEL), const),
                  pl.BlockSpec((D_MODEL, D_FF), const),
                  pl.BlockSpec((D_FF, D_MODEL), const),
                  pl.BlockSpec((1, D_MODEL), const)],
        out_specs=pl.BlockSpec((tm, D_MODEL), row),
        out_shape=jax.ShapeDtypeStruct((T, D_MODEL), F32),
        compiler_params=pltpu.CompilerParams(dimension_semantics=("arbitrary",),
                                             vmem_limit_bytes=VMEM_LIMIT),
        name="mlp",
    )(h2, g, w_up, w_down, gf)


def _rotary_tables(seq):
    half = ROT_DIM // 2
    freqs = ROPE_THETA ** (-jnp.arange(0, ROT_DIM, 2, dtype=F32) / ROT_DIM)
    ang = jnp.arange(seq, dtype=F32)[:, None] * freqs[None, :]
    cos, sin = jnp.cos(ang), jnp.sin(ang)
    zeros = jnp.zeros((seq, HEAD_DIM - ROT_DIM), F32)
    ones = jnp.ones((seq, HEAD_DIM - ROT_DIM), F32)
    z8 = jnp.zeros((seq, half), F32)
    cos_h = jnp.concatenate([cos, cos, ones], axis=1)
    sa_h = jnp.concatenate([-sin, z8, zeros], axis=1)
    sb_h = jnp.concatenate([z8, sin, zeros], axis=1)
    rep = LANES // HEAD_DIM
    return tuple(jnp.tile(t, (1, rep)) for t in (cos_h, sa_h, sb_h))


def _expand_matrix():
    e = np.zeros((LANES, D_ATT), np.float32)
    for h in range(ATT_HEADS):
        e[h, h * HEAD_DIM:(h + 1) * HEAD_DIM] = 1.0
    return jnp.asarray(e, dtype=BF16)


def kernel(x, mem, norm_mix_g, w_in, conv_w, conv_b, conv_ln_g, conv_ln_b, w_out, norm_x_g,
           norm_mem_g, w_xq, w_xk, w_xv, w_xo, norm_mlp_g, w_up, w_down, norm_final_g):
    B, S, D = x.shape
    n_mem = mem.shape[1]
    depth = w_in.shape[0]
    T = B * S
    cos_t, sa_t, sb_t = _rotary_tables(S)
    bias = _attn_bias()
    hmask = _head_masks()
    expand = _expand_matrix()
    row = lambda g: g.reshape(1, -1)

    h = x.reshape(T, D)
    for l in range(depth):
        q, k, v, u = _in_proj(h, row(norm_mix_g[l]), w_in[l].astype(BF16), cos_t, sa_t, sb_t, S)
        os_, lses = [], []
        for p, (_, d) in enumerate(DILATED_PATTERNS):
            o, lse = _attn(q[p], k[p], v[p], bias, hmask, B, S, d)
            os_.append(o)
            lses.append(lse)
        c = _conv(u.reshape(B, S, D_CONV), conv_w[l], row(conv_b[l]), row(conv_ln_g[l]),
                  row(conv_ln_b[l])).reshape(T, D_CONV)
        xk, xv = _mem_kv(mem.reshape(B * n_mem, D), row(norm_mem_g[l]),
                         w_xk[l].astype(BF16), w_xv[l].astype(BF16))
        h = _mix_xattn(h, os_, lses, c, expand, w_out[l].astype(BF16), row(norm_x_g[l]),
                       w_xq[l].astype(BF16), xk.reshape(B, n_mem, D), xv.reshape(B, n_mem, D),
                       w_xo[l].astype(BF16), S)
        last = l == depth - 1
        gf = row(norm_final_g) if last else None
        assert last, "final norm is fused into the last layer's MLP kernel"
        h = _mlp(h, row(norm_mlp_g[l]), w_up[l].astype(BF16), w_down[l].astype(BF16), gf)
    return h.reshape(B, S, D)
```

```python
import functools
import math

import numpy as np
import jax
import jax.numpy as jnp
from jax import lax
from jax.experimental import pallas as pl
from jax.experimental.pallas import tpu as pltpu

F32 = jnp.float32
BF16 = jnp.bfloat16

D_MODEL = 1024
ATT_HEADS = 8
HEAD_DIM = 64
D_ATT = ATT_HEADS * HEAD_DIM
D_CONV = D_MODEL - D_ATT
DILATED_PATTERNS = ((128, 1), (512, 4), (2048, 16))
ROPE_THETA = 500000.0
ROT_DIM = HEAD_DIM // 4
CONV_WIDTH = 31
CONV_PAD = (CONV_WIDTH - 1) // 2
XATT_HEADS = 4
XATT_HEAD_DIM = D_MODEL // XATT_HEADS
D_FF = 4 * D_MODEL
EPS = 1e-6
NEG_INF = -1e30
LN2 = math.log(2.0)
Q_SCALE = HEAD_DIM ** -0.5 / LN2

LANES = 128
SUBLANES = 8
ATT_HALF = 64
ATT_QB = 2 * ATT_HALF
ATT_WIN = ATT_QB + 2 * ATT_HALF
CONV_HALO = 16
VMEM_LIMIT = 56 * 1024 * 1024


def _dot(a, b):
    return jnp.dot(a, b, preferred_element_type=F32)


def _dot_nt(a, b):
    return lax.dot_general(a, b, (((1,), (1,)), ((), ())), preferred_element_type=F32)


def _rms(x, g):
    var = jnp.mean(x * x, axis=-1, keepdims=True)
    return x * lax.rsqrt(var + EPS) * g


def _in_proj_kernel(x_ref, g_ref, w_ref, cos_ref, sa_ref, sb_ref, *refs, tm):
    n_pat = len(DILATED_PATTERNS)
    qkv_refs = [refs[a * n_pat:(a + 1) * n_pat] for a in range(3)]
    u_ref = refs[3 * n_pat]
    ybuf = refs[3 * n_pat + 1]
    xn = _rms(x_ref[...], g_ref[...]).astype(BF16)
    cos, sa, sb = cos_ref[...], sa_ref[...], sb_ref[...]
    n_slab = D_ATT // LANES

    def plain(y, j):
        return y[:, LANES * j:LANES * (j + 1)]

    def rot(y, j):
        yj = plain(y, j)
        return (yj * cos + pltpu.roll(yj, LANES - ROT_DIM // 2, 1) * sa
                + pltpu.roll(yj, ROT_DIM // 2, 1) * sb)

    def emit(y, outs, transform):
        for j in range(n_slab):
            ybuf[j] = transform(y, j)
        for (_, d), o_ref in zip(DILATED_PATTERNS, outs):
            for r in range(d):
                for j in range(n_slab):
                    if d == 1:
                        piece = ybuf[j]
                    else:
                        piece = ybuf[j, pl.ds(r, tm // d, stride=d), :]
                    c0 = r * D_ATT + j * LANES
                    o_ref[:, c0:c0 + LANES] = piece.astype(BF16)

    emit(_dot(xn, w_ref[:, 0:D_ATT]), qkv_refs[0], lambda y, j: rot(y, j) * Q_SCALE)
    emit(_dot(xn, w_ref[:, D_ATT:2 * D_ATT]), qkv_refs[1], rot)
    emit(_dot(xn, w_ref[:, 2 * D_ATT:3 * D_ATT]), qkv_refs[2], plain)
    c0 = 3 * D_ATT
    a = _dot(xn, w_ref[:, c0:c0 + D_CONV])
    gt = _dot(xn, w_ref[:, c0 + D_CONV:c0 + 2 * D_CONV])
    u_ref[...] = a * (1.0 / (1.0 + jnp.exp(-gt)))


def _in_proj(x2, g, w_in, cos_t, sa_t, sb_t, seq, tm=512):
    T = x2.shape[0]
    d_in = w_in.shape[1]
    n_s = seq // tm
    n_pat = len(DILATED_PATTERNS)
    row = lambda i: (i, 0)
    tab = lambda i: (i % n_s, 0)
    const = lambda i: (0, 0)
    qkv_specs, qkv_shapes = [], []
    for _ in range(3):
        for _, d in DILATED_PATTERNS:
            qkv_specs.append(pl.BlockSpec((tm // d, d * D_ATT), row))
            qkv_shapes.append(jax.ShapeDtypeStruct((T // d, d * D_ATT), BF16))
    outs = pl.pallas_call(
        functools.partial(_in_proj_kernel, tm=tm),
        grid=(T // tm,),
        in_specs=[pl.BlockSpec((tm, D_MODEL), row),
                  pl.BlockSpec((1, D_MODEL), const),
                  pl.BlockSpec((D_MODEL, d_in), const),
                  pl.BlockSpec((tm, LANES), tab),
                  pl.BlockSpec((tm, LANES), tab),
                  pl.BlockSpec((tm, LANES), tab)],
        out_specs=qkv_specs + [pl.BlockSpec((tm, D_CONV), row)],
        out_shape=qkv_shapes + [jax.ShapeDtypeStruct((T, D_CONV), F32)],
        scratch_shapes=[pltpu.VMEM((D_ATT // LANES, tm, LANES), F32)],
        compiler_params=pltpu.CompilerParams(dimension_semantics=("arbitrary",),
                                             vmem_limit_bytes=VMEM_LIMIT),
        name="in_proj",
    )(x2, g, w_in, cos_t, sa_t, sb_t)
    return outs[0:n_pat], outs[n_pat:2 * n_pat], outs[2 * n_pat:3 * n_pat], outs[3 * n_pat]


def _attn_kernel(q_ref, kc_ref, kp_ref, kn_ref, vc_ref, vp_ref, vn_ref, bias_ref, hm_ref,
                 o_ref, lse_ref, kext, vext, *, lb, nblk_total):
    i = pl.program_id(2)
    kext[0:ATT_HALF] = kp_ref[0]
    kext[ATT_HALF:ATT_HALF + lb] = kc_ref[0]
    kext[ATT_HALF + lb:] = kn_ref[0]
    vext[0:ATT_HALF] = vp_ref[0]
    vext[ATT_HALF:ATT_HALF + lb] = vc_ref[0]
    vext[ATT_HALF + lb:] = vn_ref[0]

    lane = lax.broadcasted_iota(jnp.int32, (ATT_QB, LANES), 1)
    first_head = lane < HEAD_DIM
    mask_a = hm_ref[0:1, :]
    mask_b = hm_ref[1:2, :]
    nblk = lb // ATT_QB

    def body(n, carry):
        r0 = pl.multiple_of(n * ATT_QB, ATT_QB)
        gblk = i * nblk + n
        bidx = jnp.where(gblk == 0, 0, jnp.where(gblk == nblk_total - 1, 2, 1))
        bias = bias_ref[bidx]
        lse_tile = jnp.zeros((ATT_QB, LANES), F32)
        for hp in range(D_ATT // LANES):
            c0 = LANES * hp
            q2 = q_ref[0, pl.ds(r0, ATT_QB), c0:c0 + LANES]
            kw = kext[pl.ds(r0, ATT_WIN), c0:c0 + LANES]
            vw = vext[pl.ds(r0, ATT_WIN), c0:c0 + LANES]
            halves = []
            for hm in (mask_a, mask_b):
                s = _dot_nt(q2 * hm, kw) + bias
                m = jnp.max(s, axis=-1, keepdims=True)
                p = jnp.exp2(s - m)
                l = jnp.sum(p, axis=-1, keepdims=True)
                pv = _dot(p.astype(BF16), vw)
                halves.append((pv * (1.0 / l), (m + jnp.log2(l)) * LN2))
            o2 = jnp.where(first_head, halves[0][0], halves[1][0])
            o_ref[0, pl.ds(r0, ATT_QB), c0:c0 + LANES] = o2.astype(o_ref.dtype)
            lse_tile = jnp.where(lane == 2 * hp, halves[0][1], lse_tile)
            lse_tile = jnp.where(lane == 2 * hp + 1, halves[1][1], lse_tile)
        lse_ref[0, pl.ds(r0, ATT_QB), :] = lse_tile
        return carry

    lax.fori_loop(0, nblk, body, 0, unroll=True)


def _attn_bias():
    i = np.arange(ATT_QB)[:, None]
    j = np.arange(ATT_WIN)[None, :]
    band = (j >= i) & (j <= i + 2 * ATT_HALF)
    first = band & (j >= ATT_HALF)
    last = band & (j < ATT_HALF + ATT_QB)
    tabs = np.stack([first, band, last]).astype(np.float32)
    return jnp.asarray((1.0 - tabs) * NEG_INF, dtype=F32)


def _head_masks():
    lane = np.arange(LANES)
    m = np.zeros((16, LANES), np.float32)
    m[0] = lane < HEAD_DIM
    m[1] = lane >= HEAD_DIM
    return jnp.asarray(m, dtype=BF16)


def _attn(q, k, v, bias, hmask, batch, seq, d, lb=512):
    L = seq // d
    lb = min(lb, L)
    nblk_total = L // ATT_QB
    assert nblk_total >= 2 and L % lb == 0 and lb % ATT_QB == 0
    view = lambda t: t.reshape(batch, L, d * D_ATT)
    hb = lb // ATT_HALF
    cur = lambda b, r, i: (b, i, r)
    prev = lambda b, r, i: (b, jnp.maximum(i * hb - 1, 0), r)
    nxt = lambda b, r, i: (b, jnp.minimum((i + 1) * hb, L // ATT_HALF - 1), r)
    blk = pl.BlockSpec((1, lb, D_ATT), cur)
    halo_p = pl.BlockSpec((1, ATT_HALF, D_ATT), prev)
    halo_n = pl.BlockSpec((1, ATT_HALF, D_ATT), nxt)
    o, lse = pl.pallas_call(
        functools.partial(_attn_kernel, lb=lb, nblk_total=nblk_total),
        grid=(batch, d, L // lb),
        in_specs=[blk, blk, halo_p, halo_n, blk, halo_p, halo_n,
                  pl.BlockSpec((3, ATT_QB, ATT_WIN), lambda b, r, i: (0, 0, 0)),
                  pl.BlockSpec((16, LANES), lambda b, r, i: (0, 0))],
        out_specs=[blk, pl.BlockSpec((1, lb, LANES), cur)],
        out_shape=[jax.ShapeDtypeStruct((batch, L, d * D_ATT), BF16),
                   jax.ShapeDtypeStruct((batch, L, d * LANES), F32)],
        scratch_shapes=[pltpu.VMEM((lb + 2 * ATT_HALF, D_ATT), BF16),
                        pltpu.VMEM((lb + 2 * ATT_HALF, D_ATT), BF16)],
        compiler_params=pltpu.CompilerParams(
            dimension_semantics=("arbitrary", "arbitrary", "arbitrary"),
            vmem_limit_bytes=VMEM_LIMIT),
        name=f"attn_d{d}",
    )(view(q), view(k), view(k), view(k), view(v), view(v), view(v), bias, hmask)
    return o.reshape(batch * L, d * D_ATT), lse.reshape(batch * L, d * LANES)


def _conv_kernel(uc_ref, up_ref, un_ref, w_ref, b_ref, lg_ref, lb_ref, c_ref, ext, ybuf, *, ts,
                 rows, ln_rows):
    i = pl.program_id(1)
    n = pl.num_programs(1)
    n_slab = D_CONV // LANES
    for g in range(n_slab):
        ls = slice(g * LANES, (g + 1) * LANES)
        ext[g, 0:CONV_HALO] = jnp.where(i > 0, up_ref[0, :, ls], 0.0)
        ext[g, CONV_HALO:CONV_HALO + ts] = uc_ref[0, :, ls]
        ext[g, CONV_HALO + ts:] = jnp.where(i < n - 1, un_ref[0, :, ls], 0.0)
    lg = lg_ref[...]
    lb = lb_ref[...]
    off = CONV_HALO - CONV_PAD
    n_blk = ts // rows

    def taps(idx, carry):
        g = idx // n_blk
        base = pl.multiple_of((idx % n_blk) * rows, rows)
        acc = jnp.broadcast_to(b_ref[g], (rows, LANES))
        for j in range(SUBLANES):
            steps = [m for m in range((CONV_WIDTH + off) // SUBLANES + 1)
                     if 0 <= SUBLANES * m + j - off < CONV_WIDTH]
            win = ext[g, pl.ds(base + j, rows + SUBLANES * max(steps)), :]
            for m in steps:
                k = SUBLANES * m + j - off
                acc = acc + win[SUBLANES * m:SUBLANES * m + rows] * w_ref[g, k:k + 1, :]
        ybuf[g, pl.ds(base, rows), :] = acc
        return carry

    lax.fori_loop(0, n_slab * n_blk, taps, 0)
    for r0 in range(0, ts, ln_rows):
        acc = jnp.concatenate([ybuf[g, r0:r0 + ln_rows, :] for g in range(n_slab)], axis=1)
        mu = jnp.mean(acc, axis=-1, keepdims=True)
        cen = acc - mu
        var = jnp.mean(cen * cen, axis=-1, keepdims=True)
        y = cen * lax.rsqrt(var + EPS) * lg + lb
        c_ref[0, r0:r0 + ln_rows, :] = (y * (1.0 / (1.0 + jnp.exp(-y)))).astype(c_ref.dtype)


def _conv(u3, conv_w, conv_b, ln_g, ln_b, ts=256, rows=128, ln_rows=32):
    B, S, C = u3.shape
    hb = ts // CONV_HALO
    cur = lambda b, i: (b, i, 0)
    prev = lambda b, i: (b, jnp.maximum(i * hb - 1, 0), 0)
    nxt = lambda b, i: (b, jnp.minimum((i + 1) * hb, S // CONV_HALO - 1), 0)
    const = lambda b, i: (0, 0)
    const3 = lambda b, i: (0, 0, 0)
    n_slab = C // LANES
    w_slab = jnp.transpose(conv_w.reshape(CONV_WIDTH, n_slab, LANES), (1, 0, 2))
    b_slab = conv_b.reshape(n_slab, 1, LANES)
    return pl.pallas_call(
        functools.partial(_conv_kernel, ts=ts, rows=rows, ln_rows=ln_rows),
        grid=(B, S // ts),
        in_specs=[pl.BlockSpec((1, ts, C), cur),
                  pl.BlockSpec((1, CONV_HALO, C), prev),
                  pl.BlockSpec((1, CONV_HALO, C), nxt),
                  pl.BlockSpec((n_slab, CONV_WIDTH, LANES), const3),
                  pl.BlockSpec((n_slab, 1, LANES), const3),
                  pl.BlockSpec((1, C), const),
                  pl.BlockSpec((1, C), const)],
        out_specs=pl.BlockSpec((1, ts, C), cur),
        out_shape=jax.ShapeDtypeStruct((B, S, C), BF16),
        scratch_shapes=[pltpu.VMEM((C // LANES, ts + 2 * CONV_HALO, LANES), F32),
                        pltpu.VMEM((C // LANES, ts, LANES), F32)],
        compiler_params=pltpu.CompilerParams(dimension_semantics=("arbitrary", "arbitrary"),
                                             vmem_limit_bytes=VMEM_LIMIT),
        name="conv",
    )(u3, u3, u3, w_slab, b_slab, ln_g, ln_b)


def _mem_kv_kernel(mem_ref, g_ref, wk_ref, wv_ref, k_ref, v_ref):
    mn = _rms(mem_ref[...], g_ref[...]).astype(BF16)
    k_ref[...] = _dot(mn, wk_ref[...]).astype(BF16)
    v_ref[...] = _dot(mn, wv_ref[...]).astype(BF16)


def _mem_kv(mem2, g, wk, wv):
    R = mem2.shape[0]
    full = lambda shape: pl.BlockSpec(shape, lambda i: (0, 0))
    return pl.pallas_call(
        _mem_kv_kernel,
        grid=(1,),
        in_specs=[full((R, D_MODEL)), full((1, D_MODEL)),
                  full((D_MODEL, D_MODEL)), full((D_MODEL, D_MODEL))],
        out_specs=[full((R, D_MODEL))] * 2,
        out_shape=[jax.ShapeDtypeStruct((R, D_MODEL), BF16)] * 2,
        compiler_params=pltpu.CompilerParams(dimension_semantics=("arbitrary",),
                                             vmem_limit_bytes=VMEM_LIMIT),
        name="mem_kv",
    )(mem2, g, wk, wv)


def _mix_xattn_kernel(x_ref, o1_ref, o2_ref, o3_ref, l1_ref, l2_ref, l3_ref, c_ref, e_ref,
                      wo_ref, gx_ref, wq_ref, xk_ref, xv_ref, wxo_ref, h_ref, obuf, lbuf, *, tm):
    n_slab = D_ATT // LANES
    o_pos, l_pos = [], []
    for p, ((_, d), o_ref, l_ref) in enumerate(zip(DILATED_PATTERNS, (o1_ref, o2_ref, o3_ref),
                                                   (l1_ref, l2_ref, l3_ref))):
        if d == 1:
            o_pos.append(o_ref[...].astype(F32))
            l_pos.append(l_ref[...])
            continue
        for r in range(d):
            rows = pl.ds(r, tm // d, stride=d)
            lbuf[p, rows, :] = l_ref[:, r * LANES:(r + 1) * LANES]
            for j in range(n_slab):
                c0 = r * D_ATT + j * LANES
                obuf[p, j, rows, :] = o_ref[:, c0:c0 + LANES].astype(F32)
        o_pos.append(jnp.concatenate([obuf[p, j] for j in range(n_slab)], axis=1))
        l_pos.append(lbuf[p])

    l1, l2, l3 = l_pos
    m = jnp.maximum(jnp.maximum(l1, l2), l3)
    e1, e2, e3 = jnp.exp(l1 - m), jnp.exp(l2 - m), jnp.exp(l3 - m)
    inv = 1.0 / (e1 + e2 + e3)
    expand = e_ref[...]

    def widen(w):
        hi = w.astype(BF16)
        lo = (w - hi.astype(F32)).astype(BF16)
        return _dot(hi, expand) + _dot(lo, expand)

    att = (widen(e1 * inv) * o_pos[0] + widen(e2 * inv) * o_pos[1] + widen(e3 * inv) * o_pos[2])
    h1 = (x_ref[...] + _dot(att.astype(BF16), wo_ref[0:D_ATT, :])
          + _dot(c_ref[...], wo_ref[D_ATT:, :]))

    xq = (_dot(_rms(h1, gx_ref[...]).astype(BF16), wq_ref[...])
          * (XATT_HEAD_DIM ** -0.5)).astype(BF16)
    heads = []
    for h in range(XATT_HEADS):
        sl = slice(h * XATT_HEAD_DIM, (h + 1) * XATT_HEAD_DIM)
        s = _dot_nt(xq[:, sl], xk_ref[0, :, sl])
        mx = jnp.max(s, axis=-1, keepdims=True)
        p = jnp.exp(s - mx)
        den = jnp.sum(p, axis=-1, keepdims=True)
        heads.append((_dot(p.astype(BF16), xv_ref[0, :, sl]) * (1.0 / den)).astype(BF16))
    xo = jnp.concatenate(heads, axis=1)
    h_ref[...] = h1 + _dot(xo, wxo_ref[...])


def _mix_xattn(x2, os_, lses, c2, expand, w_out, gx, w_xq, xk, xv, w_xo, seq, tm=256):
    T = x2.shape[0]
    n_mem = xk.shape[1]
    per_b = seq // tm
    row = lambda i: (i, 0)
    const = lambda i: (0, 0)
    memb = lambda i: (i // per_b, 0, 0)
    sq = pl.BlockSpec((D_MODEL, D_MODEL), const)
    n_pat = len(DILATED_PATTERNS)
    return pl.pallas_call(
        functools.partial(_mix_xattn_kernel, tm=tm),
        grid=(T // tm,),
        in_specs=[pl.BlockSpec((tm, D_MODEL), row)]
                 + [pl.BlockSpec((tm // d, d * D_ATT), row) for _, d in DILATED_PATTERNS]
                 + [pl.BlockSpec((tm // d, d * LANES), row) for _, d in DILATED_PATTERNS]
                 + [pl.BlockSpec((tm, D_CONV), row),
                    pl.BlockSpec((LANES, D_ATT), const),
                    sq, pl.BlockSpec((1, D_MODEL), const), sq,
                    pl.BlockSpec((1, n_mem, D_MODEL), memb),
                    pl.BlockSpec((1, n_mem, D_MODEL), memb),
                    sq],
        out_specs=pl.BlockSpec((tm, D_MODEL), row),
        out_shape=jax.ShapeDtypeStruct((T, D_MODEL), F32),
        scratch_shapes=[pltpu.VMEM((n_pat, D_ATT // LANES, tm, LANES), F32),
                        pltpu.VMEM((n_pat, tm, LANES), F32)],
        compiler_params=pltpu.CompilerParams(dimension_semantics=("arbitrary",),
                                             vmem_limit_bytes=VMEM_LIMIT),
        name="mix_xattn",
    )(x2, *os_, *lses, c2, expand, w_out, gx, w_xq, xk, xv, w_xo)


def _mlp_kernel(h_ref, g_ref, wu_ref, wd_ref, gf_ref, out_ref, *, chunk):
    h = h_ref[...]
    hn = _rms(h, g_ref[...]).astype(BF16)
    acc = h
    for j in range(D_FF // chunk):
        u = jnp.maximum(_dot(hn, wu_ref[:, j * chunk:(j + 1) * chunk]), 0.0)
        acc = acc + _dot((u * u).astype(BF16), wd_ref[j * chunk:(j + 1) * chunk, :])
    out_ref[...] = _rms(acc, gf_ref[...])


def _mlp(h2, g, w_up, w_down, gf, tm=512, chunk=1024):
    T = h2.shape[0]
    row = lambda i: (i, 0)
    const = lambda i: (0, 0)
    return pl.pallas_call(
        functools.partial(_mlp_kernel, chunk=chunk),
        grid=(T // tm,),
        in_specs=[pl.BlockSpec((tm, D_MODEL), row),
                  pl.BlockSpec((1, D_MODEL), const),
                  pl.BlockSpec((D_MODEL, D_FF), const),
                  pl.BlockSpec((D_FF, D_MODEL), const),
                  pl.BlockSpec((1, D_MODEL), const)],
        out_specs=pl.BlockSpec((tm, D_MODEL), row),
        out_shape=jax.ShapeDtypeStruct((T, D_MODEL), F32),
        compiler_params=pltpu.CompilerParams(dimension_semantics=("arbitrary",),
                                             vmem_limit_bytes=VMEM_LIMIT),
        name="mlp",
    )(h2, g, w_up, w_down, gf)


def _rotary_tables(seq):
    half = ROT_DIM // 2
    freqs = ROPE_THETA ** (-jnp.arange(0, ROT_DIM, 2, dtype=F32) / ROT_DIM)
    ang = jnp.arange(seq, dtype=F32)[:, None] * freqs[None, :]
    cos, sin = jnp.cos(ang), jnp.sin(ang)
    zeros = jnp.zeros((seq, HEAD_DIM - ROT_DIM), F32)
    ones = jnp.ones((seq, HEAD_DIM - ROT_DIM), F32)
    z8 = jnp.zeros((seq, half), F32)
    cos_h = jnp.concatenate([cos, cos, ones], axis=1)
    sa_h = jnp.concatenate([-sin, z8, zeros], axis=1)
    sb_h = jnp.concatenate([z8, sin, zeros], axis=1)
    rep = LANES // HEAD_DIM
    return tuple(jnp.tile(t, (1, rep)) for t in (cos_h, sa_h, sb_h))


def _expand_matrix():
    e = np.zeros((LANES, D_ATT), np.float32)
    for h in range(ATT_HEADS):
        e[h, h * HEAD_DIM:(h + 1) * HEAD_DIM] = 1.0
    return jnp.asarray(e, dtype=BF16)


def kernel(x, mem, norm_mix_g, w_in, conv_w, conv_b, conv_ln_g, conv_ln_b, w_out, norm_x_g,
           norm_mem_g, w_xq, w_xk, w_xv, w_xo, norm_mlp_g, w_up, w_down, norm_final_g):
    B, S, D = x.shape
    n_mem = mem.shape[1]
    depth = w_in.shape[0]
    T = B * S
    cos_t, sa_t, sb_t = _rotary_tables(S)
    bias = _attn_bias()
    hmask = _head_masks()
    expand = _expand_matrix()
    row = lambda g: g.reshape(1, -1)

    h = x.reshape(T, D)
    for l in range(depth):
        q, k, v, u = _in_proj(h, row(norm_mix_g[l]), w_in[l].astype(BF16), cos_t, sa_t, sb_t, S)
        os_, lses = [], []
        for p, (_, d) in enumerate(DILATED_PATTERNS):
            o, lse = _attn(q[p], k[p], v[p], bias, hmask, B, S, d)
            os_.append(o)
            lses.append(lse)
        c = _conv(u.reshape(B, S, D_CONV), conv_w[l], row(conv_b[l]), row(conv_ln_g[l]),
                  row(conv_ln_b[l])).reshape(T, D_CONV)
        xk, xv = _mem_kv(mem.reshape(B * n_mem, D), row(norm_mem_g[l]),
                         w_xk[l].astype(BF16), w_xv[l].astype(BF16))
        h = _mix_xattn(h, os_, lses, c, expand, w_out[l].astype(BF16), row(norm_x_g[l]),
                       w_xq[l].astype(BF16), xk.reshape(B, n_mem, D), xv.reshape(B, n_mem, D),
                       w_xo[l].astype(BF16), S)
        last = l == depth - 1
        gf = row(norm_final_g) if last else None
        assert last, "final norm is fused into the last layer's MLP kernel"
        h = _mlp(h, row(norm_mlp_g[l]), w_up[l].astype(BF16), w_down[l].astype(BF16), gf)
    return h.reshape(B, S, D)
```

```python
import functools
import math

import numpy as np
import jax
import jax.numpy as jnp
from jax import lax
from jax.experimental import pallas as pl
from jax.experimental.pallas import tpu as pltpu

F32 = jnp.float32
BF16 = jnp.bfloat16

D_MODEL = 1024
ATT_HEADS = 8
HEAD_DIM = 64
D_ATT = ATT_HEADS * HEAD_DIM
D_CONV = D_MODEL - D_ATT
DILATED_PATTERNS = ((128, 1), (512, 4), (2048, 16))
ROPE_THETA = 500000.0
ROT_DIM = HEAD_DIM // 4
CONV_WIDTH = 31
CONV_PAD = (CONV_WIDTH - 1) // 2
XATT_HEADS = 4
XATT_HEAD_DIM = D_MODEL // XATT_HEADS
D_FF = 4 * D_MODEL
EPS = 1e-6
NEG_INF = -1e30
LN2 = math.log(2.0)
Q_SCALE = HEAD_DIM ** -0.5 / LN2

LANES = 128
SUBLANES = 8
ATT_HALF = 64
ATT_QB = 2 * ATT_HALF
ATT_WIN = ATT_QB + 2 * ATT_HALF
CONV_HALO = 16
VMEM_LIMIT = 56 * 1024 * 1024


def _dot(a, b):
    return jnp.dot(a, b, preferred_element_type=F32)


def _dot_nt(a, b):
    return lax.dot_general(a, b, (((1,), (1,)), ((), ())), preferred_element_type=F32)


def _rms(x, g):
    var = jnp.mean(x * x, axis=-1, keepdims=True)
    return x * lax.rsqrt(var + EPS) * g


def _cast_once(pairs, chunk=512):
    @pl.when(pl.program_id(0) == 0)
    def _():
        for src, dst in pairs:
            for c in range(0, src.shape[1], chunk):
                dst[:, c:c + chunk] = src[:, c:c + chunk].astype(BF16)


def _in_proj_kernel(x_ref, g_ref, wf_ref, cos_ref, sa_ref, sb_ref, *refs, tm):
    n_pat = len(DILATED_PATTERNS)
    qkv_refs = [refs[a * n_pat:(a + 1) * n_pat] for a in range(3)]
    u_ref = refs[3 * n_pat]
    ybuf = refs[3 * n_pat + 1]
    w_ref = refs[3 * n_pat + 2]
    _cast_once([(wf_ref, w_ref)])
    xn = _rms(x_ref[...], g_ref[...]).astype(BF16)
    cos, sa, sb = cos_ref[...], sa_ref[...], sb_ref[...]
    n_slab = D_ATT // LANES

    def plain(y, j):
        return y[:, LANES * j:LANES * (j + 1)]

    def rot(y, j):
        yj = plain(y, j)
        return (yj * cos + pltpu.roll(yj, LANES - ROT_DIM // 2, 1) * sa
                + pltpu.roll(yj, ROT_DIM // 2, 1) * sb)

    def emit(y, outs, transform):
        for j in range(n_slab):
            ybuf[j] = transform(y, j)
        for (_, d), o_ref in zip(DILATED_PATTERNS, outs):
            for r in range(d):
                for j in range(n_slab):
                    if d == 1:
                        piece = ybuf[j]
                    else:
                        piece = ybuf[j, pl.ds(r, tm // d, stride=d), :]
                    c0 = r * D_ATT + j * LANES
                    o_ref[:, c0:c0 + LANES] = piece.astype(BF16)

    emit(_dot(xn, w_ref[:, 0:D_ATT]), qkv_refs[0], lambda y, j: rot(y, j) * Q_SCALE)
    emit(_dot(xn, w_ref[:, D_ATT:2 * D_ATT]), qkv_refs[1], rot)
    emit(_dot(xn, w_ref[:, 2 * D_ATT:3 * D_ATT]), qkv_refs[2], plain)
    c0 = 3 * D_ATT
    a = _dot(xn, w_ref[:, c0:c0 + D_CONV])
    gt = _dot(xn, w_ref[:, c0 + D_CONV:c0 + 2 * D_CONV])
    u_ref[...] = a * (1.0 / (1.0 + jnp.exp(-gt)))


def _in_proj(x2, g, w_in, cos_t, sa_t, sb_t, seq, tm=512):
    T = x2.shape[0]
    d_in = w_in.shape[1]
    n_s = seq // tm
    n_pat = len(DILATED_PATTERNS)
    row = lambda i: (i, 0)
    tab = lambda i: (i % n_s, 0)
    const = lambda i: (0, 0)
    qkv_specs, qkv_shapes = [], []
    for _ in range(3):
        for _, d in DILATED_PATTERNS:
            qkv_specs.append(pl.BlockSpec((tm // d, d * D_ATT), row))
            qkv_shapes.append(jax.ShapeDtypeStruct((T // d, d * D_ATT), BF16))
    outs = pl.pallas_call(
        functools.partial(_in_proj_kernel, tm=tm),
        grid=(T // tm,),
        in_specs=[pl.BlockSpec((tm, D_MODEL), row),
                  pl.BlockSpec((1, D_MODEL), const),
                  pl.BlockSpec((D_MODEL, d_in), const, pipeline_mode=pl.Buffered(1)),
                  pl.BlockSpec((tm, LANES), tab),
                  pl.BlockSpec((tm, LANES), tab),
                  pl.BlockSpec((tm, LANES), tab)],
        out_specs=qkv_specs + [pl.BlockSpec((tm, D_CONV), row)],
        out_shape=qkv_shapes + [jax.ShapeDtypeStruct((T, D_CONV), F32)],
        scratch_shapes=[pltpu.VMEM((D_ATT // LANES, tm, LANES), F32),
                        pltpu.VMEM((D_MODEL, d_in), BF16)],
        compiler_params=pltpu.CompilerParams(dimension_semantics=("arbitrary",),
                                             vmem_limit_bytes=VMEM_LIMIT),
        name="in_proj",
    )(x2, g, w_in, cos_t, sa_t, sb_t)
    return outs[0:n_pat], outs[n_pat:2 * n_pat], outs[2 * n_pat:3 * n_pat], outs[3 * n_pat]


def _attn_kernel(q_ref, kc_ref, kp_ref, kn_ref, vc_ref, vp_ref, vn_ref, bias_ref, hm_ref,
                 o_ref, lse_ref, kext, vext, *, lb, nblk_total):
    i = pl.program_id(2)
    kext[0:ATT_HALF] = kp_ref[0]
    kext[ATT_HALF:ATT_HALF + lb] = kc_ref[0]
    kext[ATT_HALF + lb:] = kn_ref[0]
    vext[0:ATT_HALF] = vp_ref[0]
    vext[ATT_HALF:ATT_HALF + lb] = vc_ref[0]
    vext[ATT_HALF + lb:] = vn_ref[0]

    lane = lax.broadcasted_iota(jnp.int32, (ATT_QB, LANES), 1)
    first_head = lane < HEAD_DIM
    mask_a = hm_ref[0:1, :]
    mask_b = hm_ref[1:2, :]
    nblk = lb // ATT_QB

    def body(n, carry):
        r0 = pl.multiple_of(n * ATT_QB, ATT_QB)
        gblk = i * nblk + n
        bidx = jnp.where(gblk == 0, 0, jnp.where(gblk == nblk_total - 1, 2, 1))
        bias = bias_ref[bidx]
        for hp in range(D_ATT // LANES):
            c0 = LANES * hp
            q2 = q_ref[0, pl.ds(r0, ATT_QB), c0:c0 + LANES]
            kw = kext[pl.ds(r0, ATT_WIN), c0:c0 + LANES]
            vw = vext[pl.ds(r0, ATT_WIN), c0:c0 + LANES]
            halves = []
            for hm in (mask_a, mask_b):
                s = _dot_nt(q2 * hm, kw) + bias
                m = jnp.max(s, axis=-1, keepdims=True)
                p = jnp.exp2(s - m)
                l = jnp.sum(p, axis=-1, keepdims=True)
                pv = _dot(p.astype(BF16), vw)
                halves.append((pv * (1.0 / l), (m + jnp.log2(l)) * LN2))
            o2 = jnp.where(first_head, halves[0][0], halves[1][0])
            o_ref[0, pl.ds(r0, ATT_QB), c0:c0 + LANES] = o2.astype(o_ref.dtype)
            lse_ref[0, pl.ds(r0, ATT_QB), c0:c0 + LANES] = jnp.where(first_head, halves[0][1],
                                                                     halves[1][1])
        return carry

    lax.fori_loop(0, nblk, body, 0, unroll=True)


def _attn_bias():
    i = np.arange(ATT_QB)[:, None]
    j = np.arange(ATT_WIN)[None, :]
    band = (j >= i) & (j <= i + 2 * ATT_HALF)
    first = band & (j >= ATT_HALF)
    last = band & (j < ATT_HALF + ATT_QB)
    tabs = np.stack([first, band, last]).astype(np.float32)
    return jnp.asarray((1.0 - tabs) * NEG_INF, dtype=F32)


def _head_masks():
    lane = np.arange(LANES)
    m = np.zeros((16, LANES), np.float32)
    m[0] = lane < HEAD_DIM
    m[1] = lane >= HEAD_DIM
    return jnp.asarray(m, dtype=BF16)


def _attn(q, k, v, bias, hmask, batch, seq, d, lb=512):
    L = seq // d
    lb = min(lb, L)
    nblk_total = L // ATT_QB
    assert nblk_total >= 2 and L % lb == 0 and lb % ATT_QB == 0
    view = lambda t: t.reshape(batch, L, d * D_ATT)
    hb = lb // ATT_HALF
    cur = lambda b, r, i: (b, i, r)
    prev = lambda b, r, i: (b, jnp.maximum(i * hb - 1, 0), r)
    nxt = lambda b, r, i: (b, jnp.minimum((i + 1) * hb, L // ATT_HALF - 1), r)
    blk = pl.BlockSpec((1, lb, D_ATT), cur)
    halo_p = pl.BlockSpec((1, ATT_HALF, D_ATT), prev)
    halo_n = pl.BlockSpec((1, ATT_HALF, D_ATT), nxt)
    o, lse = pl.pallas_call(
        functools.partial(_attn_kernel, lb=lb, nblk_total=nblk_total),
        grid=(batch, d, L // lb),
        in_specs=[blk, blk, halo_p, halo_n, blk, halo_p, halo_n,
                  pl.BlockSpec((3, ATT_QB, ATT_WIN), lambda b, r, i: (0, 0, 0)),
                  pl.BlockSpec((16, LANES), lambda b, r, i: (0, 0))],
        out_specs=[blk, blk],
        out_shape=[jax.ShapeDtypeStruct((batch, L, d * D_ATT), BF16),
                   jax.ShapeDtypeStruct((batch, L, d * D_ATT), F32)],
        scratch_shapes=[pltpu.VMEM((lb + 2 * ATT_HALF, D_ATT), BF16),
                        pltpu.VMEM((lb + 2 * ATT_HALF, D_ATT), BF16)],
        compiler_params=pltpu.CompilerParams(
            dimension_semantics=("arbitrary", "arbitrary", "arbitrary"),
            vmem_limit_bytes=VMEM_LIMIT),
        name=f"attn_d{d}",
    )(view(q), view(k), view(k), view(k), view(v), view(v), view(v), bias, hmask)
    return o.reshape(batch * L, d * D_ATT), lse.reshape(batch * L, d * D_ATT)


def _conv_kernel(uc_ref, up_ref, un_ref, w_ref, b_ref, lg_ref, lb_ref, c_ref, ext, ybuf, *, ts,
                 rows, ln_rows):
    i = pl.program_id(1)
    n = pl.num_programs(1)
    n_slab = D_CONV // LANES
    for g in range(n_slab):
        ls = slice(g * LANES, (g + 1) * LANES)
        ext[g, 0:CONV_HALO] = jnp.where(i > 0, up_ref[0, :, ls], 0.0)
        ext[g, CONV_HALO:CONV_HALO + ts] = uc_ref[0, :, ls]
        ext[g, CONV_HALO + ts:] = jnp.where(i < n - 1, un_ref[0, :, ls], 0.0)
    lg = lg_ref[...]
    lb = lb_ref[...]
    off = CONV_HALO - CONV_PAD
    n_blk = ts // rows

    def taps(idx, carry):
        g = idx // n_blk
        base = pl.multiple_of((idx % n_blk) * rows, rows)
        acc = jnp.broadcast_to(b_ref[g], (rows, LANES))
        for j in range(SUBLANES):
            steps = [m for m in range((CONV_WIDTH + off) // SUBLANES + 1)
                     if 0 <= SUBLANES * m + j - off < CONV_WIDTH]
            win = ext[g, pl.ds(base + j, rows + SUBLANES * max(steps)), :]
            for m in steps:
                k = SUBLANES * m + j - off
                acc = acc + win[SUBLANES * m:SUBLANES * m + rows] * w_ref[g, k:k + 1, :]
        ybuf[g, pl.ds(base, rows), :] = acc
        return carry

    lax.fori_loop(0, n_slab * n_blk, taps, 0)
    for r0 in range(0, ts, ln_rows):
        acc = jnp.concatenate([ybuf[g, r0:r0 + ln_rows, :] for g in range(n_slab)], axis=1)
        mu = jnp.mean(acc, axis=-1, keepdims=True)
        cen = acc - mu
        var = jnp.mean(cen * cen, axis=-1, keepdims=True)
        y = cen * lax.rsqrt(var + EPS) * lg + lb
        c_ref[0, r0:r0 + ln_rows, :] = (y * (1.0 / (1.0 + jnp.exp(-y)))).astype(c_ref.dtype)


def _conv(u3, conv_w, conv_b, ln_g, ln_b, ts=256, rows=128, ln_rows=32):
    B, S, C = u3.shape
    hb = ts // CONV_HALO
    cur = lambda b, i: (b, i, 0)
    prev = lambda b, i: (b, jnp.maximum(i * hb - 1, 0), 0)
    nxt = lambda b, i: (b, jnp.minimum((i + 1) * hb, S // CONV_HALO - 1), 0)
    const = lambda b, i: (0, 0)
    const3 = lambda b, i: (0, 0, 0)
    n_slab = C // LANES
    w_slab = jnp.transpose(conv_w.reshape(CONV_WIDTH, n_slab, LANES), (1, 0, 2))
    b_slab = conv_b.reshape(n_slab, 1, LANES)
    return pl.pallas_call(
        functools.partial(_conv_kernel, ts=ts, rows=rows, ln_rows=ln_rows),
        grid=(B, S // ts),
        in_specs=[pl.BlockSpec((1, ts, C), cur),
                  pl.BlockSpec((1, CONV_HALO, C), prev),
                  pl.BlockSpec((1, CONV_HALO, C), nxt),
                  pl.BlockSpec((n_slab, CONV_WIDTH, LANES), const3),
                  pl.BlockSpec((n_slab, 1, LANES), const3),
                  pl.BlockSpec((1, C), const),
                  pl.BlockSpec((1, C), const)],
        out_specs=pl.BlockSpec((1, ts, C), cur),
        out_shape=jax.ShapeDtypeStruct((B, S, C), BF16),
        scratch_shapes=[pltpu.VMEM((C // LANES, ts + 2 * CONV_HALO, LANES), F32),
                        pltpu.VMEM((C // LANES, ts, LANES), F32)],
        compiler_params=pltpu.CompilerParams(dimension_semantics=("arbitrary", "arbitrary"),
                                             vmem_limit_bytes=VMEM_LIMIT),
        name="conv",
    )(u3, u3, u3, w_slab, b_slab, ln_g, ln_b)


def _mem_kv_kernel(mem_ref, g_ref, wk_ref, wv_ref, k_ref, v_ref):
    mn = _rms(mem_ref[...], g_ref[...]).astype(BF16)
    k_ref[...] = _dot(mn, wk_ref[...].astype(BF16)).astype(BF16)
    v_ref[...] = _dot(mn, wv_ref[...].astype(BF16)).astype(BF16)


def _mem_kv(mem2, g, wk, wv):
    R = mem2.shape[0]
    full = lambda shape: pl.BlockSpec(shape, lambda i: (0, 0))
    return pl.pallas_call(
        _mem_kv_kernel,
        grid=(1,),
        in_specs=[full((R, D_MODEL)), full((1, D_MODEL)),
                  full((D_MODEL, D_MODEL)), full((D_MODEL, D_MODEL))],
        out_specs=[full((R, D_MODEL))] * 2,
        out_shape=[jax.ShapeDtypeStruct((R, D_MODEL), BF16)] * 2,
        compiler_params=pltpu.CompilerParams(dimension_semantics=("arbitrary",),
                                             vmem_limit_bytes=VMEM_LIMIT),
        name="mem_kv",
    )(mem2, g, wk, wv)


def _mix_xattn_kernel(x_ref, o1_ref, o2_ref, o3_ref, l1_ref, l2_ref, l3_ref, c_ref,
                      wof_ref, gx_ref, wqf_ref, xk_ref, xv_ref, wxof_ref, h_ref,
                      wo_ref, wq_ref, wxo_ref, *bufs, tm, sub):
    _cast_once([(wof_ref, wo_ref), (wqf_ref, wq_ref), (wxof_ref, wxo_ref)])
    n_slab = D_ATT // LANES
    n_pat = len(DILATED_PATTERNS)
    o_refs = (o1_ref, o2_ref, o3_ref)
    l_refs = (l1_ref, l2_ref, l3_ref)
    n_grp = tm // sub

    def mix(g):
        t0 = g * sub
        obuf, lbuf = bufs[2 * g], bufs[2 * g + 1]
        for p, (_, d) in enumerate(DILATED_PATTERNS):
            src = slice(t0 // d, (t0 + sub) // d)
            for r in range(d if d > 1 else 0):
                rows = pl.ds(r, sub // d, stride=d)
                for j in range(n_slab):
                    c0 = r * D_ATT + j * LANES
                    obuf[p, j, rows, :] = o_refs[p][src, c0:c0 + LANES].astype(F32)
                    lbuf[p, j, rows, :] = l_refs[p][src, c0:c0 + LANES]

        def slab(p, j, refs, buf):
            if DILATED_PATTERNS[p][1] == 1:
                return refs[p][t0:t0 + sub, j * LANES:(j + 1) * LANES]
            return buf[p, j]

        att = []
        for j in range(n_slab):
            ls = [slab(p, j, l_refs, lbuf) for p in range(n_pat)]
            os_ = [slab(p, j, o_refs, obuf).astype(F32) for p in range(n_pat)]
            m = jnp.maximum(jnp.maximum(ls[0], ls[1]), ls[2])
            es = [jnp.exp(l - m) for l in ls]
            inv = 1.0 / (es[0] + es[1] + es[2])
            att.append(((es[0] * inv) * os_[0] + (es[1] * inv) * os_[1]
                        + (es[2] * inv) * os_[2]).astype(BF16))
        return jnp.concatenate(att, axis=1)

    def project(g, att):
        tr = slice(g * sub, (g + 1) * sub)
        h1 = (x_ref[tr, :] + _dot(att, wo_ref[0:D_ATT, :])
              + _dot(c_ref[tr, :], wo_ref[D_ATT:, :]))
        xq = (_dot(_rms(h1, gx_ref[...]).astype(BF16), wq_ref[...])
              * (XATT_HEAD_DIM ** -0.5)).astype(BF16)
        return h1, xq

    def cross(g, h1, xq):
        heads = []
        for h in range(XATT_HEADS):
            sl = slice(h * XATT_HEAD_DIM, (h + 1) * XATT_HEAD_DIM)
            s = _dot_nt(xq[:, sl], xk_ref[0, :, sl])
            mx = jnp.max(s, axis=-1, keepdims=True)
            p = jnp.exp(s - mx)
            den = jnp.sum(p, axis=-1, keepdims=True)
            heads.append((_dot(p.astype(BF16), xv_ref[0, :, sl]) * (1.0 / den)).astype(BF16))
        xo = jnp.concatenate(heads, axis=1)
        h_ref[g * sub:(g + 1) * sub, :] = h1 + _dot(xo, wxo_ref[...])

    for g in range(n_grp):
        h1, xq = project(g, mix(g))
        cross(g, h1, xq)


def _mix_xattn(x2, os_, lses, c2, w_out, gx, w_xq, xk, xv, w_xo, seq, tm=512, sub=256):
    T = x2.shape[0]
    n_mem = xk.shape[1]
    per_b = seq // tm
    row = lambda i: (i, 0)
    const = lambda i: (0, 0)
    memb = lambda i: (i // per_b, 0, 0)
    sq = pl.BlockSpec((D_MODEL, D_MODEL), const, pipeline_mode=pl.Buffered(1))
    n_pat = len(DILATED_PATTERNS)
    return pl.pallas_call(
        functools.partial(_mix_xattn_kernel, tm=tm, sub=sub),
        grid=(T // tm,),
        in_specs=[pl.BlockSpec((tm, D_MODEL), row)]
                 + [pl.BlockSpec((tm // d, d * D_ATT), row) for _, d in DILATED_PATTERNS]
                 + [pl.BlockSpec((tm // d, d * D_ATT), row) for _, d in DILATED_PATTERNS]
                 + [pl.BlockSpec((tm, D_CONV), row),
                    sq, pl.BlockSpec((1, D_MODEL), const), sq,
                    pl.BlockSpec((1, n_mem, D_MODEL), memb),
                    pl.BlockSpec((1, n_mem, D_MODEL), memb),
                    sq],
        out_specs=pl.BlockSpec((tm, D_MODEL), row),
        out_shape=jax.ShapeDtypeStruct((T, D_MODEL), F32),
        scratch_shapes=[pltpu.VMEM((D_MODEL, D_MODEL), BF16)] * 3
                       + [pltpu.VMEM((n_pat, D_ATT // LANES, sub, LANES), F32)] * (2 * (tm // sub)),
        compiler_params=pltpu.CompilerParams(dimension_semantics=("arbitrary",),
                                             vmem_limit_bytes=VMEM_LIMIT),
        name="mix_xattn",
    )(x2, *os_, *lses, c2, w_out, gx, w_xq, xk, xv, w_xo)


def _mlp_kernel(h_ref, g_ref, wu_ref, wd_ref, gf_ref, out_ref, *, chunk, final_norm):
    h = h_ref[...]
    hn = _rms(h, g_ref[...]).astype(BF16)
    acc = h
    for j in range(D_FF // chunk):
        u = jnp.maximum(_dot(hn, wu_ref[:, j * chunk:(j + 1) * chunk]), 0.0)
        acc = acc + _dot((u * u).astype(BF16), wd_ref[j * chunk:(j + 1) * chunk, :])
    out_ref[...] = _rms(acc, gf_ref[...]) if final_norm else acc


def _mlp(h2, g, w_up, w_down, gf, final_norm, tm=512, chunk=1024):
    T = h2.shape[0]
    row = lambda i: (i, 0)
    const = lambda i: (0, 0)
    return pl.pallas_call(
        functools.partial(_mlp_kernel, chunk=chunk, final_norm=final_norm),
        grid=(T // tm,),
        in_specs=[pl.BlockSpec((tm, D_MODEL), row),
                  pl.BlockSpec((1, D_MODEL), const),
                  pl.BlockSpec((D_MODEL, D_FF), const),
                  pl.BlockSpec((D_FF, D_MODEL), const),
                  pl.BlockSpec((1, D_MODEL), const)],
        out_specs=pl.BlockSpec((tm, D_MODEL), row),
        out_shape=jax.ShapeDtypeStruct((T, D_MODEL), F32),
        compiler_params=pltpu.CompilerParams(dimension_semantics=("arbitrary",),
                                             vmem_limit_bytes=VMEM_LIMIT),
        name="mlp",
    )(h2, g, w_up, w_down, gf)


def _rotary_tables(seq):
    half = ROT_DIM // 2
    freqs = ROPE_THETA ** (-jnp.arange(0, ROT_DIM, 2, dtype=F32) / ROT_DIM)
    ang = jnp.arange(seq, dtype=F32)[:, None] * freqs[None, :]
    cos, sin = jnp.cos(ang), jnp.sin(ang)
    zeros = jnp.zeros((seq, HEAD_DIM - ROT_DIM), F32)
    ones = jnp.ones((seq, HEAD_DIM - ROT_DIM), F32)
    z8 = jnp.zeros((seq, half), F32)
    cos_h = jnp.concatenate([cos, cos, ones], axis=1)
    sa_h = jnp.concatenate([-sin, z8, zeros], axis=1)
    sb_h = jnp.concatenate([z8, sin, zeros], axis=1)
    rep = LANES // HEAD_DIM
    return tuple(jnp.tile(t, (1, rep)) for t in (cos_h, sa_h, sb_h))


def kernel(x, mem, norm_mix_g, w_in, conv_w, conv_b, conv_ln_g, conv_ln_b, w_out, norm_x_g,
           norm_mem_g, w_xq, w_xk, w_xv, w_xo, norm_mlp_g, w_up, w_down, norm_final_g):
    B, S, D = x.shape
    n_mem = mem.shape[1]
    depth = w_in.shape[0]
    T = B * S
    cos_t, sa_t, sb_t = _rotary_tables(S)
    bias = _attn_bias()
    hmask = _head_masks()
    row = lambda g: g.reshape(1, -1)

    h = x.reshape(T, D)
    for l in range(depth):
        q, k, v, u = _in_proj(h, row(norm_mix_g[l]), w_in[l], cos_t, sa_t, sb_t, S)
        os_, lses = [], []
        for p, (_, d) in enumerate(DILATED_PATTERNS):
            o, lse = _attn(q[p], k[p], v[p], bias, hmask, B, S, d)
            os_.append(o)
            lses.append(lse)
        c = _conv(u.reshape(B, S, D_CONV), conv_w[l], row(conv_b[l]), row(conv_ln_g[l]),
                  row(conv_ln_b[l])).reshape(T, D_CONV)
        xk, xv = _mem_kv(mem.reshape(B * n_mem, D), row(norm_mem_g[l]),
                         w_xk[l], w_xv[l])
        h = _mix_xattn(h, os_, lses, c, w_out[l], row(norm_x_g[l]), w_xq[l],
                       xk.reshape(B, n_mem, D), xv.reshape(B, n_mem, D), w_xo[l], S)
        h = _mlp(h, row(norm_mlp_g[l]), w_up[l].astype(BF16), w_down[l].astype(BF16),
                 row(norm_final_g), final_norm=(l == depth - 1))
    return h.reshape(B, S, D)
```

```python
import functools
import math

import numpy as np
import jax
import jax.numpy as jnp
from jax import lax
from jax.experimental import pallas as pl
from jax.experimental.pallas import tpu as pltpu

F32 = jnp.float32
BF16 = jnp.bfloat16

D_MODEL = 1024
ATT_HEADS = 8
HEAD_DIM = 64
D_ATT = ATT_HEADS * HEAD_DIM
D_CONV = D_MODEL - D_ATT
DILATED_PATTERNS = ((128, 1), (512, 4), (2048, 16))
ROPE_THETA = 500000.0
ROT_DIM = HEAD_DIM // 4
CONV_WIDTH = 31
CONV_PAD = (CONV_WIDTH - 1) // 2
XATT_HEADS = 4
XATT_HEAD_DIM = D_MODEL // XATT_HEADS
D_FF = 4 * D_MODEL
EPS = 1e-6
NEG_INF = -1e30
LN2 = math.log(2.0)
Q_SCALE = HEAD_DIM ** -0.5 / LN2

LANES = 128
SUBLANES = 8
ATT_HALF = 64
ATT_QB = 2 * ATT_HALF
ATT_WIN = ATT_QB + 2 * ATT_HALF
CONV_HALO = 16
VMEM_LIMIT = 56 * 1024 * 1024


def _dot(a, b):
    return jnp.dot(a, b, preferred_element_type=F32)


def _dot_nt(a, b):
    return lax.dot_general(a, b, (((1,), (1,)), ((), ())), preferred_element_type=F32)


def _dot_tn(a, b):
    return lax.dot_general(a, b, (((0,), (0,)), ((), ())), preferred_element_type=F32)


def _rms(x, g):
    var = jnp.mean(x * x, axis=-1, keepdims=True)
    return x * lax.rsqrt(var + EPS) * g


def _cast_once(pairs, chunk=512):
    @pl.when(pl.program_id(0) == 0)
    def _():
        for src, dst in pairs:
            for c in range(0, src.shape[1], chunk):
                dst[:, c:c + chunk] = src[:, c:c + chunk].astype(BF16)


def _in_proj_kernel(x_ref, g_ref, wf_ref, cos_ref, sa_ref, sb_ref, *refs, tm):
    n_pat = len(DILATED_PATTERNS)
    qkv_refs = [refs[a * n_pat:(a + 1) * n_pat] for a in range(3)]
    u_ref = refs[3 * n_pat]
    ybuf = refs[3 * n_pat + 1]
    w_ref = refs[3 * n_pat + 2]
    _cast_once([(wf_ref, w_ref)])
    xn = _rms(x_ref[...], g_ref[...]).astype(BF16)
    cos, sa, sb = cos_ref[...], sa_ref[...], sb_ref[...]
    n_slab = D_ATT // LANES

    def plain(y, j):
        return y[:, LANES * j:LANES * (j + 1)]

    def rot(y, j):
        yj = plain(y, j)
        return (yj * cos + pltpu.roll(yj, LANES - ROT_DIM // 2, 1) * sa
                + pltpu.roll(yj, ROT_DIM // 2, 1) * sb)

    def emit(y, outs, transform):
        for j in range(n_slab):
            ybuf[j] = transform(y, j)
        for (_, d), o_ref in zip(DILATED_PATTERNS, outs):
            for r in range(d):
                for j in range(n_slab):
                    if d == 1:
                        piece = ybuf[j]
                    else:
                        piece = ybuf[j, pl.ds(r, tm // d, stride=d), :]
                    c0 = r * D_ATT + j * LANES
                    o_ref[:, c0:c0 + LANES] = piece.astype(BF16)

    emit(_dot(xn, w_ref[:, 0:D_ATT]), qkv_refs[0], lambda y, j: rot(y, j) * Q_SCALE)
    emit(_dot(xn, w_ref[:, D_ATT:2 * D_ATT]), qkv_refs[1], rot)
    emit(_dot(xn, w_ref[:, 2 * D_ATT:3 * D_ATT]), qkv_refs[2], plain)
    c0 = 3 * D_ATT
    a = _dot(xn, w_ref[:, c0:c0 + D_CONV])
    gt = _dot(xn, w_ref[:, c0 + D_CONV:c0 + 2 * D_CONV])
    u_ref[...] = a * (1.0 / (1.0 + jnp.exp(-gt)))


def _in_proj(x2, g, w_in, cos_t, sa_t, sb_t, seq, tm=512):
    T = x2.shape[0]
    d_in = w_in.shape[1]
    n_s = seq // tm
    n_pat = len(DILATED_PATTERNS)
    row = lambda i: (i, 0)
    tab = lambda i: (i % n_s, 0)
    const = lambda i: (0, 0)
    qkv_specs, qkv_shapes = [], []
    for _ in range(3):
        for _, d in DILATED_PATTERNS:
            qkv_specs.append(pl.BlockSpec((tm // d, d * D_ATT), row))
            qkv_shapes.append(jax.ShapeDtypeStruct((T // d, d * D_ATT), BF16))
    outs = pl.pallas_call(
        functools.partial(_in_proj_kernel, tm=tm),
        grid=(T // tm,),
        in_specs=[pl.BlockSpec((tm, D_MODEL), row),
                  pl.BlockSpec((1, D_MODEL), const),
                  pl.BlockSpec((D_MODEL, d_in), const, pipeline_mode=pl.Buffered(1)),
                  pl.BlockSpec((tm, LANES), tab),
                  pl.BlockSpec((tm, LANES), tab),
                  pl.BlockSpec((tm, LANES), tab)],
        out_specs=qkv_specs + [pl.BlockSpec((tm, D_CONV), row)],
        out_shape=qkv_shapes + [jax.ShapeDtypeStruct((T, D_CONV), F32)],
        scratch_shapes=[pltpu.VMEM((D_ATT // LANES, tm, LANES), F32),
                        pltpu.VMEM((D_MODEL, d_in), BF16)],
        compiler_params=pltpu.CompilerParams(dimension_semantics=("arbitrary",),
                                             vmem_limit_bytes=VMEM_LIMIT),
        name="in_proj",
    )(x2, g, w_in, cos_t, sa_t, sb_t)
    return outs[0:n_pat], outs[n_pat:2 * n_pat], outs[2 * n_pat:3 * n_pat], outs[3 * n_pat]


def _attn_kernel(q_ref, kc_ref, kp_ref, kn_ref, vc_ref, vp_ref, vn_ref, bias_ref, hm_ref,
                 o_ref, lse_ref, kext, vext, *, lb, nblk_total):
    i = pl.program_id(2)
    kext[0:ATT_HALF] = kp_ref[0]
    kext[ATT_HALF:ATT_HALF + lb] = kc_ref[0]
    kext[ATT_HALF + lb:] = kn_ref[0]
    vext[0:ATT_HALF] = vp_ref[0]
    vext[ATT_HALF:ATT_HALF + lb] = vc_ref[0]
    vext[ATT_HALF + lb:] = vn_ref[0]

    mask_a = hm_ref[0:1, :]
    mask_b = hm_ref[1:2, :]
    nblk = lb // ATT_QB

    def body(n, carry):
        r0 = pl.multiple_of(n * ATT_QB, ATT_QB)
        gblk = i * nblk + n
        bidx = jnp.where(gblk == 0, 0, jnp.where(gblk == nblk_total - 1, 2, 1))
        bias = bias_ref[bidx]
        for hp in range(D_ATT // LANES):
            c0 = LANES * hp
            q2 = q_ref[0, pl.ds(r0, ATT_QB), c0:c0 + LANES]
            qs = jnp.concatenate([q2 * mask_a, q2 * mask_b], axis=0)
            parts = []
            for kh in range(ATT_WIN // ATT_QB):
                k0 = r0 + kh * ATT_QB
                kw = kext[pl.ds(k0, ATT_QB), c0:c0 + LANES]
                vw = vext[pl.ds(k0, ATT_QB), c0:c0 + LANES]
                s = _dot_nt(kw, qs) + bias[kh * ATT_QB:(kh + 1) * ATT_QB]
                m = jnp.max(s, axis=0, keepdims=True)
                p = jnp.exp2(s - m)
                l = jnp.sum(p, axis=0, keepdims=True)
                parts.append((m, l, _dot_tn(vw, p.astype(BF16))))
            (m1, l1, o1), (m2, l2, o2) = parts
            m = jnp.maximum(m1, m2)
            a1 = jnp.exp2(m1 - m)
            a2 = jnp.exp2(m2 - m)
            l = a1 * l1 + a2 * l2
            inv = 1.0 / l
            ot = o1 * (a1 * inv) + o2 * (a2 * inv)
            lse = jnp.broadcast_to((m + jnp.log2(l)) * LN2, ot.shape)

            def own_heads(t):
                return jnp.transpose(jnp.concatenate(
                    [t[0:HEAD_DIM, 0:ATT_QB], t[HEAD_DIM:, ATT_QB:]], axis=0))

            o_ref[0, pl.ds(r0, ATT_QB), c0:c0 + LANES] = own_heads(ot).astype(o_ref.dtype)
            lse_ref[0, pl.ds(r0, ATT_QB), c0:c0 + LANES] = own_heads(lse)
        return carry

    lax.fori_loop(0, nblk, body, 0, unroll=True)


def _attn_bias():
    i = np.arange(ATT_QB)[:, None]
    j = np.arange(ATT_WIN)[None, :]
    band = (j >= i) & (j <= i + 2 * ATT_HALF)
    first = band & (j >= ATT_HALF)
    last = band & (j < ATT_HALF + ATT_QB)
    tabs = np.stack([first, band, last]).astype(np.float32)
    tabs = np.concatenate([tabs.transpose(0, 2, 1)] * 2, axis=2)
    return jnp.asarray((1.0 - tabs) * NEG_INF, dtype=F32)


def _head_masks():
    lane = np.arange(LANES)
    m = np.zeros((16, LANES), np.float32)
    m[0] = lane < HEAD_DIM
    m[1] = lane >= HEAD_DIM
    return jnp.asarray(m, dtype=BF16)


def _attn(q, k, v, bias, hmask, batch, seq, d, lb=512):
    L = seq // d
    lb = min(lb, L)
    nblk_total = L // ATT_QB
    assert nblk_total >= 2 and L % lb == 0 and lb % ATT_QB == 0
    view = lambda t: t.reshape(batch, L, d * D_ATT)
    hb = lb // ATT_HALF
    cur = lambda b, r, i: (b, i, r)
    prev = lambda b, r, i: (b, jnp.maximum(i * hb - 1, 0), r)
    nxt = lambda b, r, i: (b, jnp.minimum((i + 1) * hb, L // ATT_HALF - 1), r)
    blk = pl.BlockSpec((1, lb, D_ATT), cur)
    halo_p = pl.BlockSpec((1, ATT_HALF, D_ATT), prev)
    halo_n = pl.BlockSpec((1, ATT_HALF, D_ATT), nxt)
    o, lse = pl.pallas_call(
        functools.partial(_attn_kernel, lb=lb, nblk_total=nblk_total),
        grid=(batch, d, L // lb),
        in_specs=[blk, blk, halo_p, halo_n, blk, halo_p, halo_n,
                  pl.BlockSpec((3, ATT_WIN, 2 * ATT_QB), lambda b, r, i: (0, 0, 0)),
                  pl.BlockSpec((16, LANES), lambda b, r, i: (0, 0))],
        out_specs=[blk, blk],
        out_shape=[jax.ShapeDtypeStruct((batch, L, d * D_ATT), BF16),
                   jax.ShapeDtypeStruct((batch, L, d * D_ATT), F32)],
        scratch_shapes=[pltpu.VMEM((lb + 2 * ATT_HALF, D_ATT), BF16),
                        pltpu.VMEM((lb + 2 * ATT_HALF, D_ATT), BF16)],
        compiler_params=pltpu.CompilerParams(
            dimension_semantics=("arbitrary", "arbitrary", "arbitrary"),
            vmem_limit_bytes=VMEM_LIMIT),
        name=f"attn_d{d}",
    )(view(q), view(k), view(k), view(k), view(v), view(v), view(v), bias, hmask)
    return o.reshape(batch * L, d * D_ATT), lse.reshape(batch * L, d * D_ATT)


def _conv_kernel(uc_ref, up_ref, un_ref, w_ref, b_ref, lg_ref, lb_ref, c_ref, ext, ybuf, *, ts,
                 rows, ln_rows):
    i = pl.program_id(1)
    n = pl.num_programs(1)
    n_slab = D_CONV // LANES
    for g in range(n_slab):
        ls = slice(g * LANES, (g + 1) * LANES)
        ext[g, 0:CONV_HALO] = jnp.where(i > 0, up_ref[0, :, ls], 0.0)
        ext[g, CONV_HALO:CONV_HALO + ts] = uc_ref[0, :, ls]
        ext[g, CONV_HALO + ts:] = jnp.where(i < n - 1, un_ref[0, :, ls], 0.0)
    lg = lg_ref[...]
    lb = lb_ref[...]
    off = CONV_HALO - CONV_PAD
    n_blk = ts // rows

    def taps(idx, carry):
        g = idx // n_blk
        base = pl.multiple_of((idx % n_blk) * rows, rows)
        acc = jnp.broadcast_to(b_ref[g], (rows, LANES))
        for j in range(SUBLANES):
            steps = [m for m in range((CONV_WIDTH + off) // SUBLANES + 1)
                     if 0 <= SUBLANES * m + j - off < CONV_WIDTH]
            win = ext[g, pl.ds(base + j, rows + SUBLANES * max(steps)), :]
            for m in steps:
                k = SUBLANES * m + j - off
                acc = acc + win[SUBLANES * m:SUBLANES * m + rows] * w_ref[g, k:k + 1, :]
        ybuf[g, pl.ds(base, rows), :] = acc
        return carry

    lax.fori_loop(0, n_slab * n_blk, taps, 0)
    for r0 in range(0, ts, ln_rows):
        acc = jnp.concatenate([ybuf[g, r0:r0 + ln_rows, :] for g in range(n_slab)], axis=1)
        mu = jnp.mean(acc, axis=-1, keepdims=True)
        cen = acc - mu
        var = jnp.mean(cen * cen, axis=-1, keepdims=True)
        y = cen * lax.rsqrt(var + EPS) * lg + lb
        c_ref[0, r0:r0 + ln_rows, :] = (y * (1.0 / (1.0 + jnp.exp(-y)))).astype(c_ref.dtype)


def _conv(u3, conv_w, conv_b, ln_g, ln_b, ts=256, rows=128, ln_rows=32):
    B, S, C = u3.shape
    hb = ts // CONV_HALO
    cur = lambda b, i: (b, i, 0)
    prev = lambda b, i: (b, jnp.maximum(i * hb - 1, 0), 0)
    nxt = lambda b, i: (b, jnp.minimum((i + 1) * hb, S // CONV_HALO - 1), 0)
    const = lambda b, i: (0, 0)
    const3 = lambda b, i: (0, 0, 0)
    n_slab = C // LANES
    w_slab = jnp.transpose(conv_w.reshape(CONV_WIDTH, n_slab, LANES), (1, 0, 2))
    b_slab = conv_b.reshape(n_slab, 1, LANES)
    return pl.pallas_call(
        functools.partial(_conv_kernel, ts=ts, rows=rows, ln_rows=ln_rows),
        grid=(B, S // ts),
        in_specs=[pl.BlockSpec((1, ts, C), cur),
                  pl.BlockSpec((1, CONV_HALO, C), prev),
                  pl.BlockSpec((1, CONV_HALO, C), nxt),
                  pl.BlockSpec((n_slab, CONV_WIDTH, LANES), const3),
                  pl.BlockSpec((n_slab, 1, LANES), const3),
                  pl.BlockSpec((1, C), const),
                  pl.BlockSpec((1, C), const)],
        out_specs=pl.BlockSpec((1, ts, C), cur),
        out_shape=jax.ShapeDtypeStruct((B, S, C), BF16),
        scratch_shapes=[pltpu.VMEM((C // LANES, ts + 2 * CONV_HALO, LANES), F32),
                        pltpu.VMEM((C // LANES, ts, LANES), F32)],
        compiler_params=pltpu.CompilerParams(dimension_semantics=("arbitrary", "arbitrary"),
                                             vmem_limit_bytes=VMEM_LIMIT),
        name="conv",
    )(u3, u3, u3, w_slab, b_slab, ln_g, ln_b)


def _mem_kv_kernel(mem_ref, g_ref, wk_ref, wv_ref, k_ref, v_ref):
    mn = _rms(mem_ref[...], g_ref[...]).astype(BF16)
    k_ref[...] = _dot(mn, wk_ref[...].astype(BF16)).astype(BF16)
    v_ref[...] = _dot(mn, wv_ref[...].astype(BF16)).astype(BF16)


def _mem_kv(mem2, g, wk, wv):
    R = mem2.shape[0]
    full = lambda shape: pl.BlockSpec(shape, lambda i: (0, 0))
    return pl.pallas_call(
        _mem_kv_kernel,
        grid=(1,),
        in_specs=[full((R, D_MODEL)), full((1, D_MODEL)),
                  full((D_MODEL, D_MODEL)), full((D_MODEL, D_MODEL))],
        out_specs=[full((R, D_MODEL))] * 2,
        out_shape=[jax.ShapeDtypeStruct((R, D_MODEL), BF16)] * 2,
        compiler_params=pltpu.CompilerParams(dimension_semantics=("arbitrary",),
                                             vmem_limit_bytes=VMEM_LIMIT),
        name="mem_kv",
    )(mem2, g, wk, wv)


def _mix_xattn_kernel(x_ref, o1_ref, o2_ref, o3_ref, l1_ref, l2_ref, l3_ref, c_ref,
                      wof_ref, gx_ref, wqf_ref, xk_ref, xv_ref, wxof_ref, h_ref,
                      wo_ref, wq_ref, wxo_ref, *bufs, tm, sub):
    _cast_once([(wof_ref, wo_ref), (wqf_ref, wq_ref), (wxof_ref, wxo_ref)])
    n_slab = D_ATT // LANES
    n_pat = len(DILATED_PATTERNS)
    o_refs = (o1_ref, o2_ref, o3_ref)
    l_refs = (l1_ref, l2_ref, l3_ref)
    n_grp = tm // sub

    def mix(g):
        t0 = g * sub
        obuf, lbuf = bufs[2 * g], bufs[2 * g + 1]
        for p, (_, d) in enumerate(DILATED_PATTERNS):
            src = slice(t0 // d, (t0 + sub) // d)
            for r in range(d if d > 1 else 0):
                rows = pl.ds(r, sub // d, stride=d)
                for j in range(n_slab):
                    c0 = r * D_ATT + j * LANES
                    obuf[p, j, rows, :] = o_refs[p][src, c0:c0 + LANES].astype(F32)
                    lbuf[p, j, rows, :] = l_refs[p][src, c0:c0 + LANES]

        def slab(p, j, refs, buf):
            if DILATED_PATTERNS[p][1] == 1:
                return refs[p][t0:t0 + sub, j * LANES:(j + 1) * LANES]
            return buf[p, j]

        att = []
        for j in range(n_slab):
            ls = [slab(p, j, l_refs, lbuf) for p in range(n_pat)]
            os_ = [slab(p, j, o_refs, obuf).astype(F32) for p in range(n_pat)]
            m = jnp.maximum(jnp.maximum(ls[0], ls[1]), ls[2])
            es = [jnp.exp(l - m) for l in ls]
            inv = 1.0 / (es[0] + es[1] + es[2])
            att.append(((es[0] * inv) * os_[0] + (es[1] * inv) * os_[1]
                        + (es[2] * inv) * os_[2]).astype(BF16))
        return jnp.concatenate(att, axis=1)

    def project(g, att):
        tr = slice(g * sub, (g + 1) * sub)
        h1 = (x_ref[tr, :] + _dot(att, wo_ref[0:D_ATT, :])
              + _dot(c_ref[tr, :], wo_ref[D_ATT:, :]))
        xq = (_dot(_rms(h1, gx_ref[...]).astype(BF16), wq_ref[...])
              * (XATT_HEAD_DIM ** -0.5)).astype(BF16)
        return h1, xq

    def cross(g, h1, xq):
        heads = []
        for h in range(XATT_HEADS):
            sl = slice(h * XATT_HEAD_DIM, (h + 1) * XATT_HEAD_DIM)
            s = _dot_nt(xq[:, sl], xk_ref[0, :, sl])
            mx = jnp.max(s, axis=-1, keepdims=True)
            p = jnp.exp(s - mx)
            den = jnp.sum(p, axis=-1, keepdims=True)
            heads.append((_dot(p.astype(BF16), xv_ref[0, :, sl]) * (1.0 / den)).astype(BF16))
        xo = jnp.concatenate(heads, axis=1)
        h_ref[g * sub:(g + 1) * sub, :] = h1 + _dot(xo, wxo_ref[...])

    for g in range(n_grp):
        h1, xq = project(g, mix(g))
        cross(g, h1, xq)


def _mix_xattn(x2, os_, lses, c2, w_out, gx, w_xq, xk, xv, w_xo, seq, tm=512, sub=256):
    T = x2.shape[0]
    n_mem = xk.shape[1]
    per_b = seq // tm
    row = lambda i: (i, 0)
    const = lambda i: (0, 0)
    memb = lambda i: (i // per_b, 0, 0)
    sq = pl.BlockSpec((D_MODEL, D_MODEL), const, pipeline_mode=pl.Buffered(1))
    n_pat = len(DILATED_PATTERNS)
    return pl.pallas_call(
        functools.partial(_mix_xattn_kernel, tm=tm, sub=sub),
        grid=(T // tm,),
        in_specs=[pl.BlockSpec((tm, D_MODEL), row)]
                 + [pl.BlockSpec((tm // d, d * D_ATT), row) for _, d in DILATED_PATTERNS]
                 + [pl.BlockSpec((tm // d, d * D_ATT), row) for _, d in DILATED_PATTERNS]
                 + [pl.BlockSpec((tm, D_CONV), row),
                    sq, pl.BlockSpec((1, D_MODEL), const), sq,
                    pl.BlockSpec((1, n_mem, D_MODEL), memb),
                    pl.BlockSpec((1, n_mem, D_MODEL), memb),
                    sq],
        out_specs=pl.BlockSpec((tm, D_MODEL), row),
        out_shape=jax.ShapeDtypeStruct((T, D_MODEL), F32),
        scratch_shapes=[pltpu.VMEM((D_MODEL, D_MODEL), BF16)] * 3
                       + [pltpu.VMEM((n_pat, D_ATT // LANES, sub, LANES), F32)] * (2 * (tm // sub)),
        compiler_params=pltpu.CompilerParams(dimension_semantics=("arbitrary",),
                                             vmem_limit_bytes=VMEM_LIMIT),
        name="mix_xattn",
    )(x2, *os_, *lses, c2, w_out, gx, w_xq, xk, xv, w_xo)


def _mlp_kernel(h_ref, g_ref, wu_ref, wd_ref, gf_ref, out_ref, *, chunk, final_norm):
    h = h_ref[...]
    hn = _rms(h, g_ref[...]).astype(BF16)
    acc = h
    for j in range(D_FF // chunk):
        u = jnp.maximum(_dot(hn, wu_ref[:, j * chunk:(j + 1) * chunk]), 0.0)
        acc = acc + _dot((u * u).astype(BF16), wd_ref[j * chunk:(j + 1) * chunk, :])
    out_ref[...] = _rms(acc, gf_ref[...]) if final_norm else acc


def _mlp(h2, g, w_up, w_down, gf, final_norm, tm=512, chunk=1024):
    T = h2.shape[0]
    row = lambda i: (i, 0)
    const = lambda i: (0, 0)
    return pl.pallas_call(
        functools.partial(_mlp_kernel, chunk=chunk, final_norm=final_norm),
        grid=(T // tm,),
        in_specs=[pl.BlockSpec((tm, D_MODEL), row),
                  pl.BlockSpec((1, D_MODEL), const),
                  pl.BlockSpec((D_MODEL, D_FF), const),
                  pl.BlockSpec((D_FF, D_MODEL), const),
                  pl.BlockSpec((1, D_MODEL), const)],
        out_specs=pl.BlockSpec((tm, D_MODEL), row),
        out_shape=jax.ShapeDtypeStruct((T, D_MODEL), F32),
        compiler_params=pltpu.CompilerParams(dimension_semantics=("arbitrary",),
                                             vmem_limit_bytes=VMEM_LIMIT),
        name="mlp",
    )(h2, g, w_up, w_down, gf)


def _rotary_tables(seq):
    half = ROT_DIM // 2
    freqs = ROPE_THETA ** (-jnp.arange(0, ROT_DIM, 2, dtype=F32) / ROT_DIM)
    ang = jnp.arange(seq, dtype=F32)[:, None] * freqs[None, :]
    cos, sin = jnp.cos(ang), jnp.sin(ang)
    zeros = jnp.zeros((seq, HEAD_DIM - ROT_DIM), F32)
    ones = jnp.ones((seq, HEAD_DIM - ROT_DIM), F32)
    z8 = jnp.zeros((seq, half), F32)
    cos_h = jnp.concatenate([cos, cos, ones], axis=1)
    sa_h = jnp.concatenate([-sin, z8, zeros], axis=1)
    sb_h = jnp.concatenate([z8, sin, zeros], axis=1)
    rep = LANES // HEAD_DIM
    return tuple(jnp.tile(t, (1, rep)) for t in (cos_h, sa_h, sb_h))


def kernel(x, mem, norm_mix_g, w_in, conv_w, conv_b, conv_ln_g, conv_ln_b, w_out, norm_x_g,
           norm_mem_g, w_xq, w_xk, w_xv, w_xo, norm_mlp_g, w_up, w_down, norm_final_g):
    B, S, D = x.shape
    n_mem = mem.shape[1]
    depth = w_in.shape[0]
    T = B * S
    cos_t, sa_t, sb_t = _rotary_tables(S)
    bias = _attn_bias()
    hmask = _head_masks()
    row = lambda g: g.reshape(1, -1)

    h = x.reshape(T, D)
    for l in range(depth):
        q, k, v, u = _in_proj(h, row(norm_mix_g[l]), w_in[l], cos_t, sa_t, sb_t, S)
        os_, lses = [], []
        for p, (_, d) in enumerate(DILATED_PATTERNS):
            o, lse = _attn(q[p], k[p], v[p], bias, hmask, B, S, d)
            os_.append(o)
            lses.append(lse)
        c = _conv(u.reshape(B, S, D_CONV), conv_w[l], row(conv_b[l]), row(conv_ln_g[l]),
                  row(conv_ln_b[l])).reshape(T, D_CONV)
        xk, xv = _mem_kv(mem.reshape(B * n_mem, D), row(norm_mem_g[l]),
                         w_xk[l], w_xv[l])
        h = _mix_xattn(h, os_, lses, c, w_out[l], row(norm_x_g[l]), w_xq[l],
                       xk.reshape(B, n_mem, D), xv.reshape(B, n_mem, D), w_xo[l], S)
        h = _mlp(h, row(norm_mlp_g[l]), w_up[l].astype(BF16), w_down[l].astype(BF16),
                 row(norm_final_g), final_norm=(l == depth - 1))
    return h.reshape(B, S, D)
```

```python
import functools
import math

import numpy as np
import jax
import jax.numpy as jnp
from jax import lax
from jax.experimental import pallas as pl
from jax.experimental.pallas import tpu as pltpu

F32 = jnp.float32
BF16 = jnp.bfloat16

D_MODEL = 1024
ATT_HEADS = 8
HEAD_DIM = 64
D_ATT = ATT_HEADS * HEAD_DIM
D_CONV = D_MODEL - D_ATT
DILATED_PATTERNS = ((128, 1), (512, 4), (2048, 16))
DIL_STEP = 4
assert all(d == DIL_STEP ** k for k, (_, d) in enumerate(DILATED_PATTERNS))
ROPE_THETA = 500000.0
ROT_DIM = HEAD_DIM // 4
CONV_WIDTH = 31
CONV_PAD = (CONV_WIDTH - 1) // 2
XATT_HEADS = 4
XATT_HEAD_DIM = D_MODEL // XATT_HEADS
D_FF = 4 * D_MODEL
EPS = 1e-6
NEG_INF = -1e30
LN2 = math.log(2.0)
Q_SCALE = HEAD_DIM ** -0.5 / LN2

LANES = 128
SUBLANES = 8
ATT_HALF = 64
ATT_QB = 2 * ATT_HALF
ATT_WIN = ATT_QB + 2 * ATT_HALF
CONV_HALO = 16
VMEM_LIMIT = 56 * 1024 * 1024


def _dot(a, b):
    return jnp.dot(a, b, preferred_element_type=F32)


def _dot_nt(a, b):
    return lax.dot_general(a, b, (((1,), (1,)), ((), ())), preferred_element_type=F32)


def _dot_tn(a, b):
    return lax.dot_general(a, b, (((0,), (0,)), ((), ())), preferred_element_type=F32)


def _rms(x, g):
    var = jnp.mean(x * x, axis=-1, keepdims=True)
    return x * lax.rsqrt(var + EPS) * g


def _cast_once(pairs, chunk=512):
    @pl.when(pl.program_id(0) == 0)
    def _():
        for src, dst in pairs:
            for c in range(0, src.shape[1], chunk):
                dst[:, c:c + chunk] = src[:, c:c + chunk].astype(BF16)


def _in_proj_kernel(x_ref, g_ref, wf_ref, cos_ref, sa_ref, sb_ref, *refs, tm):
    n_pat = len(DILATED_PATTERNS)
    qkv_refs = [refs[a * n_pat:(a + 1) * n_pat] for a in range(3)]
    u_ref = refs[3 * n_pat]
    w_ref = refs[3 * n_pat + 1]
    stage = refs[3 * n_pat + 2:]
    _cast_once([(wf_ref, w_ref)])
    xn = _rms(x_ref[...], g_ref[...]).astype(BF16)
    cos, sa, sb = cos_ref[...], sa_ref[...], sb_ref[...]
    n_slab = D_ATT // LANES

    def plain(y, j):
        return y[:, LANES * j:LANES * (j + 1)]

    def rot(y, j):
        yj = plain(y, j)
        return (yj * cos + pltpu.roll(yj, LANES - ROT_DIM // 2, 1) * sa
                + pltpu.roll(yj, ROT_DIM // 2, 1) * sb)

    def emit(y, outs, transform):
        for j in range(n_slab):
            piece = transform(y, j)
            stage[0][0, j] = piece
            outs[0][:, j * LANES:(j + 1) * LANES] = piece.astype(BF16)
        for k in range(1, n_pat):
            d_prev, d = DILATED_PATTERNS[k - 1][1], DILATED_PATTERNS[k][1]
            for r_prev in range(d_prev):
                for t in range(DIL_STEP):
                    r = t * d_prev + r_prev
                    for j in range(n_slab):
                        piece = stage[k - 1][r_prev, j, pl.ds(t, tm // d, stride=DIL_STEP), :]
                        if k + 1 < n_pat:
                            stage[k][r, j] = piece
                        c0 = r * D_ATT + j * LANES
                        outs[k][:, c0:c0 + LANES] = piece.astype(BF16)

    emit(_dot(xn, w_ref[:, 0:D_ATT]), qkv_refs[0], lambda y, j: rot(y, j) * Q_SCALE)
    emit(_dot(xn, w_ref[:, D_ATT:2 * D_ATT]), qkv_refs[1], rot)
    emit(_dot(xn, w_ref[:, 2 * D_ATT:3 * D_ATT]), qkv_refs[2], plain)
    c0 = 3 * D_ATT
    a = _dot(xn, w_ref[:, c0:c0 + D_CONV])
    gt = _dot(xn, w_ref[:, c0 + D_CONV:c0 + 2 * D_CONV])
    u_ref[...] = a * (1.0 / (1.0 + jnp.exp(-gt)))


def _in_proj(x2, g, w_in, cos_t, sa_t, sb_t, seq, tm=512):
    T = x2.shape[0]
    d_in = w_in.shape[1]
    n_s = seq // tm
    n_pat = len(DILATED_PATTERNS)
    row = lambda i: (i, 0)
    tab = lambda i: (i % n_s, 0)
    const = lambda i: (0, 0)
    qkv_specs, qkv_shapes = [], []
    for _ in range(3):
        for _, d in DILATED_PATTERNS:
            qkv_specs.append(pl.BlockSpec((tm // d, d * D_ATT), row))
            qkv_shapes.append(jax.ShapeDtypeStruct((T // d, d * D_ATT), BF16))
    outs = pl.pallas_call(
        functools.partial(_in_proj_kernel, tm=tm),
        grid=(T // tm,),
        in_specs=[pl.BlockSpec((tm, D_MODEL), row),
                  pl.BlockSpec((1, D_MODEL), const),
                  pl.BlockSpec((D_MODEL, d_in), const, pipeline_mode=pl.Buffered(1)),
                  pl.BlockSpec((tm, LANES), tab),
                  pl.BlockSpec((tm, LANES), tab),
                  pl.BlockSpec((tm, LANES), tab)],
        out_specs=qkv_specs + [pl.BlockSpec((tm, D_CONV), row)],
        out_shape=qkv_shapes + [jax.ShapeDtypeStruct((T, D_CONV), F32)],
        scratch_shapes=[pltpu.VMEM((D_MODEL, d_in), BF16)]
                       + [pltpu.VMEM((d, D_ATT // LANES, tm // d, LANES), F32)
                          for _, d in DILATED_PATTERNS[:-1]],
        compiler_params=pltpu.CompilerParams(dimension_semantics=("arbitrary",),
                                             vmem_limit_bytes=VMEM_LIMIT),
        name="in_proj",
    )(x2, g, w_in, cos_t, sa_t, sb_t)
    return outs[0:n_pat], outs[n_pat:2 * n_pat], outs[2 * n_pat:3 * n_pat], outs[3 * n_pat]


def _attn_kernel(q_ref, kc_ref, kp_ref, kn_ref, vc_ref, vp_ref, vn_ref, bias_ref, hm_ref,
                 o_ref, lse_ref, kext, vext, *, lb, nblk_total):
    i = pl.program_id(2)
    kext[0:ATT_HALF] = kp_ref[0]
    kext[ATT_HALF:ATT_HALF + lb] = kc_ref[0]
    kext[ATT_HALF + lb:] = kn_ref[0]
    vext[0:ATT_HALF] = vp_ref[0]
    vext[ATT_HALF:ATT_HALF + lb] = vc_ref[0]
    vext[ATT_HALF + lb:] = vn_ref[0]

    mask_a = hm_ref[0:1, :]
    mask_b = hm_ref[1:2, :]
    nblk = lb // ATT_QB

    def body(n, carry):
        r0 = pl.multiple_of(n * ATT_QB, ATT_QB)
        gblk = i * nblk + n
        bidx = jnp.where(gblk == 0, 0, jnp.where(gblk == nblk_total - 1, 2, 1))
        bias = bias_ref[bidx]
        for hp in range(D_ATT // LANES):
            c0 = LANES * hp
            q2 = q_ref[0, pl.ds(r0, ATT_QB), c0:c0 + LANES]
            qs = jnp.concatenate([q2 * mask_a, q2 * mask_b], axis=0)
            parts = []
            for kh in range(ATT_WIN // ATT_QB):
                k0 = r0 + kh * ATT_QB
                kw = kext[pl.ds(k0, ATT_QB), c0:c0 + LANES]
                vw = vext[pl.ds(k0, ATT_QB), c0:c0 + LANES]
                s = _dot_nt(kw, qs) + bias[kh * ATT_QB:(kh + 1) * ATT_QB]
                m = jnp.max(s, axis=0, keepdims=True)
                p = jnp.exp2(s - m)
                l = jnp.sum(p, axis=0, keepdims=True)
                parts.append((m, l, _dot_tn(vw, p.astype(BF16))))
            (m1, l1, o1), (m2, l2, o2) = parts
            m = jnp.maximum(m1, m2)
            a1 = jnp.exp2(m1 - m)
            a2 = jnp.exp2(m2 - m)
            l = a1 * l1 + a2 * l2
            inv = 1.0 / l
            ot = o1 * (a1 * inv) + o2 * (a2 * inv)
            lse = jnp.broadcast_to((m + jnp.log2(l)) * LN2, ot.shape)

            def own_heads(t):
                return jnp.transpose(jnp.concatenate(
                    [t[0:HEAD_DIM, 0:ATT_QB], t[HEAD_DIM:, ATT_QB:]], axis=0))

            o_ref[0, pl.ds(r0, ATT_QB), c0:c0 + LANES] = own_heads(ot).astype(o_ref.dtype)
            lse_ref[0, pl.ds(r0, ATT_QB), c0:c0 + LANES] = own_heads(lse)
        return carry

    lax.fori_loop(0, nblk, body, 0, unroll=True)


def _attn_bias():
    i = np.arange(ATT_QB)[:, None]
    j = np.arange(ATT_WIN)[None, :]
    band = (j >= i) & (j <= i + 2 * ATT_HALF)
    first = band & (j >= ATT_HALF)
    last = band & (j < ATT_HALF + ATT_QB)
    tabs = np.stack([first, band, last]).astype(np.float32)
    tabs = np.concatenate([tabs.transpose(0, 2, 1)] * 2, axis=2)
    return jnp.asarray((1.0 - tabs) * NEG_INF, dtype=F32)


def _head_masks():
    lane = np.arange(LANES)
    m = np.zeros((16, LANES), np.float32)
    m[0] = lane < HEAD_DIM
    m[1] = lane >= HEAD_DIM
    return jnp.asarray(m, dtype=BF16)


def _attn(q, k, v, bias, hmask, batch, seq, d, lb=512):
    L = seq // d
    lb = min(lb, L)
    nblk_total = L // ATT_QB
    assert nblk_total >= 2 and L % lb == 0 and lb % ATT_QB == 0
    view = lambda t: t.reshape(batch, L, d * D_ATT)
    hb = lb // ATT_HALF
    cur = lambda b, r, i: (b, i, r)
    prev = lambda b, r, i: (b, jnp.maximum(i * hb - 1, 0), r)
    nxt = lambda b, r, i: (b, jnp.minimum((i + 1) * hb, L // ATT_HALF - 1), r)
    blk = pl.BlockSpec((1, lb, D_ATT), cur)
    halo_p = pl.BlockSpec((1, ATT_HALF, D_ATT), prev)
    halo_n = pl.BlockSpec((1, ATT_HALF, D_ATT), nxt)
    o, lse = pl.pallas_call(
        functools.partial(_attn_kernel, lb=lb, nblk_total=nblk_total),
        grid=(batch, d, L // lb),
        in_specs=[blk, blk, halo_p, halo_n, blk, halo_p, halo_n,
                  pl.BlockSpec((3, ATT_WIN, 2 * ATT_QB), lambda b, r, i: (0, 0, 0)),
                  pl.BlockSpec((16, LANES), lambda b, r, i: (0, 0))],
        out_specs=[blk, blk],
        out_shape=[jax.ShapeDtypeStruct((batch, L, d * D_ATT), BF16),
                   jax.ShapeDtypeStruct((batch, L, d * D_ATT), F32)],
        scratch_shapes=[pltpu.VMEM((lb + 2 * ATT_HALF, D_ATT), BF16),
                        pltpu.VMEM((lb + 2 * ATT_HALF, D_ATT), BF16)],
        compiler_params=pltpu.CompilerParams(
            dimension_semantics=("arbitrary", "arbitrary", "arbitrary"),
            vmem_limit_bytes=VMEM_LIMIT),
        name=f"attn_d{d}",
    )(view(q), view(k), view(k), view(k), view(v), view(v), view(v), bias, hmask)
    return o.reshape(batch * L, d * D_ATT), lse.reshape(batch * L, d * D_ATT)


def _conv_kernel(uc_ref, up_ref, un_ref, w_ref, b_ref, lg_ref, lb_ref, c_ref, ext, ybuf, *, ts,
                 rows, ln_rows):
    i = pl.program_id(1)
    n = pl.num_programs(1)
    n_slab = D_CONV // LANES
    for g in range(n_slab):
        ls = slice(g * LANES, (g + 1) * LANES)
        ext[g, 0:CONV_HALO] = jnp.where(i > 0, up_ref[0, :, ls], 0.0)
        ext[g, CONV_HALO:CONV_HALO + ts] = uc_ref[0, :, ls]
        ext[g, CONV_HALO + ts:] = jnp.where(i < n - 1, un_ref[0, :, ls], 0.0)
    lg = lg_ref[...]
    lb = lb_ref[...]
    off = CONV_HALO - CONV_PAD
    n_blk = ts // rows

    def taps(idx, carry):
        g = idx // n_blk
        base = pl.multiple_of((idx % n_blk) * rows, rows)
        acc = jnp.broadcast_to(b_ref[g], (rows, LANES))
        for j in range(SUBLANES):
            steps = [m for m in range((CONV_WIDTH + off) // SUBLANES + 1)
                     if 0 <= SUBLANES * m + j - off < CONV_WIDTH]
            win = ext[g, pl.ds(base + j, rows + SUBLANES * max(steps)), :]
            for m in steps:
                k = SUBLANES * m + j - off
                acc = acc + win[SUBLANES * m:SUBLANES * m + rows] * w_ref[g, k:k + 1, :]
        ybuf[g, pl.ds(base, rows), :] = acc
        return carry

    lax.fori_loop(0, n_slab * n_blk, taps, 0)
    for r0 in range(0, ts, ln_rows):
        acc = jnp.concatenate([ybuf[g, r0:r0 + ln_rows, :] for g in range(n_slab)], axis=1)
        mu = jnp.mean(acc, axis=-1, keepdims=True)
        cen = acc - mu
        var = jnp.mean(cen * cen, axis=-1, keepdims=True)
        y = cen * lax.rsqrt(var + EPS) * lg + lb
        c_ref[0, r0:r0 + ln_rows, :] = (y * (1.0 / (1.0 + jnp.exp(-y)))).astype(c_ref.dtype)


def _conv(u3, conv_w, conv_b, ln_g, ln_b, ts=512, rows=128, ln_rows=32):
    B, S, C = u3.shape
    hb = ts // CONV_HALO
    cur = lambda b, i: (b, i, 0)
    prev = lambda b, i: (b, jnp.maximum(i * hb - 1, 0), 0)
    nxt = lambda b, i: (b, jnp.minimum((i + 1) * hb, S // CONV_HALO - 1), 0)
    const = lambda b, i: (0, 0)
    const3 = lambda b, i: (0, 0, 0)
    n_slab = C // LANES
    w_slab = jnp.transpose(conv_w.reshape(CONV_WIDTH, n_slab, LANES), (1, 0, 2))
    b_slab = conv_b.reshape(n_slab, 1, LANES)
    return pl.pallas_call(
        functools.partial(_conv_kernel, ts=ts, rows=rows, ln_rows=ln_rows),
        grid=(B, S // ts),
        in_specs=[pl.BlockSpec((1, ts, C), cur),
                  pl.BlockSpec((1, CONV_HALO, C), prev),
                  pl.BlockSpec((1, CONV_HALO, C), nxt),
                  pl.BlockSpec((n_slab, CONV_WIDTH, LANES), const3),
                  pl.BlockSpec((n_slab, 1, LANES), const3),
                  pl.BlockSpec((1, C), const),
                  pl.BlockSpec((1, C), const)],
        out_specs=pl.BlockSpec((1, ts, C), cur),
        out_shape=jax.ShapeDtypeStruct((B, S, C), BF16),
        scratch_shapes=[pltpu.VMEM((C // LANES, ts + 2 * CONV_HALO, LANES), F32),
                        pltpu.VMEM((C // LANES, ts, LANES), F32)],
        compiler_params=pltpu.CompilerParams(dimension_semantics=("arbitrary", "arbitrary"),
                                             vmem_limit_bytes=VMEM_LIMIT),
        name="conv",
    )(u3, u3, u3, w_slab, b_slab, ln_g, ln_b)


def _mem_kv_kernel(mem_ref, g_ref, wk_ref, wv_ref, k_ref, v_ref):
    mn = _rms(mem_ref[...], g_ref[...]).astype(BF16)
    k_ref[...] = _dot(mn, wk_ref[...].astype(BF16)).astype(BF16)
    v_ref[...] = _dot(mn, wv_ref[...].astype(BF16)).astype(BF16)


def _mem_kv(mem2, g, wk, wv):
    R = mem2.shape[0]
    full = lambda shape: pl.BlockSpec(shape, lambda i: (0, 0))
    return pl.pallas_call(
        _mem_kv_kernel,
        grid=(1,),
        in_specs=[full((R, D_MODEL)), full((1, D_MODEL)),
                  full((D_MODEL, D_MODEL)), full((D_MODEL, D_MODEL))],
        out_specs=[full((R, D_MODEL))] * 2,
        out_shape=[jax.ShapeDtypeStruct((R, D_MODEL), BF16)] * 2,
        compiler_params=pltpu.CompilerParams(dimension_semantics=("arbitrary",),
                                             vmem_limit_bytes=VMEM_LIMIT),
        name="mem_kv",
    )(mem2, g, wk, wv)


def _mix_xattn_kernel(x_ref, o1_ref, o2_ref, o3_ref, l1_ref, l2_ref, l3_ref, c_ref,
                      wof_ref, gx_ref, wqf_ref, xk_ref, xv_ref, wxof_ref, wuf_ref, wdf_ref,
                      h_ref, wub_ref, wdb_ref, wo_ref, wq_ref, wxo_ref, *bufs, tm, sub):
    _cast_once([(wof_ref, wo_ref), (wqf_ref, wq_ref), (wxof_ref, wxo_ref)])
    wub_ref[...] = wuf_ref[...].astype(BF16)
    wdb_ref[...] = wdf_ref[...].astype(BF16)
    n_slab = D_ATT // LANES
    n_pat = len(DILATED_PATTERNS)
    o_refs = (o1_ref, o2_ref, o3_ref)
    l_refs = (l1_ref, l2_ref, l3_ref)
    n_grp = tm // sub

    def mix(g):
        t0 = g * sub
        obuf, lbuf = bufs[2 * g], bufs[2 * g + 1]
        otmp, ltmp = bufs[2 * n_grp + 2 * g], bufs[2 * n_grp + 2 * g + 1]
        for p, (_, d) in enumerate(DILATED_PATTERNS):
            if d == 1:
                continue
            src = slice(t0 // d, (t0 + sub) // d)
            for ref, dst, tmp, conv in ((o_refs[p], obuf, otmp, lambda t: t.astype(F32)),
                                        (l_refs[p], lbuf, ltmp, lambda t: t)):
                for j in range(n_slab):
                    pieces = {r: conv(ref[src, r * D_ATT + j * LANES:r * D_ATT + (j + 1) * LANES])
                              for r in range(d)}
                    dd = d
                    while dd > DIL_STEP:
                        lower = dd // DIL_STEP
                        merged = {}
                        for r_low in range(lower):
                            for t in range(DIL_STEP):
                                tmp[r_low, j, pl.ds(t, sub // dd, stride=DIL_STEP), :] = (
                                    pieces[t * lower + r_low])
                            merged[r_low] = tmp[r_low, j, 0:sub // lower, :]
                        pieces, dd = merged, lower
                    for r in range(dd):
                        dst[p, j, pl.ds(r, sub // dd, stride=DIL_STEP), :] = pieces[r]

        def slab(p, j, refs, buf):
            if DILATED_PATTERNS[p][1] == 1:
                return refs[p][t0:t0 + sub, j * LANES:(j + 1) * LANES]
            return buf[p, j]

        att = []
        for j in range(n_slab):
            ls = [slab(p, j, l_refs, lbuf) for p in range(n_pat)]
            os_ = [slab(p, j, o_refs, obuf).astype(F32) for p in range(n_pat)]
            m = jnp.maximum(jnp.maximum(ls[0], ls[1]), ls[2])
            es = [jnp.exp(l - m) for l in ls]
            inv = 1.0 / (es[0] + es[1] + es[2])
            att.append(((es[0] * inv) * os_[0] + (es[1] * inv) * os_[1]
                        + (es[2] * inv) * os_[2]).astype(BF16))
        return jnp.concatenate(att, axis=1)

    def project(g, att):
        tr = slice(g * sub, (g + 1) * sub)
        h1 = (x_ref[tr, :] + _dot(att, wo_ref[0:D_ATT, :])
              + _dot(c_ref[tr, :], wo_ref[D_ATT:, :]))
        xq = (_dot(_rms(h1, gx_ref[...]).astype(BF16), wq_ref[...])
              * (XATT_HEAD_DIM ** -0.5)).astype(BF16)
        return h1, xq

    def cross(g, h1, xq):
        heads = []
        for h in range(XATT_HEADS):
            sl = slice(h * XATT_HEAD_DIM, (h + 1) * XATT_HEAD_DIM)
            s = _dot_nt(xq[:, sl], xk_ref[0, :, sl])
            mx = jnp.max(s, axis=-1, keepdims=True)
            p = jnp.exp(s - mx)
            den = jnp.sum(p, axis=-1, keepdims=True)
            heads.append((_dot(p.astype(BF16), xv_ref[0, :, sl]) * (1.0 / den)).astype(BF16))
        xo = jnp.concatenate(heads, axis=1)
        h_ref[g * sub:(g + 1) * sub, :] = h1 + _dot(xo, wxo_ref[...])

    for g in range(n_grp):
        h1, xq = project(g, mix(g))
        cross(g, h1, xq)


def _mix_xattn(x2, os_, lses, c2, w_out, gx, w_xq, xk, xv, w_xo, w_up, w_down, seq, tm=512,
               sub=256):
    T = x2.shape[0]
    n_mem = xk.shape[1]
    per_b = seq // tm
    n_steps = T // tm
    up_rows, dn_rows = w_up.shape[0] // n_steps, w_down.shape[0] // n_steps
    n_slab = D_ATT // LANES
    d_max = DILATED_PATTERNS[-1][1]
    row = lambda i: (i, 0)
    const = lambda i: (0, 0)
    memb = lambda i: (i // per_b, 0, 0)
    sq = pl.BlockSpec((D_MODEL, D_MODEL), const, pipeline_mode=pl.Buffered(1))
    n_pat = len(DILATED_PATTERNS)
    return pl.pallas_call(
        functools.partial(_mix_xattn_kernel, tm=tm, sub=sub),
        grid=(T // tm,),
        in_specs=[pl.BlockSpec((tm, D_MODEL), row)]
                 + [pl.BlockSpec((tm // d, d * D_ATT), row) for _, d in DILATED_PATTERNS]
                 + [pl.BlockSpec((tm // d, d * D_ATT), row) for _, d in DILATED_PATTERNS]
                 + [pl.BlockSpec((tm, D_CONV), row),
                    sq, pl.BlockSpec((1, D_MODEL), const), sq,
                    pl.BlockSpec((1, n_mem, D_MODEL), memb),
                    pl.BlockSpec((1, n_mem, D_MODEL), memb),
                    sq,
                    pl.BlockSpec((up_rows, w_up.shape[1]), row),
                    pl.BlockSpec((dn_rows, w_down.shape[1]), row)],
        out_specs=[pl.BlockSpec((tm, D_MODEL), row),
                   pl.BlockSpec((up_rows, w_up.shape[1]), row),
                   pl.BlockSpec((dn_rows, w_down.shape[1]), row)],
        out_shape=[jax.ShapeDtypeStruct((T, D_MODEL), F32),
                   jax.ShapeDtypeStruct(w_up.shape, BF16),
                   jax.ShapeDtypeStruct(w_down.shape, BF16)],
        scratch_shapes=[pltpu.VMEM((D_MODEL, D_MODEL), BF16)] * 3
                       + [pltpu.VMEM((n_pat, n_slab, sub, LANES), F32)] * 2 * (tm // sub)
                       + [pltpu.VMEM((d_max // DIL_STEP, n_slab, sub // DIL_STEP, LANES), F32)
                          ] * 2 * (tm // sub),
        compiler_params=pltpu.CompilerParams(dimension_semantics=("arbitrary",),
                                             vmem_limit_bytes=VMEM_LIMIT),
        name="mix_xattn",
    )(x2, *os_, *lses, c2, w_out, gx, w_xq, xk, xv, w_xo, w_up, w_down)


def _mlp_kernel(h_ref, g_ref, wu_ref, wd_ref, gf_ref, out_ref, *, chunk, final_norm):
    h = h_ref[...]
    hn = _rms(h, g_ref[...]).astype(BF16)
    acc = h
    for j in range(D_FF // chunk):
        u = jnp.maximum(_dot(hn, wu_ref[:, j * chunk:(j + 1) * chunk]), 0.0)
        acc = acc + _dot((u * u).astype(BF16), wd_ref[j * chunk:(j + 1) * chunk, :])
    out_ref[...] = _rms(acc, gf_ref[...]) if final_norm else acc


def _mlp(h2, g, w_up, w_down, gf, final_norm, tm=512, chunk=1024):
    T = h2.shape[0]
    row = lambda i: (i, 0)
    const = lambda i: (0, 0)
    return pl.pallas_call(
        functools.partial(_mlp_kernel, chunk=chunk, final_norm=final_norm),
        grid=(T // tm,),
        in_specs=[pl.BlockSpec((tm, D_MODEL), row),
                  pl.BlockSpec((1, D_MODEL), const),
                  pl.BlockSpec((D_MODEL, D_FF), const),
                  pl.BlockSpec((D_FF, D_MODEL), const),
                  pl.BlockSpec((1, D_MODEL), const)],
        out_specs=pl.BlockSpec((tm, D_MODEL), row),
        out_shape=jax.ShapeDtypeStruct((T, D_MODEL), F32),
        compiler_params=pltpu.CompilerParams(dimension_semantics=("arbitrary",),
                                             vmem_limit_bytes=VMEM_LIMIT),
        name="mlp",
    )(h2, g, w_up, w_down, gf)


def _rotary_tables(seq):
    half = ROT_DIM // 2
    freqs = ROPE_THETA ** (-jnp.arange(0, ROT_DIM, 2, dtype=F32) / ROT_DIM)
    ang = jnp.arange(seq, dtype=F32)[:, None] * freqs[None, :]
    cos, sin = jnp.cos(ang), jnp.sin(ang)
    zeros = jnp.zeros((seq, HEAD_DIM - ROT_DIM), F32)
    ones = jnp.ones((seq, HEAD_DIM - ROT_DIM), F32)
    z8 = jnp.zeros((seq, half), F32)
    cos_h = jnp.concatenate([cos, cos, ones], axis=1)
    sa_h = jnp.concatenate([-sin, z8, zeros], axis=1)
    sb_h = jnp.concatenate([z8, sin, zeros], axis=1)
    rep = LANES // HEAD_DIM
    return tuple(jnp.tile(t, (1, rep)) for t in (cos_h, sa_h, sb_h))


def kernel(x, mem, norm_mix_g, w_in, conv_w, conv_b, conv_ln_g, conv_ln_b, w_out, norm_x_g,
           norm_mem_g, w_xq, w_xk, w_xv, w_xo, norm_mlp_g, w_up, w_down, norm_final_g):
    B, S, D = x.shape
    n_mem = mem.shape[1]
    depth = w_in.shape[0]
    T = B * S
    cos_t, sa_t, sb_t = _rotary_tables(S)
    bias = _attn_bias()
    hmask = _head_masks()
    row = lambda g: g.reshape(1, -1)

    h = x.reshape(T, D)
    for l in range(depth):
        q, k, v, u = _in_proj(h, row(norm_mix_g[l]), w_in[l], cos_t, sa_t, sb_t, S)
        os_, lses = [], []
        for p, (_, d) in enumerate(DILATED_PATTERNS):
            o, lse = _attn(q[p], k[p], v[p], bias, hmask, B, S, d)
            os_.append(o)
            lses.append(lse)
        c = _conv(u.reshape(B, S, D_CONV), conv_w[l], row(conv_b[l]), row(conv_ln_g[l]),
                  row(conv_ln_b[l])).reshape(T, D_CONV)
        xk, xv = _mem_kv(mem.reshape(B * n_mem, D), row(norm_mem_g[l]),
                         w_xk[l], w_xv[l])
        h, w_up_b, w_down_b = _mix_xattn(h, os_, lses, c, w_out[l], row(norm_x_g[l]), w_xq[l],
                                         xk.reshape(B, n_mem, D), xv.reshape(B, n_mem, D),
                                         w_xo[l], w_up[l], w_down[l], S)
        h = _mlp(h, row(norm_mlp_g[l]), w_up_b, w_down_b, row(norm_final_g),
                 final_norm=(l == depth - 1))
    return h.reshape(B, S, D)
```

```python
import functools
import math

import numpy as np
import jax
import jax.numpy as jnp
from jax import lax
from jax.experimental import pallas as pl
from jax.experimental.pallas import tpu as pltpu

F32 = jnp.float32
BF16 = jnp.bfloat16

D_MODEL = 1024
ATT_HEADS = 8
HEAD_DIM = 64
D_ATT = ATT_HEADS * HEAD_DIM
D_CONV = D_MODEL - D_ATT
DILATED_PATTERNS = ((128, 1), (512, 4), (2048, 16))
DIL_STEP = 4
assert all(d == DIL_STEP ** k for k, (_, d) in enumerate(DILATED_PATTERNS))
ROPE_THETA = 500000.0
ROT_DIM = HEAD_DIM // 4
CONV_WIDTH = 31
CONV_PAD = (CONV_WIDTH - 1) // 2
XATT_HEADS = 4
XATT_HEAD_DIM = D_MODEL // XATT_HEADS
D_FF = 4 * D_MODEL
EPS = 1e-6
NEG_INF = -1e30
LN2 = math.log(2.0)
Q_SCALE = HEAD_DIM ** -0.5 / LN2

LANES = 128
SUBLANES = 8
ATT_HALF = 64
ATT_QB = 2 * ATT_HALF
ATT_WIN = ATT_QB + 2 * ATT_HALF
CONV_HALO = 16
VMEM_LIMIT = 56 * 1024 * 1024


def _dot(a, b):
    return jnp.dot(a, b, preferred_element_type=F32)


def _dot_nt(a, b):
    return lax.dot_general(a, b, (((1,), (1,)), ((), ())), preferred_element_type=F32)


def _dot_tn(a, b):
    return lax.dot_general(a, b, (((0,), (0,)), ((), ())), preferred_element_type=F32)


def _rms(x, g):
    var = jnp.mean(x * x, axis=-1, keepdims=True)
    return x * lax.rsqrt(var + EPS) * g


def _cast_once(pairs, chunk=512):
    @pl.when(pl.program_id(0) == 0)
    def _():
        for src, dst in pairs:
            for c in range(0, src.shape[1], chunk):
                dst[:, c:c + chunk] = src[:, c:c + chunk].astype(BF16)


def _in_proj_kernel(x_ref, g_ref, wf_ref, cos_ref, sa_ref, sb_ref, *refs, tm):
    n_pat = len(DILATED_PATTERNS)
    qkv_refs = [refs[a * n_pat:(a + 1) * n_pat] for a in range(3)]
    u_ref = refs[3 * n_pat]
    w_ref = refs[3 * n_pat + 1]
    stage = refs[3 * n_pat + 2:]
    _cast_once([(wf_ref, w_ref)])
    xn = _rms(x_ref[...], g_ref[...]).astype(BF16)
    cos, sa, sb = cos_ref[...], sa_ref[...], sb_ref[...]
    n_slab = D_ATT // LANES

    def plain(y, j):
        return y[:, LANES * j:LANES * (j + 1)]

    def rot(y, j):
        yj = plain(y, j)
        return (yj * cos + pltpu.roll(yj, LANES - ROT_DIM // 2, 1) * sa
                + pltpu.roll(yj, ROT_DIM // 2, 1) * sb)

    def emit(y, outs, transform):
        for j in range(n_slab):
            piece = transform(y, j)
            stage[0][0, j] = piece
            outs[0][:, j * LANES:(j + 1) * LANES] = piece.astype(BF16)
        for k in range(1, n_pat):
            d_prev, d = DILATED_PATTERNS[k - 1][1], DILATED_PATTERNS[k][1]
            for r_prev in range(d_prev):
                for t in range(DIL_STEP):
                    r = t * d_prev + r_prev
                    for j in range(n_slab):
                        piece = stage[k - 1][r_prev, j, pl.ds(t, tm // d, stride=DIL_STEP), :]
                        if k + 1 < n_pat:
                            stage[k][r, j] = piece
                        c0 = r * D_ATT + j * LANES
                        outs[k][:, c0:c0 + LANES] = piece.astype(BF16)

    emit(_dot(xn, w_ref[:, 0:D_ATT]), qkv_refs[0], lambda y, j: rot(y, j) * Q_SCALE)
    emit(_dot(xn, w_ref[:, D_ATT:2 * D_ATT]), qkv_refs[1], rot)
    emit(_dot(xn, w_ref[:, 2 * D_ATT:3 * D_ATT]), qkv_refs[2], plain)
    c0 = 3 * D_ATT
    a = _dot(xn, w_ref[:, c0:c0 + D_CONV])
    gt = _dot(xn, w_ref[:, c0 + D_CONV:c0 + 2 * D_CONV])
    u_ref[...] = a * (1.0 / (1.0 + jnp.exp(-gt)))


def _in_proj(x2, g, w_in, cos_t, sa_t, sb_t, seq, tm=512):
    T = x2.shape[0]
    d_in = w_in.shape[1]
    n_s = seq // tm
    n_pat = len(DILATED_PATTERNS)
    row = lambda i: (i, 0)
    tab = lambda i: (i % n_s, 0)
    const = lambda i: (0, 0)
    qkv_specs, qkv_shapes = [], []
    for _ in range(3):
        for _, d in DILATED_PATTERNS:
            qkv_specs.append(pl.BlockSpec((tm // d, d * D_ATT), row))
            qkv_shapes.append(jax.ShapeDtypeStruct((T // d, d * D_ATT), BF16))
    outs = pl.pallas_call(
        functools.partial(_in_proj_kernel, tm=tm),
        grid=(T // tm,),
        in_specs=[pl.BlockSpec((tm, D_MODEL), row),
                  pl.BlockSpec((1, D_MODEL), const),
                  pl.BlockSpec((D_MODEL, d_in), const, pipeline_mode=pl.Buffered(1)),
                  pl.BlockSpec((tm, LANES), tab),
                  pl.BlockSpec((tm, LANES), tab),
                  pl.BlockSpec((tm, LANES), tab)],
        out_specs=qkv_specs + [pl.BlockSpec((tm, D_CONV), row)],
        out_shape=qkv_shapes + [jax.ShapeDtypeStruct((T, D_CONV), F32)],
        scratch_shapes=[pltpu.VMEM((D_MODEL, d_in), BF16)]
                       + [pltpu.VMEM((d, D_ATT // LANES, tm // d, LANES), F32)
                          for _, d in DILATED_PATTERNS[:-1]],
        compiler_params=pltpu.CompilerParams(dimension_semantics=("arbitrary",),
                                             vmem_limit_bytes=VMEM_LIMIT),
        name="in_proj",
    )(x2, g, w_in, cos_t, sa_t, sb_t)
    return outs[0:n_pat], outs[n_pat:2 * n_pat], outs[2 * n_pat:3 * n_pat], outs[3 * n_pat]


def _attn_kernel(q_ref, kc_ref, kp_ref, kn_ref, vc_ref, vp_ref, vn_ref, bias_ref, hm_ref,
                 o_ref, lse_ref, kext, vext, *, lb, nblk_total):
    i = pl.program_id(2)
    kext[0:ATT_HALF] = kp_ref[0]
    kext[ATT_HALF:ATT_HALF + lb] = kc_ref[0]
    kext[ATT_HALF + lb:] = kn_ref[0]
    vext[0:ATT_HALF] = vp_ref[0]
    vext[ATT_HALF:ATT_HALF + lb] = vc_ref[0]
    vext[ATT_HALF + lb:] = vn_ref[0]

    mask_a = hm_ref[0:1, :]
    mask_b = hm_ref[1:2, :]
    nblk = lb // ATT_QB

    def body(n, carry):
        r0 = pl.multiple_of(n * ATT_QB, ATT_QB)
        gblk = i * nblk + n
        bidx = jnp.where(gblk == 0, 0, jnp.where(gblk == nblk_total - 1, 2, 1))
        bias = bias_ref[bidx]
        for hp in range(q_ref.shape[2] // LANES):
            c0 = LANES * hp
            q2 = q_ref[0, pl.ds(r0, ATT_QB), c0:c0 + LANES]
            qs = jnp.concatenate([q2 * mask_a, q2 * mask_b], axis=0)
            parts = []
            for kh in range(ATT_WIN // ATT_QB):
                k0 = r0 + kh * ATT_QB
                kw = kext[pl.ds(k0, ATT_QB), c0:c0 + LANES]
                vw = vext[pl.ds(k0, ATT_QB), c0:c0 + LANES]
                s = _dot_nt(kw, qs) + bias[kh * ATT_QB:(kh + 1) * ATT_QB]
                m = jnp.max(s, axis=0, keepdims=True)
                p = jnp.exp2(s - m)
                l = jnp.sum(p, axis=0, keepdims=True)
                parts.append((m, l, _dot_tn(vw, p.astype(BF16))))
            (m1, l1, o1), (m2, l2, o2) = parts
            m = jnp.maximum(m1, m2)
            a1 = jnp.exp2(m1 - m)
            a2 = jnp.exp2(m2 - m)
            l = a1 * l1 + a2 * l2
            inv = 1.0 / l
            ot = o1 * (a1 * inv) + o2 * (a2 * inv)
            lse = jnp.broadcast_to((m + jnp.log2(l)) * LN2, ot.shape)

            def own_heads(t):
                return jnp.transpose(jnp.concatenate(
                    [t[0:HEAD_DIM, 0:ATT_QB], t[HEAD_DIM:, ATT_QB:]], axis=0))

            o_ref[0, pl.ds(r0, ATT_QB), c0:c0 + LANES] = own_heads(ot).astype(o_ref.dtype)
            lse_ref[0, pl.ds(r0, ATT_QB), c0:c0 + LANES] = own_heads(lse)
        return carry

    lax.fori_loop(0, nblk, body, 0, unroll=True)


def _attn_bias():
    i = np.arange(ATT_QB)[:, None]
    j = np.arange(ATT_WIN)[None, :]
    band = (j >= i) & (j <= i + 2 * ATT_HALF)
    first = band & (j >= ATT_HALF)
    last = band & (j < ATT_HALF + ATT_QB)
    tabs = np.stack([first, band, last]).astype(np.float32)
    tabs = np.concatenate([tabs.transpose(0, 2, 1)] * 2, axis=2)
    return jnp.asarray((1.0 - tabs) * NEG_INF, dtype=F32)


def _head_masks():
    lane = np.arange(LANES)
    m = np.zeros((16, LANES), np.float32)
    m[0] = lane < HEAD_DIM
    m[1] = lane >= HEAD_DIM
    return jnp.asarray(m, dtype=BF16)


def _attn(q, k, v, bias, hmask, batch, seq, d, blocks_per_step=16, lb_max=2048):
    L = seq // d
    lb = min(lb_max, L)
    rpb = min(d, max(1, blocks_per_step * ATT_QB // lb))
    width = rpb * D_ATT
    nblk_total = L // ATT_QB
    assert nblk_total >= 2 and L % lb == 0 and lb % ATT_QB == 0 and d % rpb == 0
    view = lambda t: t.reshape(batch, L, d * D_ATT)
    hb = lb // ATT_HALF
    cur = lambda b, r, i: (b, i, r)
    prev = lambda b, r, i: (b, jnp.maximum(i * hb - 1, 0), r)
    nxt = lambda b, r, i: (b, jnp.minimum((i + 1) * hb, L // ATT_HALF - 1), r)
    blk = pl.BlockSpec((1, lb, width), cur)
    halo_p = pl.BlockSpec((1, ATT_HALF, width), prev)
    halo_n = pl.BlockSpec((1, ATT_HALF, width), nxt)
    o, lse = pl.pallas_call(
        functools.partial(_attn_kernel, lb=lb, nblk_total=nblk_total),
        grid=(batch, d // rpb, L // lb),
        in_specs=[blk, blk, halo_p, halo_n, blk, halo_p, halo_n,
                  pl.BlockSpec((3, ATT_WIN, 2 * ATT_QB), lambda b, r, i: (0, 0, 0)),
                  pl.BlockSpec((16, LANES), lambda b, r, i: (0, 0))],
        out_specs=[blk, blk],
        out_shape=[jax.ShapeDtypeStruct((batch, L, d * D_ATT), BF16),
                   jax.ShapeDtypeStruct((batch, L, d * D_ATT), F32)],
        scratch_shapes=[pltpu.VMEM((lb + 2 * ATT_HALF, width), BF16),
                        pltpu.VMEM((lb + 2 * ATT_HALF, width), BF16)],
        compiler_params=pltpu.CompilerParams(
            dimension_semantics=("arbitrary", "arbitrary", "arbitrary"),
            vmem_limit_bytes=VMEM_LIMIT),
        name=f"attn_d{d}",
    )(view(q), view(k), view(k), view(k), view(v), view(v), view(v), bias, hmask)
    return o.reshape(batch * L, d * D_ATT), lse.reshape(batch * L, d * D_ATT)


def _conv_kernel(uc_ref, up_ref, un_ref, w_ref, b_ref, lg_ref, lb_ref, c_ref, ext, ybuf, *, ts,
                 rows, ln_rows):
    i = pl.program_id(1)
    n = pl.num_programs(1)
    n_slab = D_CONV // LANES
    for g in range(n_slab):
        ls = slice(g * LANES, (g + 1) * LANES)
        ext[g, 0:CONV_HALO] = jnp.where(i > 0, up_ref[0, :, ls], 0.0)
        ext[g, CONV_HALO:CONV_HALO + ts] = uc_ref[0, :, ls]
        ext[g, CONV_HALO + ts:] = jnp.where(i < n - 1, un_ref[0, :, ls], 0.0)
    lg = lg_ref[...]
    lb = lb_ref[...]
    off = CONV_HALO - CONV_PAD
    n_blk = ts // rows

    def taps(idx, carry):
        g = idx // n_blk
        base = pl.multiple_of((idx % n_blk) * rows, rows)
        acc = jnp.broadcast_to(b_ref[g], (rows, LANES))
        for j in range(SUBLANES):
            steps = [m for m in range((CONV_WIDTH + off) // SUBLANES + 1)
                     if 0 <= SUBLANES * m + j - off < CONV_WIDTH]
            win = ext[g, pl.ds(base + j, rows + SUBLANES * max(steps)), :]
            for m in steps:
                k = SUBLANES * m + j - off
                acc = acc + win[SUBLANES * m:SUBLANES * m + rows] * w_ref[g, k:k + 1, :]
        ybuf[g, pl.ds(base, rows), :] = acc
        return carry

    lax.fori_loop(0, n_slab * n_blk, taps, 0)
    for r0 in range(0, ts, ln_rows):
        acc = jnp.concatenate([ybuf[g, r0:r0 + ln_rows, :] for g in range(n_slab)], axis=1)
        mu = jnp.mean(acc, axis=-1, keepdims=True)
        cen = acc - mu
        var = jnp.mean(cen * cen, axis=-1, keepdims=True)
        y = cen * lax.rsqrt(var + EPS) * lg + lb
        c_ref[0, r0:r0 + ln_rows, :] = (y * (1.0 / (1.0 + jnp.exp(-y)))).astype(c_ref.dtype)


def _conv(u3, conv_w, conv_b, ln_g, ln_b, ts=512, rows=128, ln_rows=32):
    B, S, C = u3.shape
    hb = ts // CONV_HALO
    cur = lambda b, i: (b, i, 0)
    prev = lambda b, i: (b, jnp.maximum(i * hb - 1, 0), 0)
    nxt = lambda b, i: (b, jnp.minimum((i + 1) * hb, S // CONV_HALO - 1), 0)
    const = lambda b, i: (0, 0)
    const3 = lambda b, i: (0, 0, 0)
    n_slab = C // LANES
    w_slab = jnp.transpose(conv_w.reshape(CONV_WIDTH, n_slab, LANES), (1, 0, 2))
    b_slab = conv_b.reshape(n_slab, 1, LANES)
    return pl.pallas_call(
        functools.partial(_conv_kernel, ts=ts, rows=rows, ln_rows=ln_rows),
        grid=(B, S // ts),
        in_specs=[pl.BlockSpec((1, ts, C), cur),
                  pl.BlockSpec((1, CONV_HALO, C), prev),
                  pl.BlockSpec((1, CONV_HALO, C), nxt),
                  pl.BlockSpec((n_slab, CONV_WIDTH, LANES), const3),
                  pl.BlockSpec((n_slab, 1, LANES), const3),
                  pl.BlockSpec((1, C), const),
                  pl.BlockSpec((1, C), const)],
        out_specs=pl.BlockSpec((1, ts, C), cur),
        out_shape=jax.ShapeDtypeStruct((B, S, C), BF16),
        scratch_shapes=[pltpu.VMEM((C // LANES, ts + 2 * CONV_HALO, LANES), F32),
                        pltpu.VMEM((C // LANES, ts, LANES), F32)],
        compiler_params=pltpu.CompilerParams(dimension_semantics=("arbitrary", "arbitrary"),
                                             vmem_limit_bytes=VMEM_LIMIT),
        name="conv",
    )(u3, u3, u3, w_slab, b_slab, ln_g, ln_b)


def _mem_kv_kernel(mem_ref, g_ref, wk_ref, wv_ref, k_ref, v_ref):
    mn = _rms(mem_ref[...], g_ref[...]).astype(BF16)
    k_ref[...] = _dot(mn, wk_ref[...].astype(BF16)).astype(BF16)
    v_ref[...] = _dot(mn, wv_ref[...].astype(BF16)).astype(BF16)


def _mem_kv(mem2, g, wk, wv):
    R = mem2.shape[0]
    full = lambda shape: pl.BlockSpec(shape, lambda i: (0, 0))
    return pl.pallas_call(
        _mem_kv_kernel,
        grid=(1,),
        in_specs=[full((R, D_MODEL)), full((1, D_MODEL)),
                  full((D_MODEL, D_MODEL)), full((D_MODEL, D_MODEL))],
        out_specs=[full((R, D_MODEL))] * 2,
        out_shape=[jax.ShapeDtypeStruct((R, D_MODEL), BF16)] * 2,
        compiler_params=pltpu.CompilerParams(dimension_semantics=("arbitrary",),
                                             vmem_limit_bytes=VMEM_LIMIT),
        name="mem_kv",
    )(mem2, g, wk, wv)


def _mix_xattn_kernel(x_ref, o1_ref, o2_ref, o3_ref, l1_ref, l2_ref, l3_ref, c_ref,
                      wof_ref, gx_ref, wqf_ref, xk_ref, xv_ref, wxof_ref, wuf_ref, wdf_ref,
                      h_ref, wub_ref, wdb_ref, wo_ref, wq_ref, wxo_ref, *bufs, tm, sub):
    _cast_once([(wof_ref, wo_ref), (wqf_ref, wq_ref), (wxof_ref, wxo_ref)])
    wub_ref[...] = wuf_ref[...].astype(BF16)
    wdb_ref[...] = wdf_ref[...].astype(BF16)
    n_slab = D_ATT // LANES
    n_pat = len(DILATED_PATTERNS)
    o_refs = (o1_ref, o2_ref, o3_ref)
    l_refs = (l1_ref, l2_ref, l3_ref)
    n_grp = tm // sub

    def mix(g):
        t0 = g * sub
        obuf, lbuf = bufs[2 * g], bufs[2 * g + 1]
        otmp, ltmp = bufs[2 * n_grp + 2 * g], bufs[2 * n_grp + 2 * g + 1]
        for p, (_, d) in enumerate(DILATED_PATTERNS):
            if d == 1:
                continue
            src = slice(t0 // d, (t0 + sub) // d)
            for ref, dst, tmp, conv in ((o_refs[p], obuf, otmp, lambda t: t.astype(F32)),
                                        (l_refs[p], lbuf, ltmp, lambda t: t)):
                for j in range(n_slab):
                    pieces = {r: conv(ref[src, r * D_ATT + j * LANES:r * D_ATT + (j + 1) * LANES])
                              for r in range(d)}
                    dd = d
                    while dd > DIL_STEP:
                        lower = dd // DIL_STEP
                        merged = {}
                        for r_low in range(lower):
                            for t in range(DIL_STEP):
                                tmp[r_low, j, pl.ds(t, sub // dd, stride=DIL_STEP), :] = (
                                    pieces[t * lower + r_low])
                            merged[r_low] = tmp[r_low, j, 0:sub // lower, :]
                        pieces, dd = merged, lower
                    for r in range(dd):
                        dst[p, j, pl.ds(r, sub // dd, stride=DIL_STEP), :] = pieces[r]

        def slab(p, j, refs, buf):
            if DILATED_PATTERNS[p][1] == 1:
                return refs[p][t0:t0 + sub, j * LANES:(j + 1) * LANES]
            return buf[p, j]

        att = []
        for j in range(n_slab):
            ls = [slab(p, j, l_refs, lbuf) for p in range(n_pat)]
            os_ = [slab(p, j, o_refs, obuf).astype(F32) for p in range(n_pat)]
            m = jnp.maximum(jnp.maximum(ls[0], ls[1]), ls[2])
            es = [jnp.exp(l - m) for l in ls]
            inv = 1.0 / (es[0] + es[1] + es[2])
            att.append(((es[0] * inv) * os_[0] + (es[1] * inv) * os_[1]
                        + (es[2] * inv) * os_[2]).astype(BF16))
        return jnp.concatenate(att, axis=1)

    def project(g, att):
        tr = slice(g * sub, (g + 1) * sub)
        h1 = (x_ref[tr, :] + _dot(att, wo_ref[0:D_ATT, :])
              + _dot(c_ref[tr, :], wo_ref[D_ATT:, :]))
        xq = (_dot(_rms(h1, gx_ref[...]).astype(BF16), wq_ref[...])
              * (XATT_HEAD_DIM ** -0.5)).astype(BF16)
        return h1, xq

    def cross(g, h1, xq):
        heads = []
        for h in range(XATT_HEADS):
            sl = slice(h * XATT_HEAD_DIM, (h + 1) * XATT_HEAD_DIM)
            s = _dot_nt(xq[:, sl], xk_ref[0, :, sl])
            mx = jnp.max(s, axis=-1, keepdims=True)
            p = jnp.exp(s - mx)
            den = jnp.sum(p, axis=-1, keepdims=True)
            heads.append((_dot(p.astype(BF16), xv_ref[0, :, sl]) * (1.0 / den)).astype(BF16))
        xo = jnp.concatenate(heads, axis=1)
        h_ref[g * sub:(g + 1) * sub, :] = h1 + _dot(xo, wxo_ref[...])

    for g in range(n_grp):
        h1, xq = project(g, mix(g))
        cross(g, h1, xq)


def _mix_xattn(x2, os_, lses, c2, w_out, gx, w_xq, xk, xv, w_xo, w_up, w_down, seq, tm=512,
               sub=256):
    T = x2.shape[0]
    n_mem = xk.shape[1]
    per_b = seq // tm
    n_steps = T // tm
    up_rows, dn_rows = w_up.shape[0] // n_steps, w_down.shape[0] // n_steps
    n_slab = D_ATT // LANES
    d_max = DILATED_PATTERNS[-1][1]
    row = lambda i: (i, 0)
    const = lambda i: (0, 0)
    memb = lambda i: (i // per_b, 0, 0)
    sq = pl.BlockSpec((D_MODEL, D_MODEL), const, pipeline_mode=pl.Buffered(1))
    n_pat = len(DILATED_PATTERNS)
    return pl.pallas_call(
        functools.partial(_mix_xattn_kernel, tm=tm, sub=sub),
        grid=(T // tm,),
        in_specs=[pl.BlockSpec((tm, D_MODEL), row)]
                 + [pl.BlockSpec((tm // d, d * D_ATT), row) for _, d in DILATED_PATTERNS]
                 + [pl.BlockSpec((tm // d, d * D_ATT), row) for _, d in DILATED_PATTERNS]
                 + [pl.BlockSpec((tm, D_CONV), row),
                    sq, pl.BlockSpec((1, D_MODEL), const), sq,
                    pl.BlockSpec((1, n_mem, D_MODEL), memb),
                    pl.BlockSpec((1, n_mem, D_MODEL), memb),
                    sq,
                    pl.BlockSpec((up_rows, w_up.shape[1]), row),
                    pl.BlockSpec((dn_rows, w_down.shape[1]), row)],
        out_specs=[pl.BlockSpec((tm, D_MODEL), row),
                   pl.BlockSpec((up_rows, w_up.shape[1]), row),
                   pl.BlockSpec((dn_rows, w_down.shape[1]), row)],
        out_shape=[jax.ShapeDtypeStruct((T, D_MODEL), F32),
                   jax.ShapeDtypeStruct(w_up.shape, BF16),
                   jax.ShapeDtypeStruct(w_down.shape, BF16)],
        scratch_shapes=[pltpu.VMEM((D_MODEL, D_MODEL), BF16)] * 3
                       + [pltpu.VMEM((n_pat, n_slab, sub, LANES), F32)] * 2 * (tm // sub)
                       + [pltpu.VMEM((d_max // DIL_STEP, n_slab, sub // DIL_STEP, LANES), F32)
                          ] * 2 * (tm // sub),
        compiler_params=pltpu.CompilerParams(dimension_semantics=("arbitrary",),
                                             vmem_limit_bytes=VMEM_LIMIT),
        name="mix_xattn",
    )(x2, *os_, *lses, c2, w_out, gx, w_xq, xk, xv, w_xo, w_up, w_down)


def _mlp_kernel(h_ref, g_ref, wu_ref, wd_ref, gf_ref, out_ref, *, chunk, final_norm):
    h = h_ref[...]
    hn = _rms(h, g_ref[...]).astype(BF16)
    acc = h
    for j in range(D_FF // chunk):
        u = jnp.maximum(_dot(hn, wu_ref[:, j * chunk:(j + 1) * chunk]), 0.0)
        acc = acc + _dot((u * u).astype(BF16), wd_ref[j * chunk:(j + 1) * chunk, :])
    out_ref[...] = _rms(acc, gf_ref[...]) if final_norm else acc


def _mlp(h2, g, w_up, w_down, gf, final_norm, tm=1024, chunk=1024):
    T = h2.shape[0]
    row = lambda i: (i, 0)
    const = lambda i: (0, 0)
    return pl.pallas_call(
        functools.partial(_mlp_kernel, chunk=chunk, final_norm=final_norm),
        grid=(T // tm,),
        in_specs=[pl.BlockSpec((tm, D_MODEL), row),
                  pl.BlockSpec((1, D_MODEL), const),
                  pl.BlockSpec((D_MODEL, D_FF), const, pipeline_mode=pl.Buffered(1)),
                  pl.BlockSpec((D_FF, D_MODEL), const, pipeline_mode=pl.Buffered(1)),
                  pl.BlockSpec((1, D_MODEL), const)],
        out_specs=pl.BlockSpec((tm, D_MODEL), row),
        out_shape=jax.ShapeDtypeStruct((T, D_MODEL), F32),
        compiler_params=pltpu.CompilerParams(dimension_semantics=("arbitrary",),
                                             vmem_limit_bytes=VMEM_LIMIT),
        name="mlp",
    )(h2, g, w_up, w_down, gf)


def _rotary_tables(seq):
    half = ROT_DIM // 2
    freqs = ROPE_THETA ** (-jnp.arange(0, ROT_DIM, 2, dtype=F32) / ROT_DIM)
    ang = jnp.arange(seq, dtype=F32)[:, None] * freqs[None, :]
    cos, sin = jnp.cos(ang), jnp.sin(ang)
    zeros = jnp.zeros((seq, HEAD_DIM - ROT_DIM), F32)
    ones = jnp.ones((seq, HEAD_DIM - ROT_DIM), F32)
    z8 = jnp.zeros((seq, half), F32)
    cos_h = jnp.concatenate([cos, cos, ones], axis=1)
    sa_h = jnp.concatenate([-sin, z8, zeros], axis=1)
    sb_h = jnp.concatenate([z8, sin, zeros], axis=1)
    rep = LANES // HEAD_DIM
    return tuple(jnp.tile(t, (1, rep)) for t in (cos_h, sa_h, sb_h))


def kernel(x, mem, norm_mix_g, w_in, conv_w, conv_b, conv_ln_g, conv_ln_b, w_out, norm_x_g,
           norm_mem_g, w_xq, w_xk, w_xv, w_xo, norm_mlp_g, w_up, w_down, norm_final_g):
    B, S, D = x.shape
    n_mem = mem.shape[1]
    depth = w_in.shape[0]
    T = B * S
    cos_t, sa_t, sb_t = _rotary_tables(S)
    bias = _attn_bias()
    hmask = _head_masks()
    row = lambda g: g.reshape(1, -1)

    h = x.reshape(T, D)
    for l in range(depth):
        q, k, v, u = _in_proj(h, row(norm_mix_g[l]), w_in[l], cos_t, sa_t, sb_t, S)
        os_, lses = [], []
        for p, (_, d) in enumerate(DILATED_PATTERNS):
            o, lse = _attn(q[p], k[p], v[p], bias, hmask, B, S, d)
            os_.append(o)
            lses.append(lse)
        c = _conv(u.reshape(B, S, D_CONV), conv_w[l], row(conv_b[l]), row(conv_ln_g[l]),
                  row(conv_ln_b[l])).reshape(T, D_CONV)
        xk, xv = _mem_kv(mem.reshape(B * n_mem, D), row(norm_mem_g[l]),
                         w_xk[l], w_xv[l])
        h, w_up_b, w_down_b = _mix_xattn(h, os_, lses, c, w_out[l], row(norm_x_g[l]), w_xq[l],
                                         xk.reshape(B, n_mem, D), xv.reshape(B, n_mem, D),
                                         w_xo[l], w_up[l], w_down[l], S)
        h = _mlp(h, row(norm_mlp_g[l]), w_up_b, w_down_b, row(norm_final_g),
                 final_norm=(l == depth - 1))
    return h.reshape(B, S, D)
```

```python
import functools
import math

import numpy as np
import jax
import jax.numpy as jnp
from jax import lax
from jax.experimental import pallas as pl
from jax.experimental.pallas import tpu as pltpu

F32 = jnp.float32
BF16 = jnp.bfloat16

D_MODEL = 1024
ATT_HEADS = 8
HEAD_DIM = 64
D_ATT = ATT_HEADS * HEAD_DIM
D_CONV = D_MODEL - D_ATT
DILATED_PATTERNS = ((128, 1), (512, 4), (2048, 16))
DIL_STEP = 4
assert all(d == DIL_STEP ** k for k, (_, d) in enumerate(DILATED_PATTERNS))
ROPE_THETA = 500000.0
ROT_DIM = HEAD_DIM // 4
CONV_WIDTH = 31
CONV_PAD = (CONV_WIDTH - 1) // 2
XATT_HEADS = 4
XATT_HEAD_DIM = D_MODEL // XATT_HEADS
D_FF = 4 * D_MODEL
EPS = 1e-6
NEG_INF = -1e30
LN2 = math.log(2.0)
Q_SCALE = HEAD_DIM ** -0.5 / LN2

LANES = 128
SUBLANES = 8
ATT_HALF = 64
ATT_QB = 2 * ATT_HALF
ATT_WIN = ATT_QB + 2 * ATT_HALF
CONV_HALO = 16
VMEM_LIMIT = 56 * 1024 * 1024


def _dot(a, b):
    return jnp.dot(a, b, preferred_element_type=F32)


def _dot_nt(a, b):
    return lax.dot_general(a, b, (((1,), (1,)), ((), ())), preferred_element_type=F32)


def _dot_tn(a, b):
    return lax.dot_general(a, b, (((0,), (0,)), ((), ())), preferred_element_type=F32)


def _rms(x, g):
    var = jnp.mean(x * x, axis=-1, keepdims=True)
    return x * lax.rsqrt(var + EPS) * g


def _cast_once(pairs, chunk=512):
    @pl.when(pl.program_id(0) == 0)
    def _():
        for src, dst in pairs:
            for c in range(0, src.shape[1], chunk):
                dst[:, c:c + chunk] = src[:, c:c + chunk].astype(BF16)


def _in_proj_kernel(x_ref, g_ref, wf_ref, cos_ref, sa_ref, sb_ref, *refs, tm):
    n_pat = len(DILATED_PATTERNS)
    qkv_refs = [refs[a * n_pat:(a + 1) * n_pat] for a in range(3)]
    u_ref = refs[3 * n_pat]
    w_ref = refs[3 * n_pat + 1]
    stage = refs[3 * n_pat + 2:]
    _cast_once([(wf_ref, w_ref)])
    xn = _rms(x_ref[...], g_ref[...]).astype(BF16)
    cos, sa, sb = cos_ref[...], sa_ref[...], sb_ref[...]
    n_slab = D_ATT // LANES

    def plain(y, j):
        return y[:, LANES * j:LANES * (j + 1)]

    def rot(y, j):
        yj = plain(y, j)
        return (yj * cos + pltpu.roll(yj, LANES - ROT_DIM // 2, 1) * sa
                + pltpu.roll(yj, ROT_DIM // 2, 1) * sb)

    def emit(y, outs, transform):
        for j in range(n_slab):
            piece = transform(y, j)
            stage[0][0, j] = piece
            outs[0][:, j * LANES:(j + 1) * LANES] = piece.astype(BF16)
        for k in range(1, n_pat):
            d_prev, d = DILATED_PATTERNS[k - 1][1], DILATED_PATTERNS[k][1]
            for r_prev in range(d_prev):
                for t in range(DIL_STEP):
                    r = t * d_prev + r_prev
                    for j in range(n_slab):
                        piece = stage[k - 1][r_prev, j, pl.ds(t, tm // d, stride=DIL_STEP), :]
                        if k + 1 < n_pat:
                            stage[k][r, j] = piece
                        c0 = r * D_ATT + j * LANES
                        outs[k][:, c0:c0 + LANES] = piece.astype(BF16)

    emit(_dot(xn, w_ref[:, 0:D_ATT]), qkv_refs[0], lambda y, j: rot(y, j) * Q_SCALE)
    emit(_dot(xn, w_ref[:, D_ATT:2 * D_ATT]), qkv_refs[1], rot)
    emit(_dot(xn, w_ref[:, 2 * D_ATT:3 * D_ATT]), qkv_refs[2], plain)
    c0 = 3 * D_ATT
    a = _dot(xn, w_ref[:, c0:c0 + D_CONV])
    gt = _dot(xn, w_ref[:, c0 + D_CONV:c0 + 2 * D_CONV])
    u_ref[...] = a * (1.0 / (1.0 + jnp.exp(-gt)))


def _in_proj(x2, g, w_in, cos_t, sa_t, sb_t, seq, tm=512):
    T = x2.shape[0]
    d_in = w_in.shape[1]
    n_s = seq // tm
    n_pat = len(DILATED_PATTERNS)
    row = lambda i: (i, 0)
    tab = lambda i: (i % n_s, 0)
    const = lambda i: (0, 0)
    qkv_specs, qkv_shapes = [], []
    for _ in range(3):
        for _, d in DILATED_PATTERNS:
            qkv_specs.append(pl.BlockSpec((tm // d, d * D_ATT), row))
            qkv_shapes.append(jax.ShapeDtypeStruct((T // d, d * D_ATT), BF16))
    outs = pl.pallas_call(
        functools.partial(_in_proj_kernel, tm=tm),
        grid=(T // tm,),
        in_specs=[pl.BlockSpec((tm, D_MODEL), row),
                  pl.BlockSpec((1, D_MODEL), const),
                  pl.BlockSpec((D_MODEL, d_in), const, pipeline_mode=pl.Buffered(1)),
                  pl.BlockSpec((tm, LANES), tab),
                  pl.BlockSpec((tm, LANES), tab),
                  pl.BlockSpec((tm, LANES), tab)],
        out_specs=qkv_specs + [pl.BlockSpec((tm, D_CONV), row)],
        out_shape=qkv_shapes + [jax.ShapeDtypeStruct((T, D_CONV), F32)],
        scratch_shapes=[pltpu.VMEM((D_MODEL, d_in), BF16)]
                       + [pltpu.VMEM((d, D_ATT // LANES, tm // d, LANES), F32)
                          for _, d in DILATED_PATTERNS[:-1]],
        compiler_params=pltpu.CompilerParams(dimension_semantics=("arbitrary",),
                                             vmem_limit_bytes=VMEM_LIMIT),
        name="in_proj",
    )(x2, g, w_in, cos_t, sa_t, sb_t)
    return outs[0:n_pat], outs[n_pat:2 * n_pat], outs[2 * n_pat:3 * n_pat], outs[3 * n_pat]


def _attn_kernel(q_ref, kc_ref, kp_ref, kn_ref, vc_ref, vp_ref, vn_ref, bias_ref, hm_ref,
                 o_ref, lse_ref, kext, vext, *, lb, nblk_total):
    i = pl.program_id(2)
    kext[0:ATT_HALF] = kp_ref[0]
    kext[ATT_HALF:ATT_HALF + lb] = kc_ref[0]
    kext[ATT_HALF + lb:] = kn_ref[0]
    vext[0:ATT_HALF] = vp_ref[0]
    vext[ATT_HALF:ATT_HALF + lb] = vc_ref[0]
    vext[ATT_HALF + lb:] = vn_ref[0]

    mask_a = hm_ref[0:1, :]
    mask_b = hm_ref[1:2, :]
    nblk = lb // ATT_QB

    def body(n, carry):
        r0 = pl.multiple_of(n * ATT_QB, ATT_QB)
        gblk = i * nblk + n
        bidx = jnp.where(gblk == 0, 0, jnp.where(gblk == nblk_total - 1, 2, 1))
        bias = bias_ref[bidx]
        for hp in range(q_ref.shape[2] // LANES):
            c0 = LANES * hp
            q2 = q_ref[0, pl.ds(r0, ATT_QB), c0:c0 + LANES]
            qs = jnp.concatenate([q2 * mask_a, q2 * mask_b], axis=0)
            parts = []
            for kh in range(ATT_WIN // ATT_QB):
                k0 = r0 + kh * ATT_QB
                kw = kext[pl.ds(k0, ATT_QB), c0:c0 + LANES]
                vw = vext[pl.ds(k0, ATT_QB), c0:c0 + LANES]
                s = _dot_nt(kw, qs) + bias[kh * ATT_QB:(kh + 1) * ATT_QB]
                m = jnp.max(s, axis=0, keepdims=True)
                p = jnp.exp2(s - m)
                l = jnp.sum(p, axis=0, keepdims=True)
                parts.append((m, l, _dot_tn(vw, p.astype(BF16))))
            (m1, l1, o1), (m2, l2, o2) = parts
            m = jnp.maximum(m1, m2)
            a1 = jnp.exp2(m1 - m)
            a2 = jnp.exp2(m2 - m)
            l = a1 * l1 + a2 * l2
            inv = 1.0 / l
            ot = o1 * (a1 * inv) + o2 * (a2 * inv)
            lse = jnp.broadcast_to((m + jnp.log2(l)) * LN2, ot.shape)

            def own_heads(t):
                return jnp.transpose(jnp.concatenate(
                    [t[0:HEAD_DIM, 0:ATT_QB], t[HEAD_DIM:, ATT_QB:]], axis=0))

            o_ref[0, pl.ds(r0, ATT_QB), c0:c0 + LANES] = own_heads(ot).astype(o_ref.dtype)
            lse_ref[0, pl.ds(r0, ATT_QB), c0:c0 + LANES] = own_heads(lse)
        return carry

    lax.fori_loop(0, nblk, body, 0, unroll=True)


def _attn_bias():
    i = np.arange(ATT_QB)[:, None]
    j = np.arange(ATT_WIN)[None, :]
    band = (j >= i) & (j <= i + 2 * ATT_HALF)
    first = band & (j >= ATT_HALF)
    last = band & (j < ATT_HALF + ATT_QB)
    tabs = np.stack([first, band, last]).astype(np.float32)
    tabs = np.concatenate([tabs.transpose(0, 2, 1)] * 2, axis=2)
    return jnp.asarray((1.0 - tabs) * NEG_INF, dtype=F32)


def _head_masks():
    lane = np.arange(LANES)
    m = np.zeros((16, LANES), np.float32)
    m[0] = lane < HEAD_DIM
    m[1] = lane >= HEAD_DIM
    return jnp.asarray(m, dtype=BF16)


def _attn(q, k, v, bias, hmask, batch, seq, d, blocks_per_step=16, lb_max=2048):
    L = seq // d
    lb = min(lb_max, L)
    rpb = min(d, max(1, blocks_per_step * ATT_QB // lb))
    width = rpb * D_ATT
    nblk_total = L // ATT_QB
    assert nblk_total >= 2 and L % lb == 0 and lb % ATT_QB == 0 and d % rpb == 0
    view = lambda t: t.reshape(batch, L, d * D_ATT)
    hb = lb // ATT_HALF
    cur = lambda b, r, i: (b, i, r)
    prev = lambda b, r, i: (b, jnp.maximum(i * hb - 1, 0), r)
    nxt = lambda b, r, i: (b, jnp.minimum((i + 1) * hb, L // ATT_HALF - 1), r)
    blk = pl.BlockSpec((1, lb, width), cur)
    halo_p = pl.BlockSpec((1, ATT_HALF, width), prev)
    halo_n = pl.BlockSpec((1, ATT_HALF, width), nxt)
    o, lse = pl.pallas_call(
        functools.partial(_attn_kernel, lb=lb, nblk_total=nblk_total),
        grid=(batch, d // rpb, L // lb),
        in_specs=[blk, blk, halo_p, halo_n, blk, halo_p, halo_n,
                  pl.BlockSpec((3, ATT_WIN, 2 * ATT_QB), lambda b, r, i: (0, 0, 0)),
                  pl.BlockSpec((16, LANES), lambda b, r, i: (0, 0))],
        out_specs=[blk, blk],
        out_shape=[jax.ShapeDtypeStruct((batch, L, d * D_ATT), BF16),
                   jax.ShapeDtypeStruct((batch, L, d * D_ATT), F32)],
        scratch_shapes=[pltpu.VMEM((lb + 2 * ATT_HALF, width), BF16),
                        pltpu.VMEM((lb + 2 * ATT_HALF, width), BF16)],
        compiler_params=pltpu.CompilerParams(
            dimension_semantics=("arbitrary", "arbitrary", "arbitrary"),
            vmem_limit_bytes=VMEM_LIMIT),
        name=f"attn_d{d}",
    )(view(q), view(k), view(k), view(k), view(v), view(v), view(v), bias, hmask)
    return o.reshape(batch * L, d * D_ATT), lse.reshape(batch * L, d * D_ATT)


def _conv_kernel(uc_ref, up_ref, un_ref, w_ref, b_ref, lg_ref, lb_ref, c_ref, ext, ybuf, *, ts,
                 rows, ln_rows):
    i = pl.program_id(1)
    n = pl.num_programs(1)
    n_slab = D_CONV // LANES
    for g in range(n_slab):
        ls = slice(g * LANES, (g + 1) * LANES)
        ext[g, 0:CONV_HALO] = jnp.where(i > 0, up_ref[0, :, ls], 0.0)
        ext[g, CONV_HALO:CONV_HALO + ts] = uc_ref[0, :, ls]
        ext[g, CONV_HALO + ts:] = jnp.where(i < n - 1, un_ref[0, :, ls], 0.0)
    lg = lg_ref[...]
    lb = lb_ref[...]
    off = CONV_HALO - CONV_PAD
    n_blk = ts // rows

    def taps(idx, carry):
        g = idx // n_blk
        base = pl.multiple_of((idx % n_blk) * rows, rows)
        acc = jnp.broadcast_to(b_ref[g], (rows, LANES))
        for j in range(SUBLANES):
            steps = [m for m in range((CONV_WIDTH + off) // SUBLANES + 1)
                     if 0 <= SUBLANES * m + j - off < CONV_WIDTH]
            win = ext[g, pl.ds(base + j, rows + SUBLANES * max(steps)), :]
            for m in steps:
                k = SUBLANES * m + j - off
                acc = acc + win[SUBLANES * m:SUBLANES * m + rows] * w_ref[g, k:k + 1, :]
        ybuf[g, pl.ds(base, rows), :] = acc
        return carry

    lax.fori_loop(0, n_slab * n_blk, taps, 0, unroll=2)
    for r0 in range(0, ts, ln_rows):
        acc = jnp.concatenate([ybuf[g, r0:r0 + ln_rows, :] for g in range(n_slab)], axis=1)
        mu = jnp.mean(acc, axis=-1, keepdims=True)
        cen = acc - mu
        var = jnp.mean(cen * cen, axis=-1, keepdims=True)
        y = cen * lax.rsqrt(var + EPS) * lg + lb
        c_ref[0, r0:r0 + ln_rows, :] = (y * (1.0 / (1.0 + jnp.exp(-y)))).astype(c_ref.dtype)


def _conv(u3, conv_w, conv_b, ln_g, ln_b, ts=512, rows=128, ln_rows=32):
    B, S, C = u3.shape
    hb = ts // CONV_HALO
    cur = lambda b, i: (b, i, 0)
    prev = lambda b, i: (b, jnp.maximum(i * hb - 1, 0), 0)
    nxt = lambda b, i: (b, jnp.minimum((i + 1) * hb, S // CONV_HALO - 1), 0)
    const = lambda b, i: (0, 0)
    const3 = lambda b, i: (0, 0, 0)
    n_slab = C // LANES
    w_slab = jnp.transpose(conv_w.reshape(CONV_WIDTH, n_slab, LANES), (1, 0, 2))
    b_slab = conv_b.reshape(n_slab, 1, LANES)
    return pl.pallas_call(
        functools.partial(_conv_kernel, ts=ts, rows=rows, ln_rows=ln_rows),
        grid=(B, S // ts),
        in_specs=[pl.BlockSpec((1, ts, C), cur),
                  pl.BlockSpec((1, CONV_HALO, C), prev),
                  pl.BlockSpec((1, CONV_HALO, C), nxt),
                  pl.BlockSpec((n_slab, CONV_WIDTH, LANES), const3),
                  pl.BlockSpec((n_slab, 1, LANES), const3),
                  pl.BlockSpec((1, C), const),
                  pl.BlockSpec((1, C), const)],
        out_specs=pl.BlockSpec((1, ts, C), cur),
        out_shape=jax.ShapeDtypeStruct((B, S, C), BF16),
        scratch_shapes=[pltpu.VMEM((C // LANES, ts + 2 * CONV_HALO, LANES), F32),
                        pltpu.VMEM((C // LANES, ts, LANES), F32)],
        compiler_params=pltpu.CompilerParams(dimension_semantics=("arbitrary", "arbitrary"),
                                             vmem_limit_bytes=VMEM_LIMIT),
        name="conv",
    )(u3, u3, u3, w_slab, b_slab, ln_g, ln_b)


def _mem_kv_kernel(mem_ref, g_ref, wk_ref, wv_ref, k_ref, v_ref):
    mn = _rms(mem_ref[...], g_ref[...]).astype(BF16)
    k_ref[...] = _dot(mn, wk_ref[...].astype(BF16)).astype(BF16)
    v_ref[...] = _dot(mn, wv_ref[...].astype(BF16)).astype(BF16)


def _mem_kv(mem2, g, wk, wv):
    R = mem2.shape[0]
    full = lambda shape: pl.BlockSpec(shape, lambda i: (0, 0))
    return pl.pallas_call(
        _mem_kv_kernel,
        grid=(1,),
        in_specs=[full((R, D_MODEL)), full((1, D_MODEL)),
                  full((D_MODEL, D_MODEL)), full((D_MODEL, D_MODEL))],
        out_specs=[full((R, D_MODEL))] * 2,
        out_shape=[jax.ShapeDtypeStruct((R, D_MODEL), BF16)] * 2,
        compiler_params=pltpu.CompilerParams(dimension_semantics=("arbitrary",),
                                             vmem_limit_bytes=VMEM_LIMIT),
        name="mem_kv",
    )(mem2, g, wk, wv)


def _mix_xattn_kernel(x_ref, o1_ref, o2_ref, o3_ref, l1_ref, l2_ref, l3_ref, c_ref,
                      wof_ref, gx_ref, wqf_ref, xk_ref, xv_ref, wxof_ref, wuf_ref, wdf_ref,
                      h_ref, wub_ref, wdb_ref, wo_ref, wq_ref, wxo_ref, *bufs, tm, sub):
    _cast_once([(wof_ref, wo_ref), (wqf_ref, wq_ref), (wxof_ref, wxo_ref)])
    wub_ref[...] = wuf_ref[...].astype(BF16)
    wdb_ref[...] = wdf_ref[...].astype(BF16)
    n_slab = D_ATT // LANES
    n_pat = len(DILATED_PATTERNS)
    o_refs = (o1_ref, o2_ref, o3_ref)
    l_refs = (l1_ref, l2_ref, l3_ref)
    n_grp = tm // sub

    def mix(g):
        t0 = g * sub
        obuf, lbuf = bufs[2 * g], bufs[2 * g + 1]
        otmp, ltmp = bufs[2 * n_grp + 2 * g], bufs[2 * n_grp + 2 * g + 1]
        for p, (_, d) in enumerate(DILATED_PATTERNS):
            if d == 1:
                continue
            src = slice(t0 // d, (t0 + sub) // d)
            for ref, dst, tmp, conv in ((o_refs[p], obuf, otmp, lambda t: t.astype(F32)),
                                        (l_refs[p], lbuf, ltmp, lambda t: t)):
                for j in range(n_slab):
                    pieces = {r: conv(ref[src, r * D_ATT + j * LANES:r * D_ATT + (j + 1) * LANES])
                              for r in range(d)}
                    dd = d
                    while dd > DIL_STEP:
                        lower = dd // DIL_STEP
                        merged = {}
                        for r_low in range(lower):
                            for t in range(DIL_STEP):
                                tmp[r_low, j, pl.ds(t, sub // dd, stride=DIL_STEP), :] = (
                                    pieces[t * lower + r_low])
                            merged[r_low] = tmp[r_low, j, 0:sub // lower, :]
                        pieces, dd = merged, lower
                    for r in range(dd):
                        dst[p, j, pl.ds(r, sub // dd, stride=DIL_STEP), :] = pieces[r]

        def slab(p, j, refs, buf):
            if DILATED_PATTERNS[p][1] == 1:
                return refs[p][t0:t0 + sub, j * LANES:(j + 1) * LANES]
            return buf[p, j]

        att = []
        for j in range(n_slab):
            ls = [slab(p, j, l_refs, lbuf) for p in range(n_pat)]
            os_ = [slab(p, j, o_refs, obuf).astype(F32) for p in range(n_pat)]
            m = jnp.maximum(jnp.maximum(ls[0], ls[1]), ls[2])
            es = [jnp.exp(l - m) for l in ls]
            inv = 1.0 / (es[0] + es[1] + es[2])
            att.append(((es[0] * inv) * os_[0] + (es[1] * inv) * os_[1]
                        + (es[2] * inv) * os_[2]).astype(BF16))
        return jnp.concatenate(att, axis=1)

    def project(g, att):
        tr = slice(g * sub, (g + 1) * sub)
        h1 = (x_ref[tr, :] + _dot(att, wo_ref[0:D_ATT, :])
              + _dot(c_ref[tr, :], wo_ref[D_ATT:, :]))
        xq = (_dot(_rms(h1, gx_ref[...]).astype(BF16), wq_ref[...])
              * (XATT_HEAD_DIM ** -0.5)).astype(BF16)
        return h1, xq

    def cross(g, h1, xq):
        heads = []
        for h in range(XATT_HEADS):
            sl = slice(h * XATT_HEAD_DIM, (h + 1) * XATT_HEAD_DIM)
            s = _dot_nt(xq[:, sl], xk_ref[0, :, sl])
            mx = jnp.max(s, axis=-1, keepdims=True)
            p = jnp.exp(s - mx)
            den = jnp.sum(p, axis=-1, keepdims=True)
            heads.append((_dot(p.astype(BF16), xv_ref[0, :, sl]) * (1.0 / den)).astype(BF16))
        xo = jnp.concatenate(heads, axis=1)
        h_ref[g * sub:(g + 1) * sub, :] = h1 + _dot(xo, wxo_ref[...])

    for g in range(n_grp):
        h1, xq = project(g, mix(g))
        cross(g, h1, xq)


def _mix_xattn(x2, os_, lses, c2, w_out, gx, w_xq, xk, xv, w_xo, w_up, w_down, seq, tm=512,
               sub=256):
    T = x2.shape[0]
    n_mem = xk.shape[1]
    per_b = seq // tm
    n_steps = T // tm
    up_rows, dn_rows = w_up.shape[0] // n_steps, w_down.shape[0] // n_steps
    n_slab = D_ATT // LANES
    d_max = DILATED_PATTERNS[-1][1]
    row = lambda i: (i, 0)
    const = lambda i: (0, 0)
    memb = lambda i: (i // per_b, 0, 0)
    sq = pl.BlockSpec((D_MODEL, D_MODEL), const, pipeline_mode=pl.Buffered(1))
    n_pat = len(DILATED_PATTERNS)
    return pl.pallas_call(
        functools.partial(_mix_xattn_kernel, tm=tm, sub=sub),
        grid=(T // tm,),
        in_specs=[pl.BlockSpec((tm, D_MODEL), row)]
                 + [pl.BlockSpec((tm // d, d * D_ATT), row) for _, d in DILATED_PATTERNS]
                 + [pl.BlockSpec((tm // d, d * D_ATT), row) for _, d in DILATED_PATTERNS]
                 + [pl.BlockSpec((tm, D_CONV), row),
                    sq, pl.BlockSpec((1, D_MODEL), const), sq,
                    pl.BlockSpec((1, n_mem, D_MODEL), memb),
                    pl.BlockSpec((1, n_mem, D_MODEL), memb),
                    sq,
                    pl.BlockSpec((up_rows, w_up.shape[1]), row),
                    pl.BlockSpec((dn_rows, w_down.shape[1]), row)],
        out_specs=[pl.BlockSpec((tm, D_MODEL), row),
                   pl.BlockSpec((up_rows, w_up.shape[1]), row),
                   pl.BlockSpec((dn_rows, w_down.shape[1]), row)],
        out_shape=[jax.ShapeDtypeStruct((T, D_MODEL), F32),
                   jax.ShapeDtypeStruct(w_up.shape, BF16),
                   jax.ShapeDtypeStruct(w_down.shape, BF16)],
        scratch_shapes=[pltpu.VMEM((D_MODEL, D_MODEL), BF16)] * 3
                       + [pltpu.VMEM((n_pat, n_slab, sub, LANES), F32)] * 2 * (tm // sub)
                       + [pltpu.VMEM((d_max // DIL_STEP, n_slab, sub // DIL_STEP, LANES), F32)
                          ] * 2 * (tm // sub),
        compiler_params=pltpu.CompilerParams(dimension_semantics=("arbitrary",),
                                             vmem_limit_bytes=VMEM_LIMIT),
        name="mix_xattn",
    )(x2, *os_, *lses, c2, w_out, gx, w_xq, xk, xv, w_xo, w_up, w_down)


def _mlp_kernel(h_ref, g_ref, wu_ref, wd_ref, gf_ref, out_ref, *, chunk, final_norm):
    h = h_ref[...]
    hn = _rms(h, g_ref[...]).astype(BF16)
    acc = h
    for j in range(D_FF // chunk):
        u = jnp.maximum(_dot(hn, wu_ref[:, j * chunk:(j + 1) * chunk]), 0.0)
        acc = acc + _dot((u * u).astype(BF16), wd_ref[j * chunk:(j + 1) * chunk, :])
    out_ref[...] = _rms(acc, gf_ref[...]) if final_norm else acc


def _mlp(h2, g, w_up, w_down, gf, final_norm, tm=1024, chunk=1024):
    T = h2.shape[0]
    row = lambda i: (i, 0)
    const = lambda i: (0, 0)
    return pl.pallas_call(
        functools.partial(_mlp_kernel, chunk=chunk, final_norm=final_norm),
        grid=(T // tm,),
        in_specs=[pl.BlockSpec((tm, D_MODEL), row),
                  pl.BlockSpec((1, D_MODEL), const),
                  pl.BlockSpec((D_MODEL, D_FF), const, pipeline_mode=pl.Buffered(1)),
                  pl.BlockSpec((D_FF, D_MODEL), const, pipeline_mode=pl.Buffered(1)),
                  pl.BlockSpec((1, D_MODEL), const)],
        out_specs=pl.BlockSpec((tm, D_MODEL), row),
        out_shape=jax.ShapeDtypeStruct((T, D_MODEL), F32),
        compiler_params=pltpu.CompilerParams(dimension_semantics=("arbitrary",),
                                             vmem_limit_bytes=VMEM_LIMIT),
        name="mlp",
    )(h2, g, w_up, w_down, gf)


def _rotary_tables(seq):
    half = ROT_DIM // 2
    freqs = ROPE_THETA ** (-jnp.arange(0, ROT_DIM, 2, dtype=F32) / ROT_DIM)
    dim = np.arange(LANES) % HEAD_DIM
    ang = jnp.arange(seq, dtype=F32)[:, None] * freqs[dim % half][None, :]
    cos, sin = jnp.cos(ang), jnp.sin(ang)
    first, second = (dim < half)[None, :], ((dim >= half) & (dim < ROT_DIM))[None, :]
    return (jnp.where(first | second, cos, 1.0),
            jnp.where(first, -sin, 0.0),
            jnp.where(second, sin, 0.0))


def kernel(x, mem, norm_mix_g, w_in, conv_w, conv_b, conv_ln_g, conv_ln_b, w_out, norm_x_g,
           norm_mem_g, w_xq, w_xk, w_xv, w_xo, norm_mlp_g, w_up, w_down, norm_final_g):
    B, S, D = x.shape
    n_mem = mem.shape[1]
    depth = w_in.shape[0]
    T = B * S
    cos_t, sa_t, sb_t = _rotary_tables(S)
    bias = _attn_bias()
    hmask = _head_masks()
    row = lambda g: g.reshape(1, -1)

    h = x.reshape(T, D)
    for l in range(depth):
        q, k, v, u = _in_proj(h, row(norm_mix_g[l]), w_in[l], cos_t, sa_t, sb_t, S)
        os_, lses = [], []
        for p, (_, d) in enumerate(DILATED_PATTERNS):
            o, lse = _attn(q[p], k[p], v[p], bias, hmask, B, S, d)
            os_.append(o)
            lses.append(lse)
        c = _conv(u.reshape(B, S, D_CONV), conv_w[l], row(conv_b[l]), row(conv_ln_g[l]),
                  row(conv_ln_b[l])).reshape(T, D_CONV)
        xk, xv = _mem_kv(mem.reshape(B * n_mem, D), row(norm_mem_g[l]),
                         w_xk[l], w_xv[l])
        h, w_up_b, w_down_b = _mix_xattn(h, os_, lses, c, w_out[l], row(norm_x_g[l]), w_xq[l],
                                         xk.reshape(B, n_mem, D), xv.reshape(B, n_mem, D),
                                         w_xo[l], w_up[l], w_down[l], S)
        h = _mlp(h, row(norm_mlp_g[l]), w_up_b, w_down_b, row(norm_final_g),
                 final_norm=(l == depth - 1))
    return h.reshape(B, S, D)
```

```python
import functools
import math

import numpy as np
import jax
import jax.numpy as jnp
from jax import lax
from jax.experimental import pallas as pl
from jax.experimental.pallas import tpu as pltpu

F32 = jnp.float32
BF16 = jnp.bfloat16

D_MODEL = 1024
ATT_HEADS = 8
HEAD_DIM = 64
D_ATT = ATT_HEADS * HEAD_DIM
D_CONV = D_MODEL - D_ATT
DILATED_PATTERNS = ((128, 1), (512, 4), (2048, 16))
DIL_STEP = 4
assert all(d == DIL_STEP ** k for k, (_, d) in enumerate(DILATED_PATTERNS))
ROPE_THETA = 500000.0
ROT_DIM = HEAD_DIM // 4
CONV_WIDTH = 31
CONV_PAD = (CONV_WIDTH - 1) // 2
XATT_HEADS = 4
XATT_HEAD_DIM = D_MODEL // XATT_HEADS
D_FF = 4 * D_MODEL
EPS = 1e-6
NEG_INF = -1e30
LN2 = math.log(2.0)
Q_SCALE = HEAD_DIM ** -0.5 / LN2

LANES = 128
SUBLANES = 8
ATT_HALF = 64
ATT_QB = 2 * ATT_HALF
ATT_WIN = ATT_QB + 2 * ATT_HALF
CONV_HALO = 16
VMEM_LIMIT = 56 * 1024 * 1024


def _dot(a, b):
    return jnp.dot(a, b, preferred_element_type=F32)


def _dot_nt(a, b):
    return lax.dot_general(a, b, (((1,), (1,)), ((), ())), preferred_element_type=F32)


def _dot_tn(a, b):
    return lax.dot_general(a, b, (((0,), (0,)), ((), ())), preferred_element_type=F32)


def _rms(x, g):
    var = jnp.mean(x * x, axis=-1, keepdims=True)
    return x * lax.rsqrt(var + EPS) * g


def _cast_once(pairs, chunk=512):
    @pl.when(pl.program_id(0) == 0)
    def _():
        for src, dst in pairs:
            for c in range(0, src.shape[1], chunk):
                dst[:, c:c + chunk] = src[:, c:c + chunk].astype(BF16)


def _in_proj_kernel(x_ref, g_ref, wf_ref, cos_ref, sa_ref, sb_ref, *refs, tm):
    n_pat = len(DILATED_PATTERNS)
    qkv_refs = [refs[a * n_pat:(a + 1) * n_pat] for a in range(3)]
    u_ref = refs[3 * n_pat]
    w_ref = refs[3 * n_pat + 1]
    stage = refs[3 * n_pat + 2:]
    _cast_once([(wf_ref, w_ref)])
    xn = _rms(x_ref[...], g_ref[...]).astype(BF16)
    cos, sa, sb = cos_ref[...], sa_ref[...], sb_ref[...]
    n_slab = D_ATT // LANES

    def plain(y, j):
        return y[:, LANES * j:LANES * (j + 1)]

    def rot(y, j):
        yj = plain(y, j)
        return (yj * cos + pltpu.roll(yj, LANES - ROT_DIM // 2, 1) * sa
                + pltpu.roll(yj, ROT_DIM // 2, 1) * sb)

    def emit(y, outs, transform):
        for j in range(n_slab):
            piece = transform(y, j)
            stage[0][0, j] = piece
            outs[0][:, j * LANES:(j + 1) * LANES] = piece.astype(BF16)
        for k in range(1, n_pat):
            d_prev, d = DILATED_PATTERNS[k - 1][1], DILATED_PATTERNS[k][1]
            for r_prev in range(d_prev):
                for t in range(DIL_STEP):
                    r = t * d_prev + r_prev
                    for j in range(n_slab):
                        piece = stage[k - 1][r_prev, j, pl.ds(t, tm // d, stride=DIL_STEP), :]
                        if k + 1 < n_pat:
                            stage[k][r, j] = piece
                        c0 = r * D_ATT + j * LANES
                        outs[k][:, c0:c0 + LANES] = piece.astype(BF16)

    emit(_dot(xn, w_ref[:, 0:D_ATT]), qkv_refs[0], lambda y, j: rot(y, j) * Q_SCALE)
    emit(_dot(xn, w_ref[:, D_ATT:2 * D_ATT]), qkv_refs[1], rot)
    emit(_dot(xn, w_ref[:, 2 * D_ATT:3 * D_ATT]), qkv_refs[2], plain)
    c0 = 3 * D_ATT
    a = _dot(xn, w_ref[:, c0:c0 + D_CONV])
    gt = _dot(xn, w_ref[:, c0 + D_CONV:c0 + 2 * D_CONV])
    u_ref[...] = a * (1.0 / (1.0 + jnp.exp(-gt)))


def _in_proj(x2, g, w_in, cos_t, sa_t, sb_t, seq, tm=512):
    T = x2.shape[0]
    d_in = w_in.shape[1]
    n_s = seq // tm
    n_pat = len(DILATED_PATTERNS)
    row = lambda i: (i, 0)
    tab = lambda i: (i % n_s, 0)
    const = lambda i: (0, 0)
    qkv_specs, qkv_shapes = [], []
    for _ in range(3):
        for _, d in DILATED_PATTERNS:
            qkv_specs.append(pl.BlockSpec((tm // d, d * D_ATT), row))
            qkv_shapes.append(jax.ShapeDtypeStruct((T // d, d * D_ATT), BF16))
    outs = pl.pallas_call(
        functools.partial(_in_proj_kernel, tm=tm),
        grid=(T // tm,),
        in_specs=[pl.BlockSpec((tm, D_MODEL), row),
                  pl.BlockSpec((1, D_MODEL), const),
                  pl.BlockSpec((D_MODEL, d_in), const, pipeline_mode=pl.Buffered(1)),
                  pl.BlockSpec((tm, LANES), tab),
                  pl.BlockSpec((tm, LANES), tab),
                  pl.BlockSpec((tm, LANES), tab)],
        out_specs=qkv_specs + [pl.BlockSpec((tm, D_CONV), row)],
        out_shape=qkv_shapes + [jax.ShapeDtypeStruct((T, D_CONV), F32)],
        scratch_shapes=[pltpu.VMEM((D_MODEL, d_in), BF16)]
                       + [pltpu.VMEM((d, D_ATT // LANES, tm // d, LANES), F32)
                          for _, d in DILATED_PATTERNS[:-1]],
        compiler_params=pltpu.CompilerParams(dimension_semantics=("arbitrary",),
                                             vmem_limit_bytes=VMEM_LIMIT),
        name="in_proj",
    )(x2, g, w_in, cos_t, sa_t, sb_t)
    return outs[0:n_pat], outs[n_pat:2 * n_pat], outs[2 * n_pat:3 * n_pat], outs[3 * n_pat]


def _attn_kernel(q_ref, kc_ref, kp_ref, kn_ref, vc_ref, vp_ref, vn_ref, bias_ref, hm_ref,
                 o_ref, lse_ref, kext, vext, *, lb, nblk_total):
    i = pl.program_id(2)
    kext[0:ATT_HALF] = kp_ref[0]
    kext[ATT_HALF:ATT_HALF + lb] = kc_ref[0]
    kext[ATT_HALF + lb:] = kn_ref[0]
    vext[0:ATT_HALF] = vp_ref[0]
    vext[ATT_HALF:ATT_HALF + lb] = vc_ref[0]
    vext[ATT_HALF + lb:] = vn_ref[0]

    mask_a = hm_ref[0:1, :]
    mask_b = hm_ref[1:2, :]
    nblk = lb // ATT_QB

    def body(n, carry):
        r0 = pl.multiple_of(n * ATT_QB, ATT_QB)
        gblk = i * nblk + n
        bidx = jnp.where(gblk == 0, 0, jnp.where(gblk == nblk_total - 1, 2, 1))
        bias = bias_ref[bidx]
        lse_rows = []
        for hp in range(q_ref.shape[2] // LANES):
            c0 = LANES * hp
            q2 = q_ref[0, pl.ds(r0, ATT_QB), c0:c0 + LANES]
            qs = jnp.concatenate([q2 * mask_a, q2 * mask_b], axis=0)
            parts = []
            for kh in range(ATT_WIN // ATT_QB):
                k0 = r0 + kh * ATT_QB
                kw = kext[pl.ds(k0, ATT_QB), c0:c0 + LANES]
                vw = vext[pl.ds(k0, ATT_QB), c0:c0 + LANES]
                s = _dot_nt(kw, qs) + bias[kh * ATT_QB:(kh + 1) * ATT_QB]
                m = jnp.max(s, axis=0, keepdims=True)
                p = jnp.exp2(s - m)
                l = jnp.sum(p, axis=0, keepdims=True)
                parts.append((m, l, _dot_tn(vw, p.astype(BF16))))
            (m1, l1, o1), (m2, l2, o2) = parts
            m = jnp.maximum(m1, m2)
            a1 = jnp.exp2(m1 - m)
            a2 = jnp.exp2(m2 - m)
            l = a1 * l1 + a2 * l2
            inv = 1.0 / l
            ot = o1 * (a1 * inv) + o2 * (a2 * inv)
            own = jnp.concatenate([ot[0:HEAD_DIM, 0:ATT_QB], ot[HEAD_DIM:, ATT_QB:]], axis=0)
            o_ref[0, pl.ds(r0, ATT_QB), c0:c0 + LANES] = jnp.transpose(own).astype(o_ref.dtype)
            lse = (m + jnp.log2(l)) * LN2
            lse_rows += [lse[:, 0:ATT_QB], lse[:, ATT_QB:]]
        for rc in range(len(lse_rows) // ATT_HEADS):
            rows = lse_rows[rc * ATT_HEADS:(rc + 1) * ATT_HEADS]
            pad = jnp.zeros((LANES - ATT_HEADS, ATT_QB), F32)
            lse_ref[0, pl.ds(r0, ATT_QB), rc * LANES:(rc + 1) * LANES] = jnp.transpose(
                jnp.concatenate(rows + [pad], axis=0))
        return carry

    lax.fori_loop(0, nblk, body, 0, unroll=True)


def _attn_bias():
    i = np.arange(ATT_QB)[:, None]
    j = np.arange(ATT_WIN)[None, :]
    band = (j >= i) & (j <= i + 2 * ATT_HALF)
    first = band & (j >= ATT_HALF)
    last = band & (j < ATT_HALF + ATT_QB)
    tabs = np.stack([first, band, last]).astype(np.float32)
    tabs = np.concatenate([tabs.transpose(0, 2, 1)] * 2, axis=2)
    return jnp.asarray((1.0 - tabs) * NEG_INF, dtype=F32)


def _head_masks():
    lane = np.arange(LANES)
    m = np.zeros((16, LANES), np.float32)
    m[0] = lane < HEAD_DIM
    m[1] = lane >= HEAD_DIM
    return jnp.asarray(m, dtype=BF16)


def _attn(q, k, v, bias, hmask, batch, seq, d, blocks_per_step=16, lb_max=2048):
    L = seq // d
    lb = min(lb_max, L)
    rpb = min(d, max(1, blocks_per_step * ATT_QB // lb))
    width = rpb * D_ATT
    nblk_total = L // ATT_QB
    assert nblk_total >= 2 and L % lb == 0 and lb % ATT_QB == 0 and d % rpb == 0
    view = lambda t: t.reshape(batch, L, d * D_ATT)
    hb = lb // ATT_HALF
    cur = lambda b, r, i: (b, i, r)
    prev = lambda b, r, i: (b, jnp.maximum(i * hb - 1, 0), r)
    nxt = lambda b, r, i: (b, jnp.minimum((i + 1) * hb, L // ATT_HALF - 1), r)
    blk = pl.BlockSpec((1, lb, width), cur)
    halo_p = pl.BlockSpec((1, ATT_HALF, width), prev)
    halo_n = pl.BlockSpec((1, ATT_HALF, width), nxt)
    o, lse = pl.pallas_call(
        functools.partial(_attn_kernel, lb=lb, nblk_total=nblk_total),
        grid=(batch, d // rpb, L // lb),
        in_specs=[blk, blk, halo_p, halo_n, blk, halo_p, halo_n,
                  pl.BlockSpec((3, ATT_WIN, 2 * ATT_QB), lambda b, r, i: (0, 0, 0)),
                  pl.BlockSpec((16, LANES), lambda b, r, i: (0, 0))],
        out_specs=[blk, pl.BlockSpec((1, lb, rpb * LANES), cur)],
        out_shape=[jax.ShapeDtypeStruct((batch, L, d * D_ATT), BF16),
                   jax.ShapeDtypeStruct((batch, L, d * LANES), F32)],
        scratch_shapes=[pltpu.VMEM((lb + 2 * ATT_HALF, width), BF16),
                        pltpu.VMEM((lb + 2 * ATT_HALF, width), BF16)],
        compiler_params=pltpu.CompilerParams(
            dimension_semantics=("arbitrary", "arbitrary", "arbitrary"),
            vmem_limit_bytes=VMEM_LIMIT),
        name=f"attn_d{d}",
    )(view(q), view(k), view(k), view(k), view(v), view(v), view(v), bias, hmask)
    return o.reshape(batch * L, d * D_ATT), lse.reshape(batch * L, d * LANES)


def _conv_kernel(uc_ref, up_ref, un_ref, w_ref, b_ref, lg_ref, lb_ref, c_ref, ext, ybuf, *, ts,
                 rows, ln_rows):
    i = pl.program_id(1)
    n = pl.num_programs(1)
    n_slab = D_CONV // LANES
    for g in range(n_slab):
        ls = slice(g * LANES, (g + 1) * LANES)
        ext[g, 0:CONV_HALO] = jnp.where(i > 0, up_ref[0, :, ls], 0.0)
        ext[g, CONV_HALO:CONV_HALO + ts] = uc_ref[0, :, ls]
        ext[g, CONV_HALO + ts:] = jnp.where(i < n - 1, un_ref[0, :, ls], 0.0)
    lg = lg_ref[...]
    lb = lb_ref[...]
    off = CONV_HALO - CONV_PAD
    n_blk = ts // rows

    def taps(idx, carry):
        g = idx // n_blk
        base = pl.multiple_of((idx % n_blk) * rows, rows)
        acc = jnp.broadcast_to(b_ref[g], (rows, LANES))
        for j in range(SUBLANES):
            steps = [m for m in range((CONV_WIDTH + off) // SUBLANES + 1)
                     if 0 <= SUBLANES * m + j - off < CONV_WIDTH]
            win = ext[g, pl.ds(base + j, rows + SUBLANES * max(steps)), :]
            for m in steps:
                k = SUBLANES * m + j - off
                acc = acc + win[SUBLANES * m:SUBLANES * m + rows] * w_ref[g, k:k + 1, :]
        ybuf[g, pl.ds(base, rows), :] = acc
        return carry

    lax.fori_loop(0, n_slab * n_blk, taps, 0, unroll=2)
    for r0 in range(0, ts, ln_rows):
        acc = jnp.concatenate([ybuf[g, r0:r0 + ln_rows, :] for g in range(n_slab)], axis=1)
        mu = jnp.mean(acc, axis=-1, keepdims=True)
        cen = acc - mu
        var = jnp.mean(cen * cen, axis=-1, keepdims=True)
        y = cen * lax.rsqrt(var + EPS) * lg + lb
        c_ref[0, r0:r0 + ln_rows, :] = (y * (1.0 / (1.0 + jnp.exp(-y)))).astype(c_ref.dtype)


def _conv(u3, conv_w, conv_b, ln_g, ln_b, ts=512, rows=128, ln_rows=32):
    B, S, C = u3.shape
    hb = ts // CONV_HALO
    cur = lambda b, i: (b, i, 0)
    prev = lambda b, i: (b, jnp.maximum(i * hb - 1, 0), 0)
    nxt = lambda b, i: (b, jnp.minimum((i + 1) * hb, S // CONV_HALO - 1), 0)
    const = lambda b, i: (0, 0)
    const3 = lambda b, i: (0, 0, 0)
    n_slab = C // LANES
    w_slab = jnp.transpose(conv_w.reshape(CONV_WIDTH, n_slab, LANES), (1, 0, 2))
    b_slab = conv_b.reshape(n_slab, 1, LANES)
    return pl.pallas_call(
        functools.partial(_conv_kernel, ts=ts, rows=rows, ln_rows=ln_rows),
        grid=(B, S // ts),
        in_specs=[pl.BlockSpec((1, ts, C), cur),
                  pl.BlockSpec((1, CONV_HALO, C), prev),
                  pl.BlockSpec((1, CONV_HALO, C), nxt),
                  pl.BlockSpec((n_slab, CONV_WIDTH, LANES), const3),
                  pl.BlockSpec((n_slab, 1, LANES), const3),
                  pl.BlockSpec((1, C), const),
                  pl.BlockSpec((1, C), const)],
        out_specs=pl.BlockSpec((1, ts, C), cur),
        out_shape=jax.ShapeDtypeStruct((B, S, C), BF16),
        scratch_shapes=[pltpu.VMEM((C // LANES, ts + 2 * CONV_HALO, LANES), F32),
                        pltpu.VMEM((C // LANES, ts, LANES), F32)],
        compiler_params=pltpu.CompilerParams(dimension_semantics=("arbitrary", "arbitrary"),
                                             vmem_limit_bytes=VMEM_LIMIT),
        name="conv",
    )(u3, u3, u3, w_slab, b_slab, ln_g, ln_b)


def _mem_kv_kernel(mem_ref, g_ref, wk_ref, wv_ref, k_ref, v_ref):
    mn = _rms(mem_ref[...], g_ref[...]).astype(BF16)
    k_ref[...] = _dot(mn, wk_ref[...].astype(BF16)).astype(BF16)
    v_ref[...] = _dot(mn, wv_ref[...].astype(BF16)).astype(BF16)


def _mem_kv(mem2, g, wk, wv):
    R = mem2.shape[0]
    full = lambda shape: pl.BlockSpec(shape, lambda i: (0, 0))
    return pl.pallas_call(
        _mem_kv_kernel,
        grid=(1,),
        in_specs=[full((R, D_MODEL)), full((1, D_MODEL)),
                  full((D_MODEL, D_MODEL)), full((D_MODEL, D_MODEL))],
        out_specs=[full((R, D_MODEL))] * 2,
        out_shape=[jax.ShapeDtypeStruct((R, D_MODEL), BF16)] * 2,
        compiler_params=pltpu.CompilerParams(dimension_semantics=("arbitrary",),
                                             vmem_limit_bytes=VMEM_LIMIT),
        name="mem_kv",
    )(mem2, g, wk, wv)


def _mix_xattn_kernel(x_ref, o1_ref, o2_ref, o3_ref, l1_ref, l2_ref, l3_ref, c_ref, e_ref,
                      wof_ref, gx_ref, wqf_ref, xk_ref, xv_ref, wxof_ref, wuf_ref, wdf_ref,
                      h_ref, wub_ref, wdb_ref, wo_ref, wq_ref, wxo_ref, *bufs, tm, sub):
    _cast_once([(wof_ref, wo_ref), (wqf_ref, wq_ref), (wxof_ref, wxo_ref)])
    wub_ref[...] = wuf_ref[...].astype(BF16)
    wdb_ref[...] = wdf_ref[...].astype(BF16)
    n_slab = D_ATT // LANES
    n_pat = len(DILATED_PATTERNS)
    o_refs = (o1_ref, o2_ref, o3_ref)
    l_refs = (l1_ref, l2_ref, l3_ref)
    n_grp = tm // sub

    def mix(g):
        t0 = g * sub
        obuf, lbuf = bufs[2 * g], bufs[2 * g + 1]
        otmp, ltmp = bufs[2 * n_grp + 2 * g], bufs[2 * n_grp + 2 * g + 1]
        for p, (_, d) in enumerate(DILATED_PATTERNS):
            if d == 1:
                continue
            src = slice(t0 // d, (t0 + sub) // d)
            jobs = [(o_refs[p], D_ATT, j, obuf, otmp, True) for j in range(n_slab)]
            jobs.append((l_refs[p], LANES, 0, lbuf, ltmp, False))
            for ref, pitch, j, dst, tmp, widen in jobs:
                pieces = {}
                for r in range(d):
                    t = ref[src, r * pitch + j * LANES:r * pitch + (j + 1) * LANES]
                    pieces[r] = t.astype(F32) if widen else t
                dd = d
                while dd > DIL_STEP:
                    lower = dd // DIL_STEP
                    merged = {}
                    for r_low in range(lower):
                        for t in range(DIL_STEP):
                            tmp[r_low, j, pl.ds(t, sub // dd, stride=DIL_STEP), :] = (
                                pieces[t * lower + r_low])
                        merged[r_low] = tmp[r_low, j, 0:sub // lower, :]
                    pieces, dd = merged, lower
                for r in range(dd):
                    dst[p, j, pl.ds(r, sub // dd, stride=DIL_STEP), :] = pieces[r]

        ls = [l_refs[p][t0:t0 + sub, :] if d == 1 else lbuf[p, 0]
              for p, (_, d) in enumerate(DILATED_PATTERNS)]
        m = jnp.maximum(jnp.maximum(ls[0], ls[1]), ls[2])
        es = [jnp.exp(l - m) for l in ls]
        inv = 1.0 / (es[0] + es[1] + es[2])
        ws = [_dot((e * inv).astype(BF16), e_ref[...]) for e in es]
        att = []
        for j in range(n_slab):
            lanes = slice(j * LANES, (j + 1) * LANES)
            os_ = [(o_refs[p][t0:t0 + sub, lanes] if d == 1 else obuf[p, j]).astype(F32)
                   for p, (_, d) in enumerate(DILATED_PATTERNS)]
            att.append((ws[0][:, lanes] * os_[0] + ws[1][:, lanes] * os_[1]
                        + ws[2][:, lanes] * os_[2]).astype(BF16))
        return jnp.concatenate(att, axis=1)

    def project(g, att):
        tr = slice(g * sub, (g + 1) * sub)
        h1 = (x_ref[tr, :] + _dot(att, wo_ref[0:D_ATT, :])
              + _dot(c_ref[tr, :], wo_ref[D_ATT:, :]))
        xq = (_dot(_rms(h1, gx_ref[...]).astype(BF16), wq_ref[...])
              * (XATT_HEAD_DIM ** -0.5)).astype(BF16)
        return h1, xq

    def cross(g, h1, xq):
        heads = []
        for h in range(XATT_HEADS):
            sl = slice(h * XATT_HEAD_DIM, (h + 1) * XATT_HEAD_DIM)
            s = _dot_nt(xq[:, sl], xk_ref[0, :, sl])
            mx = jnp.max(s, axis=-1, keepdims=True)
            p = jnp.exp(s - mx)
            den = jnp.sum(p, axis=-1, keepdims=True)
            heads.append((_dot(p.astype(BF16), xv_ref[0, :, sl]) * (1.0 / den)).astype(BF16))
        xo = jnp.concatenate(heads, axis=1)
        h_ref[g * sub:(g + 1) * sub, :] = h1 + _dot(xo, wxo_ref[...])

    for g in range(n_grp):
        h1, xq = project(g, mix(g))
        cross(g, h1, xq)


def _mix_xattn(x2, os_, lses, c2, w_out, gx, w_xq, xk, xv, w_xo, w_up, w_down, seq, tm=512,
               sub=256):
    T = x2.shape[0]
    n_mem = xk.shape[1]
    per_b = seq // tm
    n_steps = T // tm
    up_rows, dn_rows = w_up.shape[0] // n_steps, w_down.shape[0] // n_steps
    n_slab = D_ATT // LANES
    d_max = DILATED_PATTERNS[-1][1]
    row = lambda i: (i, 0)
    const = lambda i: (0, 0)
    memb = lambda i: (i // per_b, 0, 0)
    sq = pl.BlockSpec((D_MODEL, D_MODEL), const, pipeline_mode=pl.Buffered(1))
    n_pat = len(DILATED_PATTERNS)
    return pl.pallas_call(
        functools.partial(_mix_xattn_kernel, tm=tm, sub=sub),
        grid=(T // tm,),
        in_specs=[pl.BlockSpec((tm, D_MODEL), row)]
                 + [pl.BlockSpec((tm // d, d * D_ATT), row) for _, d in DILATED_PATTERNS]
                 + [pl.BlockSpec((tm // d, d * LANES), row) for _, d in DILATED_PATTERNS]
                 + [pl.BlockSpec((tm, D_CONV), row),
                    pl.BlockSpec((LANES, D_ATT), const),
                    sq, pl.BlockSpec((1, D_MODEL), const), sq,
                    pl.BlockSpec((1, n_mem, D_MODEL), memb),
                    pl.BlockSpec((1, n_mem, D_MODEL), memb),
                    sq,
                    pl.BlockSpec((up_rows, w_up.shape[1]), row),
                    pl.BlockSpec((dn_rows, w_down.shape[1]), row)],
        out_specs=[pl.BlockSpec((tm, D_MODEL), row),
                   pl.BlockSpec((up_rows, w_up.shape[1]), row),
                   pl.BlockSpec((dn_rows, w_down.shape[1]), row)],
        out_shape=[jax.ShapeDtypeStruct((T, D_MODEL), F32),
                   jax.ShapeDtypeStruct(w_up.shape, BF16),
                   jax.ShapeDtypeStruct(w_down.shape, BF16)],
        scratch_shapes=[pltpu.VMEM((D_MODEL, D_MODEL), BF16)] * 3
                       + [pltpu.VMEM((n_pat, n_slab, sub, LANES), F32),
                          pltpu.VMEM((n_pat, 1, sub, LANES), F32)] * (tm // sub)
                       + [pltpu.VMEM((d_max // DIL_STEP, n_slab, sub // DIL_STEP, LANES), F32),
                          pltpu.VMEM((d_max // DIL_STEP, 1, sub // DIL_STEP, LANES), F32)
                          ] * (tm // sub),
        compiler_params=pltpu.CompilerParams(dimension_semantics=("arbitrary",),
                                             vmem_limit_bytes=VMEM_LIMIT),
        name="mix_xattn",
    )(x2, *os_, *lses, c2, _head_expand(), w_out, gx, w_xq, xk, xv, w_xo, w_up, w_down)


def _head_expand():
    e = np.zeros((LANES, D_ATT), np.float32)
    for h in range(ATT_HEADS):
        e[h, h * HEAD_DIM:(h + 1) * HEAD_DIM] = 1.0
    return jnp.asarray(e, dtype=BF16)


def _mlp_kernel(h_ref, g_ref, wu_ref, wd_ref, gf_ref, out_ref, *, chunk, final_norm):
    h = h_ref[...]
    hn = _rms(h, g_ref[...]).astype(BF16)
    acc = h
    for j in range(D_FF // chunk):
        u = jnp.maximum(_dot(hn, wu_ref[:, j * chunk:(j + 1) * chunk]), 0.0)
        acc = acc + _dot((u * u).astype(BF16), wd_ref[j * chunk:(j + 1) * chunk, :])
    out_ref[...] = _rms(acc, gf_ref[...]) if final_norm else acc


def _mlp(h2, g, w_up, w_down, gf, final_norm, tm=1024, chunk=1024):
    T = h2.shape[0]
    row = lambda i: (i, 0)
    const = lambda i: (0, 0)
    return pl.pallas_call(
        functools.partial(_mlp_kernel, chunk=chunk, final_norm=final_norm),
        grid=(T // tm,),
        in_specs=[pl.BlockSpec((tm, D_MODEL), row),
                  pl.BlockSpec((1, D_MODEL), const),
                  pl.BlockSpec((D_MODEL, D_FF), const, pipeline_mode=pl.Buffered(1)),
                  pl.BlockSpec((D_FF, D_MODEL), const, pipeline_mode=pl.Buffered(1)),
                  pl.BlockSpec((1, D_MODEL), const)],
        out_specs=pl.BlockSpec((tm, D_MODEL), row),
        out_shape=jax.ShapeDtypeStruct((T, D_MODEL), F32),
        compiler_params=pltpu.CompilerParams(dimension_semantics=("arbitrary",),
                                             vmem_limit_bytes=VMEM_LIMIT),
        name="mlp",
    )(h2, g, w_up, w_down, gf)


def _rotary_tables(seq):
    half = ROT_DIM // 2
    freqs = ROPE_THETA ** (-jnp.arange(0, ROT_DIM, 2, dtype=F32) / ROT_DIM)
    ang = jnp.arange(seq, dtype=F32)[:, None] * freqs[None, :]
    cos, sin = jnp.cos(ang), jnp.sin(ang)
    zeros = jnp.zeros((seq, HEAD_DIM - ROT_DIM), F32)
    ones = jnp.ones((seq, HEAD_DIM - ROT_DIM), F32)
    z8 = jnp.zeros((seq, half), F32)
    rep = LANES // HEAD_DIM
    return (jnp.concatenate([cos, cos, ones] * rep, axis=1),
            jnp.concatenate([-sin, z8, zeros] * rep, axis=1),
            jnp.concatenate([z8, sin, zeros] * rep, axis=1))


def kernel(x, mem, norm_mix_g, w_in, conv_w, conv_b, conv_ln_g, conv_ln_b, w_out, norm_x_g,
           norm_mem_g, w_xq, w_xk, w_xv, w_xo, norm_mlp_g, w_up, w_down, norm_final_g):
    B, S, D = x.shape
    n_mem = mem.shape[1]
    depth = w_in.shape[0]
    T = B * S
    cos_t, sa_t, sb_t = _rotary_tables(S)
    bias = _attn_bias()
    hmask = _head_masks()
    row = lambda g: g.reshape(1, -1)

    h = x.reshape(T, D)
    for l in range(depth):
        q, k, v, u = _in_proj(h, row(norm_mix_g[l]), w_in[l], cos_t, sa_t, sb_t, S)
        os_, lses = [], []
        for p, (_, d) in enumerate(DILATED_PATTERNS):
            o, lse = _attn(q[p], k[p], v[p], bias, hmask, B, S, d)
            os_.append(o)
            lses.append(lse)
        c = _conv(u.reshape(B, S, D_CONV), conv_w[l], row(conv_b[l]), row(conv_ln_g[l]),
                  row(conv_ln_b[l])).reshape(T, D_CONV)
        xk, xv = _mem_kv(mem.reshape(B * n_mem, D), row(norm_mem_g[l]),
                         w_xk[l], w_xv[l])
        h, w_up_b, w_down_b = _mix_xattn(h, os_, lses, c, w_out[l], row(norm_x_g[l]), w_xq[l],
                                         xk.reshape(B, n_mem, D), xv.reshape(B, n_mem, D),
                                         w_xo[l], w_up[l], w_down[l], S)
        h = _mlp(h, row(norm_mlp_g[l]), w_up_b, w_down_b, row(norm_final_g),
                 final_norm=(l == depth - 1))
    return h.reshape(B, S, D)
```

```python
import functools
import math

import numpy as np
import jax
import jax.numpy as jnp
from jax import lax
from jax.experimental import pallas as pl
from jax.experimental.pallas import tpu as pltpu

F32 = jnp.float32
BF16 = jnp.bfloat16

D_MODEL = 1024
ATT_HEADS = 8
HEAD_DIM = 64
D_ATT = ATT_HEADS * HEAD_DIM
D_CONV = D_MODEL - D_ATT
DILATED_PATTERNS = ((128, 1), (512, 4), (2048, 16))
DIL_STEP = 4
assert all(d == DIL_STEP ** k for k, (_, d) in enumerate(DILATED_PATTERNS))
ROPE_THETA = 500000.0
ROT_DIM = HEAD_DIM // 4
CONV_WIDTH = 31
CONV_PAD = (CONV_WIDTH - 1) // 2
XATT_HEADS = 4
XATT_HEAD_DIM = D_MODEL // XATT_HEADS
D_FF = 4 * D_MODEL
EPS = 1e-6
NEG_INF = -1e30
LN2 = math.log(2.0)
Q_SCALE = HEAD_DIM ** -0.5 / LN2

LANES = 128
SUBLANES = 8
ATT_HALF = 64
ATT_QB = 2 * ATT_HALF
ATT_WIN = ATT_QB + 2 * ATT_HALF
CONV_HALO = 16
VMEM_LIMIT = 56 * 1024 * 1024


def _dot(a, b):
    return jnp.dot(a, b, preferred_element_type=F32)


def _dot_nt(a, b):
    return lax.dot_general(a, b, (((1,), (1,)), ((), ())), preferred_element_type=F32)


def _dot_tn(a, b):
    return lax.dot_general(a, b, (((0,), (0,)), ((), ())), preferred_element_type=F32)


def _rms(x, g):
    var = jnp.mean(x * x, axis=-1, keepdims=True)
    return x * lax.rsqrt(var + EPS) * g


def _cast_once(pairs, chunk=512):
    @pl.when(pl.program_id(0) == 0)
    def _():
        for src, dst in pairs:
            for c in range(0, src.shape[1], chunk):
                dst[:, c:c + chunk] = src[:, c:c + chunk].astype(BF16)


def _in_proj_kernel(x_ref, g_ref, wf_ref, cos_ref, sa_ref, sb_ref, *refs, tm):
    n_pat = len(DILATED_PATTERNS)
    qkv_refs = [refs[a * n_pat:(a + 1) * n_pat] for a in range(3)]
    u_ref = refs[3 * n_pat]
    w_ref = refs[3 * n_pat + 1]
    stage = refs[3 * n_pat + 2:]
    _cast_once([(wf_ref, w_ref)])
    xn = _rms(x_ref[...], g_ref[...]).astype(BF16)
    cos, sa, sb = cos_ref[...], sa_ref[...], sb_ref[...]
    n_slab = D_ATT // LANES

    def plain(y, j):
        return y[:, LANES * j:LANES * (j + 1)]

    def rot(y, j):
        yj = plain(y, j)
        return (yj * cos + pltpu.roll(yj, LANES - ROT_DIM // 2, 1) * sa
                + pltpu.roll(yj, ROT_DIM // 2, 1) * sb)

    def emit(y, outs, transform):
        for j in range(n_slab):
            piece = transform(y, j)
            stage[0][0, j] = piece
            outs[0][:, j * LANES:(j + 1) * LANES] = piece.astype(BF16)
        for k in range(1, n_pat):
            d_prev, d = DILATED_PATTERNS[k - 1][1], DILATED_PATTERNS[k][1]
            for r_prev in range(d_prev):
                for t in range(DIL_STEP):
                    r = t * d_prev + r_prev
                    for j in range(n_slab):
                        piece = stage[k - 1][r_prev, j, pl.ds(t, tm // d, stride=DIL_STEP), :]
                        if k + 1 < n_pat:
                            stage[k][r, j] = piece
                        c0 = r * D_ATT + j * LANES
                        outs[k][:, c0:c0 + LANES] = piece.astype(BF16)

    emit(_dot(xn, w_ref[:, 0:D_ATT]), qkv_refs[0], lambda y, j: rot(y, j) * Q_SCALE)
    emit(_dot(xn, w_ref[:, D_ATT:2 * D_ATT]), qkv_refs[1], rot)
    emit(_dot(xn, w_ref[:, 2 * D_ATT:3 * D_ATT]), qkv_refs[2], plain)
    c0 = 3 * D_ATT
    a = _dot(xn, w_ref[:, c0:c0 + D_CONV])
    gt = _dot(xn, w_ref[:, c0 + D_CONV:c0 + 2 * D_CONV])
    u_ref[...] = a * (1.0 / (1.0 + jnp.exp(-gt)))


def _in_proj(x2, g, w_in, cos_t, sa_t, sb_t, seq, tm=512):
    T = x2.shape[0]
    d_in = w_in.shape[1]
    n_s = seq // tm
    n_pat = len(DILATED_PATTERNS)
    row = lambda i: (i, 0)
    tab = lambda i: (i % n_s, 0)
    const = lambda i: (0, 0)
    qkv_specs, qkv_shapes = [], []
    for _ in range(3):
        for _, d in DILATED_PATTERNS:
            qkv_specs.append(pl.BlockSpec((tm // d, d * D_ATT), row))
            qkv_shapes.append(jax.ShapeDtypeStruct((T // d, d * D_ATT), BF16))
    outs = pl.pallas_call(
        functools.partial(_in_proj_kernel, tm=tm),
        grid=(T // tm,),
        in_specs=[pl.BlockSpec((tm, D_MODEL), row),
                  pl.BlockSpec((1, D_MODEL), const),
                  pl.BlockSpec((D_MODEL, d_in), const, pipeline_mode=pl.Buffered(1)),
                  pl.BlockSpec((tm, LANES), tab),
                  pl.BlockSpec((tm, LANES), tab),
                  pl.BlockSpec((tm, LANES), tab)],
        out_specs=qkv_specs + [pl.BlockSpec((tm, D_CONV), row)],
        out_shape=qkv_shapes + [jax.ShapeDtypeStruct((T, D_CONV), F32)],
        scratch_shapes=[pltpu.VMEM((D_MODEL, d_in), BF16)]
                       + [pltpu.VMEM((d, D_ATT // LANES, tm // d, LANES), F32)
                          for _, d in DILATED_PATTERNS[:-1]],
        compiler_params=pltpu.CompilerParams(dimension_semantics=("arbitrary",),
                                             vmem_limit_bytes=VMEM_LIMIT),
        name="in_proj",
    )(x2, g, w_in, cos_t, sa_t, sb_t)
    return outs[0:n_pat], outs[n_pat:2 * n_pat], outs[2 * n_pat:3 * n_pat], outs[3 * n_pat]


def _attn_kernel(q_ref, kc_ref, kp_ref, kn_ref, vc_ref, vp_ref, vn_ref, bias_ref, hm_ref,
                 o_ref, lse_ref, kext, vext, *, lb, nblk_total):
    i = pl.program_id(2)
    kext[0:ATT_HALF] = kp_ref[0]
    kext[ATT_HALF:ATT_HALF + lb] = kc_ref[0]
    kext[ATT_HALF + lb:] = kn_ref[0]
    vext[0:ATT_HALF] = vp_ref[0]
    vext[ATT_HALF:ATT_HALF + lb] = vc_ref[0]
    vext[ATT_HALF + lb:] = vn_ref[0]

    mask_a = hm_ref[0:1, :]
    mask_b = hm_ref[1:2, :]
    nblk = lb // ATT_QB

    def body(n, carry):
        r0 = pl.multiple_of(n * ATT_QB, ATT_QB)
        gblk = i * nblk + n
        bidx = jnp.where(gblk == 0, 0, jnp.where(gblk == nblk_total - 1, 2, 1))
        bias = bias_ref[bidx]
        lse_rows = []
        for hp in range(q_ref.shape[2] // LANES):
            c0 = LANES * hp
            q2 = q_ref[0, pl.ds(r0, ATT_QB), c0:c0 + LANES]
            qs = jnp.concatenate([q2 * mask_a, q2 * mask_b], axis=0)
            parts = []
            for kh in range(ATT_WIN // ATT_QB):
                k0 = r0 + kh * ATT_QB
                kw = kext[pl.ds(k0, ATT_QB), c0:c0 + LANES]
                vw = vext[pl.ds(k0, ATT_QB), c0:c0 + LANES]
                s = _dot_nt(kw, qs) + bias[kh * ATT_QB:(kh + 1) * ATT_QB]
                m = jnp.max(s, axis=0, keepdims=True)
                p = jnp.exp2(s - m)
                l = jnp.sum(p, axis=0, keepdims=True)
                parts.append((m, l, _dot_tn(vw, p.astype(BF16))))
            (m1, l1, o1), (m2, l2, o2) = parts
            m = jnp.maximum(m1, m2)
            a1 = jnp.exp2(m1 - m)
            a2 = jnp.exp2(m2 - m)
            l = a1 * l1 + a2 * l2
            inv = 1.0 / l
            ot = o1 * (a1 * inv) + o2 * (a2 * inv)
            own = jnp.concatenate([ot[0:HEAD_DIM, 0:ATT_QB], ot[HEAD_DIM:, ATT_QB:]], axis=0)
            o_ref[0, pl.ds(r0, ATT_QB), c0:c0 + LANES] = jnp.transpose(own).astype(o_ref.dtype)
            lse = (m + jnp.log2(l)) * LN2
            lse_rows += [lse[:, 0:ATT_QB], lse[:, ATT_QB:]]
        for rc in range(len(lse_rows) // ATT_HEADS):
            rows = lse_rows[rc * ATT_HEADS:(rc + 1) * ATT_HEADS]
            pad = jnp.zeros((LANES - ATT_HEADS, ATT_QB), F32)
            lse_ref[0, pl.ds(r0, ATT_QB), rc * LANES:(rc + 1) * LANES] = jnp.transpose(
                jnp.concatenate(rows + [pad], axis=0))
        return carry

    lax.fori_loop(0, nblk, body, 0, unroll=True)


def _attn_bias():
    i = np.arange(ATT_QB)[:, None]
    j = np.arange(ATT_WIN)[None, :]
    band = (j >= i) & (j <= i + 2 * ATT_HALF)
    first = band & (j >= ATT_HALF)
    last = band & (j < ATT_HALF + ATT_QB)
    tabs = np.stack([first, band, last]).astype(np.float32)
    tabs = np.concatenate([tabs.transpose(0, 2, 1)] * 2, axis=2)
    return jnp.asarray((1.0 - tabs) * NEG_INF, dtype=F32)


def _head_masks():
    lane = np.arange(LANES)
    m = np.zeros((16, LANES), np.float32)
    m[0] = lane < HEAD_DIM
    m[1] = lane >= HEAD_DIM
    return jnp.asarray(m, dtype=BF16)


def _attn(q, k, v, bias, hmask, batch, seq, d, blocks_per_step=16, lb_max=2048):
    L = seq // d
    lb = min(lb_max, L)
    rpb = min(d, max(1, blocks_per_step * ATT_QB // lb))
    width = rpb * D_ATT
    nblk_total = L // ATT_QB
    assert nblk_total >= 2 and L % lb == 0 and lb % ATT_QB == 0 and d % rpb == 0
    view = lambda t: t.reshape(batch, L, d * D_ATT)
    hb = lb // ATT_HALF
    cur = lambda b, r, i: (b, i, r)
    prev = lambda b, r, i: (b, jnp.maximum(i * hb - 1, 0), r)
    nxt = lambda b, r, i: (b, jnp.minimum((i + 1) * hb, L // ATT_HALF - 1), r)
    blk = pl.BlockSpec((1, lb, width), cur)
    halo_p = pl.BlockSpec((1, ATT_HALF, width), prev)
    halo_n = pl.BlockSpec((1, ATT_HALF, width), nxt)
    o, lse = pl.pallas_call(
        functools.partial(_attn_kernel, lb=lb, nblk_total=nblk_total),
        grid=(batch, d // rpb, L // lb),
        in_specs=[blk, blk, halo_p, halo_n, blk, halo_p, halo_n,
                  pl.BlockSpec((3, ATT_WIN, 2 * ATT_QB), lambda b, r, i: (0, 0, 0)),
                  pl.BlockSpec((16, LANES), lambda b, r, i: (0, 0))],
        out_specs=[blk, pl.BlockSpec((1, lb, rpb * LANES), cur)],
        out_shape=[jax.ShapeDtypeStruct((batch, L, d * D_ATT), BF16),
                   jax.ShapeDtypeStruct((batch, L, d * LANES), F32)],
        scratch_shapes=[pltpu.VMEM((lb + 2 * ATT_HALF, width), BF16),
                        pltpu.VMEM((lb + 2 * ATT_HALF, width), BF16)],
        compiler_params=pltpu.CompilerParams(
            dimension_semantics=("arbitrary", "arbitrary", "arbitrary"),
            vmem_limit_bytes=VMEM_LIMIT),
        name=f"attn_d{d}",
    )(view(q), view(k), view(k), view(k), view(v), view(v), view(v), bias, hmask)
    return o.reshape(batch * L, d * D_ATT), lse.reshape(batch * L, d * LANES)


def _conv_kernel(uc_ref, up_ref, un_ref, w_ref, b_ref, lg_ref, lb_ref, c_ref, ext, ybuf, *, ts,
                 rows, ln_rows):
    i = pl.program_id(1)
    n = pl.num_programs(1)
    n_slab = D_CONV // LANES
    for g in range(n_slab):
        ls = slice(g * LANES, (g + 1) * LANES)
        ext[g, 0:CONV_HALO] = jnp.where(i > 0, up_ref[0, :, ls], 0.0)
        ext[g, CONV_HALO:CONV_HALO + ts] = uc_ref[0, :, ls]
        ext[g, CONV_HALO + ts:] = jnp.where(i < n - 1, un_ref[0, :, ls], 0.0)
    lg = lg_ref[...]
    lb = lb_ref[...]
    off = CONV_HALO - CONV_PAD
    n_blk = ts // rows

    def taps(idx, carry):
        g = idx // n_blk
        base = pl.multiple_of((idx % n_blk) * rows, rows)
        acc = jnp.broadcast_to(b_ref[g], (rows, LANES))
        for j in range(SUBLANES):
            steps = [m for m in range((CONV_WIDTH + off) // SUBLANES + 1)
                     if 0 <= SUBLANES * m + j - off < CONV_WIDTH]
            win = ext[g, pl.ds(base + j, rows + SUBLANES * max(steps)), :]
            for m in steps:
                k = SUBLANES * m + j - off
                acc = acc + win[SUBLANES * m:SUBLANES * m + rows] * w_ref[g, k:k + 1, :]
        ybuf[g, pl.ds(base, rows), :] = acc
        return carry

    lax.fori_loop(0, n_slab * n_blk, taps, 0, unroll=2)
    for r0 in range(0, ts, ln_rows):
        acc = jnp.concatenate([ybuf[g, r0:r0 + ln_rows, :] for g in range(n_slab)], axis=1)
        mu = jnp.mean(acc, axis=-1, keepdims=True)
        cen = acc - mu
        var = jnp.mean(cen * cen, axis=-1, keepdims=True)
        y = cen * lax.rsqrt(var + EPS) * lg + lb
        c_ref[0, r0:r0 + ln_rows, :] = (y * (1.0 / (1.0 + jnp.exp(-y)))).astype(c_ref.dtype)


def _conv(u3, conv_w, conv_b, ln_g, ln_b, ts=512, rows=128, ln_rows=32):
    B, S, C = u3.shape
    hb = ts // CONV_HALO
    cur = lambda b, i: (b, i, 0)
    prev = lambda b, i: (b, jnp.maximum(i * hb - 1, 0), 0)
    nxt = lambda b, i: (b, jnp.minimum((i + 1) * hb, S // CONV_HALO - 1), 0)
    const = lambda b, i: (0, 0)
    const3 = lambda b, i: (0, 0, 0)
    n_slab = C // LANES
    w_slab = jnp.transpose(conv_w.reshape(CONV_WIDTH, n_slab, LANES), (1, 0, 2))
    b_slab = conv_b.reshape(n_slab, 1, LANES)
    return pl.pallas_call(
        functools.partial(_conv_kernel, ts=ts, rows=rows, ln_rows=ln_rows),
        grid=(B, S // ts),
        in_specs=[pl.BlockSpec((1, ts, C), cur),
                  pl.BlockSpec((1, CONV_HALO, C), prev),
                  pl.BlockSpec((1, CONV_HALO, C), nxt),
                  pl.BlockSpec((n_slab, CONV_WIDTH, LANES), const3),
                  pl.BlockSpec((n_slab, 1, LANES), const3),
                  pl.BlockSpec((1, C), const),
                  pl.BlockSpec((1, C), const)],
        out_specs=pl.BlockSpec((1, ts, C), cur),
        out_shape=jax.ShapeDtypeStruct((B, S, C), BF16),
        scratch_shapes=[pltpu.VMEM((C // LANES, ts + 2 * CONV_HALO, LANES), F32),
                        pltpu.VMEM((C // LANES, ts, LANES), F32)],
        compiler_params=pltpu.CompilerParams(dimension_semantics=("arbitrary", "arbitrary"),
                                             vmem_limit_bytes=VMEM_LIMIT),
        name="conv",
    )(u3, u3, u3, w_slab, b_slab, ln_g, ln_b)


def _mem_kv_kernel(mem_ref, g_ref, wk_ref, wv_ref, k_ref, v_ref):
    mn = _rms(mem_ref[...], g_ref[...]).astype(BF16)
    k_ref[...] = _dot(mn, wk_ref[...].astype(BF16)).astype(BF16)
    v_ref[...] = _dot(mn, wv_ref[...].astype(BF16)).astype(BF16)


def _mem_kv(mem2, g, wk, wv):
    R = mem2.shape[0]
    full = lambda shape: pl.BlockSpec(shape, lambda i: (0, 0))
    return pl.pallas_call(
        _mem_kv_kernel,
        grid=(1,),
        in_specs=[full((R, D_MODEL)), full((1, D_MODEL)),
                  full((D_MODEL, D_MODEL)), full((D_MODEL, D_MODEL))],
        out_specs=[full((R, D_MODEL))] * 2,
        out_shape=[jax.ShapeDtypeStruct((R, D_MODEL), BF16)] * 2,
        compiler_params=pltpu.CompilerParams(dimension_semantics=("arbitrary",),
                                             vmem_limit_bytes=VMEM_LIMIT),
        name="mem_kv",
    )(mem2, g, wk, wv)


def _mix_xattn_kernel(x_ref, o1_ref, o2_ref, o3_ref, l1_ref, l2_ref, l3_ref, c_ref, e_ref,
                      wof_ref, gx_ref, wqf_ref, xk_ref, xv_ref, wxof_ref, wuf_ref, wdf_ref,
                      h_ref, wub_ref, wdb_ref, wo_ref, wq_ref, wxo_ref, *bufs, tm, sub):
    _cast_once([(wof_ref, wo_ref), (wqf_ref, wq_ref), (wxof_ref, wxo_ref)])
    wub_ref[...] = wuf_ref[...].astype(BF16)
    wdb_ref[...] = wdf_ref[...].astype(BF16)
    n_slab = D_ATT // LANES
    n_pat = len(DILATED_PATTERNS)
    o_refs = (o1_ref, o2_ref, o3_ref)
    l_refs = (l1_ref, l2_ref, l3_ref)
    n_grp = tm // sub

    def mix(g):
        t0 = g * sub
        obuf, lbuf = bufs[2 * g], bufs[2 * g + 1]
        otmp, ltmp = bufs[2 * n_grp + 2 * g], bufs[2 * n_grp + 2 * g + 1]
        for p, (_, d) in enumerate(DILATED_PATTERNS):
            if d == 1:
                continue
            src = slice(t0 // d, (t0 + sub) // d)
            jobs = [(o_refs[p], D_ATT, j, obuf, otmp, True) for j in range(n_slab)]
            jobs.append((l_refs[p], LANES, 0, lbuf, ltmp, False))
            for ref, pitch, j, dst, tmp, widen in jobs:
                pieces = {}
                for r in range(d):
                    t = ref[src, r * pitch + j * LANES:r * pitch + (j + 1) * LANES]
                    pieces[r] = t.astype(F32) if widen else t
                dd = d
                while dd > DIL_STEP:
                    lower = dd // DIL_STEP
                    merged = {}
                    for r_low in range(lower):
                        for t in range(DIL_STEP):
                            tmp[r_low, j, pl.ds(t, sub // dd, stride=DIL_STEP), :] = (
                                pieces[t * lower + r_low])
                        merged[r_low] = tmp[r_low, j, 0:sub // lower, :]
                    pieces, dd = merged, lower
                for r in range(dd):
                    dst[p, j, pl.ds(r, sub // dd, stride=DIL_STEP), :] = pieces[r]

        ls = [l_refs[p][t0:t0 + sub, :] if d == 1 else lbuf[p, 0]
              for p, (_, d) in enumerate(DILATED_PATTERNS)]
        m = jnp.maximum(jnp.maximum(ls[0], ls[1]), ls[2])
        es = [jnp.exp(l - m) for l in ls]
        inv = 1.0 / (es[0] + es[1] + es[2])
        ws = [_dot((e * inv).astype(BF16), e_ref[...]) for e in es]
        att = []
        for j in range(n_slab):
            lanes = slice(j * LANES, (j + 1) * LANES)
            os_ = [(o_refs[p][t0:t0 + sub, lanes] if d == 1 else obuf[p, j]).astype(F32)
                   for p, (_, d) in enumerate(DILATED_PATTERNS)]
            att.append((ws[0][:, lanes] * os_[0] + ws[1][:, lanes] * os_[1]
                        + ws[2][:, lanes] * os_[2]).astype(BF16))
        return jnp.concatenate(att, axis=1)

    def project(g, att):
        tr = slice(g * sub, (g + 1) * sub)
        h1 = (x_ref[tr, :] + _dot(att, wo_ref[0:D_ATT, :])
              + _dot(c_ref[tr, :], wo_ref[D_ATT:, :]))
        xq = (_dot(_rms(h1, gx_ref[...]).astype(BF16), wq_ref[...])
              * (XATT_HEAD_DIM ** -0.5)).astype(BF16)
        return h1, xq

    def cross(g, h1, xq):
        heads = []
        for h in range(XATT_HEADS):
            sl = slice(h * XATT_HEAD_DIM, (h + 1) * XATT_HEAD_DIM)
            s = _dot_nt(xq[:, sl], xk_ref[0, :, sl])
            mx = jnp.max(s, axis=-1, keepdims=True)
            p = jnp.exp(s - mx)
            den = jnp.sum(p, axis=-1, keepdims=True)
            heads.append((_dot(p.astype(BF16), xv_ref[0, :, sl]) * (1.0 / den)).astype(BF16))
        xo = jnp.concatenate(heads, axis=1)
        h_ref[g * sub:(g + 1) * sub, :] = h1 + _dot(xo, wxo_ref[...])

    for g in range(n_grp):
        h1, xq = project(g, mix(g))
        cross(g, h1, xq)


def _mix_xattn(x2, os_, lses, c2, w_out, gx, w_xq, xk, xv, w_xo, w_up, w_down, seq, tm=512,
               sub=256):
    T = x2.shape[0]
    n_mem = xk.shape[1]
    per_b = seq // tm
    n_steps = T // tm
    up_rows, dn_rows = w_up.shape[0] // n_steps, w_down.shape[0] // n_steps
    n_slab = D_ATT // LANES
    d_max = DILATED_PATTERNS[-1][1]
    row = lambda i: (i, 0)
    const = lambda i: (0, 0)
    memb = lambda i: (i // per_b, 0, 0)
    sq = pl.BlockSpec((D_MODEL, D_MODEL), const, pipeline_mode=pl.Buffered(1))
    n_pat = len(DILATED_PATTERNS)
    return pl.pallas_call(
        functools.partial(_mix_xattn_kernel, tm=tm, sub=sub),
        grid=(T // tm,),
        in_specs=[pl.BlockSpec((tm, D_MODEL), row)]
                 + [pl.BlockSpec((tm // d, d * D_ATT), row) for _, d in DILATED_PATTERNS]
                 + [pl.BlockSpec((tm // d, d * LANES), row) for _, d in DILATED_PATTERNS]
                 + [pl.BlockSpec((tm, D_CONV), row),
                    pl.BlockSpec((LANES, D_ATT), const),
                    sq, pl.BlockSpec((1, D_MODEL), const), sq,
                    pl.BlockSpec((1, n_mem, D_MODEL), memb),
                    pl.BlockSpec((1, n_mem, D_MODEL), memb),
                    sq,
                    pl.BlockSpec((up_rows, w_up.shape[1]), row),
                    pl.BlockSpec((dn_rows, w_down.shape[1]), row)],
        out_specs=[pl.BlockSpec((tm, D_MODEL), row),
                   pl.BlockSpec((up_rows, w_up.shape[1]), row),
                   pl.BlockSpec((dn_rows, w_down.shape[1]), row)],
        out_shape=[jax.ShapeDtypeStruct((T, D_MODEL), F32),
                   jax.ShapeDtypeStruct(w_up.shape, BF16),
                   jax.ShapeDtypeStruct(w_down.shape, BF16)],
        scratch_shapes=[pltpu.VMEM((D_MODEL, D_MODEL), BF16)] * 3
                       + [pltpu.VMEM((n_pat, n_slab, sub, LANES), F32),
                          pltpu.VMEM((n_pat, 1, sub, LANES), F32)] * (tm // sub)
                       + [pltpu.VMEM((d_max // DIL_STEP, n_slab, sub // DIL_STEP, LANES), F32),
                          pltpu.VMEM((d_max // DIL_STEP, 1, sub // DIL_STEP, LANES), F32)
                          ] * (tm // sub),
        compiler_params=pltpu.CompilerParams(dimension_semantics=("arbitrary",),
                                             vmem_limit_bytes=VMEM_LIMIT),
        name="mix_xattn",
    )(x2, *os_, *lses, c2, _head_expand(), w_out, gx, w_xq, xk, xv, w_xo, w_up, w_down)


def _head_expand():
    e = np.zeros((LANES, D_ATT), np.float32)
    for h in range(ATT_HEADS):
        e[h, h * HEAD_DIM:(h + 1) * HEAD_DIM] = 1.0
    return jnp.asarray(e, dtype=BF16)


def _mlp_kernel(h_ref, g_ref, wu_ref, wd_ref, gf_ref, out_ref, *, chunk, final_norm):
    h = h_ref[...]
    hn = _rms(h, g_ref[...]).astype(BF16)
    acc = h
    for j in range(D_FF // chunk):
        u = jnp.maximum(_dot(hn, wu_ref[:, j * chunk:(j + 1) * chunk]), 0.0)
        acc = acc + _dot((u * u).astype(BF16), wd_ref[j * chunk:(j + 1) * chunk, :])
    out_ref[...] = _rms(acc, gf_ref[...]) if final_norm else acc


def _mlp(h2, g, w_up, w_down, gf, final_norm, tm=1024, chunk=1024):
    T = h2.shape[0]
    row = lambda i: (i, 0)
    const = lambda i: (0, 0)
    return pl.pallas_call(
        functools.partial(_mlp_kernel, chunk=chunk, final_norm=final_norm),
        grid=(T // tm,),
        in_specs=[pl.BlockSpec((tm, D_MODEL), row),
                  pl.BlockSpec((1, D_MODEL), const),
                  pl.BlockSpec((D_MODEL, D_FF), const, pipeline_mode=pl.Buffered(1)),
                  pl.BlockSpec((D_FF, D_MODEL), const, pipeline_mode=pl.Buffered(1)),
                  pl.BlockSpec((1, D_MODEL), const)],
        out_specs=pl.BlockSpec((tm, D_MODEL), row),
        out_shape=jax.ShapeDtypeStruct((T, D_MODEL), F32),
        compiler_params=pltpu.CompilerParams(dimension_semantics=("arbitrary",),
                                             vmem_limit_bytes=VMEM_LIMIT),
        name="mlp",
    )(h2, g, w_up, w_down, gf)


def _rotary_tables(seq):
    half = ROT_DIM // 2
    freqs = ROPE_THETA ** (-np.arange(0, ROT_DIM, 2, dtype=np.float64) / ROT_DIM)
    ang = np.arange(seq, dtype=np.float64)[:, None] * freqs[None, :]
    cos, sin = np.cos(ang), np.sin(ang)
    zeros = np.zeros((seq, HEAD_DIM - ROT_DIM))
    z8 = np.zeros((seq, half))
    rep = LANES // HEAD_DIM
    tabs = ([cos, cos, zeros + 1.0], [-sin, z8, zeros], [z8, sin, zeros])
    return tuple(jnp.asarray(np.concatenate(t * rep, axis=1), dtype=F32) for t in tabs)


def kernel(x, mem, norm_mix_g, w_in, conv_w, conv_b, conv_ln_g, conv_ln_b, w_out, norm_x_g,
           norm_mem_g, w_xq, w_xk, w_xv, w_xo, norm_mlp_g, w_up, w_down, norm_final_g):
    B, S, D = x.shape
    n_mem = mem.shape[1]
    depth = w_in.shape[0]
    T = B * S
    cos_t, sa_t, sb_t = _rotary_tables(S)
    bias = _attn_bias()
    hmask = _head_masks()
    row = lambda g: g.reshape(1, -1)

    h = x.reshape(T, D)
    for l in range(depth):
        q, k, v, u = _in_proj(h, row(norm_mix_g[l]), w_in[l], cos_t, sa_t, sb_t, S)
        os_, lses = [], []
        for p, (_, d) in enumerate(DILATED_PATTERNS):
            o, lse = _attn(q[p], k[p], v[p], bias, hmask, B, S, d)
            os_.append(o)
            lses.append(lse)
        c = _conv(u.reshape(B, S, D_CONV), conv_w[l], row(conv_b[l]), row(conv_ln_g[l]),
                  row(conv_ln_b[l])).reshape(T, D_CONV)
        xk, xv = _mem_kv(mem.reshape(B * n_mem, D), row(norm_mem_g[l]),
                         w_xk[l], w_xv[l])
        h, w_up_b, w_down_b = _mix_xattn(h, os_, lses, c, w_out[l], row(norm_x_g[l]), w_xq[l],
                                         xk.reshape(B, n_mem, D), xv.reshape(B, n_mem, D),
                                         w_xo[l], w_up[l], w_down[l], S)
        h = _mlp(h, row(norm_mlp_g[l]), w_up_b, w_down_b, row(norm_final_g),
                 final_norm=(l == depth - 1))
    return h.reshape(B, S, D)
```

```python
import functools
import math

import numpy as np
import jax
import jax.numpy as jnp
from jax import lax
from jax.experimental import pallas as pl
from jax.experimental.pallas import tpu as pltpu

F32 = jnp.float32
BF16 = jnp.bfloat16

D_MODEL = 1024
ATT_HEADS = 8
HEAD_DIM = 64
D_ATT = ATT_HEADS * HEAD_DIM
D_CONV = D_MODEL - D_ATT
DILATED_PATTERNS = ((128, 1), (512, 4), (2048, 16))
DIL_STEP = 4
assert all(d == DIL_STEP ** k for k, (_, d) in enumerate(DILATED_PATTERNS))
ROPE_THETA = 500000.0
ROT_DIM = HEAD_DIM // 4
CONV_WIDTH = 31
CONV_PAD = (CONV_WIDTH - 1) // 2
XATT_HEADS = 4
XATT_HEAD_DIM = D_MODEL // XATT_HEADS
D_FF = 4 * D_MODEL
EPS = 1e-6
NEG_INF = -1e30
LN2 = math.log(2.0)
Q_SCALE = HEAD_DIM ** -0.5 / LN2

LANES = 128
SUBLANES = 8
ATT_HALF = 64
ATT_QB = 2 * ATT_HALF
ATT_WIN = ATT_QB + 2 * ATT_HALF
CONV_HALO = 16
VMEM_LIMIT = 56 * 1024 * 1024


def _dot(a, b):
    return jnp.dot(a, b, preferred_element_type=F32)


def _dot_nt(a, b):
    return lax.dot_general(a, b, (((1,), (1,)), ((), ())), preferred_element_type=F32)


def _dot_tn(a, b):
    return lax.dot_general(a, b, (((0,), (0,)), ((), ())), preferred_element_type=F32)


def _rms(x, g):
    var = jnp.mean(x * x, axis=-1, keepdims=True)
    return x * lax.rsqrt(var + EPS) * g


def _cast_once(pairs, chunk=512):
    @pl.when(pl.program_id(0) == 0)
    def _():
        for src, dst in pairs:
            for c in range(0, src.shape[1], chunk):
                dst[:, c:c + chunk] = src[:, c:c + chunk].astype(BF16)


def _in_proj_kernel(x_ref, g_ref, wf_ref, cos_ref, sa_ref, sb_ref, *refs, tm):
    n_pat = len(DILATED_PATTERNS)
    qkv_refs = [refs[a * n_pat:(a + 1) * n_pat] for a in range(3)]
    u_ref = refs[3 * n_pat]
    w_ref = refs[3 * n_pat + 1]
    stage = refs[3 * n_pat + 2:]
    _cast_once([(wf_ref, w_ref)])
    xn = _rms(x_ref[...], g_ref[...]).astype(BF16)
    cos, sa, sb = cos_ref[...], sa_ref[...], sb_ref[...]
    n_slab = D_ATT // LANES

    def plain(y, j):
        return y[:, LANES * j:LANES * (j + 1)]

    def rot(y, j):
        yj = plain(y, j)
        return (yj * cos + pltpu.roll(yj, LANES - ROT_DIM // 2, 1) * sa
                + pltpu.roll(yj, ROT_DIM // 2, 1) * sb)

    def emit(y, outs, transform):
        for j in range(n_slab):
            piece = transform(y, j)
            stage[0][0, j] = piece
            outs[0][:, j * LANES:(j + 1) * LANES] = piece.astype(BF16)
        for k in range(1, n_pat):
            d_prev, d = DILATED_PATTERNS[k - 1][1], DILATED_PATTERNS[k][1]
            for r_prev in range(d_prev):
                for t in range(DIL_STEP):
                    r = t * d_prev + r_prev
                    for j in range(n_slab):
                        piece = stage[k - 1][r_prev, j, pl.ds(t, tm // d, stride=DIL_STEP), :]
                        if k + 1 < n_pat:
                            stage[k][r, j] = piece
                        c0 = r * D_ATT + j * LANES
                        outs[k][:, c0:c0 + LANES] = piece.astype(BF16)

    emit(_dot(xn, w_ref[:, 0:D_ATT]), qkv_refs[0], lambda y, j: rot(y, j) * Q_SCALE)
    emit(_dot(xn, w_ref[:, D_ATT:2 * D_ATT]), qkv_refs[1], rot)
    emit(_dot(xn, w_ref[:, 2 * D_ATT:3 * D_ATT]), qkv_refs[2], plain)
    c0 = 3 * D_ATT
    a = _dot(xn, w_ref[:, c0:c0 + D_CONV])
    gt = _dot(xn, w_ref[:, c0 + D_CONV:c0 + 2 * D_CONV])
    u_ref[...] = a * (1.0 / (1.0 + jnp.exp(-gt)))


def _in_proj(x2, g, w_in, cos_t, sa_t, sb_t, seq, tm=512):
    T = x2.shape[0]
    d_in = w_in.shape[1]
    n_s = seq // tm
    n_pat = len(DILATED_PATTERNS)
    row = lambda i: (i, 0)
    tab = lambda i: (i % n_s, 0)
    const = lambda i: (0, 0)
    qkv_specs, qkv_shapes = [], []
    for _ in range(3):
        for _, d in DILATED_PATTERNS:
            qkv_specs.append(pl.BlockSpec((tm // d, d * D_ATT), row))
            qkv_shapes.append(jax.ShapeDtypeStruct((T // d, d * D_ATT), BF16))
    outs = pl.pallas_call(
        functools.partial(_in_proj_kernel, tm=tm),
        grid=(T // tm,),
        in_specs=[pl.BlockSpec((tm, D_MODEL), row),
                  pl.BlockSpec((1, D_MODEL), const),
                  pl.BlockSpec((D_MODEL, d_in), const, pipeline_mode=pl.Buffered(1)),
                  pl.BlockSpec((tm, LANES), tab),
                  pl.BlockSpec((tm, LANES), tab),
                  pl.BlockSpec((tm, LANES), tab)],
        out_specs=qkv_specs + [pl.BlockSpec((tm, D_CONV), row)],
        out_shape=qkv_shapes + [jax.ShapeDtypeStruct((T, D_CONV), F32)],
        scratch_shapes=[pltpu.VMEM((D_MODEL, d_in), BF16)]
                       + [pltpu.VMEM((d, D_ATT // LANES, tm // d, LANES), F32)
                          for _, d in DILATED_PATTERNS[:-1]],
        compiler_params=pltpu.CompilerParams(dimension_semantics=("arbitrary",),
                                             vmem_limit_bytes=VMEM_LIMIT),
        name="in_proj",
    )(x2, g, w_in, cos_t, sa_t, sb_t)
    return outs[0:n_pat], outs[n_pat:2 * n_pat], outs[2 * n_pat:3 * n_pat], outs[3 * n_pat]


def _attn_kernel(q_ref, kc_ref, kp_ref, kn_ref, vc_ref, vp_ref, vn_ref, bias_ref, hm_ref,
                 o_ref, lse_ref, kext, vext, *, lb, nblk_total):
    i = pl.program_id(2)
    kext[0:ATT_HALF] = kp_ref[0]
    kext[ATT_HALF:ATT_HALF + lb] = kc_ref[0]
    kext[ATT_HALF + lb:] = kn_ref[0]
    vext[0:ATT_HALF] = vp_ref[0]
    vext[ATT_HALF:ATT_HALF + lb] = vc_ref[0]
    vext[ATT_HALF + lb:] = vn_ref[0]

    mask_a = hm_ref[0:1, :]
    mask_b = hm_ref[1:2, :]
    nblk = lb // ATT_QB

    def body(n, carry):
        r0 = pl.multiple_of(n * ATT_QB, ATT_QB)
        gblk = i * nblk + n
        bidx = jnp.where(gblk == 0, 0, jnp.where(gblk == nblk_total - 1, 2, 1))
        bias = bias_ref[bidx]
        lse_rows = []
        for hp in range(q_ref.shape[2] // LANES):
            c0 = LANES * hp
            q2 = q_ref[0, pl.ds(r0, ATT_QB), c0:c0 + LANES]
            qs = jnp.concatenate([q2 * mask_a, q2 * mask_b], axis=0)
            parts = []
            for kh in range(ATT_WIN // ATT_QB):
                k0 = r0 + kh * ATT_QB
                kw = kext[pl.ds(k0, ATT_QB), c0:c0 + LANES]
                vw = vext[pl.ds(k0, ATT_QB), c0:c0 + LANES]
                s = _dot_nt(kw, qs) + bias[kh * ATT_QB:(kh + 1) * ATT_QB]
                m = jnp.max(s, axis=0, keepdims=True)
                p = jnp.exp2(s - m)
                l = jnp.sum(p, axis=0, keepdims=True)
                parts.append((m, l, _dot_tn(vw, p.astype(BF16))))
            (m1, l1, o1), (m2, l2, o2) = parts
            m = jnp.maximum(m1, m2)
            a1 = jnp.exp2(m1 - m)
            a2 = jnp.exp2(m2 - m)
            l = a1 * l1 + a2 * l2
            inv = 1.0 / l
            ot = o1 * (a1 * inv) + o2 * (a2 * inv)
            own = jnp.concatenate([ot[0:HEAD_DIM, 0:ATT_QB], ot[HEAD_DIM:, ATT_QB:]], axis=0)
            o_ref[0, pl.ds(r0, ATT_QB), c0:c0 + LANES] = jnp.transpose(own).astype(o_ref.dtype)
            lse = (m + jnp.log2(l)) * LN2
            lse_rows += [lse[:, 0:ATT_QB], lse[:, ATT_QB:]]
        for rc in range(len(lse_rows) // ATT_HEADS):
            rows = lse_rows[rc * ATT_HEADS:(rc + 1) * ATT_HEADS]
            pad = jnp.zeros((LANES - ATT_HEADS, ATT_QB), F32)
            lse_ref[0, pl.ds(r0, ATT_QB), rc * LANES:(rc + 1) * LANES] = jnp.transpose(
                jnp.concatenate(rows + [pad], axis=0))
        return carry

    lax.fori_loop(0, nblk, body, 0, unroll=True)


def _attn_bias():
    i = np.arange(ATT_QB)[:, None]
    j = np.arange(ATT_WIN)[None, :]
    band = (j >= i) & (j <= i + 2 * ATT_HALF)
    first = band & (j >= ATT_HALF)
    last = band & (j < ATT_HALF + ATT_QB)
    tabs = np.stack([first, band, last]).astype(np.float32)
    tabs = np.concatenate([tabs.transpose(0, 2, 1)] * 2, axis=2)
    return jnp.asarray((1.0 - tabs) * NEG_INF, dtype=F32)


def _head_masks():
    lane = np.arange(LANES)
    m = np.zeros((16, LANES), np.float32)
    m[0] = lane < HEAD_DIM
    m[1] = lane >= HEAD_DIM
    return jnp.asarray(m, dtype=BF16)


def _attn(q, k, v, bias, hmask, batch, seq, d, blocks_per_step=16, lb_max=2048):
    L = seq // d
    lb = min(lb_max, L)
    rpb = min(d, max(1, blocks_per_step * ATT_QB // lb))
    width = rpb * D_ATT
    nblk_total = L // ATT_QB
    assert nblk_total >= 2 and L % lb == 0 and lb % ATT_QB == 0 and d % rpb == 0
    view = lambda t: t.reshape(batch, L, d * D_ATT)
    hb = lb // ATT_HALF
    cur = lambda b, r, i: (b, i, r)
    prev = lambda b, r, i: (b, jnp.maximum(i * hb - 1, 0), r)
    nxt = lambda b, r, i: (b, jnp.minimum((i + 1) * hb, L // ATT_HALF - 1), r)
    blk = pl.BlockSpec((1, lb, width), cur)
    halo_p = pl.BlockSpec((1, ATT_HALF, width), prev)
    halo_n = pl.BlockSpec((1, ATT_HALF, width), nxt)
    o, lse = pl.pallas_call(
        functools.partial(_attn_kernel, lb=lb, nblk_total=nblk_total),
        grid=(batch, d // rpb, L // lb),
        in_specs=[blk, blk, halo_p, halo_n, blk, halo_p, halo_n,
                  pl.BlockSpec((3, ATT_WIN, 2 * ATT_QB), lambda b, r, i: (0, 0, 0)),
                  pl.BlockSpec((16, LANES), lambda b, r, i: (0, 0))],
        out_specs=[blk, pl.BlockSpec((1, lb, rpb * LANES), cur)],
        out_shape=[jax.ShapeDtypeStruct((batch, L, d * D_ATT), BF16),
                   jax.ShapeDtypeStruct((batch, L, d * LANES), F32)],
        scratch_shapes=[pltpu.VMEM((lb + 2 * ATT_HALF, width), BF16),
                        pltpu.VMEM((lb + 2 * ATT_HALF, width), BF16)],
        compiler_params=pltpu.CompilerParams(
            dimension_semantics=("arbitrary", "arbitrary", "arbitrary"),
            vmem_limit_bytes=VMEM_LIMIT),
        name=f"attn_d{d}",
    )(view(q), view(k), view(k), view(k), view(v), view(v), view(v), bias, hmask)
    return o.reshape(batch * L, d * D_ATT), lse.reshape(batch * L, d * LANES)


def _conv_kernel(uc_ref, up_ref, un_ref, w_ref, b_ref, lg_ref, lb_ref, c_ref, ext, ybuf, *, ts,
                 rows, ln_rows):
    i = pl.program_id(1)
    n = pl.num_programs(1)
    n_slab = D_CONV // LANES
    for g in range(n_slab):
        ls = slice(g * LANES, (g + 1) * LANES)
        ext[g, 0:CONV_HALO] = jnp.where(i > 0, up_ref[0, :, ls], 0.0)
        ext[g, CONV_HALO:CONV_HALO + ts] = uc_ref[0, :, ls]
        ext[g, CONV_HALO + ts:] = jnp.where(i < n - 1, un_ref[0, :, ls], 0.0)
    lg = lg_ref[...]
    lb = lb_ref[...]
    off = CONV_HALO - CONV_PAD
    n_blk = ts // rows

    def taps(idx, carry):
        g = idx // n_blk
        base = pl.multiple_of((idx % n_blk) * rows, rows)
        acc = jnp.broadcast_to(b_ref[g], (rows, LANES))
        for j in range(SUBLANES):
            steps = [m for m in range((CONV_WIDTH + off) // SUBLANES + 1)
                     if 0 <= SUBLANES * m + j - off < CONV_WIDTH]
            win = ext[g, pl.ds(base + j, rows + SUBLANES * max(steps)), :]
            for m in steps:
                k = SUBLANES * m + j - off
                acc = acc + win[SUBLANES * m:SUBLANES * m + rows] * w_ref[g, k:k + 1, :]
        ybuf[g, pl.ds(base, rows), :] = acc
        return carry

    lax.fori_loop(0, n_slab * n_blk, taps, 0, unroll=2)
    for r0 in range(0, ts, ln_rows):
        acc = jnp.concatenate([ybuf[g, r0:r0 + ln_rows, :] for g in range(n_slab)], axis=1)
        mu = jnp.mean(acc, axis=-1, keepdims=True)
        cen = acc - mu
        var = jnp.mean(cen * cen, axis=-1, keepdims=True)
        y = cen * lax.rsqrt(var + EPS) * lg + lb
        c_ref[0, r0:r0 + ln_rows, :] = (y * (1.0 / (1.0 + jnp.exp(-y)))).astype(c_ref.dtype)


def _conv(u3, conv_w, conv_b, ln_g, ln_b, ts=512, rows=128, ln_rows=32):
    B, S, C = u3.shape
    hb = ts // CONV_HALO
    cur = lambda b, i: (b, i, 0)
    prev = lambda b, i: (b, jnp.maximum(i * hb - 1, 0), 0)
    nxt = lambda b, i: (b, jnp.minimum((i + 1) * hb, S // CONV_HALO - 1), 0)
    const = lambda b, i: (0, 0)
    const3 = lambda b, i: (0, 0, 0)
    n_slab = C // LANES
    w_slab = jnp.transpose(conv_w.reshape(CONV_WIDTH, n_slab, LANES), (1, 0, 2))
    b_slab = conv_b.reshape(n_slab, 1, LANES)
    return pl.pallas_call(
        functools.partial(_conv_kernel, ts=ts, rows=rows, ln_rows=ln_rows),
        grid=(B, S // ts),
        in_specs=[pl.BlockSpec((1, ts, C), cur),
                  pl.BlockSpec((1, CONV_HALO, C), prev),
                  pl.BlockSpec((1, CONV_HALO, C), nxt),
                  pl.BlockSpec((n_slab, CONV_WIDTH, LANES), const3),
                  pl.BlockSpec((n_slab, 1, LANES), const3),
                  pl.BlockSpec((1, C), const),
                  pl.BlockSpec((1, C), const)],
        out_specs=pl.BlockSpec((1, ts, C), cur),
        out_shape=jax.ShapeDtypeStruct((B, S, C), BF16),
        scratch_shapes=[pltpu.VMEM((C // LANES, ts + 2 * CONV_HALO, LANES), F32),
                        pltpu.VMEM((C // LANES, ts, LANES), F32)],
        compiler_params=pltpu.CompilerParams(dimension_semantics=("arbitrary", "arbitrary"),
                                             vmem_limit_bytes=VMEM_LIMIT),
        name="conv",
    )(u3, u3, u3, w_slab, b_slab, ln_g, ln_b)


def _mem_kv_kernel(mem_ref, g_ref, wk_ref, wv_ref, k_ref, v_ref):
    mn = _rms(mem_ref[...], g_ref[...]).astype(BF16)
    k_ref[...] = _dot(mn, wk_ref[...].astype(BF16)).astype(BF16)
    v_ref[...] = _dot(mn, wv_ref[...].astype(BF16)).astype(BF16)


def _mem_kv(mem2, g, wk, wv):
    R = mem2.shape[0]
    full = lambda shape: pl.BlockSpec(shape, lambda i: (0, 0))
    return pl.pallas_call(
        _mem_kv_kernel,
        grid=(1,),
        in_specs=[full((R, D_MODEL)), full((1, D_MODEL)),
                  full((D_MODEL, D_MODEL)), full((D_MODEL, D_MODEL))],
        out_specs=[full((R, D_MODEL))] * 2,
        out_shape=[jax.ShapeDtypeStruct((R, D_MODEL), BF16)] * 2,
        compiler_params=pltpu.CompilerParams(dimension_semantics=("arbitrary",),
                                             vmem_limit_bytes=VMEM_LIMIT),
        name="mem_kv",
    )(mem2, g, wk, wv)


def _mix_xattn_kernel(x_ref, o1_ref, o2_ref, o3_ref, l1_ref, l2_ref, l3_ref, c_ref, e_ref,
                      wof_ref, gx_ref, wqf_ref, xk_ref, xv_ref, wxof_ref, wuf_ref, wdf_ref,
                      h_ref, wub_ref, wdb_ref, wo_ref, wq_ref, wxo_ref, *bufs, tm, sub):
    _cast_once([(wof_ref, wo_ref), (wqf_ref, wq_ref), (wxof_ref, wxo_ref)])
    wub_ref[...] = wuf_ref[...].astype(BF16)
    wdb_ref[...] = wdf_ref[...].astype(BF16)
    n_slab = D_ATT // LANES
    n_pat = len(DILATED_PATTERNS)
    o_refs = (o1_ref, o2_ref, o3_ref)
    l_refs = (l1_ref, l2_ref, l3_ref)
    n_grp = tm // sub

    def mix(g):
        t0 = g * sub
        obuf, lbuf = bufs[2 * g], bufs[2 * g + 1]
        otmp, ltmp = bufs[2 * n_grp + 2 * g], bufs[2 * n_grp + 2 * g + 1]
        for p, (_, d) in enumerate(DILATED_PATTERNS):
            if d == 1:
                continue
            src = slice(t0 // d, (t0 + sub) // d)
            jobs = [(o_refs[p], D_ATT, j, obuf, otmp, True) for j in range(n_slab)]
            jobs.append((l_refs[p], LANES, 0, lbuf, ltmp, False))
            for ref, pitch, j, dst, tmp, widen in jobs:
                pieces = {}
                for r in range(d):
                    t = ref[src, r * pitch + j * LANES:r * pitch + (j + 1) * LANES]
                    pieces[r] = t.astype(F32) if widen else t
                dd = d
                while dd > DIL_STEP:
                    lower = dd // DIL_STEP
                    merged = {}
                    for r_low in range(lower):
                        for t in range(DIL_STEP):
                            tmp[r_low, j, pl.ds(t, sub // dd, stride=DIL_STEP), :] = (
                                pieces[t * lower + r_low])
                        merged[r_low] = tmp[r_low, j, 0:sub // lower, :]
                    pieces, dd = merged, lower
                for r in range(dd):
                    dst[p, j, pl.ds(r, sub // dd, stride=DIL_STEP), :] = pieces[r]

        ls = [l_refs[p][t0:t0 + sub, :] if d == 1 else lbuf[p, 0]
              for p, (_, d) in enumerate(DILATED_PATTERNS)]
        m = jnp.maximum(jnp.maximum(ls[0], ls[1]), ls[2])
        es = [jnp.exp(l - m) for l in ls]
        inv = 1.0 / (es[0] + es[1] + es[2])
        ws = [_dot((e * inv).astype(BF16), e_ref[...]) for e in es]
        att = []
        for j in range(n_slab):
            lanes = slice(j * LANES, (j + 1) * LANES)
            os_ = [(o_refs[p][t0:t0 + sub, lanes] if d == 1 else obuf[p, j]).astype(F32)
                   for p, (_, d) in enumerate(DILATED_PATTERNS)]
            att.append((ws[0][:, lanes] * os_[0] + ws[1][:, lanes] * os_[1]
                        + ws[2][:, lanes] * os_[2]).astype(BF16))
        return jnp.concatenate(att, axis=1)

    def project(g, att):
        tr = slice(g * sub, (g + 1) * sub)
        h1 = (x_ref[tr, :] + _dot(att, wo_ref[0:D_ATT, :])
              + _dot(c_ref[tr, :], wo_ref[D_ATT:, :]))
        xq = (_dot(_rms(h1, gx_ref[...]).astype(BF16), wq_ref[...])
              * (XATT_HEAD_DIM ** -0.5)).astype(BF16)
        return h1, xq

    def cross(g, h1, xq):
        heads = []
        for h in range(XATT_HEADS):
            sl = slice(h * XATT_HEAD_DIM, (h + 1) * XATT_HEAD_DIM)
            s = _dot_nt(xq[:, sl], xk_ref[0, :, sl])
            mx = jnp.max(s, axis=-1, keepdims=True)
            p = jnp.exp(s - mx)
            den = jnp.sum(p, axis=-1, keepdims=True)
            heads.append((_dot(p.astype(BF16), xv_ref[0, :, sl]) * (1.0 / den)).astype(BF16))
        xo = jnp.concatenate(heads, axis=1)
        h_ref[g * sub:(g + 1) * sub, :] = h1 + _dot(xo, wxo_ref[...])

    for g in range(n_grp):
        h1, xq = project(g, mix(g))
        cross(g, h1, xq)


def _mix_xattn(x2, os_, lses, c2, w_out, gx, w_xq, xk, xv, w_xo, w_up, w_down, seq, tm=512,
               sub=512):
    T = x2.shape[0]
    n_mem = xk.shape[1]
    per_b = seq // tm
    n_steps = T // tm
    up_rows, dn_rows = w_up.shape[0] // n_steps, w_down.shape[0] // n_steps
    n_slab = D_ATT // LANES
    d_max = DILATED_PATTERNS[-1][1]
    row = lambda i: (i, 0)
    const = lambda i: (0, 0)
    memb = lambda i: (i // per_b, 0, 0)
    sq = pl.BlockSpec((D_MODEL, D_MODEL), const, pipeline_mode=pl.Buffered(1))
    n_pat = len(DILATED_PATTERNS)
    return pl.pallas_call(
        functools.partial(_mix_xattn_kernel, tm=tm, sub=sub),
        grid=(T // tm,),
        in_specs=[pl.BlockSpec((tm, D_MODEL), row)]
                 + [pl.BlockSpec((tm // d, d * D_ATT), row) for _, d in DILATED_PATTERNS]
                 + [pl.BlockSpec((tm // d, d * LANES), row) for _, d in DILATED_PATTERNS]
                 + [pl.BlockSpec((tm, D_CONV), row),
                    pl.BlockSpec((LANES, D_ATT), const),
                    sq, pl.BlockSpec((1, D_MODEL), const), sq,
                    pl.BlockSpec((1, n_mem, D_MODEL), memb),
                    pl.BlockSpec((1, n_mem, D_MODEL), memb),
                    sq,
                    pl.BlockSpec((up_rows, w_up.shape[1]), row),
                    pl.BlockSpec((dn_rows, w_down.shape[1]), row)],
        out_specs=[pl.BlockSpec((tm, D_MODEL), row),
                   pl.BlockSpec((up_rows, w_up.shape[1]), row),
                   pl.BlockSpec((dn_rows, w_down.shape[1]), row)],
        out_shape=[jax.ShapeDtypeStruct((T, D_MODEL), F32),
                   jax.ShapeDtypeStruct(w_up.shape, BF16),
                   jax.ShapeDtypeStruct(w_down.shape, BF16)],
        scratch_shapes=[pltpu.VMEM((D_MODEL, D_MODEL), BF16)] * 3
                       + [pltpu.VMEM((n_pat, n_slab, sub, LANES), F32),
                          pltpu.VMEM((n_pat, 1, sub, LANES), F32)] * (tm // sub)
                       + [pltpu.VMEM((d_max // DIL_STEP, n_slab, sub // DIL_STEP, LANES), F32),
                          pltpu.VMEM((d_max // DIL_STEP, 1, sub // DIL_STEP, LANES), F32)
                          ] * (tm // sub),
        compiler_params=pltpu.CompilerParams(dimension_semantics=("arbitrary",),
                                             vmem_limit_bytes=VMEM_LIMIT),
        name="mix_xattn",
    )(x2, *os_, *lses, c2, _head_expand(), w_out, gx, w_xq, xk, xv, w_xo, w_up, w_down)


def _head_expand():
    e = np.zeros((LANES, D_ATT), np.float32)
    for h in range(ATT_HEADS):
        e[h, h * HEAD_DIM:(h + 1) * HEAD_DIM] = 1.0
    return jnp.asarray(e, dtype=BF16)


def _mlp_kernel(h_ref, g_ref, wu_ref, wd_ref, gf_ref, out_ref, *, chunk, final_norm):
    h = h_ref[...]
    hn = _rms(h, g_ref[...]).astype(BF16)
    acc = h
    for j in range(D_FF // chunk):
        u = jnp.maximum(_dot(hn, wu_ref[:, j * chunk:(j + 1) * chunk]), 0.0)
        acc = acc + _dot((u * u).astype(BF16), wd_ref[j * chunk:(j + 1) * chunk, :])
    out_ref[...] = _rms(acc, gf_ref[...]) if final_norm else acc


def _mlp(h2, g, w_up, w_down, gf, final_norm, tm=1024, chunk=1024):
    T = h2.shape[0]
    row = lambda i: (i, 0)
    const = lambda i: (0, 0)
    return pl.pallas_call(
        functools.partial(_mlp_kernel, chunk=chunk, final_norm=final_norm),
        grid=(T // tm,),
        in_specs=[pl.BlockSpec((tm, D_MODEL), row),
                  pl.BlockSpec((1, D_MODEL), const),
                  pl.BlockSpec((D_MODEL, D_FF), const, pipeline_mode=pl.Buffered(1)),
                  pl.BlockSpec((D_FF, D_MODEL), const, pipeline_mode=pl.Buffered(1)),
                  pl.BlockSpec((1, D_MODEL), const)],
        out_specs=pl.BlockSpec((tm, D_MODEL), row),
        out_shape=jax.ShapeDtypeStruct((T, D_MODEL), F32),
        compiler_params=pltpu.CompilerParams(dimension_semantics=("arbitrary",),
                                             vmem_limit_bytes=VMEM_LIMIT),
        name="mlp",
    )(h2, g, w_up, w_down, gf)


def _rotary_tables(seq):
    half = ROT_DIM // 2
    freqs = ROPE_THETA ** (-np.arange(0, ROT_DIM, 2, dtype=np.float64) / ROT_DIM)
    ang = np.arange(seq, dtype=np.float64)[:, None] * freqs[None, :]
    cos, sin = np.cos(ang), np.sin(ang)
    zeros = np.zeros((seq, HEAD_DIM - ROT_DIM))
    z8 = np.zeros((seq, half))
    rep = LANES // HEAD_DIM
    tabs = ([cos, cos, zeros + 1.0], [-sin, z8, zeros], [z8, sin, zeros])
    return tuple(jnp.asarray(np.concatenate(t * rep, axis=1), dtype=F32) for t in tabs)


def kernel(x, mem, norm_mix_g, w_in, conv_w, conv_b, conv_ln_g, conv_ln_b, w_out, norm_x_g,
           norm_mem_g, w_xq, w_xk, w_xv, w_xo, norm_mlp_g, w_up, w_down, norm_final_g):
    B, S, D = x.shape
    n_mem = mem.shape[1]
    depth = w_in.shape[0]
    T = B * S
    cos_t, sa_t, sb_t = _rotary_tables(S)
    bias = _attn_bias()
    hmask = _head_masks()
    row = lambda g: g.reshape(1, -1)

    h = x.reshape(T, D)
    for l in range(depth):
        q, k, v, u = _in_proj(h, row(norm_mix_g[l]), w_in[l], cos_t, sa_t, sb_t, S)
        os_, lses = [], []
        for p, (_, d) in enumerate(DILATED_PATTERNS):
            o, lse = _attn(q[p], k[p], v[p], bias, hmask, B, S, d)
            os_.append(o)
            lses.append(lse)
        c = _conv(u.reshape(B, S, D_CONV), conv_w[l], row(conv_b[l]), row(conv_ln_g[l]),
                  row(conv_ln_b[l])).reshape(T, D_CONV)
        xk, xv = _mem_kv(mem.reshape(B * n_mem, D), row(norm_mem_g[l]),
                         w_xk[l], w_xv[l])
        h, w_up_b, w_down_b = _mix_xattn(h, os_, lses, c, w_out[l], row(norm_x_g[l]), w_xq[l],
                                         xk.reshape(B, n_mem, D), xv.reshape(B, n_mem, D),
                                         w_xo[l], w_up[l], w_down[l], S)
        h = _mlp(h, row(norm_mlp_g[l]), w_up_b, w_down_b, row(norm_final_g),
                 final_norm=(l == depth - 1))
    return h.reshape(B, S, D)
```

```python
import functools
import math

import numpy as np
import jax
import jax.numpy as jnp
from jax import lax
from jax.experimental import pallas as pl
from jax.experimental.pallas import tpu as pltpu

F32 = jnp.float32
BF16 = jnp.bfloat16

D_MODEL = 1024
ATT_HEADS = 8
HEAD_DIM = 64
D_ATT = ATT_HEADS * HEAD_DIM
D_CONV = D_MODEL - D_ATT
DILATED_PATTERNS = ((128, 1), (512, 4), (2048, 16))
DIL_STEP = 4
assert all(d == DIL_STEP ** k for k, (_, d) in enumerate(DILATED_PATTERNS))
ROPE_THETA = 500000.0
ROT_DIM = HEAD_DIM // 4
CONV_WIDTH = 31
CONV_PAD = (CONV_WIDTH - 1) // 2
XATT_HEADS = 4
XATT_HEAD_DIM = D_MODEL // XATT_HEADS
D_FF = 4 * D_MODEL
EPS = 1e-6
NEG_INF = -1e30
LN2 = math.log(2.0)
Q_SCALE = HEAD_DIM ** -0.5 / LN2

LANES = 128
SUBLANES = 8
ATT_HALF = 64
ATT_QB = 2 * ATT_HALF
ATT_WIN = ATT_QB + 2 * ATT_HALF
CONV_HALO = 16
VMEM_LIMIT = 56 * 1024 * 1024


def _dot(a, b):
    return jnp.dot(a, b, preferred_element_type=F32)


def _dot_nt(a, b):
    return lax.dot_general(a, b, (((1,), (1,)), ((), ())), preferred_element_type=F32)


def _dot_tn(a, b):
    return lax.dot_general(a, b, (((0,), (0,)), ((), ())), preferred_element_type=F32)


def _rms(x, g):
    var = jnp.mean(x * x, axis=-1, keepdims=True)
    return x * lax.rsqrt(var + EPS) * g


def _cast_once(pairs, chunk=512):
    @pl.when(pl.program_id(0) == 0)
    def _():
        for src, dst in pairs:
            for c in range(0, src.shape[1], chunk):
                dst[:, c:c + chunk] = src[:, c:c + chunk].astype(BF16)


def _in_proj_kernel(x_ref, g_ref, wf_ref, cos_ref, sa_ref, sb_ref, *refs, tm):
    n_pat = len(DILATED_PATTERNS)
    qkv_refs = [refs[a * n_pat:(a + 1) * n_pat] for a in range(3)]
    u_ref = refs[3 * n_pat]
    w_ref = refs[3 * n_pat + 1]
    stage = refs[3 * n_pat + 2:]
    _cast_once([(wf_ref, w_ref)])
    xn = _rms(x_ref[...], g_ref[...]).astype(BF16)
    cos, sa, sb = cos_ref[...], sa_ref[...], sb_ref[...]
    n_slab = D_ATT // LANES

    def plain(y, j):
        return y[:, LANES * j:LANES * (j + 1)]

    def rot(y, j):
        yj = plain(y, j)
        return (yj * cos + pltpu.roll(yj, LANES - ROT_DIM // 2, 1) * sa
                + pltpu.roll(yj, ROT_DIM // 2, 1) * sb)

    def emit(y, outs, transform):
        for j in range(n_slab):
            piece = transform(y, j)
            stage[0][0, j] = piece
            outs[0][:, j * LANES:(j + 1) * LANES] = piece.astype(BF16)
        for k in range(1, n_pat):
            d_prev, d = DILATED_PATTERNS[k - 1][1], DILATED_PATTERNS[k][1]
            for r_prev in range(d_prev):
                for t in range(DIL_STEP):
                    r = t * d_prev + r_prev
                    for j in range(n_slab):
                        piece = stage[k - 1][r_prev, j, pl.ds(t, tm // d, stride=DIL_STEP), :]
                        if k + 1 < n_pat:
                            stage[k][r, j] = piece
                        c0 = r * D_ATT + j * LANES
                        outs[k][:, c0:c0 + LANES] = piece.astype(BF16)

    emit(_dot(xn, w_ref[:, 0:D_ATT]), qkv_refs[0], lambda y, j: rot(y, j) * Q_SCALE)
    emit(_dot(xn, w_ref[:, D_ATT:2 * D_ATT]), qkv_refs[1], rot)
    emit(_dot(xn, w_ref[:, 2 * D_ATT:3 * D_ATT]), qkv_refs[2], plain)
    c0 = 3 * D_ATT
    a = _dot(xn, w_ref[:, c0:c0 + D_CONV])
    gt = _dot(xn, w_ref[:, c0 + D_CONV:c0 + 2 * D_CONV])
    u_ref[...] = a * (1.0 / (1.0 + jnp.exp(-gt)))


def _in_proj(x2, g, w_in, cos_t, sa_t, sb_t, seq, tm=512):
    T = x2.shape[0]
    d_in = w_in.shape[1]
    n_s = seq // tm
    n_pat = len(DILATED_PATTERNS)
    row = lambda i: (i, 0)
    tab = lambda i: (i % n_s, 0)
    const = lambda i: (0, 0)
    qkv_specs, qkv_shapes = [], []
    for _ in range(3):
        for _, d in DILATED_PATTERNS:
            qkv_specs.append(pl.BlockSpec((tm // d, d * D_ATT), row))
            qkv_shapes.append(jax.ShapeDtypeStruct((T // d, d * D_ATT), BF16))
    outs = pl.pallas_call(
        functools.partial(_in_proj_kernel, tm=tm),
        grid=(T // tm,),
        in_specs=[pl.BlockSpec((tm, D_MODEL), row),
                  pl.BlockSpec((1, D_MODEL), const),
                  pl.BlockSpec((D_MODEL, d_in), const, pipeline_mode=pl.Buffered(1)),
                  pl.BlockSpec((tm, LANES), tab),
                  pl.BlockSpec((tm, LANES), tab),
                  pl.BlockSpec((tm, LANES), tab)],
        out_specs=qkv_specs + [pl.BlockSpec((tm, D_CONV), row)],
        out_shape=qkv_shapes + [jax.ShapeDtypeStruct((T, D_CONV), F32)],
        scratch_shapes=[pltpu.VMEM((D_MODEL, d_in), BF16)]
                       + [pltpu.VMEM((d, D_ATT // LANES, tm // d, LANES), F32)
                          for _, d in DILATED_PATTERNS[:-1]],
        compiler_params=pltpu.CompilerParams(dimension_semantics=("arbitrary",),
                                             vmem_limit_bytes=VMEM_LIMIT),
        name="in_proj",
    )(x2, g, w_in, cos_t, sa_t, sb_t)
    return outs[0:n_pat], outs[n_pat:2 * n_pat], outs[2 * n_pat:3 * n_pat], outs[3 * n_pat]


def _attn_kernel(q_ref, kc_ref, kp_ref, kn_ref, vc_ref, vp_ref, vn_ref, bias_ref, hm_ref,
                 o_ref, lse_ref, kext, vext, *, lb, nblk_total):
    i = pl.program_id(2)
    kext[0:ATT_HALF] = kp_ref[0]
    kext[ATT_HALF:ATT_HALF + lb] = kc_ref[0]
    kext[ATT_HALF + lb:] = kn_ref[0]
    vext[0:ATT_HALF] = vp_ref[0]
    vext[ATT_HALF:ATT_HALF + lb] = vc_ref[0]
    vext[ATT_HALF + lb:] = vn_ref[0]

    mask_a = hm_ref[0:1, :]
    mask_b = hm_ref[1:2, :]
    nblk = lb // ATT_QB

    def body(n, carry):
        r0 = pl.multiple_of(n * ATT_QB, ATT_QB)
        gblk = i * nblk + n
        bidx = jnp.where(gblk == 0, 0, jnp.where(gblk == nblk_total - 1, 2, 1))
        bias = bias_ref[bidx]
        lse_rows = []
        for hp in range(q_ref.shape[2] // LANES):
            c0 = LANES * hp
            q2 = q_ref[0, pl.ds(r0, ATT_QB), c0:c0 + LANES]
            qs = jnp.concatenate([q2 * mask_a, q2 * mask_b], axis=0)
            parts = []
            for kh in range(ATT_WIN // ATT_QB):
                k0 = r0 + kh * ATT_QB
                kw = kext[pl.ds(k0, ATT_QB), c0:c0 + LANES]
                vw = vext[pl.ds(k0, ATT_QB), c0:c0 + LANES]
                s = _dot_nt(kw, qs) + bias[kh * ATT_QB:(kh + 1) * ATT_QB]
                m = jnp.max(s, axis=0, keepdims=True)
                p = jnp.exp2(s - m)
                l = jnp.sum(p, axis=0, keepdims=True)
                parts.append((m, l, _dot_tn(vw, p.astype(BF16))))
            (m1, l1, o1), (m2, l2, o2) = parts
            m = jnp.maximum(m1, m2)
            a1 = jnp.exp2(m1 - m)
            a2 = jnp.exp2(m2 - m)
            l = a1 * l1 + a2 * l2
            inv = 1.0 / l
            ot = o1 * (a1 * inv) + o2 * (a2 * inv)
            own = jnp.concatenate([ot[0:HEAD_DIM, 0:ATT_QB], ot[HEAD_DIM:, ATT_QB:]], axis=0)
            o_ref[0, pl.ds(r0, ATT_QB), c0:c0 + LANES] = jnp.transpose(own).astype(o_ref.dtype)
            lse = (m + jnp.log2(l)) * LN2
            lse_rows += [lse[:, 0:ATT_QB], lse[:, ATT_QB:]]
        for rc in range(len(lse_rows) // ATT_HEADS):
            rows = lse_rows[rc * ATT_HEADS:(rc + 1) * ATT_HEADS]
            pad = jnp.zeros((LANES - ATT_HEADS, ATT_QB), F32)
            lse_ref[0, pl.ds(r0, ATT_QB), rc * LANES:(rc + 1) * LANES] = jnp.transpose(
                jnp.concatenate(rows + [pad], axis=0))
        return carry

    lax.fori_loop(0, nblk, body, 0, unroll=True)


def _attn_bias():
    i = np.arange(ATT_QB)[:, None]
    j = np.arange(ATT_WIN)[None, :]
    band = (j >= i) & (j <= i + 2 * ATT_HALF)
    first = band & (j >= ATT_HALF)
    last = band & (j < ATT_HALF + ATT_QB)
    tabs = np.stack([first, band, last]).astype(np.float32)
    tabs = np.concatenate([tabs.transpose(0, 2, 1)] * 2, axis=2)
    return jnp.asarray((1.0 - tabs) * NEG_INF, dtype=F32)


def _head_masks():
    lane = np.arange(LANES)
    m = np.zeros((16, LANES), np.float32)
    m[0] = lane < HEAD_DIM
    m[1] = lane >= HEAD_DIM
    return jnp.asarray(m, dtype=BF16)


def _attn(q, k, v, bias, hmask, batch, seq, d, blocks_per_step=16, lb_max=2048):
    L = seq // d
    lb = min(lb_max, L)
    rpb = min(d, max(1, blocks_per_step * ATT_QB // lb))
    width = rpb * D_ATT
    nblk_total = L // ATT_QB
    assert nblk_total >= 2 and L % lb == 0 and lb % ATT_QB == 0 and d % rpb == 0
    view = lambda t: t.reshape(batch, L, d * D_ATT)
    hb = lb // ATT_HALF
    cur = lambda b, r, i: (b, i, r)
    prev = lambda b, r, i: (b, jnp.maximum(i * hb - 1, 0), r)
    nxt = lambda b, r, i: (b, jnp.minimum((i + 1) * hb, L // ATT_HALF - 1), r)
    blk = pl.BlockSpec((1, lb, width), cur)
    halo_p = pl.BlockSpec((1, ATT_HALF, width), prev)
    halo_n = pl.BlockSpec((1, ATT_HALF, width), nxt)
    o, lse = pl.pallas_call(
        functools.partial(_attn_kernel, lb=lb, nblk_total=nblk_total),
        grid=(batch, d // rpb, L // lb),
        in_specs=[blk, blk, halo_p, halo_n, blk, halo_p, halo_n,
                  pl.BlockSpec((3, ATT_WIN, 2 * ATT_QB), lambda b, r, i: (0, 0, 0)),
                  pl.BlockSpec((16, LANES), lambda b, r, i: (0, 0))],
        out_specs=[blk, pl.BlockSpec((1, lb, rpb * LANES), cur)],
        out_shape=[jax.ShapeDtypeStruct((batch, L, d * D_ATT), BF16),
                   jax.ShapeDtypeStruct((batch, L, d * LANES), F32)],
        scratch_shapes=[pltpu.VMEM((lb + 2 * ATT_HALF, width), BF16),
                        pltpu.VMEM((lb + 2 * ATT_HALF, width), BF16)],
        compiler_params=pltpu.CompilerParams(
            dimension_semantics=("arbitrary", "arbitrary", "arbitrary"),
            vmem_limit_bytes=VMEM_LIMIT),
        name=f"attn_d{d}",
    )(view(q), view(k), view(k), view(k), view(v), view(v), view(v), bias, hmask)
    return o.reshape(batch * L, d * D_ATT), lse.reshape(batch * L, d * LANES)


def _conv_kernel(uc_ref, up_ref, un_ref, w_ref, b_ref, lg_ref, lb_ref, *refs, ts, rows, ln_rows,
                 n_cast):
    cast_in, c_ref = refs[:n_cast], refs[n_cast]
    cast_out, (ext, ybuf) = refs[n_cast + 1:2 * n_cast + 1], refs[2 * n_cast + 1:]
    for src, dst in zip(cast_in, cast_out):
        dst[...] = src[...].astype(BF16)
    i = pl.program_id(1)
    n = pl.num_programs(1)
    n_slab = D_CONV // LANES
    for g in range(n_slab):
        ls = slice(g * LANES, (g + 1) * LANES)
        ext[g, 0:CONV_HALO] = jnp.where(i > 0, up_ref[0, :, ls], 0.0)
        ext[g, CONV_HALO:CONV_HALO + ts] = uc_ref[0, :, ls]
        ext[g, CONV_HALO + ts:] = jnp.where(i < n - 1, un_ref[0, :, ls], 0.0)
    lg = lg_ref[...]
    lb = lb_ref[...]
    off = CONV_HALO - CONV_PAD
    n_blk = ts // rows

    def taps(idx, carry):
        g = idx // n_blk
        base = pl.multiple_of((idx % n_blk) * rows, rows)
        acc = jnp.broadcast_to(b_ref[g], (rows, LANES))
        for j in range(SUBLANES):
            steps = [m for m in range((CONV_WIDTH + off) // SUBLANES + 1)
                     if 0 <= SUBLANES * m + j - off < CONV_WIDTH]
            win = ext[g, pl.ds(base + j, rows + SUBLANES * max(steps)), :]
            for m in steps:
                k = SUBLANES * m + j - off
                acc = acc + win[SUBLANES * m:SUBLANES * m + rows] * w_ref[g, k:k + 1, :]
        ybuf[g, pl.ds(base, rows), :] = acc
        return carry

    lax.fori_loop(0, n_slab * n_blk, taps, 0, unroll=2)
    for r0 in range(0, ts, ln_rows):
        acc = jnp.concatenate([ybuf[g, r0:r0 + ln_rows, :] for g in range(n_slab)], axis=1)
        mu = jnp.mean(acc, axis=-1, keepdims=True)
        cen = acc - mu
        var = jnp.mean(cen * cen, axis=-1, keepdims=True)
        y = cen * lax.rsqrt(var + EPS) * lg + lb
        c_ref[0, r0:r0 + ln_rows, :] = (y * (1.0 / (1.0 + jnp.exp(-y)))).astype(c_ref.dtype)


def _conv(u3, conv_w, conv_b, ln_g, ln_b, weights, ts=512, rows=128, ln_rows=32):
    B, S, C = u3.shape
    n_i = S // ts
    sl = lambda b, i: (b * n_i + i, 0)
    w_rows = [w.shape[0] // (B * n_i) for w in weights]
    w_specs = [pl.BlockSpec((r, w.shape[1]), sl) for r, w in zip(w_rows, weights)]
    hb = ts // CONV_HALO
    cur = lambda b, i: (b, i, 0)
    prev = lambda b, i: (b, jnp.maximum(i * hb - 1, 0), 0)
    nxt = lambda b, i: (b, jnp.minimum((i + 1) * hb, S // CONV_HALO - 1), 0)
    const = lambda b, i: (0, 0)
    const3 = lambda b, i: (0, 0, 0)
    n_slab = C // LANES
    w_slab = jnp.transpose(conv_w.reshape(CONV_WIDTH, n_slab, LANES), (1, 0, 2))
    b_slab = conv_b.reshape(n_slab, 1, LANES)
    outs = pl.pallas_call(
        functools.partial(_conv_kernel, ts=ts, rows=rows, ln_rows=ln_rows, n_cast=len(weights)),
        grid=(B, S // ts),
        in_specs=[pl.BlockSpec((1, ts, C), cur),
                  pl.BlockSpec((1, CONV_HALO, C), prev),
                  pl.BlockSpec((1, CONV_HALO, C), nxt),
                  pl.BlockSpec((n_slab, CONV_WIDTH, LANES), const3),
                  pl.BlockSpec((n_slab, 1, LANES), const3),
                  pl.BlockSpec((1, C), const),
                  pl.BlockSpec((1, C), const)] + w_specs,
        out_specs=[pl.BlockSpec((1, ts, C), cur)] + w_specs,
        out_shape=[jax.ShapeDtypeStruct((B, S, C), BF16)]
                  + [jax.ShapeDtypeStruct(w.shape, BF16) for w in weights],
        scratch_shapes=[pltpu.VMEM((C // LANES, ts + 2 * CONV_HALO, LANES), F32),
                        pltpu.VMEM((C // LANES, ts, LANES), F32)],
        compiler_params=pltpu.CompilerParams(dimension_semantics=("arbitrary", "arbitrary"),
                                             vmem_limit_bytes=VMEM_LIMIT),
        name="conv",
    )(u3, u3, u3, w_slab, b_slab, ln_g, ln_b, *weights)
    return outs[0], outs[1:]


def _mem_kv_kernel(mem_ref, g_ref, wk_ref, wv_ref, k_ref, v_ref):
    mn = _rms(mem_ref[...], g_ref[...]).astype(BF16)
    k_ref[...] = _dot(mn, wk_ref[...].astype(BF16)).astype(BF16)
    v_ref[...] = _dot(mn, wv_ref[...].astype(BF16)).astype(BF16)


def _mem_kv(mem2, g, wk, wv):
    R = mem2.shape[0]
    full = lambda shape: pl.BlockSpec(shape, lambda i: (0, 0))
    return pl.pallas_call(
        _mem_kv_kernel,
        grid=(1,),
        in_specs=[full((R, D_MODEL)), full((1, D_MODEL)),
                  full((D_MODEL, D_MODEL)), full((D_MODEL, D_MODEL))],
        out_specs=[full((R, D_MODEL))] * 2,
        out_shape=[jax.ShapeDtypeStruct((R, D_MODEL), BF16)] * 2,
        compiler_params=pltpu.CompilerParams(dimension_semantics=("arbitrary",),
                                             vmem_limit_bytes=VMEM_LIMIT),
        name="mem_kv",
    )(mem2, g, wk, wv)


def _post_kernel(x_ref, o1_ref, o2_ref, o3_ref, l1_ref, l2_ref, l3_ref, c_ref, e_ref,
                 wo_ref, gx_ref, wq_ref, xk_ref, xv_ref, wxo_ref, gm_ref, wu_ref, wd_ref, gf_ref,
                 out_ref, *bufs, tm, sub, chunk, final_norm):
    n_slab = D_ATT // LANES
    n_pat = len(DILATED_PATTERNS)
    o_refs = (o1_ref, o2_ref, o3_ref)
    l_refs = (l1_ref, l2_ref, l3_ref)
    n_grp = tm // sub

    def mix(g):
        t0 = g * sub
        obuf, lbuf = bufs[2 * g], bufs[2 * g + 1]
        otmp, ltmp = bufs[2 * n_grp + 2 * g], bufs[2 * n_grp + 2 * g + 1]
        for p, (_, d) in enumerate(DILATED_PATTERNS):
            if d == 1:
                continue
            src = slice(t0 // d, (t0 + sub) // d)
            jobs = [(o_refs[p], D_ATT, j, obuf, otmp, True) for j in range(n_slab)]
            jobs.append((l_refs[p], LANES, 0, lbuf, ltmp, False))
            for ref, pitch, j, dst, tmp, widen in jobs:
                pieces = {}
                for r in range(d):
                    t = ref[src, r * pitch + j * LANES:r * pitch + (j + 1) * LANES]
                    pieces[r] = t.astype(F32) if widen else t
                dd = d
                while dd > DIL_STEP:
                    lower = dd // DIL_STEP
                    merged = {}
                    for r_low in range(lower):
                        for t in range(DIL_STEP):
                            tmp[r_low, j, pl.ds(t, sub // dd, stride=DIL_STEP), :] = (
                                pieces[t * lower + r_low])
                        merged[r_low] = tmp[r_low, j, 0:sub // lower, :]
                    pieces, dd = merged, lower
                for r in range(dd):
                    dst[p, j, pl.ds(r, sub // dd, stride=DIL_STEP), :] = pieces[r]

        ls = [l_refs[p][t0:t0 + sub, :] if d == 1 else lbuf[p, 0]
              for p, (_, d) in enumerate(DILATED_PATTERNS)]
        m = jnp.maximum(jnp.maximum(ls[0], ls[1]), ls[2])
        es = [jnp.exp(l - m) for l in ls]
        inv = 1.0 / (es[0] + es[1] + es[2])
        ws = [_dot((e * inv).astype(BF16), e_ref[...]) for e in es]
        att = []
        for j in range(n_slab):
            lanes = slice(j * LANES, (j + 1) * LANES)
            os_ = [(o_refs[p][t0:t0 + sub, lanes] if d == 1 else obuf[p, j]).astype(F32)
                   for p, (_, d) in enumerate(DILATED_PATTERNS)]
            att.append((ws[0][:, lanes] * os_[0] + ws[1][:, lanes] * os_[1]
                        + ws[2][:, lanes] * os_[2]).astype(BF16))
        return jnp.concatenate(att, axis=1)

    def project(g, att):
        tr = slice(g * sub, (g + 1) * sub)
        h1 = (x_ref[tr, :] + _dot(att, wo_ref[0:D_ATT, :])
              + _dot(c_ref[tr, :], wo_ref[D_ATT:, :]))
        xq = (_dot(_rms(h1, gx_ref[...]).astype(BF16), wq_ref[...])
              * (XATT_HEAD_DIM ** -0.5)).astype(BF16)
        return h1, xq

    def cross(g, h1, xq):
        heads = []
        for h in range(XATT_HEADS):
            sl = slice(h * XATT_HEAD_DIM, (h + 1) * XATT_HEAD_DIM)
            s = _dot_nt(xq[:, sl], xk_ref[0, :, sl])
            mx = jnp.max(s, axis=-1, keepdims=True)
            p = jnp.exp(s - mx)
            den = jnp.sum(p, axis=-1, keepdims=True)
            heads.append((_dot(p.astype(BF16), xv_ref[0, :, sl]) * (1.0 / den)).astype(BF16))
        xo = jnp.concatenate(heads, axis=1)
        return h1 + _dot(xo, wxo_ref[...])

    def mlp(g, h):
        hn = _rms(h, gm_ref[...]).astype(BF16)
        acc = h
        for j in range(D_FF // chunk):
            u = jnp.maximum(_dot(hn, wu_ref[:, j * chunk:(j + 1) * chunk]), 0.0)
            acc = acc + _dot((u * u).astype(BF16), wd_ref[j * chunk:(j + 1) * chunk, :])
        out_ref[g * sub:(g + 1) * sub, :] = _rms(acc, gf_ref[...]) if final_norm else acc

    for g in range(n_grp):
        h1, xq = project(g, mix(g))
        mlp(g, cross(g, h1, xq))


def _post(x2, os_, lses, c2, w_out, gx, w_xq, xk, xv, w_xo, gm, w_up, w_down, gf, final_norm,
          seq, tm=512, sub=512, chunk=1024):
    T = x2.shape[0]
    n_mem = xk.shape[1]
    per_b = seq // tm
    n_slab = D_ATT // LANES
    d_max = DILATED_PATTERNS[-1][1]
    row = lambda i: (i, 0)
    const = lambda i: (0, 0)
    memb = lambda i: (i // per_b, 0, 0)
    whole = lambda w: pl.BlockSpec(w.shape, const, pipeline_mode=pl.Buffered(1))
    vec = pl.BlockSpec((1, D_MODEL), const)
    n_pat = len(DILATED_PATTERNS)
    return pl.pallas_call(
        functools.partial(_post_kernel, tm=tm, sub=sub, chunk=chunk, final_norm=final_norm),
        grid=(T // tm,),
        in_specs=[pl.BlockSpec((tm, D_MODEL), row)]
                 + [pl.BlockSpec((tm // d, d * D_ATT), row) for _, d in DILATED_PATTERNS]
                 + [pl.BlockSpec((tm // d, d * LANES), row) for _, d in DILATED_PATTERNS]
                 + [pl.BlockSpec((tm, D_CONV), row),
                    pl.BlockSpec((LANES, D_ATT), const),
                    whole(w_out), vec, whole(w_xq),
                    pl.BlockSpec((1, n_mem, D_MODEL), memb),
                    pl.BlockSpec((1, n_mem, D_MODEL), memb),
                    whole(w_xo), vec, whole(w_up), whole(w_down), vec],
        out_specs=pl.BlockSpec((tm, D_MODEL), row),
        out_shape=jax.ShapeDtypeStruct((T, D_MODEL), F32),
        scratch_shapes=[pltpu.VMEM((n_pat, n_slab, sub, LANES), F32),
                          pltpu.VMEM((n_pat, 1, sub, LANES), F32)] * (tm // sub)
                       + [pltpu.VMEM((d_max // DIL_STEP, n_slab, sub // DIL_STEP, LANES), F32),
                          pltpu.VMEM((d_max // DIL_STEP, 1, sub // DIL_STEP, LANES), F32)
                          ] * (tm // sub),
        compiler_params=pltpu.CompilerParams(dimension_semantics=("arbitrary",),
                                             vmem_limit_bytes=VMEM_LIMIT),
        name="post",
    )(x2, *os_, *lses, c2, _head_expand(), w_out, gx, w_xq, xk, xv, w_xo, gm, w_up, w_down, gf)


def _head_expand():
    e = np.zeros((LANES, D_ATT), np.float32)
    for h in range(ATT_HEADS):
        e[h, h * HEAD_DIM:(h + 1) * HEAD_DIM] = 1.0
    return jnp.asarray(e, dtype=BF16)


def _rotary_tables(seq):
    half = ROT_DIM // 2
    freqs = ROPE_THETA ** (-np.arange(0, ROT_DIM, 2, dtype=np.float64) / ROT_DIM)
    ang = np.arange(seq, dtype=np.float64)[:, None] * freqs[None, :]
    cos, sin = np.cos(ang), np.sin(ang)
    zeros = np.zeros((seq, HEAD_DIM - ROT_DIM))
    z8 = np.zeros((seq, half))
    rep = LANES // HEAD_DIM
    tabs = ([cos, cos, zeros + 1.0], [-sin, z8, zeros], [z8, sin, zeros])
    return tuple(jnp.asarray(np.concatenate(t * rep, axis=1), dtype=F32) for t in tabs)


def kernel(x, mem, norm_mix_g, w_in, conv_w, conv_b, conv_ln_g, conv_ln_b, w_out, norm_x_g,
           norm_mem_g, w_xq, w_xk, w_xv, w_xo, norm_mlp_g, w_up, w_down, norm_final_g):
    B, S, D = x.shape
    n_mem = mem.shape[1]
    depth = w_in.shape[0]
    T = B * S
    cos_t, sa_t, sb_t = _rotary_tables(S)
    bias = _attn_bias()
    hmask = _head_masks()
    row = lambda g: g.reshape(1, -1)

    h = x.reshape(T, D)
    for l in range(depth):
        q, k, v, u = _in_proj(h, row(norm_mix_g[l]), w_in[l], cos_t, sa_t, sb_t, S)
        os_, lses = [], []
        for p, (_, d) in enumerate(DILATED_PATTERNS):
            o, lse = _attn(q[p], k[p], v[p], bias, hmask, B, S, d)
            os_.append(o)
            lses.append(lse)
        c, (w_out_b, w_xq_b, w_xo_b, w_up_b, w_down_b) = _conv(
            u.reshape(B, S, D_CONV), conv_w[l], row(conv_b[l]), row(conv_ln_g[l]),
            row(conv_ln_b[l]), (w_out[l], w_xq[l], w_xo[l], w_up[l], w_down[l]))
        xk, xv = _mem_kv(mem.reshape(B * n_mem, D), row(norm_mem_g[l]), w_xk[l], w_xv[l])
        h = _post(h, os_, lses, c.reshape(T, D_CONV), w_out_b, row(norm_x_g[l]), w_xq_b,
                  xk.reshape(B, n_mem, D), xv.reshape(B, n_mem, D), w_xo_b, row(norm_mlp_g[l]),
                  w_up_b, w_down_b, row(norm_final_g), final_norm=(l == depth - 1), seq=S)
    return h.reshape(B, S, D)
```

```python
import functools
import math

import numpy as np
import jax
import jax.numpy as jnp
from jax import lax
from jax.experimental import pallas as pl
from jax.experimental.pallas import tpu as pltpu

F32 = jnp.float32
BF16 = jnp.bfloat16

D_MODEL = 1024
ATT_HEADS = 8
HEAD_DIM = 64
D_ATT = ATT_HEADS * HEAD_DIM
D_CONV = D_MODEL - D_ATT
DILATED_PATTERNS = ((128, 1), (512, 4), (2048, 16))
DIL_STEP = 4
assert all(d == DIL_STEP ** k for k, (_, d) in enumerate(DILATED_PATTERNS))
ROPE_THETA = 500000.0
ROT_DIM = HEAD_DIM // 4
CONV_WIDTH = 31
CONV_PAD = (CONV_WIDTH - 1) // 2
XATT_HEADS = 4
XATT_HEAD_DIM = D_MODEL // XATT_HEADS
D_FF = 4 * D_MODEL
EPS = 1e-6
NEG_INF = -1e30
LN2 = math.log(2.0)
Q_SCALE = HEAD_DIM ** -0.5 / LN2

LANES = 128
SUBLANES = 8
ATT_HALF = 64
ATT_QB = 2 * ATT_HALF
ATT_WIN = ATT_QB + 2 * ATT_HALF
CONV_HALO = 16
VMEM_LIMIT = 56 * 1024 * 1024


def _dot(a, b):
    return jnp.dot(a, b, preferred_element_type=F32)


def _dot_nt(a, b):
    return lax.dot_general(a, b, (((1,), (1,)), ((), ())), preferred_element_type=F32)


def _dot_tn(a, b):
    return lax.dot_general(a, b, (((0,), (0,)), ((), ())), preferred_element_type=F32)


def _rms(x, g):
    var = jnp.mean(x * x, axis=-1, keepdims=True)
    return x * lax.rsqrt(var + EPS) * g


def _cast_once(pairs, chunk=512):
    @pl.when(pl.program_id(0) == 0)
    def _():
        for src, dst in pairs:
            for c in range(0, src.shape[1], chunk):
                dst[:, c:c + chunk] = src[:, c:c + chunk].astype(BF16)


def _in_proj_kernel(x_ref, g_ref, wf_ref, cos_ref, sa_ref, sb_ref, *refs, tm):
    n_pat = len(DILATED_PATTERNS)
    qkv_refs = [refs[a * n_pat:(a + 1) * n_pat] for a in range(3)]
    u_ref = refs[3 * n_pat]
    w_ref = refs[3 * n_pat + 1]
    stage = refs[3 * n_pat + 2:]
    _cast_once([(wf_ref, w_ref)])
    xn = _rms(x_ref[...], g_ref[...]).astype(BF16)
    cos, sa, sb = cos_ref[...], sa_ref[...], sb_ref[...]
    n_slab = D_ATT // LANES

    def plain(y, j):
        return y[:, LANES * j:LANES * (j + 1)]

    def rot(y, j):
        yj = plain(y, j)
        return (yj * cos + pltpu.roll(yj, LANES - ROT_DIM // 2, 1) * sa
                + pltpu.roll(yj, ROT_DIM // 2, 1) * sb)

    def emit(y, outs, transform):
        for j in range(n_slab):
            piece = transform(y, j)
            stage[0][0, j] = piece
            outs[0][:, j * LANES:(j + 1) * LANES] = piece.astype(BF16)
        for k in range(1, n_pat):
            d_prev, d = DILATED_PATTERNS[k - 1][1], DILATED_PATTERNS[k][1]
            for r_prev in range(d_prev):
                for t in range(DIL_STEP):
                    r = t * d_prev + r_prev
                    for j in range(n_slab):
                        piece = stage[k - 1][r_prev, j, pl.ds(t, tm // d, stride=DIL_STEP), :]
                        if k + 1 < n_pat:
                            stage[k][r, j] = piece
                        c0 = r * D_ATT + j * LANES
                        outs[k][:, c0:c0 + LANES] = piece.astype(BF16)

    emit(_dot(xn, w_ref[:, 0:D_ATT]), qkv_refs[0], lambda y, j: rot(y, j) * Q_SCALE)
    emit(_dot(xn, w_ref[:, D_ATT:2 * D_ATT]), qkv_refs[1], rot)
    emit(_dot(xn, w_ref[:, 2 * D_ATT:3 * D_ATT]), qkv_refs[2], plain)
    c0 = 3 * D_ATT
    a = _dot(xn, w_ref[:, c0:c0 + D_CONV])
    gt = _dot(xn, w_ref[:, c0 + D_CONV:c0 + 2 * D_CONV])
    u_ref[...] = a * (1.0 / (1.0 + jnp.exp(-gt)))


def _in_proj(x2, g, w_in, cos_t, sa_t, sb_t, seq, tm=1024):
    T = x2.shape[0]
    d_in = w_in.shape[1]
    n_s = seq // tm
    n_pat = len(DILATED_PATTERNS)
    row = lambda i: (i, 0)
    tab = lambda i: (i % n_s, 0)
    const = lambda i: (0, 0)
    qkv_specs, qkv_shapes = [], []
    for _ in range(3):
        for _, d in DILATED_PATTERNS:
            qkv_specs.append(pl.BlockSpec((tm // d, d * D_ATT), row))
            qkv_shapes.append(jax.ShapeDtypeStruct((T // d, d * D_ATT), BF16))
    outs = pl.pallas_call(
        functools.partial(_in_proj_kernel, tm=tm),
        grid=(T // tm,),
        in_specs=[pl.BlockSpec((tm, D_MODEL), row),
                  pl.BlockSpec((1, D_MODEL), const),
                  pl.BlockSpec((D_MODEL, d_in), const, pipeline_mode=pl.Buffered(1)),
                  pl.BlockSpec((tm, LANES), tab),
                  pl.BlockSpec((tm, LANES), tab),
                  pl.BlockSpec((tm, LANES), tab)],
        out_specs=qkv_specs + [pl.BlockSpec((tm, D_CONV), row)],
        out_shape=qkv_shapes + [jax.ShapeDtypeStruct((T, D_CONV), F32)],
        scratch_shapes=[pltpu.VMEM((D_MODEL, d_in), BF16)]
                       + [pltpu.VMEM((d, D_ATT // LANES, tm // d, LANES), F32)
                          for _, d in DILATED_PATTERNS[:-1]],
        compiler_params=pltpu.CompilerParams(dimension_semantics=("arbitrary",),
                                             vmem_limit_bytes=VMEM_LIMIT),
        name="in_proj",
    )(x2, g, w_in, cos_t, sa_t, sb_t)
    return outs[0:n_pat], outs[n_pat:2 * n_pat], outs[2 * n_pat:3 * n_pat], outs[3 * n_pat]


def _attn_kernel(q_ref, kc_ref, kp_ref, kn_ref, vc_ref, vp_ref, vn_ref, bias_ref, hm_ref,
                 o_ref, lse_ref, kext, vext, *, lb, nblk_total):
    i = pl.program_id(2)
    kext[0:ATT_HALF] = kp_ref[0]
    kext[ATT_HALF:ATT_HALF + lb] = kc_ref[0]
    kext[ATT_HALF + lb:] = kn_ref[0]
    vext[0:ATT_HALF] = vp_ref[0]
    vext[ATT_HALF:ATT_HALF + lb] = vc_ref[0]
    vext[ATT_HALF + lb:] = vn_ref[0]

    mask_a = hm_ref[0:1, :]
    mask_b = hm_ref[1:2, :]
    nblk = lb // ATT_QB

    def body(n, carry):
        r0 = pl.multiple_of(n * ATT_QB, ATT_QB)
        gblk = i * nblk + n
        bidx = jnp.where(gblk == 0, 0, jnp.where(gblk == nblk_total - 1, 2, 1))
        bias = bias_ref[bidx]
        lse_rows = []
        for hp in range(q_ref.shape[2] // LANES):
            c0 = LANES * hp
            q2 = q_ref[0, pl.ds(r0, ATT_QB), c0:c0 + LANES]
            qs = jnp.concatenate([q2 * mask_a, q2 * mask_b], axis=0)
            parts = []
            for kh in range(ATT_WIN // ATT_QB):
                k0 = r0 + kh * ATT_QB
                kw = kext[pl.ds(k0, ATT_QB), c0:c0 + LANES]
                vw = vext[pl.ds(k0, ATT_QB), c0:c0 + LANES]
                s = _dot_nt(kw, qs) + bias[kh * ATT_QB:(kh + 1) * ATT_QB]
                m = jnp.max(s, axis=0, keepdims=True)
                p = jnp.exp2(s - m)
                l = jnp.sum(p, axis=0, keepdims=True)
                parts.append((m, l, _dot_tn(vw, p.astype(BF16))))
            (m1, l1, o1), (m2, l2, o2) = parts
            m = jnp.maximum(m1, m2)
            a1 = jnp.exp2(m1 - m)
            a2 = jnp.exp2(m2 - m)
            l = a1 * l1 + a2 * l2
            inv = 1.0 / l
            ot = o1 * (a1 * inv) + o2 * (a2 * inv)
            own = jnp.concatenate([ot[0:HEAD_DIM, 0:ATT_QB], ot[HEAD_DIM:, ATT_QB:]], axis=0)
            o_ref[0, pl.ds(r0, ATT_QB), c0:c0 + LANES] = jnp.transpose(own).astype(o_ref.dtype)
            lse = (m + jnp.log2(l)) * LN2
            lse_rows += [lse[:, 0:ATT_QB], lse[:, ATT_QB:]]
        for rc in range(len(lse_rows) // ATT_HEADS):
            rows = lse_rows[rc * ATT_HEADS:(rc + 1) * ATT_HEADS]
            pad = jnp.zeros((LANES - ATT_HEADS, ATT_QB), F32)
            lse_ref[0, pl.ds(r0, ATT_QB), rc * LANES:(rc + 1) * LANES] = jnp.transpose(
                jnp.concatenate(rows + [pad], axis=0))
        return carry

    lax.fori_loop(0, nblk, body, 0, unroll=True)


def _attn_bias():
    i = np.arange(ATT_QB)[:, None]
    j = np.arange(ATT_WIN)[None, :]
    band = (j >= i) & (j <= i + 2 * ATT_HALF)
    first = band & (j >= ATT_HALF)
    last = band & (j < ATT_HALF + ATT_QB)
    tabs = np.stack([first, band, last]).astype(np.float32)
    tabs = np.concatenate([tabs.transpose(0, 2, 1)] * 2, axis=2)
    return jnp.asarray((1.0 - tabs) * NEG_INF, dtype=F32)


def _head_masks():
    lane = np.arange(LANES)
    m = np.zeros((16, LANES), np.float32)
    m[0] = lane < HEAD_DIM
    m[1] = lane >= HEAD_DIM
    return jnp.asarray(m, dtype=BF16)


def _attn(q, k, v, bias, hmask, batch, seq, d, blocks_per_step=16, lb_max=2048):
    L = seq // d
    lb = min(lb_max, L)
    rpb = min(d, max(1, blocks_per_step * ATT_QB // lb))
    width = rpb * D_ATT
    nblk_total = L // ATT_QB
    assert nblk_total >= 2 and L % lb == 0 and lb % ATT_QB == 0 and d % rpb == 0
    view = lambda t: t.reshape(batch, L, d * D_ATT)
    hb = lb // ATT_HALF
    cur = lambda b, r, i: (b, i, r)
    prev = lambda b, r, i: (b, jnp.maximum(i * hb - 1, 0), r)
    nxt = lambda b, r, i: (b, jnp.minimum((i + 1) * hb, L // ATT_HALF - 1), r)
    blk = pl.BlockSpec((1, lb, width), cur)
    halo_p = pl.BlockSpec((1, ATT_HALF, width), prev)
    halo_n = pl.BlockSpec((1, ATT_HALF, width), nxt)
    o, lse = pl.pallas_call(
        functools.partial(_attn_kernel, lb=lb, nblk_total=nblk_total),
        grid=(batch, d // rpb, L // lb),
        in_specs=[blk, blk, halo_p, halo_n, blk, halo_p, halo_n,
                  pl.BlockSpec((3, ATT_WIN, 2 * ATT_QB), lambda b, r, i: (0, 0, 0)),
                  pl.BlockSpec((16, LANES), lambda b, r, i: (0, 0))],
        out_specs=[blk, pl.BlockSpec((1, lb, rpb * LANES), cur)],
        out_shape=[jax.ShapeDtypeStruct((batch, L, d * D_ATT), BF16),
                   jax.ShapeDtypeStruct((batch, L, d * LANES), F32)],
        scratch_shapes=[pltpu.VMEM((lb + 2 * ATT_HALF, width), BF16),
                        pltpu.VMEM((lb + 2 * ATT_HALF, width), BF16)],
        compiler_params=pltpu.CompilerParams(
            dimension_semantics=("arbitrary", "arbitrary", "arbitrary"),
            vmem_limit_bytes=VMEM_LIMIT),
        name=f"attn_d{d}",
    )(view(q), view(k), view(k), view(k), view(v), view(v), view(v), bias, hmask)
    return o.reshape(batch * L, d * D_ATT), lse.reshape(batch * L, d * LANES)


def _conv_kernel(uc_ref, up_ref, un_ref, w_ref, b_ref, lg_ref, lb_ref, *refs, ts, rows, ln_rows,
                 n_cast):
    cast_in, c_ref = refs[:n_cast], refs[n_cast]
    cast_out, (ext, ybuf) = refs[n_cast + 1:2 * n_cast + 1], refs[2 * n_cast + 1:]
    for src, dst in zip(cast_in, cast_out):
        dst[...] = src[...].astype(BF16)
    i = pl.program_id(1)
    n = pl.num_programs(1)
    n_slab = D_CONV // LANES
    for g in range(n_slab):
        ls = slice(g * LANES, (g + 1) * LANES)
        ext[g, 0:CONV_HALO] = jnp.where(i > 0, up_ref[0, :, ls], 0.0)
        ext[g, CONV_HALO:CONV_HALO + ts] = uc_ref[0, :, ls]
        ext[g, CONV_HALO + ts:] = jnp.where(i < n - 1, un_ref[0, :, ls], 0.0)
    lg = lg_ref[...]
    lb = lb_ref[...]
    off = CONV_HALO - CONV_PAD
    n_blk = ts // rows

    def taps(idx, carry):
        g = idx // n_blk
        base = pl.multiple_of((idx % n_blk) * rows, rows)
        acc = jnp.broadcast_to(b_ref[g], (rows, LANES))
        for j in range(SUBLANES):
            steps = [m for m in range((CONV_WIDTH + off) // SUBLANES + 1)
                     if 0 <= SUBLANES * m + j - off < CONV_WIDTH]
            win = ext[g, pl.ds(base + j, rows + SUBLANES * max(steps)), :]
            for m in steps:
                k = SUBLANES * m + j - off
                acc = acc + win[SUBLANES * m:SUBLANES * m + rows] * w_ref[g, k:k + 1, :]
        ybuf[g, pl.ds(base, rows), :] = acc
        return carry

    lax.fori_loop(0, n_slab * n_blk, taps, 0, unroll=2)
    for r0 in range(0, ts, ln_rows):
        acc = jnp.concatenate([ybuf[g, r0:r0 + ln_rows, :] for g in range(n_slab)], axis=1)
        mu = jnp.mean(acc, axis=-1, keepdims=True)
        cen = acc - mu
        var = jnp.mean(cen * cen, axis=-1, keepdims=True)
        y = cen * lax.rsqrt(var + EPS) * lg + lb
        c_ref[0, r0:r0 + ln_rows, :] = (y * (1.0 / (1.0 + jnp.exp(-y)))).astype(c_ref.dtype)


def _conv(u3, conv_w, conv_b, ln_g, ln_b, weights, ts=512, rows=128, ln_rows=32):
    B, S, C = u3.shape
    n_i = S // ts
    sl = lambda b, i: (b * n_i + i, 0)
    w_rows = [w.shape[0] // (B * n_i) for w in weights]
    w_specs = [pl.BlockSpec((r, w.shape[1]), sl) for r, w in zip(w_rows, weights)]
    hb = ts // CONV_HALO
    cur = lambda b, i: (b, i, 0)
    prev = lambda b, i: (b, jnp.maximum(i * hb - 1, 0), 0)
    nxt = lambda b, i: (b, jnp.minimum((i + 1) * hb, S // CONV_HALO - 1), 0)
    const = lambda b, i: (0, 0)
    const3 = lambda b, i: (0, 0, 0)
    n_slab = C // LANES
    w_slab = jnp.transpose(conv_w.reshape(CONV_WIDTH, n_slab, LANES), (1, 0, 2))
    b_slab = conv_b.reshape(n_slab, 1, LANES)
    outs = pl.pallas_call(
        functools.partial(_conv_kernel, ts=ts, rows=rows, ln_rows=ln_rows, n_cast=len(weights)),
        grid=(B, S // ts),
        in_specs=[pl.BlockSpec((1, ts, C), cur),
                  pl.BlockSpec((1, CONV_HALO, C), prev),
                  pl.BlockSpec((1, CONV_HALO, C), nxt),
                  pl.BlockSpec((n_slab, CONV_WIDTH, LANES), const3),
                  pl.BlockSpec((n_slab, 1, LANES), const3),
                  pl.BlockSpec((1, C), const),
                  pl.BlockSpec((1, C), const)] + w_specs,
        out_specs=[pl.BlockSpec((1, ts, C), cur)] + w_specs,
        out_shape=[jax.ShapeDtypeStruct((B, S, C), BF16)]
                  + [jax.ShapeDtypeStruct(w.shape, BF16) for w in weights],
        scratch_shapes=[pltpu.VMEM((C // LANES, ts + 2 * CONV_HALO, LANES), F32),
                        pltpu.VMEM((C // LANES, ts, LANES), F32)],
        compiler_params=pltpu.CompilerParams(dimension_semantics=("arbitrary", "arbitrary"),
                                             vmem_limit_bytes=VMEM_LIMIT),
        name="conv",
    )(u3, u3, u3, w_slab, b_slab, ln_g, ln_b, *weights)
    return outs[0], outs[1:]


def _mem_kv_kernel(mem_ref, g_ref, wk_ref, wv_ref, k_ref, v_ref):
    mn = _rms(mem_ref[...], g_ref[...]).astype(BF16)
    k_ref[...] = _dot(mn, wk_ref[...].astype(BF16)).astype(BF16)
    v_ref[...] = _dot(mn, wv_ref[...].astype(BF16)).astype(BF16)


def _mem_kv(mem2, g, wk, wv):
    R = mem2.shape[0]
    full = lambda shape: pl.BlockSpec(shape, lambda i: (0, 0))
    return pl.pallas_call(
        _mem_kv_kernel,
        grid=(1,),
        in_specs=[full((R, D_MODEL)), full((1, D_MODEL)),
                  full((D_MODEL, D_MODEL)), full((D_MODEL, D_MODEL))],
        out_specs=[full((R, D_MODEL))] * 2,
        out_shape=[jax.ShapeDtypeStruct((R, D_MODEL), BF16)] * 2,
        compiler_params=pltpu.CompilerParams(dimension_semantics=("arbitrary",),
                                             vmem_limit_bytes=VMEM_LIMIT),
        name="mem_kv",
    )(mem2, g, wk, wv)


def _post_kernel(x_ref, o1_ref, o2_ref, o3_ref, l1_ref, l2_ref, l3_ref, c_ref, e_ref,
                 wo_ref, gx_ref, wq_ref, xk_ref, xv_ref, wxo_ref, gm_ref, wu_ref, wd_ref, gf_ref,
                 out_ref, *bufs, tm, sub, chunk, final_norm):
    n_slab = D_ATT // LANES
    n_pat = len(DILATED_PATTERNS)
    o_refs = (o1_ref, o2_ref, o3_ref)
    l_refs = (l1_ref, l2_ref, l3_ref)
    n_grp = tm // sub

    def mix(g):
        t0 = g * sub
        obuf, lbuf = bufs[2 * g], bufs[2 * g + 1]
        otmp, ltmp = bufs[2 * n_grp + 2 * g], bufs[2 * n_grp + 2 * g + 1]
        for p, (_, d) in enumerate(DILATED_PATTERNS):
            if d == 1:
                continue
            src = slice(t0 // d, (t0 + sub) // d)
            jobs = [(o_refs[p], D_ATT, j, obuf, otmp, True) for j in range(n_slab)]
            jobs.append((l_refs[p], LANES, 0, lbuf, ltmp, False))
            for ref, pitch, j, dst, tmp, widen in jobs:
                pieces = {}
                for r in range(d):
                    t = ref[src, r * pitch + j * LANES:r * pitch + (j + 1) * LANES]
                    pieces[r] = t.astype(F32) if widen else t
                dd = d
                while dd > DIL_STEP:
                    lower = dd // DIL_STEP
                    merged = {}
                    for r_low in range(lower):
                        for t in range(DIL_STEP):
                            tmp[r_low, j, pl.ds(t, sub // dd, stride=DIL_STEP), :] = (
                                pieces[t * lower + r_low])
                        merged[r_low] = tmp[r_low, j, 0:sub // lower, :]
                    pieces, dd = merged, lower
                for r in range(dd):
                    dst[p, j, pl.ds(r, sub // dd, stride=DIL_STEP), :] = pieces[r]

        ls = [l_refs[p][t0:t0 + sub, :] if d == 1 else lbuf[p, 0]
              for p, (_, d) in enumerate(DILATED_PATTERNS)]
        m = jnp.maximum(jnp.maximum(ls[0], ls[1]), ls[2])
        es = [jnp.exp(l - m) for l in ls]
        inv = 1.0 / (es[0] + es[1] + es[2])
        ws = [_dot((e * inv).astype(BF16), e_ref[...]) for e in es]
        att = []
        for j in range(n_slab):
            lanes = slice(j * LANES, (j + 1) * LANES)
            os_ = [(o_refs[p][t0:t0 + sub, lanes] if d == 1 else obuf[p, j]).astype(F32)
                   for p, (_, d) in enumerate(DILATED_PATTERNS)]
            att.append((ws[0][:, lanes] * os_[0] + ws[1][:, lanes] * os_[1]
                        + ws[2][:, lanes] * os_[2]).astype(BF16))
        return jnp.concatenate(att, axis=1)

    def project(g, att):
        tr = slice(g * sub, (g + 1) * sub)
        h1 = (x_ref[tr, :] + _dot(att, wo_ref[0:D_ATT, :])
              + _dot(c_ref[tr, :], wo_ref[D_ATT:, :]))
        xq = (_dot(_rms(h1, gx_ref[...]).astype(BF16), wq_ref[...])
              * (XATT_HEAD_DIM ** -0.5)).astype(BF16)
        return h1, xq

    def cross(g, h1, xq):
        heads = []
        for h in range(XATT_HEADS):
            sl = slice(h * XATT_HEAD_DIM, (h + 1) * XATT_HEAD_DIM)
            s = _dot_nt(xq[:, sl], xk_ref[0, :, sl])
            mx = jnp.max(s, axis=-1, keepdims=True)
            p = jnp.exp(s - mx)
            den = jnp.sum(p, axis=-1, keepdims=True)
            heads.append((_dot(p.astype(BF16), xv_ref[0, :, sl]) * (1.0 / den)).astype(BF16))
        xo = jnp.concatenate(heads, axis=1)
        return h1 + _dot(xo, wxo_ref[...])

    def mlp(g, h):
        hn = _rms(h, gm_ref[...]).astype(BF16)
        acc = h
        for j in range(D_FF // chunk):
            u = jnp.maximum(_dot(hn, wu_ref[:, j * chunk:(j + 1) * chunk]), 0.0)
            acc = acc + _dot((u * u).astype(BF16), wd_ref[j * chunk:(j + 1) * chunk, :])
        out_ref[g * sub:(g + 1) * sub, :] = _rms(acc, gf_ref[...]) if final_norm else acc

    for g in range(n_grp):
        h1, xq = project(g, mix(g))
        mlp(g, cross(g, h1, xq))


def _post(x2, os_, lses, c2, w_out, gx, w_xq, xk, xv, w_xo, gm, w_up, w_down, gf, final_norm,
          seq, tm=512, sub=512, chunk=1024):
    T = x2.shape[0]
    n_mem = xk.shape[1]
    per_b = seq // tm
    n_slab = D_ATT // LANES
    d_max = DILATED_PATTERNS[-1][1]
    row = lambda i: (i, 0)
    const = lambda i: (0, 0)
    memb = lambda i: (i // per_b, 0, 0)
    whole = lambda w: pl.BlockSpec(w.shape, const, pipeline_mode=pl.Buffered(1))
    vec = pl.BlockSpec((1, D_MODEL), const)
    n_pat = len(DILATED_PATTERNS)
    return pl.pallas_call(
        functools.partial(_post_kernel, tm=tm, sub=sub, chunk=chunk, final_norm=final_norm),
        grid=(T // tm,),
        in_specs=[pl.BlockSpec((tm, D_MODEL), row)]
                 + [pl.BlockSpec((tm // d, d * D_ATT), row) for _, d in DILATED_PATTERNS]
                 + [pl.BlockSpec((tm // d, d * LANES), row) for _, d in DILATED_PATTERNS]
                 + [pl.BlockSpec((tm, D_CONV), row),
                    pl.BlockSpec((LANES, D_ATT), const),
                    whole(w_out), vec, whole(w_xq),
                    pl.BlockSpec((1, n_mem, D_MODEL), memb),
                    pl.BlockSpec((1, n_mem, D_MODEL), memb),
                    whole(w_xo), vec, whole(w_up), whole(w_down), vec],
        out_specs=pl.BlockSpec((tm, D_MODEL), row),
        out_shape=jax.ShapeDtypeStruct((T, D_MODEL), F32),
        scratch_shapes=[pltpu.VMEM((n_pat, n_slab, sub, LANES), F32),
                          pltpu.VMEM((n_pat, 1, sub, LANES), F32)] * (tm // sub)
                       + [pltpu.VMEM((d_max // DIL_STEP, n_slab, sub // DIL_STEP, LANES), F32),
                          pltpu.VMEM((d_max // DIL_STEP, 1, sub // DIL_STEP, LANES), F32)
                          ] * (tm // sub),
        compiler_params=pltpu.CompilerParams(dimension_semantics=("arbitrary",),
                                             vmem_limit_bytes=VMEM_LIMIT),
        name="post",
    )(x2, *os_, *lses, c2, _head_expand(), w_out, gx, w_xq, xk, xv, w_xo, gm, w_up, w_down, gf)


def _head_expand():
    e = np.zeros((LANES, D_ATT), np.float32)
    for h in range(ATT_HEADS):
        e[h, h * HEAD_DIM:(h + 1) * HEAD_DIM] = 1.0
    return jnp.asarray(e, dtype=BF16)


def _rotary_tables(seq):
    half = ROT_DIM // 2
    freqs = ROPE_THETA ** (-np.arange(0, ROT_DIM, 2, dtype=np.float64) / ROT_DIM)
    ang = np.arange(seq, dtype=np.float64)[:, None] * freqs[None, :]
    cos, sin = np.cos(ang), np.sin(ang)
    zeros = np.zeros((seq, HEAD_DIM - ROT_DIM))
    z8 = np.zeros((seq, half))
    rep = LANES // HEAD_DIM
    tabs = ([cos, cos, zeros + 1.0], [-sin, z8, zeros], [z8, sin, zeros])
    return tuple(jnp.asarray(np.concatenate(t * rep, axis=1), dtype=F32) for t in tabs)


def kernel(x, mem, norm_mix_g, w_in, conv_w, conv_b, conv_ln_g, conv_ln_b, w_out, norm_x_g,
           norm_mem_g, w_xq, w_xk, w_xv, w_xo, norm_mlp_g, w_up, w_down, norm_final_g):
    B, S, D = x.shape
    n_mem = mem.shape[1]
    depth = w_in.shape[0]
    T = B * S
    cos_t, sa_t, sb_t = _rotary_tables(S)
    bias = _attn_bias()
    hmask = _head_masks()
    row = lambda g: g.reshape(1, -1)

    h = x.reshape(T, D)
    for l in range(depth):
        q, k, v, u = _in_proj(h, row(norm_mix_g[l]), w_in[l], cos_t, sa_t, sb_t, S)
        os_, lses = [], []
        for p, (_, d) in enumerate(DILATED_PATTERNS):
            o, lse = _attn(q[p], k[p], v[p], bias, hmask, B, S, d)
            os_.append(o)
            lses.append(lse)
        c, (w_out_b, w_xq_b, w_xo_b, w_up_b, w_down_b) = _conv(
            u.reshape(B, S, D_CONV), conv_w[l], row(conv_b[l]), row(conv_ln_g[l]),
            row(conv_ln_b[l]), (w_out[l], w_xq[l], w_xo[l], w_up[l], w_down[l]))
        xk, xv = _mem_kv(mem.reshape(B * n_mem, D), row(norm_mem_g[l]), w_xk[l], w_xv[l])
        h = _post(h, os_, lses, c.reshape(T, D_CONV), w_out_b, row(norm_x_g[l]), w_xq_b,
                  xk.reshape(B, n_mem, D), xv.reshape(B, n_mem, D), w_xo_b, row(norm_mlp_g[l]),
                  w_up_b, w_down_b, row(norm_final_g), final_norm=(l == depth - 1), seq=S)
    return h.reshape(B, S, D)
```

```python
import functools
import math

import numpy as np
import jax
import jax.numpy as jnp
from jax import lax
from jax.experimental import pallas as pl
from jax.experimental.pallas import tpu as pltpu

F32 = jnp.float32
BF16 = jnp.bfloat16

D_MODEL = 1024
ATT_HEADS = 8
HEAD_DIM = 64
D_ATT = ATT_HEADS * HEAD_DIM
D_CONV = D_MODEL - D_ATT
DILATED_PATTERNS = ((128, 1), (512, 4), (2048, 16))
DIL_STEP = 4
assert all(d == DIL_STEP ** k for k, (_, d) in enumerate(DILATED_PATTERNS))
ROPE_THETA = 500000.0
ROT_DIM = HEAD_DIM // 4
CONV_WIDTH = 31
CONV_PAD = (CONV_WIDTH - 1) // 2
XATT_HEADS = 4
XATT_HEAD_DIM = D_MODEL // XATT_HEADS
D_FF = 4 * D_MODEL
EPS = 1e-6
NEG_INF = -1e30
LN2 = math.log(2.0)
Q_SCALE = HEAD_DIM ** -0.5 / LN2

LANES = 128
SUBLANES = 8
ATT_HALF = 64
ATT_QB = 2 * ATT_HALF
ATT_WIN = ATT_QB + 2 * ATT_HALF
CONV_HALO = 16
GLU_COLS = 256
VMEM_LIMIT = 56 * 1024 * 1024


def _dot(a, b):
    return jnp.dot(a, b, preferred_element_type=F32)


def _dot_nt(a, b):
    return lax.dot_general(a, b, (((1,), (1,)), ((), ())), preferred_element_type=F32)


def _dot_tn(a, b):
    return lax.dot_general(a, b, (((0,), (0,)), ((), ())), preferred_element_type=F32)


def _rms(x, g):
    var = jnp.mean(x * x, axis=-1, keepdims=True)
    return x * lax.rsqrt(var + EPS) * g


def _cast_once(pairs, chunk=512):
    @pl.when(pl.program_id(0) == 0)
    def _():
        for src, dst in pairs:
            for c in range(0, src.shape[1], chunk):
                dst[:, c:c + chunk] = src[:, c:c + chunk].astype(BF16)


def _in_proj_kernel(x_ref, g_ref, wf_ref, cos_ref, sa_ref, sb_ref, *refs, tm):
    n_pat = len(DILATED_PATTERNS)
    qkv_refs = [refs[a * n_pat:(a + 1) * n_pat] for a in range(3)]
    u_ref = refs[3 * n_pat]
    w_ref = refs[3 * n_pat + 1]
    stage = refs[3 * n_pat + 2:]
    _cast_once([(wf_ref, w_ref)])
    xn = _rms(x_ref[...], g_ref[...]).astype(BF16)
    cos, sa, sb = cos_ref[...], sa_ref[...], sb_ref[...]
    n_slab = D_ATT // LANES

    def plain(y, j):
        return y[:, LANES * j:LANES * (j + 1)]

    def rot(y, j):
        yj = plain(y, j)
        return (yj * cos + pltpu.roll(yj, LANES - ROT_DIM // 2, 1) * sa
                + pltpu.roll(yj, ROT_DIM // 2, 1) * sb)

    def emit(y, outs, transform):
        for j in range(n_slab):
            piece = transform(y, j)
            stage[0][0, j] = piece
            outs[0][:, j * LANES:(j + 1) * LANES] = piece.astype(BF16)
        for k in range(1, n_pat):
            d_prev, d = DILATED_PATTERNS[k - 1][1], DILATED_PATTERNS[k][1]
            for r_prev in range(d_prev):
                for t in range(DIL_STEP):
                    r = t * d_prev + r_prev
                    for j in range(n_slab):
                        piece = stage[k - 1][r_prev, j, pl.ds(t, tm // d, stride=DIL_STEP), :]
                        if k + 1 < n_pat:
                            stage[k][r, j] = piece
                        c0 = r * D_ATT + j * LANES
                        outs[k][:, c0:c0 + LANES] = piece.astype(BF16)

    emit(_dot(xn, w_ref[:, 0:D_ATT]), qkv_refs[0], lambda y, j: rot(y, j) * Q_SCALE)
    emit(_dot(xn, w_ref[:, D_ATT:2 * D_ATT]), qkv_refs[1], rot)
    emit(_dot(xn, w_ref[:, 2 * D_ATT:3 * D_ATT]), qkv_refs[2], plain)
    c0 = 3 * D_ATT
    for cc in range(0, D_CONV, GLU_COLS):
        a = _dot(xn, w_ref[:, c0 + cc:c0 + cc + GLU_COLS])
        gt = _dot(xn, w_ref[:, c0 + D_CONV + cc:c0 + D_CONV + cc + GLU_COLS])
        u_ref[:, cc:cc + GLU_COLS] = (a * (1.0 / (1.0 + jnp.exp(-gt)))).astype(u_ref.dtype)


def _in_proj(x2, g, w_in, cos_t, sa_t, sb_t, seq, tm=1024):
    T = x2.shape[0]
    d_in = w_in.shape[1]
    n_s = seq // tm
    n_pat = len(DILATED_PATTERNS)
    row = lambda i: (i, 0)
    tab = lambda i: (i % n_s, 0)
    const = lambda i: (0, 0)
    qkv_specs, qkv_shapes = [], []
    for _ in range(3):
        for _, d in DILATED_PATTERNS:
            qkv_specs.append(pl.BlockSpec((tm // d, d * D_ATT), row))
            qkv_shapes.append(jax.ShapeDtypeStruct((T // d, d * D_ATT), BF16))
    outs = pl.pallas_call(
        functools.partial(_in_proj_kernel, tm=tm),
        grid=(T // tm,),
        in_specs=[pl.BlockSpec((tm, D_MODEL), row),
                  pl.BlockSpec((1, D_MODEL), const),
                  pl.BlockSpec((D_MODEL, d_in), const, pipeline_mode=pl.Buffered(1)),
                  pl.BlockSpec((tm, LANES), tab),
                  pl.BlockSpec((tm, LANES), tab),
                  pl.BlockSpec((tm, LANES), tab)],
        out_specs=qkv_specs + [pl.BlockSpec((tm, D_CONV), row)],
        out_shape=qkv_shapes + [jax.ShapeDtypeStruct((T, D_CONV), BF16)],
        scratch_shapes=[pltpu.VMEM((D_MODEL, d_in), BF16)]
                       + [pltpu.VMEM((d, D_ATT // LANES, tm // d, LANES), F32)
                          for _, d in DILATED_PATTERNS[:-1]],
        compiler_params=pltpu.CompilerParams(dimension_semantics=("arbitrary",),
                                             vmem_limit_bytes=VMEM_LIMIT),
        name="in_proj",
    )(x2, g, w_in, cos_t, sa_t, sb_t)
    return outs[0:n_pat], outs[n_pat:2 * n_pat], outs[2 * n_pat:3 * n_pat], outs[3 * n_pat]


def _attn_kernel(q_ref, kc_ref, kp_ref, kn_ref, vc_ref, vp_ref, vn_ref, bias_ref, hm_ref,
                 o_ref, lse_ref, kext, vext, *, lb, nblk_total):
    i = pl.program_id(2)
    kext[0:ATT_HALF] = kp_ref[0]
    kext[ATT_HALF:ATT_HALF + lb] = kc_ref[0]
    kext[ATT_HALF + lb:] = kn_ref[0]
    vext[0:ATT_HALF] = vp_ref[0]
    vext[ATT_HALF:ATT_HALF + lb] = vc_ref[0]
    vext[ATT_HALF + lb:] = vn_ref[0]

    mask_a = hm_ref[0:1, :]
    mask_b = hm_ref[1:2, :]
    nblk = lb // ATT_QB

    def body(n, carry):
        r0 = pl.multiple_of(n * ATT_QB, ATT_QB)
        gblk = i * nblk + n
        bidx = jnp.where(gblk == 0, 0, jnp.where(gblk == nblk_total - 1, 2, 1))
        bias = bias_ref[bidx]
        lse_rows = []
        for hp in range(q_ref.shape[2] // LANES):
            c0 = LANES * hp
            q2 = q_ref[0, pl.ds(r0, ATT_QB), c0:c0 + LANES]
            qs = jnp.concatenate([q2 * mask_a, q2 * mask_b], axis=0)
            parts = []
            for kh in range(ATT_WIN // ATT_QB):
                k0 = r0 + kh * ATT_QB
                kw = kext[pl.ds(k0, ATT_QB), c0:c0 + LANES]
                vw = vext[pl.ds(k0, ATT_QB), c0:c0 + LANES]
                s = _dot_nt(kw, qs) + bias[kh * ATT_QB:(kh + 1) * ATT_QB]
                m = jnp.max(s, axis=0, keepdims=True)
                p = jnp.exp2(s - m)
                l = jnp.sum(p, axis=0, keepdims=True)
                parts.append((m, l, _dot_tn(vw, p.astype(BF16))))
            (m1, l1, o1), (m2, l2, o2) = parts
            m = jnp.maximum(m1, m2)
            a1 = jnp.exp2(m1 - m)
            a2 = jnp.exp2(m2 - m)
            l = a1 * l1 + a2 * l2
            inv = 1.0 / l
            ot = o1 * (a1 * inv) + o2 * (a2 * inv)
            own = jnp.concatenate([ot[0:HEAD_DIM, 0:ATT_QB], ot[HEAD_DIM:, ATT_QB:]], axis=0)
            o_ref[0, pl.ds(r0, ATT_QB), c0:c0 + LANES] = jnp.transpose(own).astype(o_ref.dtype)
            lse = (m + jnp.log2(l)) * LN2
            lse_rows += [lse[:, 0:ATT_QB], lse[:, ATT_QB:]]
        for rc in range(len(lse_rows) // ATT_HEADS):
            rows = lse_rows[rc * ATT_HEADS:(rc + 1) * ATT_HEADS]
            pad = jnp.zeros((LANES - ATT_HEADS, ATT_QB), F32)
            lse_ref[0, pl.ds(r0, ATT_QB), rc * LANES:(rc + 1) * LANES] = jnp.transpose(
                jnp.concatenate(rows + [pad], axis=0))
        return carry

    lax.fori_loop(0, nblk, body, 0, unroll=True)


def _attn_bias():
    i = np.arange(ATT_QB)[:, None]
    j = np.arange(ATT_WIN)[None, :]
    band = (j >= i) & (j <= i + 2 * ATT_HALF)
    first = band & (j >= ATT_HALF)
    last = band & (j < ATT_HALF + ATT_QB)
    tabs = np.stack([first, band, last]).astype(np.float32)
    tabs = np.concatenate([tabs.transpose(0, 2, 1)] * 2, axis=2)
    return jnp.asarray((1.0 - tabs) * NEG_INF, dtype=F32)


def _head_masks():
    lane = np.arange(LANES)
    m = np.zeros((16, LANES), np.float32)
    m[0] = lane < HEAD_DIM
    m[1] = lane >= HEAD_DIM
    return jnp.asarray(m, dtype=BF16)


def _attn(q, k, v, bias, hmask, batch, seq, d, blocks_per_step=16, lb_max=2048):
    L = seq // d
    lb = min(lb_max, L)
    rpb = min(d, max(1, blocks_per_step * ATT_QB // lb))
    width = rpb * D_ATT
    nblk_total = L // ATT_QB
    assert nblk_total >= 2 and L % lb == 0 and lb % ATT_QB == 0 and d % rpb == 0
    view = lambda t: t.reshape(batch, L, d * D_ATT)
    hb = lb // ATT_HALF
    cur = lambda b, r, i: (b, i, r)
    prev = lambda b, r, i: (b, jnp.maximum(i * hb - 1, 0), r)
    nxt = lambda b, r, i: (b, jnp.minimum((i + 1) * hb, L // ATT_HALF - 1), r)
    blk = pl.BlockSpec((1, lb, width), cur)
    halo_p = pl.BlockSpec((1, ATT_HALF, width), prev)
    halo_n = pl.BlockSpec((1, ATT_HALF, width), nxt)
    o, lse = pl.pallas_call(
        functools.partial(_attn_kernel, lb=lb, nblk_total=nblk_total),
        grid=(batch, d // rpb, L // lb),
        in_specs=[blk, blk, halo_p, halo_n, blk, halo_p, halo_n,
                  pl.BlockSpec((3, ATT_WIN, 2 * ATT_QB), lambda b, r, i: (0, 0, 0)),
                  pl.BlockSpec((16, LANES), lambda b, r, i: (0, 0))],
        out_specs=[blk, pl.BlockSpec((1, lb, rpb * LANES), cur)],
        out_shape=[jax.ShapeDtypeStruct((batch, L, d * D_ATT), BF16),
                   jax.ShapeDtypeStruct((batch, L, d * LANES), F32)],
        scratch_shapes=[pltpu.VMEM((lb + 2 * ATT_HALF, width), BF16),
                        pltpu.VMEM((lb + 2 * ATT_HALF, width), BF16)],
        compiler_params=pltpu.CompilerParams(
            dimension_semantics=("arbitrary", "arbitrary", "arbitrary"),
            vmem_limit_bytes=VMEM_LIMIT),
        name=f"attn_d{d}",
    )(view(q), view(k), view(k), view(k), view(v), view(v), view(v), bias, hmask)
    return o.reshape(batch * L, d * D_ATT), lse.reshape(batch * L, d * LANES)


def _conv_kernel(uc_ref, up_ref, un_ref, w_ref, b_ref, lg_ref, lb_ref, *refs, ts, rows, ln_rows,
                 n_cast):
    cast_in, c_ref = refs[:n_cast], refs[n_cast]
    cast_out, (ext, ybuf) = refs[n_cast + 1:2 * n_cast + 1], refs[2 * n_cast + 1:]
    for src, dst in zip(cast_in, cast_out):
        dst[...] = src[...].astype(BF16)
    i = pl.program_id(1)
    n = pl.num_programs(1)
    n_slab = D_CONV // LANES
    for g in range(n_slab):
        ls = slice(g * LANES, (g + 1) * LANES)
        ext[g, 0:CONV_HALO] = jnp.where(i > 0, up_ref[0, :, ls].astype(F32), 0.0)
        ext[g, CONV_HALO:CONV_HALO + ts] = uc_ref[0, :, ls].astype(F32)
        ext[g, CONV_HALO + ts:] = jnp.where(i < n - 1, un_ref[0, :, ls].astype(F32), 0.0)
    lg = lg_ref[...]
    lb = lb_ref[...]
    off = CONV_HALO - CONV_PAD
    n_blk = ts // rows

    def taps(idx, carry):
        g = idx // n_blk
        base = pl.multiple_of((idx % n_blk) * rows, rows)
        acc = jnp.broadcast_to(b_ref[g], (rows, LANES))
        for j in range(SUBLANES):
            steps = [m for m in range((CONV_WIDTH + off) // SUBLANES + 1)
                     if 0 <= SUBLANES * m + j - off < CONV_WIDTH]
            win = ext[g, pl.ds(base + j, rows + SUBLANES * max(steps)), :]
            for m in steps:
                k = SUBLANES * m + j - off
                acc = acc + win[SUBLANES * m:SUBLANES * m + rows] * w_ref[g, k:k + 1, :]
        ybuf[g, pl.ds(base, rows), :] = acc
        return carry

    lax.fori_loop(0, n_slab * n_blk, taps, 0, unroll=2)
    for r0 in range(0, ts, ln_rows):
        acc = jnp.concatenate([ybuf[g, r0:r0 + ln_rows, :] for g in range(n_slab)], axis=1)
        mu = jnp.mean(acc, axis=-1, keepdims=True)
        cen = acc - mu
        var = jnp.mean(cen * cen, axis=-1, keepdims=True)
        y = cen * lax.rsqrt(var + EPS) * lg + lb
        c_ref[0, r0:r0 + ln_rows, :] = (y * (1.0 / (1.0 + jnp.exp(-y)))).astype(c_ref.dtype)


def _conv(u3, conv_w, conv_b, ln_g, ln_b, weights, ts=512, rows=128, ln_rows=32):
    B, S, C = u3.shape
    n_i = S // ts
    sl = lambda b, i: (b * n_i + i, 0)
    w_rows = [w.shape[0] // (B * n_i) for w in weights]
    w_specs = [pl.BlockSpec((r, w.shape[1]), sl) for r, w in zip(w_rows, weights)]
    hb = ts // CONV_HALO
    cur = lambda b, i: (b, i, 0)
    prev = lambda b, i: (b, jnp.maximum(i * hb - 1, 0), 0)
    nxt = lambda b, i: (b, jnp.minimum((i + 1) * hb, S // CONV_HALO - 1), 0)
    const = lambda b, i: (0, 0)
    const3 = lambda b, i: (0, 0, 0)
    n_slab = C // LANES
    w_slab = jnp.transpose(conv_w.reshape(CONV_WIDTH, n_slab, LANES), (1, 0, 2))
    b_slab = conv_b.reshape(n_slab, 1, LANES)
    outs = pl.pallas_call(
        functools.partial(_conv_kernel, ts=ts, rows=rows, ln_rows=ln_rows, n_cast=len(weights)),
        grid=(B, S // ts),
        in_specs=[pl.BlockSpec((1, ts, C), cur),
                  pl.BlockSpec((1, CONV_HALO, C), prev),
                  pl.BlockSpec((1, CONV_HALO, C), nxt),
                  pl.BlockSpec((n_slab, CONV_WIDTH, LANES), const3),
                  pl.BlockSpec((n_slab, 1, LANES), const3),
                  pl.BlockSpec((1, C), const),
                  pl.BlockSpec((1, C), const)] + w_specs,
        out_specs=[pl.BlockSpec((1, ts, C), cur)] + w_specs,
        out_shape=[jax.ShapeDtypeStruct((B, S, C), BF16)]
                  + [jax.ShapeDtypeStruct(w.shape, BF16) for w in weights],
        scratch_shapes=[pltpu.VMEM((C // LANES, ts + 2 * CONV_HALO, LANES), F32),
                        pltpu.VMEM((C // LANES, ts, LANES), F32)],
        compiler_params=pltpu.CompilerParams(dimension_semantics=("arbitrary", "arbitrary"),
                                             vmem_limit_bytes=VMEM_LIMIT),
        name="conv",
    )(u3, u3, u3, w_slab, b_slab, ln_g, ln_b, *weights)
    return outs[0], outs[1:]


def _mem_kv_kernel(mem_ref, g_ref, wk_ref, wv_ref, k_ref, v_ref):
    mn = _rms(mem_ref[...], g_ref[...]).astype(BF16)
    k_ref[...] = _dot(mn, wk_ref[...].astype(BF16)).astype(BF16)
    v_ref[...] = _dot(mn, wv_ref[...].astype(BF16)).astype(BF16)


def _mem_kv(mem2, g, wk, wv):
    R = mem2.shape[0]
    full = lambda shape: pl.BlockSpec(shape, lambda i: (0, 0))
    return pl.pallas_call(
        _mem_kv_kernel,
        grid=(1,),
        in_specs=[full((R, D_MODEL)), full((1, D_MODEL)),
                  full((D_MODEL, D_MODEL)), full((D_MODEL, D_MODEL))],
        out_specs=[full((R, D_MODEL))] * 2,
        out_shape=[jax.ShapeDtypeStruct((R, D_MODEL), BF16)] * 2,
        compiler_params=pltpu.CompilerParams(dimension_semantics=("arbitrary",),
                                             vmem_limit_bytes=VMEM_LIMIT),
        name="mem_kv",
    )(mem2, g, wk, wv)


def _post_kernel(x_ref, o1_ref, o2_ref, o3_ref, l1_ref, l2_ref, l3_ref, c_ref, e_ref,
                 wo_ref, gx_ref, wq_ref, xk_ref, xv_ref, wxo_ref, gm_ref, wu_ref, wd_ref, gf_ref,
                 out_ref, *bufs, tm, sub, chunk, final_norm):
    n_slab = D_ATT // LANES
    n_pat = len(DILATED_PATTERNS)
    o_refs = (o1_ref, o2_ref, o3_ref)
    l_refs = (l1_ref, l2_ref, l3_ref)
    n_grp = tm // sub

    def mix(g):
        t0 = g * sub
        obuf, lbuf = bufs[2 * g], bufs[2 * g + 1]
        otmp, ltmp = bufs[2 * n_grp + 2 * g], bufs[2 * n_grp + 2 * g + 1]
        for p, (_, d) in enumerate(DILATED_PATTERNS):
            if d == 1:
                continue
            src = slice(t0 // d, (t0 + sub) // d)
            jobs = [(o_refs[p], D_ATT, j, obuf, otmp, True) for j in range(n_slab)]
            jobs.append((l_refs[p], LANES, 0, lbuf, ltmp, False))
            for ref, pitch, j, dst, tmp, widen in jobs:
                pieces = {}
                for r in range(d):
                    t = ref[src, r * pitch + j * LANES:r * pitch + (j + 1) * LANES]
                    pieces[r] = t.astype(F32) if widen else t
                dd = d
                while dd > DIL_STEP:
                    lower = dd // DIL_STEP
                    merged = {}
                    for r_low in range(lower):
                        for t in range(DIL_STEP):
                            tmp[r_low, j, pl.ds(t, sub // dd, stride=DIL_STEP), :] = (
                                pieces[t * lower + r_low])
                        merged[r_low] = tmp[r_low, j, 0:sub // lower, :]
                    pieces, dd = merged, lower
                for r in range(dd):
                    dst[p, j, pl.ds(r, sub // dd, stride=DIL_STEP), :] = pieces[r]

        ls = [l_refs[p][t0:t0 + sub, :] if d == 1 else lbuf[p, 0]
              for p, (_, d) in enumerate(DILATED_PATTERNS)]
        m = jnp.maximum(jnp.maximum(ls[0], ls[1]), ls[2])
        es = [jnp.exp(l - m) for l in ls]
        inv = 1.0 / (es[0] + es[1] + es[2])
        ws = [_dot((e * inv).astype(BF16), e_ref[...]) for e in es]
        att = []
        for j in range(n_slab):
            lanes = slice(j * LANES, (j + 1) * LANES)
            os_ = [(o_refs[p][t0:t0 + sub, lanes] if d == 1 else obuf[p, j]).astype(F32)
                   for p, (_, d) in enumerate(DILATED_PATTERNS)]
            att.append((ws[0][:, lanes] * os_[0] + ws[1][:, lanes] * os_[1]
                        + ws[2][:, lanes] * os_[2]).astype(BF16))
        return jnp.concatenate(att, axis=1)

    def project(g, att):
        tr = slice(g * sub, (g + 1) * sub)
        h1 = (x_ref[tr, :] + _dot(att, wo_ref[0:D_ATT, :])
              + _dot(c_ref[tr, :], wo_ref[D_ATT:, :]))
        xq = (_dot(_rms(h1, gx_ref[...]).astype(BF16), wq_ref[...])
              * (XATT_HEAD_DIM ** -0.5)).astype(BF16)
        return h1, xq

    def cross(g, h1, xq):
        heads = []
        for h in range(XATT_HEADS):
            sl = slice(h * XATT_HEAD_DIM, (h + 1) * XATT_HEAD_DIM)
            s = _dot_nt(xq[:, sl], xk_ref[0, :, sl])
            mx = jnp.max(s, axis=-1, keepdims=True)
            p = jnp.exp(s - mx)
            den = jnp.sum(p, axis=-1, keepdims=True)
            heads.append((_dot(p.astype(BF16), xv_ref[0, :, sl]) * (1.0 / den)).astype(BF16))
        xo = jnp.concatenate(heads, axis=1)
        return h1 + _dot(xo, wxo_ref[...])

    def mlp(g, h):
        hn = _rms(h, gm_ref[...]).astype(BF16)
        acc = h
        for j in range(D_FF // chunk):
            u = jnp.maximum(_dot(hn, wu_ref[:, j * chunk:(j + 1) * chunk]), 0.0)
            acc = acc + _dot((u * u).astype(BF16), wd_ref[j * chunk:(j + 1) * chunk, :])
        out_ref[g * sub:(g + 1) * sub, :] = _rms(acc, gf_ref[...]) if final_norm else acc

    for g in range(n_grp):
        h1, xq = project(g, mix(g))
        mlp(g, cross(g, h1, xq))


def _post(x2, os_, lses, c2, w_out, gx, w_xq, xk, xv, w_xo, gm, w_up, w_down, gf, final_norm,
          seq, tm=512, sub=512, chunk=1024):
    T = x2.shape[0]
    n_mem = xk.shape[1]
    per_b = seq // tm
    n_slab = D_ATT // LANES
    d_max = DILATED_PATTERNS[-1][1]
    row = lambda i: (i, 0)
    const = lambda i: (0, 0)
    memb = lambda i: (i // per_b, 0, 0)
    whole = lambda w: pl.BlockSpec(w.shape, const, pipeline_mode=pl.Buffered(1))
    vec = pl.BlockSpec((1, D_MODEL), const)
    n_pat = len(DILATED_PATTERNS)
    return pl.pallas_call(
        functools.partial(_post_kernel, tm=tm, sub=sub, chunk=chunk, final_norm=final_norm),
        grid=(T // tm,),
        in_specs=[pl.BlockSpec((tm, D_MODEL), row)]
                 + [pl.BlockSpec((tm // d, d * D_ATT), row) for _, d in DILATED_PATTERNS]
                 + [pl.BlockSpec((tm // d, d * LANES), row) for _, d in DILATED_PATTERNS]
                 + [pl.BlockSpec((tm, D_CONV), row),
                    pl.BlockSpec((LANES, D_ATT), const),
                    whole(w_out), vec, whole(w_xq),
                    pl.BlockSpec((1, n_mem, D_MODEL), memb),
                    pl.BlockSpec((1, n_mem, D_MODEL), memb),
                    whole(w_xo), vec, whole(w_up), whole(w_down), vec],
        out_specs=pl.BlockSpec((tm, D_MODEL), row),
        out_shape=jax.ShapeDtypeStruct((T, D_MODEL), F32),
        scratch_shapes=[pltpu.VMEM((n_pat, n_slab, sub, LANES), F32),
                          pltpu.VMEM((n_pat, 1, sub, LANES), F32)] * (tm // sub)
                       + [pltpu.VMEM((d_max // DIL_STEP, n_slab, sub // DIL_STEP, LANES), F32),
                          pltpu.VMEM((d_max // DIL_STEP, 1, sub // DIL_STEP, LANES), F32)
                          ] * (tm // sub),
        compiler_params=pltpu.CompilerParams(dimension_semantics=("arbitrary",),
                                             vmem_limit_bytes=VMEM_LIMIT),
        name="post",
    )(x2, *os_, *lses, c2, _head_expand(), w_out, gx, w_xq, xk, xv, w_xo, gm, w_up, w_down, gf)


def _head_expand():
    e = np.zeros((LANES, D_ATT), np.float32)
    for h in range(ATT_HEADS):
        e[h, h * HEAD_DIM:(h + 1) * HEAD_DIM] = 1.0
    return jnp.asarray(e, dtype=BF16)


def _rotary_tables(seq):
    half = ROT_DIM // 2
    freqs = ROPE_THETA ** (-np.arange(0, ROT_DIM, 2, dtype=np.float64) / ROT_DIM)
    ang = np.arange(seq, dtype=np.float64)[:, None] * freqs[None, :]
    cos, sin = np.cos(ang), np.sin(ang)
    zeros = np.zeros((seq, HEAD_DIM - ROT_DIM))
    z8 = np.zeros((seq, half))
    rep = LANES // HEAD_DIM
    tabs = ([cos, cos, zeros + 1.0], [-sin, z8, zeros], [z8, sin, zeros])
    return tuple(jnp.asarray(np.concatenate(t * rep, axis=1), dtype=F32) for t in tabs)


def kernel(x, mem, norm_mix_g, w_in, conv_w, conv_b, conv_ln_g, conv_ln_b, w_out, norm_x_g,
           norm_mem_g, w_xq, w_xk, w_xv, w_xo, norm_mlp_g, w_up, w_down, norm_final_g):
    B, S, D = x.shape
    n_mem = mem.shape[1]
    depth = w_in.shape[0]
    T = B * S
    cos_t, sa_t, sb_t = _rotary_tables(S)
    bias = _attn_bias()
    hmask = _head_masks()
    row = lambda g: g.reshape(1, -1)

    h = x.reshape(T, D)
    for l in range(depth):
        q, k, v, u = _in_proj(h, row(norm_mix_g[l]), w_in[l], cos_t, sa_t, sb_t, S)
        os_, lses = [], []
        for p, (_, d) in enumerate(DILATED_PATTERNS):
            o, lse = _attn(q[p], k[p], v[p], bias, hmask, B, S, d)
            os_.append(o)
            lses.append(lse)
        c, (w_out_b, w_xq_b, w_xo_b, w_up_b, w_down_b) = _conv(
            u.reshape(B, S, D_CONV), conv_w[l], row(conv_b[l]), row(conv_ln_g[l]),
            row(conv_ln_b[l]), (w_out[l], w_xq[l], w_xo[l], w_up[l], w_down[l]))
        xk, xv = _mem_kv(mem.reshape(B * n_mem, D), row(norm_mem_g[l]), w_xk[l], w_xv[l])
        h = _post(h, os_, lses, c.reshape(T, D_CONV), w_out_b, row(norm_x_g[l]), w_xq_b,
                  xk.reshape(B, n_mem, D), xv.reshape(B, n_mem, D), w_xo_b, row(norm_mlp_g[l]),
                  w_up_b, w_down_b, row(norm_final_g), final_norm=(l == depth - 1), seq=S)
    return h.reshape(B, S, D)
```

```python
import functools
import math

import numpy as np
import jax
import jax.numpy as jnp
from jax import lax
from jax.experimental import pallas as pl
from jax.experimental.pallas import tpu as pltpu

F32 = jnp.float32
BF16 = jnp.bfloat16

D_MODEL = 1024
ATT_HEADS = 8
HEAD_DIM = 64
D_ATT = ATT_HEADS * HEAD_DIM
D_CONV = D_MODEL - D_ATT
DILATED_PATTERNS = ((128, 1), (512, 4), (2048, 16))
DIL_STEP = 4
assert all(d == DIL_STEP ** k for k, (_, d) in enumerate(DILATED_PATTERNS))
ROPE_THETA = 500000.0
ROT_DIM = HEAD_DIM // 4
CONV_WIDTH = 31
CONV_PAD = (CONV_WIDTH - 1) // 2
XATT_HEADS = 4
XATT_HEAD_DIM = D_MODEL // XATT_HEADS
D_FF = 4 * D_MODEL
EPS = 1e-6
NEG_INF = -1e30
LN2 = math.log(2.0)
Q_SCALE = HEAD_DIM ** -0.5 / LN2

LANES = 128
SUBLANES = 8
ATT_HALF = 64
ATT_QB = 2 * ATT_HALF
ATT_WIN = ATT_QB + 2 * ATT_HALF
CONV_HALO = 16
GLU_COLS = 256
VMEM_LIMIT = 56 * 1024 * 1024


def _dot(a, b):
    return jnp.dot(a, b, preferred_element_type=F32)


def _dot_nt(a, b):
    return lax.dot_general(a, b, (((1,), (1,)), ((), ())), preferred_element_type=F32)


def _dot_tn(a, b):
    return lax.dot_general(a, b, (((0,), (0,)), ((), ())), preferred_element_type=F32)


def _rms(x, g):
    var = jnp.mean(x * x, axis=-1, keepdims=True)
    return x * lax.rsqrt(var + EPS) * g


def _cast_once(pairs, chunk=512):
    @pl.when(pl.program_id(0) == 0)
    def _():
        for src, dst in pairs:
            for c in range(0, src.shape[1], chunk):
                dst[:, c:c + chunk] = src[:, c:c + chunk].astype(BF16)


def _in_proj_kernel(x_ref, g_ref, wf_ref, cos_ref, sa_ref, sb_ref, *refs, tm):
    n_pat = len(DILATED_PATTERNS)
    qkv_refs = [refs[a * n_pat:(a + 1) * n_pat] for a in range(3)]
    u_ref = refs[3 * n_pat]
    w_ref = refs[3 * n_pat + 1]
    stage = refs[3 * n_pat + 2:]
    _cast_once([(wf_ref, w_ref)])
    xn = _rms(x_ref[...], g_ref[...]).astype(BF16)
    cos, sa, sb = cos_ref[...], sa_ref[...], sb_ref[...]
    n_slab = D_ATT // LANES

    def plain(y, j):
        return y[:, LANES * j:LANES * (j + 1)]

    def rot(y, j):
        yj = plain(y, j)
        return (yj * cos + pltpu.roll(yj, LANES - ROT_DIM // 2, 1) * sa
                + pltpu.roll(yj, ROT_DIM // 2, 1) * sb)

    def emit(y, outs, transform):
        for j in range(n_slab):
            piece = transform(y, j)
            stage[0][0, j] = piece
            outs[0][:, j * LANES:(j + 1) * LANES] = piece.astype(BF16)
        for k in range(1, n_pat):
            d_prev, d = DILATED_PATTERNS[k - 1][1], DILATED_PATTERNS[k][1]
            for r_prev in range(d_prev):
                for t in range(DIL_STEP):
                    r = t * d_prev + r_prev
                    for j in range(n_slab):
                        piece = stage[k - 1][r_prev, j, pl.ds(t, tm // d, stride=DIL_STEP), :]
                        if k + 1 < n_pat:
                            stage[k][r, j] = piece
                        c0 = r * D_ATT + j * LANES
                        outs[k][:, c0:c0 + LANES] = piece.astype(BF16)

    emit(_dot(xn, w_ref[:, 0:D_ATT]), qkv_refs[0], lambda y, j: rot(y, j) * Q_SCALE)
    emit(_dot(xn, w_ref[:, D_ATT:2 * D_ATT]), qkv_refs[1], rot)
    emit(_dot(xn, w_ref[:, 2 * D_ATT:3 * D_ATT]), qkv_refs[2], plain)
    c0 = 3 * D_ATT
    for cc in range(0, D_CONV, GLU_COLS):
        a = _dot(xn, w_ref[:, c0 + cc:c0 + cc + GLU_COLS])
        gt = _dot(xn, w_ref[:, c0 + D_CONV + cc:c0 + D_CONV + cc + GLU_COLS])
        u_ref[:, cc:cc + GLU_COLS] = (a * (1.0 / (1.0 + jnp.exp(-gt)))).astype(u_ref.dtype)


def _in_proj(x2, g, w_in, cos_t, sa_t, sb_t, seq, tm=1024):
    T = x2.shape[0]
    d_in = w_in.shape[1]
    n_s = seq // tm
    n_pat = len(DILATED_PATTERNS)
    row = lambda i: (i, 0)
    tab = lambda i: (i % n_s, 0)
    const = lambda i: (0, 0)
    qkv_specs, qkv_shapes = [], []
    for _ in range(3):
        for _, d in DILATED_PATTERNS:
            qkv_specs.append(pl.BlockSpec((tm // d, d * D_ATT), row))
            qkv_shapes.append(jax.ShapeDtypeStruct((T // d, d * D_ATT), BF16))
    outs = pl.pallas_call(
        functools.partial(_in_proj_kernel, tm=tm),
        grid=(T // tm,),
        in_specs=[pl.BlockSpec((tm, D_MODEL), row),
                  pl.BlockSpec((1, D_MODEL), const),
                  pl.BlockSpec((D_MODEL, d_in), const, pipeline_mode=pl.Buffered(1)),
                  pl.BlockSpec((tm, LANES), tab),
                  pl.BlockSpec((tm, LANES), tab),
                  pl.BlockSpec((tm, LANES), tab)],
        out_specs=qkv_specs + [pl.BlockSpec((tm, D_CONV), row)],
        out_shape=qkv_shapes + [jax.ShapeDtypeStruct((T, D_CONV), BF16)],
        scratch_shapes=[pltpu.VMEM((D_MODEL, d_in), BF16)]
                       + [pltpu.VMEM((d, D_ATT // LANES, tm // d, LANES), F32)
                          for _, d in DILATED_PATTERNS[:-1]],
        compiler_params=pltpu.CompilerParams(dimension_semantics=("arbitrary",),
                                             vmem_limit_bytes=VMEM_LIMIT),
        name="in_proj",
    )(x2, g, w_in, cos_t, sa_t, sb_t)
    return outs[0:n_pat], outs[n_pat:2 * n_pat], outs[2 * n_pat:3 * n_pat], outs[3 * n_pat]


def _attn_kernel(q_ref, kc_ref, kp_ref, kn_ref, vc_ref, vp_ref, vn_ref, bias_ref, hm_ref,
                 o_ref, lse_ref, kext, vext, *, lb, nblk_total):
    i = pl.program_id(2)
    kext[0:ATT_HALF] = kp_ref[0]
    kext[ATT_HALF:ATT_HALF + lb] = kc_ref[0]
    kext[ATT_HALF + lb:] = kn_ref[0]
    vext[0:ATT_HALF] = vp_ref[0]
    vext[ATT_HALF:ATT_HALF + lb] = vc_ref[0]
    vext[ATT_HALF + lb:] = vn_ref[0]

    mask_a = hm_ref[0:1, :]
    mask_b = hm_ref[1:2, :]
    nblk = lb // ATT_QB

    def body(n, carry):
        r0 = pl.multiple_of(n * ATT_QB, ATT_QB)
        gblk = i * nblk + n
        bidx = jnp.where(gblk == 0, 0, jnp.where(gblk == nblk_total - 1, 2, 1))
        bias = bias_ref[bidx]
        lse_rows = []
        for hp in range(q_ref.shape[2] // LANES):
            c0 = LANES * hp
            q2 = q_ref[0, pl.ds(r0, ATT_QB), c0:c0 + LANES]
            qs = jnp.concatenate([q2 * mask_a, q2 * mask_b], axis=0)
            parts = []
            for kh in range(ATT_WIN // ATT_QB):
                k0 = r0 + kh * ATT_QB
                kw = kext[pl.ds(k0, ATT_QB), c0:c0 + LANES]
                vw = vext[pl.ds(k0, ATT_QB), c0:c0 + LANES]
                s = _dot_nt(kw, qs) + bias[kh * ATT_QB:(kh + 1) * ATT_QB]
                m = jnp.max(s, axis=0, keepdims=True)
                p = jnp.exp2(s - m)
                l = jnp.sum(p, axis=0, keepdims=True)
                parts.append((m, l, _dot_tn(vw, p.astype(BF16))))
            (m1, l1, o1), (m2, l2, o2) = parts
            m = jnp.maximum(m1, m2)
            a1 = jnp.exp2(m1 - m)
            a2 = jnp.exp2(m2 - m)
            l = a1 * l1 + a2 * l2
            inv = 1.0 / l
            ot = o1 * (a1 * inv) + o2 * (a2 * inv)
            own = jnp.concatenate([ot[0:HEAD_DIM, 0:ATT_QB], ot[HEAD_DIM:, ATT_QB:]], axis=0)
            o_ref[0, pl.ds(r0, ATT_QB), c0:c0 + LANES] = jnp.transpose(own).astype(o_ref.dtype)
            lse = (m + jnp.log2(l)) * LN2
            lse_rows += [lse[:, 0:ATT_QB], lse[:, ATT_QB:]]
        for rc in range(len(lse_rows) // ATT_HEADS):
            rows = lse_rows[rc * ATT_HEADS:(rc + 1) * ATT_HEADS]
            pad = jnp.zeros((LANES - ATT_HEADS, ATT_QB), F32)
            lse_ref[0, pl.ds(r0, ATT_QB), rc * LANES:(rc + 1) * LANES] = jnp.transpose(
                jnp.concatenate(rows + [pad], axis=0))
        return carry

    lax.fori_loop(0, nblk, body, 0, unroll=True)


def _attn_bias():
    i = np.arange(ATT_QB)[:, None]
    j = np.arange(ATT_WIN)[None, :]
    band = (j >= i) & (j <= i + 2 * ATT_HALF)
    first = band & (j >= ATT_HALF)
    last = band & (j < ATT_HALF + ATT_QB)
    tabs = np.stack([first, band, last]).astype(np.float32)
    tabs = np.concatenate([tabs.transpose(0, 2, 1)] * 2, axis=2)
    return jnp.asarray((1.0 - tabs) * NEG_INF, dtype=F32)


def _head_masks():
    lane = np.arange(LANES)
    m = np.zeros((16, LANES), np.float32)
    m[0] = lane < HEAD_DIM
    m[1] = lane >= HEAD_DIM
    return jnp.asarray(m, dtype=BF16)


def _attn(q, k, v, bias, hmask, batch, seq, d, blocks_per_step=16, lb_max=2048):
    L = seq // d
    lb = min(lb_max, L)
    rpb = min(d, max(1, blocks_per_step * ATT_QB // lb))
    width = rpb * D_ATT
    nblk_total = L // ATT_QB
    assert nblk_total >= 2 and L % lb == 0 and lb % ATT_QB == 0 and d % rpb == 0
    view = lambda t: t.reshape(batch, L, d * D_ATT)
    hb = lb // ATT_HALF
    cur = lambda b, r, i: (b, i, r)
    prev = lambda b, r, i: (b, jnp.maximum(i * hb - 1, 0), r)
    nxt = lambda b, r, i: (b, jnp.minimum((i + 1) * hb, L // ATT_HALF - 1), r)
    blk = pl.BlockSpec((1, lb, width), cur)
    halo_p = pl.BlockSpec((1, ATT_HALF, width), prev)
    halo_n = pl.BlockSpec((1, ATT_HALF, width), nxt)
    o, lse = pl.pallas_call(
        functools.partial(_attn_kernel, lb=lb, nblk_total=nblk_total),
        grid=(batch, d // rpb, L // lb),
        in_specs=[blk, blk, halo_p, halo_n, blk, halo_p, halo_n,
                  pl.BlockSpec((3, ATT_WIN, 2 * ATT_QB), lambda b, r, i: (0, 0, 0)),
                  pl.BlockSpec((16, LANES), lambda b, r, i: (0, 0))],
        out_specs=[blk, pl.BlockSpec((1, lb, rpb * LANES), cur)],
        out_shape=[jax.ShapeDtypeStruct((batch, L, d * D_ATT), BF16),
                   jax.ShapeDtypeStruct((batch, L, d * LANES), F32)],
        scratch_shapes=[pltpu.VMEM((lb + 2 * ATT_HALF, width), BF16),
                        pltpu.VMEM((lb + 2 * ATT_HALF, width), BF16)],
        compiler_params=pltpu.CompilerParams(
            dimension_semantics=("arbitrary", "arbitrary", "arbitrary"),
            vmem_limit_bytes=VMEM_LIMIT),
        name=f"attn_d{d}",
    )(view(q), view(k), view(k), view(k), view(v), view(v), view(v), bias, hmask)
    return o.reshape(batch * L, d * D_ATT), lse.reshape(batch * L, d * LANES)


def _conv_kernel(uc_ref, up_ref, un_ref, w_ref, b_ref, lg_ref, lb_ref, *refs, ts, rows, ln_rows,
                 n_cast):
    cast_in, c_ref = refs[:n_cast], refs[n_cast]
    cast_out, (ext, ybuf) = refs[n_cast + 1:2 * n_cast + 1], refs[2 * n_cast + 1:]
    for src, dst in zip(cast_in, cast_out):
        dst[...] = src[...].astype(BF16)
    i = pl.program_id(1)
    n = pl.num_programs(1)
    n_slab = D_CONV // LANES
    for g in range(n_slab):
        ls = slice(g * LANES, (g + 1) * LANES)
        ext[g, 0:CONV_HALO] = jnp.where(i > 0, up_ref[0, :, ls].astype(F32), 0.0)
        ext[g, CONV_HALO:CONV_HALO + ts] = uc_ref[0, :, ls].astype(F32)
        ext[g, CONV_HALO + ts:] = jnp.where(i < n - 1, un_ref[0, :, ls].astype(F32), 0.0)
    lg = lg_ref[...]
    lb = lb_ref[...]
    off = CONV_HALO - CONV_PAD
    n_blk = ts // rows

    def taps(idx, carry):
        g = idx // n_blk
        base = pl.multiple_of((idx % n_blk) * rows, rows)
        acc = jnp.broadcast_to(b_ref[g], (rows, LANES))
        for j in range(SUBLANES):
            steps = [m for m in range((CONV_WIDTH + off) // SUBLANES + 1)
                     if 0 <= SUBLANES * m + j - off < CONV_WIDTH]
            win = ext[g, pl.ds(base + j, rows + SUBLANES * max(steps)), :]
            for m in steps:
                k = SUBLANES * m + j - off
                acc = acc + win[SUBLANES * m:SUBLANES * m + rows] * w_ref[g, k:k + 1, :]
        ybuf[g, pl.ds(base, rows), :] = acc
        return carry

    lax.fori_loop(0, n_slab * n_blk, taps, 0, unroll=4)
    for r0 in range(0, ts, ln_rows):
        acc = jnp.concatenate([ybuf[g, r0:r0 + ln_rows, :] for g in range(n_slab)], axis=1)
        mu = jnp.mean(acc, axis=-1, keepdims=True)
        cen = acc - mu
        var = jnp.mean(cen * cen, axis=-1, keepdims=True)
        y = cen * lax.rsqrt(var + EPS) * lg + lb
        c_ref[0, r0:r0 + ln_rows, :] = (y * (1.0 / (1.0 + jnp.exp(-y)))).astype(c_ref.dtype)


def _conv(u3, conv_w, conv_b, ln_g, ln_b, weights, ts=512, rows=64, ln_rows=32):
    B, S, C = u3.shape
    n_i = S // ts
    sl = lambda b, i: (b * n_i + i, 0)
    w_rows = [w.shape[0] // (B * n_i) for w in weights]
    w_specs = [pl.BlockSpec((r, w.shape[1]), sl) for r, w in zip(w_rows, weights)]
    hb = ts // CONV_HALO
    cur = lambda b, i: (b, i, 0)
    prev = lambda b, i: (b, jnp.maximum(i * hb - 1, 0), 0)
    nxt = lambda b, i: (b, jnp.minimum((i + 1) * hb, S // CONV_HALO - 1), 0)
    const = lambda b, i: (0, 0)
    const3 = lambda b, i: (0, 0, 0)
    n_slab = C // LANES
    w_slab = jnp.transpose(conv_w.reshape(CONV_WIDTH, n_slab, LANES), (1, 0, 2))
    b_slab = conv_b.reshape(n_slab, 1, LANES)
    outs = pl.pallas_call(
        functools.partial(_conv_kernel, ts=ts, rows=rows, ln_rows=ln_rows, n_cast=len(weights)),
        grid=(B, S // ts),
        in_specs=[pl.BlockSpec((1, ts, C), cur),
                  pl.BlockSpec((1, CONV_HALO, C), prev),
                  pl.BlockSpec((1, CONV_HALO, C), nxt),
                  pl.BlockSpec((n_slab, CONV_WIDTH, LANES), const3),
                  pl.BlockSpec((n_slab, 1, LANES), const3),
                  pl.BlockSpec((1, C), const),
                  pl.BlockSpec((1, C), const)] + w_specs,
        out_specs=[pl.BlockSpec((1, ts, C), cur)] + w_specs,
        out_shape=[jax.ShapeDtypeStruct((B, S, C), BF16)]
                  + [jax.ShapeDtypeStruct(w.shape, BF16) for w in weights],
        scratch_shapes=[pltpu.VMEM((C // LANES, ts + 2 * CONV_HALO, LANES), F32),
                        pltpu.VMEM((C // LANES, ts, LANES), F32)],
        compiler_params=pltpu.CompilerParams(dimension_semantics=("arbitrary", "arbitrary"),
                                             vmem_limit_bytes=VMEM_LIMIT),
        name="conv",
    )(u3, u3, u3, w_slab, b_slab, ln_g, ln_b, *weights)
    return outs[0], outs[1:]


def _mem_kv_kernel(mem_ref, g_ref, wk_ref, wv_ref, k_ref, v_ref, mn_ref):
    @pl.when(pl.program_id(0) == 0)
    def _():
        mn_ref[...] = _rms(mem_ref[...], g_ref[...]).astype(BF16)

    mn = mn_ref[...]
    k_ref[...] = _dot(mn, wk_ref[...].astype(BF16)).astype(BF16)
    v_ref[...] = _dot(mn, wv_ref[...].astype(BF16)).astype(BF16)


def _mem_kv(mem2, g, wk, wv, cols=256):
    R = mem2.shape[0]
    full = lambda shape: pl.BlockSpec(shape, lambda i: (0, 0))
    chunk = lambda rows: pl.BlockSpec((rows, cols), lambda i: (0, i))
    return pl.pallas_call(
        _mem_kv_kernel,
        grid=(D_MODEL // cols,),
        in_specs=[full((R, D_MODEL)), full((1, D_MODEL)), chunk(D_MODEL), chunk(D_MODEL)],
        out_specs=[chunk(R)] * 2,
        out_shape=[jax.ShapeDtypeStruct((R, D_MODEL), BF16)] * 2,
        scratch_shapes=[pltpu.VMEM((R, D_MODEL), BF16)],
        compiler_params=pltpu.CompilerParams(dimension_semantics=("arbitrary",),
                                             vmem_limit_bytes=VMEM_LIMIT),
        name="mem_kv",
    )(mem2, g, wk, wv)


def _post_kernel(x_ref, o1_ref, o2_ref, o3_ref, l1_ref, l2_ref, l3_ref, c_ref, e_ref,
                 wo_ref, gx_ref, wq_ref, xk_ref, xv_ref, wxo_ref, gm_ref, wu_ref, wd_ref, gf_ref,
                 out_ref, *bufs, tm, sub, chunk, final_norm):
    n_slab = D_ATT // LANES
    n_pat = len(DILATED_PATTERNS)
    o_refs = (o1_ref, o2_ref, o3_ref)
    l_refs = (l1_ref, l2_ref, l3_ref)
    n_grp = tm // sub

    def mix(g):
        t0 = g * sub
        obuf, lbuf = bufs[2 * g], bufs[2 * g + 1]
        otmp, ltmp = bufs[2 * n_grp + 2 * g], bufs[2 * n_grp + 2 * g + 1]
        for p, (_, d) in enumerate(DILATED_PATTERNS):
            if d == 1:
                continue
            src = slice(t0 // d, (t0 + sub) // d)
            jobs = [(o_refs[p], D_ATT, j, obuf, otmp, True) for j in range(n_slab)]
            jobs.append((l_refs[p], LANES, 0, lbuf, ltmp, False))
            for ref, pitch, j, dst, tmp, widen in jobs:
                pieces = {}
                for r in range(d):
                    t = ref[src, r * pitch + j * LANES:r * pitch + (j + 1) * LANES]
                    pieces[r] = t.astype(F32) if widen else t
                dd = d
                while dd > DIL_STEP:
                    lower = dd // DIL_STEP
                    merged = {}
                    for r_low in range(lower):
                        for t in range(DIL_STEP):
                            tmp[r_low, j, pl.ds(t, sub // dd, stride=DIL_STEP), :] = (
                                pieces[t * lower + r_low])
                        merged[r_low] = tmp[r_low, j, 0:sub // lower, :]
                    pieces, dd = merged, lower
                for r in range(dd):
                    dst[p, j, pl.ds(r, sub // dd, stride=DIL_STEP), :] = pieces[r]

        ls = [l_refs[p][t0:t0 + sub, :] if d == 1 else lbuf[p, 0]
              for p, (_, d) in enumerate(DILATED_PATTERNS)]
        m = jnp.maximum(jnp.maximum(ls[0], ls[1]), ls[2])
        es = [jnp.exp(l - m) for l in ls]
        inv = 1.0 / (es[0] + es[1] + es[2])
        ws = [_dot((e * inv).astype(BF16), e_ref[...]) for e in es]
        att = []
        for j in range(n_slab):
            lanes = slice(j * LANES, (j + 1) * LANES)
            os_ = [(o_refs[p][t0:t0 + sub, lanes] if d == 1 else obuf[p, j]).astype(F32)
                   for p, (_, d) in enumerate(DILATED_PATTERNS)]
            att.append((ws[0][:, lanes] * os_[0] + ws[1][:, lanes] * os_[1]
                        + ws[2][:, lanes] * os_[2]).astype(BF16))
        return jnp.concatenate(att, axis=1)

    def project(g, att):
        tr = slice(g * sub, (g + 1) * sub)
        h1 = (x_ref[tr, :] + _dot(att, wo_ref[0:D_ATT, :])
              + _dot(c_ref[tr, :], wo_ref[D_ATT:, :]))
        xq = (_dot(_rms(h1, gx_ref[...]).astype(BF16), wq_ref[...])
              * (XATT_HEAD_DIM ** -0.5)).astype(BF16)
        return h1, xq

    def cross(g, h1, xq):
        heads = []
        for h in range(XATT_HEADS):
            sl = slice(h * XATT_HEAD_DIM, (h + 1) * XATT_HEAD_DIM)
            s = _dot_nt(xq[:, sl], xk_ref[0, :, sl])
            mx = jnp.max(s, axis=-1, keepdims=True)
            p = jnp.exp(s - mx)
            den = jnp.sum(p, axis=-1, keepdims=True)
            heads.append((_dot(p.astype(BF16), xv_ref[0, :, sl]) * (1.0 / den)).astype(BF16))
        xo = jnp.concatenate(heads, axis=1)
        return h1 + _dot(xo, wxo_ref[...])

    def mlp(g, h):
        hn = _rms(h, gm_ref[...]).astype(BF16)
        acc = h
        for j in range(D_FF // chunk):
            u = jnp.maximum(_dot(hn, wu_ref[:, j * chunk:(j + 1) * chunk]), 0.0)
            acc = acc + _dot((u * u).astype(BF16), wd_ref[j * chunk:(j + 1) * chunk, :])
        out_ref[g * sub:(g + 1) * sub, :] = _rms(acc, gf_ref[...]) if final_norm else acc

    for g in range(n_grp):
        h1, xq = project(g, mix(g))
        mlp(g, cross(g, h1, xq))


def _post(x2, os_, lses, c2, w_out, gx, w_xq, xk, xv, w_xo, gm, w_up, w_down, gf, final_norm,
          seq, tm=512, sub=512, chunk=1024):
    T = x2.shape[0]
    n_mem = xk.shape[1]
    per_b = seq // tm
    n_slab = D_ATT // LANES
    d_max = DILATED_PATTERNS[-1][1]
    row = lambda i: (i, 0)
    const = lambda i: (0, 0)
    memb = lambda i: (i // per_b, 0, 0)
    whole = lambda w: pl.BlockSpec(w.shape, const, pipeline_mode=pl.Buffered(1))
    vec = pl.BlockSpec((1, D_MODEL), const)
    n_pat = len(DILATED_PATTERNS)
    return pl.pallas_call(
        functools.partial(_post_kernel, tm=tm, sub=sub, chunk=chunk, final_norm=final_norm),
        grid=(T // tm,),
        in_specs=[pl.BlockSpec((tm, D_MODEL), row)]
                 + [pl.BlockSpec((tm // d, d * D_ATT), row) for _, d in DILATED_PATTERNS]
                 + [pl.BlockSpec((tm // d, d * LANES), row) for _, d in DILATED_PATTERNS]
                 + [pl.BlockSpec((tm, D_CONV), row),
                    pl.BlockSpec((LANES, D_ATT), const),
                    whole(w_out), vec, whole(w_xq),
                    pl.BlockSpec((1, n_mem, D_MODEL), memb),
                    pl.BlockSpec((1, n_mem, D_MODEL), memb),
                    whole(w_xo), vec, whole(w_up), whole(w_down), vec],
        out_specs=pl.BlockSpec((tm, D_MODEL), row),
        out_shape=jax.ShapeDtypeStruct((T, D_MODEL), F32),
        scratch_shapes=[pltpu.VMEM((n_pat, n_slab, sub, LANES), F32),
                          pltpu.VMEM((n_pat, 1, sub, LANES), F32)] * (tm // sub)
                       + [pltpu.VMEM((d_max // DIL_STEP, n_slab, sub // DIL_STEP, LANES), F32),
                          pltpu.VMEM((d_max // DIL_STEP, 1, sub // DIL_STEP, LANES), F32)
                          ] * (tm // sub),
        compiler_params=pltpu.CompilerParams(dimension_semantics=("arbitrary",),
                                             vmem_limit_bytes=VMEM_LIMIT),
        name="post",
    )(x2, *os_, *lses, c2, _head_expand(), w_out, gx, w_xq, xk, xv, w_xo, gm, w_up, w_down, gf)


def _head_expand():
    e = np.zeros((LANES, D_ATT), np.float32)
    for h in range(ATT_HEADS):
        e[h, h * HEAD_DIM:(h + 1) * HEAD_DIM] = 1.0
    return jnp.asarray(e, dtype=BF16)


def _rotary_tables(seq):
    half = ROT_DIM // 2
    freqs = ROPE_THETA ** (-np.arange(0, ROT_DIM, 2, dtype=np.float64) / ROT_DIM)
    ang = np.arange(seq, dtype=np.float64)[:, None] * freqs[None, :]
    cos, sin = np.cos(ang), np.sin(ang)
    zeros = np.zeros((seq, HEAD_DIM - ROT_DIM))
    z8 = np.zeros((seq, half))
    rep = LANES // HEAD_DIM
    tabs = ([cos, cos, zeros + 1.0], [-sin, z8, zeros], [z8, sin, zeros])
    return tuple(jnp.asarray(np.concatenate(t * rep, axis=1), dtype=F32) for t in tabs)


def kernel(x, mem, norm_mix_g, w_in, conv_w, conv_b, conv_ln_g, conv_ln_b, w_out, norm_x_g,
           norm_mem_g, w_xq, w_xk, w_xv, w_xo, norm_mlp_g, w_up, w_down, norm_final_g):
    B, S, D = x.shape
    n_mem = mem.shape[1]
    depth = w_in.shape[0]
    T = B * S
    cos_t, sa_t, sb_t = _rotary_tables(S)
    bias = _attn_bias()
    hmask = _head_masks()
    row = lambda g: g.reshape(1, -1)

    h = x.reshape(T, D)
    for l in range(depth):
        q, k, v, u = _in_proj(h, row(norm_mix_g[l]), w_in[l], cos_t, sa_t, sb_t, S)
        os_, lses = [], []
        for p, (_, d) in enumerate(DILATED_PATTERNS):
            o, lse = _attn(q[p], k[p], v[p], bias, hmask, B, S, d)
            os_.append(o)
            lses.append(lse)
        c, (w_out_b, w_xq_b, w_xo_b, w_up_b, w_down_b) = _conv(
            u.reshape(B, S, D_CONV), conv_w[l], row(conv_b[l]), row(conv_ln_g[l]),
            row(conv_ln_b[l]), (w_out[l], w_xq[l], w_xo[l], w_up[l], w_down[l]))
        xk, xv = _mem_kv(mem.reshape(B * n_mem, D), row(norm_mem_g[l]), w_xk[l], w_xv[l])
        h = _post(h, os_, lses, c.reshape(T, D_CONV), w_out_b, row(norm_x_g[l]), w_xq_b,
                  xk.reshape(B, n_mem, D), xv.reshape(B, n_mem, D), w_xo_b, row(norm_mlp_g[l]),
                  w_up_b, w_down_b, row(norm_final_g), final_norm=(l == depth - 1), seq=S)
    return h.reshape(B, S, D)
```

```python
import functools
import math

import numpy as np
import jax
import jax.numpy as jnp
from jax import lax
from jax.experimental import pallas as pl
from jax.experimental.pallas import tpu as pltpu

F32 = jnp.float32
BF16 = jnp.bfloat16

D_MODEL = 1024
ATT_HEADS = 8
HEAD_DIM = 64
D_ATT = ATT_HEADS * HEAD_DIM
D_CONV = D_MODEL - D_ATT
DILATED_PATTERNS = ((128, 1), (512, 4), (2048, 16))
DIL_STEP = 4
assert all(d == DIL_STEP ** k for k, (_, d) in enumerate(DILATED_PATTERNS))
ROPE_THETA = 500000.0
ROT_DIM = HEAD_DIM // 4
CONV_WIDTH = 31
CONV_PAD = (CONV_WIDTH - 1) // 2
XATT_HEADS = 4
XATT_HEAD_DIM = D_MODEL // XATT_HEADS
D_FF = 4 * D_MODEL
EPS = 1e-6
NEG_INF = -1e30
LN2 = math.log(2.0)
Q_SCALE = HEAD_DIM ** -0.5 / LN2

LANES = 128
SUBLANES = 8
ATT_HALF = 64
ATT_QB = 2 * ATT_HALF
ATT_WIN = ATT_QB + 2 * ATT_HALF
CONV_HALO = 16
GLU_COLS = 256
VMEM_LIMIT = 56 * 1024 * 1024


def _dot(a, b):
    return jnp.dot(a, b, preferred_element_type=F32)


def _dot_nt(a, b):
    return lax.dot_general(a, b, (((1,), (1,)), ((), ())), preferred_element_type=F32)


def _dot_tn(a, b):
    return lax.dot_general(a, b, (((0,), (0,)), ((), ())), preferred_element_type=F32)


def _rms(x, g):
    var = jnp.mean(x * x, axis=-1, keepdims=True)
    return x * lax.rsqrt(var + EPS) * g


def _cast_once(pairs, chunk=512):
    @pl.when(pl.program_id(0) == 0)
    def _():
        for src, dst in pairs:
            for c in range(0, src.shape[1], chunk):
                dst[:, c:c + chunk] = src[:, c:c + chunk].astype(BF16)


def _in_proj_kernel(x_ref, g_ref, wf_ref, cos_ref, sa_ref, sb_ref, *refs, tm):
    n_pat = len(DILATED_PATTERNS)
    qkv_refs = [refs[a * n_pat:(a + 1) * n_pat] for a in range(3)]
    u_ref = refs[3 * n_pat]
    w_ref = refs[3 * n_pat + 1]
    stage = refs[3 * n_pat + 2:]
    _cast_once([(wf_ref, w_ref)])
    xn = _rms(x_ref[...], g_ref[...]).astype(BF16)
    cos, sa, sb = cos_ref[...], sa_ref[...], sb_ref[...]
    n_slab = D_ATT // LANES

    def plain(y, j):
        return y[:, LANES * j:LANES * (j + 1)]

    def rot(y, j):
        yj = plain(y, j)
        return (yj * cos + pltpu.roll(yj, LANES - ROT_DIM // 2, 1) * sa
                + pltpu.roll(yj, ROT_DIM // 2, 1) * sb)

    def emit(y, outs, transform):
        for j in range(n_slab):
            piece = transform(y, j)
            stage[0][0, j] = piece
            outs[0][:, j * LANES:(j + 1) * LANES] = piece.astype(BF16)
        for k in range(1, n_pat):
            d_prev, d = DILATED_PATTERNS[k - 1][1], DILATED_PATTERNS[k][1]
            for r_prev in range(d_prev):
                for t in range(DIL_STEP):
                    r = t * d_prev + r_prev
                    for j in range(n_slab):
                        piece = stage[k - 1][r_prev, j, pl.ds(t, tm // d, stride=DIL_STEP), :]
                        if k + 1 < n_pat:
                            stage[k][r, j] = piece
                        c0 = r * D_ATT + j * LANES
                        outs[k][:, c0:c0 + LANES] = piece.astype(BF16)

    emit(_dot(xn, w_ref[:, 0:D_ATT]), qkv_refs[0], lambda y, j: rot(y, j) * Q_SCALE)
    emit(_dot(xn, w_ref[:, D_ATT:2 * D_ATT]), qkv_refs[1], rot)
    emit(_dot(xn, w_ref[:, 2 * D_ATT:3 * D_ATT]), qkv_refs[2], plain)
    c0 = 3 * D_ATT
    for cc in range(0, D_CONV, GLU_COLS):
        a = _dot(xn, w_ref[:, c0 + cc:c0 + cc + GLU_COLS])
        gt = _dot(xn, w_ref[:, c0 + D_CONV + cc:c0 + D_CONV + cc + GLU_COLS])
        u_ref[:, cc:cc + GLU_COLS] = (a * (1.0 / (1.0 + jnp.exp(-gt)))).astype(u_ref.dtype)


def _in_proj(x2, g, w_in, cos_t, sa_t, sb_t, seq, tm=1024):
    T = x2.shape[0]
    d_in = w_in.shape[1]
    n_s = seq // tm
    n_pat = len(DILATED_PATTERNS)
    row = lambda i: (i, 0)
    tab = lambda i: (i % n_s, 0)
    const = lambda i: (0, 0)
    qkv_specs, qkv_shapes = [], []
    for _ in range(3):
        for _, d in DILATED_PATTERNS:
            qkv_specs.append(pl.BlockSpec((tm // d, d * D_ATT), row))
            qkv_shapes.append(jax.ShapeDtypeStruct((T // d, d * D_ATT), BF16))
    outs = pl.pallas_call(
        functools.partial(_in_proj_kernel, tm=tm),
        grid=(T // tm,),
        in_specs=[pl.BlockSpec((tm, D_MODEL), row),
                  pl.BlockSpec((1, D_MODEL), const),
                  pl.BlockSpec((D_MODEL, d_in), const, pipeline_mode=pl.Buffered(1)),
                  pl.BlockSpec((tm, LANES), tab),
                  pl.BlockSpec((tm, LANES), tab),
                  pl.BlockSpec((tm, LANES), tab)],
        out_specs=qkv_specs + [pl.BlockSpec((tm, D_CONV), row)],
        out_shape=qkv_shapes + [jax.ShapeDtypeStruct((T, D_CONV), BF16)],
        scratch_shapes=[pltpu.VMEM((D_MODEL, d_in), BF16)]
                       + [pltpu.VMEM((d, D_ATT // LANES, tm // d, LANES), F32)
                          for _, d in DILATED_PATTERNS[:-1]],
        compiler_params=pltpu.CompilerParams(dimension_semantics=("arbitrary",),
                                             vmem_limit_bytes=VMEM_LIMIT),
        name="in_proj",
    )(x2, g, w_in, cos_t, sa_t, sb_t)
    return outs[0:n_pat], outs[n_pat:2 * n_pat], outs[2 * n_pat:3 * n_pat], outs[3 * n_pat]


def _attn_kernel(q_ref, kc_ref, kp_ref, kn_ref, vc_ref, vp_ref, vn_ref, bias_ref, hm_ref,
                 o_ref, lse_ref, kext, vext, *, lb, nblk_total):
    i = pl.program_id(2)
    kext[0:ATT_HALF] = kp_ref[0]
    kext[ATT_HALF:ATT_HALF + lb] = kc_ref[0]
    kext[ATT_HALF + lb:] = kn_ref[0]
    vext[0:ATT_HALF] = vp_ref[0]
    vext[ATT_HALF:ATT_HALF + lb] = vc_ref[0]
    vext[ATT_HALF + lb:] = vn_ref[0]

    mask_a = hm_ref[0:1, :]
    mask_b = hm_ref[1:2, :]
    nblk = lb // ATT_QB

    def body(n, carry):
        r0 = pl.multiple_of(n * ATT_QB, ATT_QB)
        gblk = i * nblk + n
        bidx = jnp.where(gblk == 0, 0, jnp.where(gblk == nblk_total - 1, 2, 1))
        bias = bias_ref[bidx]
        lse_rows = []
        for hp in range(q_ref.shape[2] // LANES):
            c0 = LANES * hp
            q2 = q_ref[0, pl.ds(r0, ATT_QB), c0:c0 + LANES]
            qs = jnp.concatenate([q2 * mask_a, q2 * mask_b], axis=0)
            parts = []
            for kh in range(ATT_WIN // ATT_QB):
                k0 = r0 + kh * ATT_QB
                kw = kext[pl.ds(k0, ATT_QB), c0:c0 + LANES]
                vw = vext[pl.ds(k0, ATT_QB), c0:c0 + LANES]
                s = _dot_nt(kw, qs) + bias[kh * ATT_QB:(kh + 1) * ATT_QB]
                m = jnp.max(s, axis=0, keepdims=True)
                p = jnp.exp2(s - m)
                l = jnp.sum(p, axis=0, keepdims=True)
                parts.append((m, l, _dot_tn(vw, p.astype(BF16))))
            (m1, l1, o1), (m2, l2, o2) = parts
            m = jnp.maximum(m1, m2)
            a1 = jnp.exp2(m1 - m)
            a2 = jnp.exp2(m2 - m)
            l = a1 * l1 + a2 * l2
            inv = 1.0 / l
            ot = o1 * (a1 * inv) + o2 * (a2 * inv)
            own = jnp.concatenate([ot[0:HEAD_DIM, 0:ATT_QB], ot[HEAD_DIM:, ATT_QB:]], axis=0)
            o_ref[0, pl.ds(r0, ATT_QB), c0:c0 + LANES] = jnp.transpose(own).astype(o_ref.dtype)
            lse = (m + jnp.log2(l)) * LN2
            lse_rows += [lse[:, 0:ATT_QB], lse[:, ATT_QB:]]
        for rc in range(len(lse_rows) // ATT_HEADS):
            rows = lse_rows[rc * ATT_HEADS:(rc + 1) * ATT_HEADS]
            pad = jnp.zeros((LANES - ATT_HEADS, ATT_QB), F32)
            lse_ref[0, pl.ds(r0, ATT_QB), rc * LANES:(rc + 1) * LANES] = jnp.transpose(
                jnp.concatenate(rows + [pad], axis=0))
        return carry

    lax.fori_loop(0, nblk, body, 0, unroll=True)


def _attn_bias():
    i = np.arange(ATT_QB)[:, None]
    j = np.arange(ATT_WIN)[None, :]
    band = (j >= i) & (j <= i + 2 * ATT_HALF)
    first = band & (j >= ATT_HALF)
    last = band & (j < ATT_HALF + ATT_QB)
    tabs = np.stack([first, band, last]).astype(np.float32)
    tabs = np.concatenate([tabs.transpose(0, 2, 1)] * 2, axis=2)
    return jnp.asarray((1.0 - tabs) * NEG_INF, dtype=F32)


def _head_masks():
    lane = np.arange(LANES)
    m = np.zeros((16, LANES), np.float32)
    m[0] = lane < HEAD_DIM
    m[1] = lane >= HEAD_DIM
    return jnp.asarray(m, dtype=BF16)


def _attn(q, k, v, bias, hmask, batch, seq, d, blocks_per_step=16, lb_max=2048):
    L = seq // d
    lb = min(lb_max, L)
    rpb = min(d, max(1, blocks_per_step * ATT_QB // lb))
    width = rpb * D_ATT
    nblk_total = L // ATT_QB
    assert nblk_total >= 2 and L % lb == 0 and lb % ATT_QB == 0 and d % rpb == 0
    view = lambda t: t.reshape(batch, L, d * D_ATT)
    hb = lb // ATT_HALF
    cur = lambda b, r, i: (b, i, r)
    prev = lambda b, r, i: (b, jnp.maximum(i * hb - 1, 0), r)
    nxt = lambda b, r, i: (b, jnp.minimum((i + 1) * hb, L // ATT_HALF - 1), r)
    blk = pl.BlockSpec((1, lb, width), cur)
    halo_p = pl.BlockSpec((1, ATT_HALF, width), prev)
    halo_n = pl.BlockSpec((1, ATT_HALF, width), nxt)
    o, lse = pl.pallas_call(
        functools.partial(_attn_kernel, lb=lb, nblk_total=nblk_total),
        grid=(batch, d // rpb, L // lb),
        in_specs=[blk, blk, halo_p, halo_n, blk, halo_p, halo_n,
                  pl.BlockSpec((3, ATT_WIN, 2 * ATT_QB), lambda b, r, i: (0, 0, 0)),
                  pl.BlockSpec((16, LANES), lambda b, r, i: (0, 0))],
        out_specs=[blk, pl.BlockSpec((1, lb, rpb * LANES), cur)],
        out_shape=[jax.ShapeDtypeStruct((batch, L, d * D_ATT), BF16),
                   jax.ShapeDtypeStruct((batch, L, d * LANES), F32)],
        scratch_shapes=[pltpu.VMEM((lb + 2 * ATT_HALF, width), BF16),
                        pltpu.VMEM((lb + 2 * ATT_HALF, width), BF16)],
        compiler_params=pltpu.CompilerParams(
            dimension_semantics=("arbitrary", "arbitrary", "arbitrary"),
            vmem_limit_bytes=VMEM_LIMIT),
        name=f"attn_d{d}",
    )(view(q), view(k), view(k), view(k), view(v), view(v), view(v), bias, hmask)
    return o.reshape(batch * L, d * D_ATT), lse.reshape(batch * L, d * LANES)


def _conv_kernel(uc_ref, up_ref, un_ref, w_ref, b_ref, lg_ref, lb_ref, *refs, ts, rows, ln_rows,
                 n_cast):
    cast_in, c_ref = refs[:n_cast], refs[n_cast]
    cast_out, (ext, ybuf) = refs[n_cast + 1:2 * n_cast + 1], refs[2 * n_cast + 1:]
    for src, dst in zip(cast_in, cast_out):
        dst[...] = src[...].astype(BF16)
    i = pl.program_id(1)
    n = pl.num_programs(1)
    n_slab = D_CONV // LANES
    for g in range(n_slab):
        ls = slice(g * LANES, (g + 1) * LANES)
        ext[g, 0:CONV_HALO] = jnp.where(i > 0, up_ref[0, :, ls].astype(F32), 0.0)
        ext[g, CONV_HALO:CONV_HALO + ts] = uc_ref[0, :, ls].astype(F32)
        ext[g, CONV_HALO + ts:] = jnp.where(i < n - 1, un_ref[0, :, ls].astype(F32), 0.0)
    lg = lg_ref[...]
    lb = lb_ref[...]
    off = CONV_HALO - CONV_PAD
    n_blk = ts // rows

    def taps(idx, carry):
        g = idx // n_blk
        base = pl.multiple_of((idx % n_blk) * rows, rows)
        acc = jnp.broadcast_to(b_ref[g], (rows, LANES))
        for j in range(SUBLANES):
            steps = [m for m in range((CONV_WIDTH + off) // SUBLANES + 1)
                     if 0 <= SUBLANES * m + j - off < CONV_WIDTH]
            win = ext[g, pl.ds(base + j, rows + SUBLANES * max(steps)), :]
            for m in steps:
                k = SUBLANES * m + j - off
                acc = acc + win[SUBLANES * m:SUBLANES * m + rows] * w_ref[g, k:k + 1, :]
        ybuf[g, pl.ds(base, rows), :] = acc
        return carry

    lax.fori_loop(0, n_slab * n_blk, taps, 0, unroll=4)
    for r0 in range(0, ts, ln_rows):
        acc = jnp.concatenate([ybuf[g, r0:r0 + ln_rows, :] for g in range(n_slab)], axis=1)
        mu = jnp.mean(acc, axis=-1, keepdims=True)
        cen = acc - mu
        var = jnp.mean(cen * cen, axis=-1, keepdims=True)
        y = cen * lax.rsqrt(var + EPS) * lg + lb
        c_ref[0, r0:r0 + ln_rows, :] = (y * (1.0 / (1.0 + jnp.exp(-y)))).astype(c_ref.dtype)


def _conv(u3, conv_w, conv_b, ln_g, ln_b, weights, ts=512, rows=64, ln_rows=32):
    B, S, C = u3.shape
    n_i = S // ts
    sl = lambda b, i: (b * n_i + i, 0)
    w_rows = [w.shape[0] // (B * n_i) for w in weights]
    w_specs = [pl.BlockSpec((r, w.shape[1]), sl) for r, w in zip(w_rows, weights)]
    hb = ts // CONV_HALO
    cur = lambda b, i: (b, i, 0)
    prev = lambda b, i: (b, jnp.maximum(i * hb - 1, 0), 0)
    nxt = lambda b, i: (b, jnp.minimum((i + 1) * hb, S // CONV_HALO - 1), 0)
    const = lambda b, i: (0, 0)
    const3 = lambda b, i: (0, 0, 0)
    n_slab = C // LANES
    w_slab = jnp.transpose(conv_w.reshape(CONV_WIDTH, n_slab, LANES), (1, 0, 2))
    b_slab = conv_b.reshape(n_slab, 1, LANES)
    outs = pl.pallas_call(
        functools.partial(_conv_kernel, ts=ts, rows=rows, ln_rows=ln_rows, n_cast=len(weights)),
        grid=(B, S // ts),
        in_specs=[pl.BlockSpec((1, ts, C), cur),
                  pl.BlockSpec((1, CONV_HALO, C), prev),
                  pl.BlockSpec((1, CONV_HALO, C), nxt),
                  pl.BlockSpec((n_slab, CONV_WIDTH, LANES), const3),
                  pl.BlockSpec((n_slab, 1, LANES), const3),
                  pl.BlockSpec((1, C), const),
                  pl.BlockSpec((1, C), const)] + w_specs,
        out_specs=[pl.BlockSpec((1, ts, C), cur)] + w_specs,
        out_shape=[jax.ShapeDtypeStruct((B, S, C), BF16)]
                  + [jax.ShapeDtypeStruct(w.shape, BF16) for w in weights],
        scratch_shapes=[pltpu.VMEM((C // LANES, ts + 2 * CONV_HALO, LANES), F32),
                        pltpu.VMEM((C // LANES, ts, LANES), F32)],
        compiler_params=pltpu.CompilerParams(dimension_semantics=("arbitrary", "arbitrary"),
                                             vmem_limit_bytes=VMEM_LIMIT),
        name="conv",
    )(u3, u3, u3, w_slab, b_slab, ln_g, ln_b, *weights)
    return outs[0], outs[1:]


def _mem_kv_kernel(mem_ref, g_ref, wk_ref, wv_ref, k_ref, v_ref):
    mn = _rms(mem_ref[...], g_ref[...]).astype(BF16)
    k_ref[...] = _dot(mn, wk_ref[...].astype(BF16)).astype(BF16)
    v_ref[...] = _dot(mn, wv_ref[...].astype(BF16)).astype(BF16)


def _mem_kv(mem2, g, wk, wv):
    R = mem2.shape[0]
    full = lambda shape: pl.BlockSpec(shape, lambda i: (0, 0))
    return pl.pallas_call(
        _mem_kv_kernel,
        grid=(1,),
        in_specs=[full((R, D_MODEL)), full((1, D_MODEL)),
                  full((D_MODEL, D_MODEL)), full((D_MODEL, D_MODEL))],
        out_specs=[full((R, D_MODEL))] * 2,
        out_shape=[jax.ShapeDtypeStruct((R, D_MODEL), BF16)] * 2,
        compiler_params=pltpu.CompilerParams(dimension_semantics=("arbitrary",),
                                             vmem_limit_bytes=VMEM_LIMIT),
        name="mem_kv",
    )(mem2, g, wk, wv)


def _post_kernel(x_ref, o1_ref, o2_ref, o3_ref, l1_ref, l2_ref, l3_ref, c_ref, e_ref,
                 wo_ref, gx_ref, wq_ref, xk_ref, xv_ref, wxo_ref, gm_ref, wu_ref, wd_ref, gf_ref,
                 out_ref, *bufs, tm, sub, chunk, final_norm):
    n_slab = D_ATT // LANES
    n_pat = len(DILATED_PATTERNS)
    o_refs = (o1_ref, o2_ref, o3_ref)
    l_refs = (l1_ref, l2_ref, l3_ref)
    n_grp = tm // sub

    def mix(g):
        t0 = g * sub
        obuf, lbuf = bufs[2 * g], bufs[2 * g + 1]
        otmp, ltmp = bufs[2 * n_grp + 2 * g], bufs[2 * n_grp + 2 * g + 1]
        for p, (_, d) in enumerate(DILATED_PATTERNS):
            if d == 1:
                continue
            src = slice(t0 // d, (t0 + sub) // d)
            jobs = [(o_refs[p], D_ATT, j, obuf, otmp, True) for j in range(n_slab)]
            jobs.append((l_refs[p], LANES, 0, lbuf, ltmp, False))
            for ref, pitch, j, dst, tmp, widen in jobs:
                pieces = {}
                for r in range(d):
                    t = ref[src, r * pitch + j * LANES:r * pitch + (j + 1) * LANES]
                    pieces[r] = t.astype(F32) if widen else t
                dd = d
                while dd > DIL_STEP:
                    lower = dd // DIL_STEP
                    merged = {}
                    for r_low in range(lower):
                        for t in range(DIL_STEP):
                            tmp[r_low, j, pl.ds(t, sub // dd, stride=DIL_STEP), :] = (
                                pieces[t * lower + r_low])
                        merged[r_low] = tmp[r_low, j, 0:sub // lower, :]
                    pieces, dd = merged, lower
                for r in range(dd):
                    dst[p, j, pl.ds(r, sub // dd, stride=DIL_STEP), :] = pieces[r]

        ls = [l_refs[p][t0:t0 + sub, :] if d == 1 else lbuf[p, 0]
              for p, (_, d) in enumerate(DILATED_PATTERNS)]
        m = jnp.maximum(jnp.maximum(ls[0], ls[1]), ls[2])
        es = [jnp.exp(l - m) for l in ls]
        inv = 1.0 / (es[0] + es[1] + es[2])
        ws = [_dot((e * inv).astype(BF16), e_ref[...]) for e in es]
        att = []
        for j in range(n_slab):
            lanes = slice(j * LANES, (j + 1) * LANES)
            os_ = [(o_refs[p][t0:t0 + sub, lanes] if d == 1 else obuf[p, j]).astype(F32)
                   for p, (_, d) in enumerate(DILATED_PATTERNS)]
            att.append((ws[0][:, lanes] * os_[0] + ws[1][:, lanes] * os_[1]
                        + ws[2][:, lanes] * os_[2]).astype(BF16))
        return jnp.concatenate(att, axis=1)

    def project(g, att):
        tr = slice(g * sub, (g + 1) * sub)
        h1 = (x_ref[tr, :] + _dot(att, wo_ref[0:D_ATT, :])
              + _dot(c_ref[tr, :], wo_ref[D_ATT:, :]))
        xq = (_dot(_rms(h1, gx_ref[...]).astype(BF16), wq_ref[...])
              * (XATT_HEAD_DIM ** -0.5)).astype(BF16)
        return h1, xq

    def cross(g, h1, xq):
        heads = []
        for h in range(XATT_HEADS):
            sl = slice(h * XATT_HEAD_DIM, (h + 1) * XATT_HEAD_DIM)
            s = _dot_nt(xq[:, sl], xk_ref[0, :, sl])
            mx = jnp.max(s, axis=-1, keepdims=True)
            p = jnp.exp(s - mx)
            den = jnp.sum(p, axis=-1, keepdims=True)
            heads.append((_dot(p.astype(BF16), xv_ref[0, :, sl]) * (1.0 / den)).astype(BF16))
        xo = jnp.concatenate(heads, axis=1)
        return h1 + _dot(xo, wxo_ref[...])

    def mlp(g, h):
        hn = _rms(h, gm_ref[...]).astype(BF16)
        acc = h
        for j in range(D_FF // chunk):
            u = jnp.maximum(_dot(hn, wu_ref[:, j * chunk:(j + 1) * chunk]), 0.0)
            acc = acc + _dot((u * u).astype(BF16), wd_ref[j * chunk:(j + 1) * chunk, :])
        out_ref[g * sub:(g + 1) * sub, :] = _rms(acc, gf_ref[...]) if final_norm else acc

    for g in range(n_grp):
        h1, xq = project(g, mix(g))
        mlp(g, cross(g, h1, xq))


def _post(x2, os_, lses, c2, w_out, gx, w_xq, xk, xv, w_xo, gm, w_up, w_down, gf, final_norm,
          seq, tm=512, sub=512, chunk=1024):
    T = x2.shape[0]
    n_mem = xk.shape[1]
    per_b = seq // tm
    n_slab = D_ATT // LANES
    d_max = DILATED_PATTERNS[-1][1]
    row = lambda i: (i, 0)
    const = lambda i: (0, 0)
    memb = lambda i: (i // per_b, 0, 0)
    whole = lambda w: pl.BlockSpec(w.shape, const, pipeline_mode=pl.Buffered(1))
    vec = pl.BlockSpec((1, D_MODEL), const)
    n_pat = len(DILATED_PATTERNS)
    return pl.pallas_call(
        functools.partial(_post_kernel, tm=tm, sub=sub, chunk=chunk, final_norm=final_norm),
        grid=(T // tm,),
        in_specs=[pl.BlockSpec((tm, D_MODEL), row)]
                 + [pl.BlockSpec((tm // d, d * D_ATT), row) for _, d in DILATED_PATTERNS]
                 + [pl.BlockSpec((tm // d, d * LANES), row) for _, d in DILATED_PATTERNS]
                 + [pl.BlockSpec((tm, D_CONV), row),
                    pl.BlockSpec((LANES, D_ATT), const),
                    whole(w_out), vec, whole(w_xq),
                    pl.BlockSpec((1, n_mem, D_MODEL), memb),
                    pl.BlockSpec((1, n_mem, D_MODEL), memb),
                    whole(w_xo), vec, whole(w_up), whole(w_down), vec],
        out_specs=pl.BlockSpec((tm, D_MODEL), row),
        out_shape=jax.ShapeDtypeStruct((T, D_MODEL), F32),
        scratch_shapes=[pltpu.VMEM((n_pat, n_slab, sub, LANES), F32),
                          pltpu.VMEM((n_pat, 1, sub, LANES), F32)] * (tm // sub)
                       + [pltpu.VMEM((d_max // DIL_STEP, n_slab, sub // DIL_STEP, LANES), F32),
                          pltpu.VMEM((d_max // DIL_STEP, 1, sub // DIL_STEP, LANES), F32)
                          ] * (tm // sub),
        compiler_params=pltpu.CompilerParams(dimension_semantics=("arbitrary",),
                                             vmem_limit_bytes=VMEM_LIMIT),
        name="post",
    )(x2, *os_, *lses, c2, _head_expand(), w_out, gx, w_xq, xk, xv, w_xo, gm, w_up, w_down, gf)


def _head_expand():
    e = np.zeros((LANES, D_ATT), np.float32)
    for h in range(ATT_HEADS):
        e[h, h * HEAD_DIM:(h + 1) * HEAD_DIM] = 1.0
    return jnp.asarray(e, dtype=BF16)


def _rotary_tables(seq):
    half = ROT_DIM // 2
    freqs = ROPE_THETA ** (-np.arange(0, ROT_DIM, 2, dtype=np.float64) / ROT_DIM)
    ang = np.arange(seq, dtype=np.float64)[:, None] * freqs[None, :]
    cos, sin = np.cos(ang), np.sin(ang)
    zeros = np.zeros((seq, HEAD_DIM - ROT_DIM))
    z8 = np.zeros((seq, half))
    rep = LANES // HEAD_DIM
    tabs = ([cos, cos, zeros + 1.0], [-sin, z8, zeros], [z8, sin, zeros])
    return tuple(jnp.asarray(np.concatenate(t * rep, axis=1), dtype=F32) for t in tabs)


def kernel(x, mem, norm_mix_g, w_in, conv_w, conv_b, conv_ln_g, conv_ln_b, w_out, norm_x_g,
           norm_mem_g, w_xq, w_xk, w_xv, w_xo, norm_mlp_g, w_up, w_down, norm_final_g):
    B, S, D = x.shape
    n_mem = mem.shape[1]
    depth = w_in.shape[0]
    T = B * S
    cos_t, sa_t, sb_t = _rotary_tables(S)
    bias = _attn_bias()
    hmask = _head_masks()
    row = lambda g: g.reshape(1, -1)

    h = x.reshape(T, D)
    for l in range(depth):
        q, k, v, u = _in_proj(h, row(norm_mix_g[l]), w_in[l], cos_t, sa_t, sb_t, S)
        os_, lses = [], []
        for p, (_, d) in enumerate(DILATED_PATTERNS):
            o, lse = _attn(q[p], k[p], v[p], bias, hmask, B, S, d)
            os_.append(o)
            lses.append(lse)
        c, (w_out_b, w_xq_b, w_xo_b, w_up_b, w_down_b) = _conv(
            u.reshape(B, S, D_CONV), conv_w[l], row(conv_b[l]), row(conv_ln_g[l]),
            row(conv_ln_b[l]), (w_out[l], w_xq[l], w_xo[l], w_up[l], w_down[l]))
        xk, xv = _mem_kv(mem.reshape(B * n_mem, D), row(norm_mem_g[l]), w_xk[l], w_xv[l])
        h = _post(h, os_, lses, c.reshape(T, D_CONV), w_out_b, row(norm_x_g[l]), w_xq_b,
                  xk.reshape(B, n_mem, D), xv.reshape(B, n_mem, D), w_xo_b, row(norm_mlp_g[l]),
                  w_up_b, w_down_b, row(norm_final_g), final_norm=(l == depth - 1), seq=S)
    return h.reshape(B, S, D)
```

```python
import functools
import math

import numpy as np
import jax
import jax.numpy as jnp
from jax import lax
from jax.experimental import pallas as pl
from jax.experimental.pallas import tpu as pltpu

F32 = jnp.float32
BF16 = jnp.bfloat16

D_MODEL = 1024
ATT_HEADS = 8
HEAD_DIM = 64
D_ATT = ATT_HEADS * HEAD_DIM
D_CONV = D_MODEL - D_ATT
DILATED_PATTERNS = ((128, 1), (512, 4), (2048, 16))
DIL_STEP = 4
assert all(d == DIL_STEP ** k for k, (_, d) in enumerate(DILATED_PATTERNS))
ROPE_THETA = 500000.0
ROT_DIM = HEAD_DIM // 4
CONV_WIDTH = 31
CONV_PAD = (CONV_WIDTH - 1) // 2
XATT_HEADS = 4
XATT_HEAD_DIM = D_MODEL // XATT_HEADS
D_FF = 4 * D_MODEL
EPS = 1e-6
NEG_INF = -1e30
LN2 = math.log(2.0)
Q_SCALE = HEAD_DIM ** -0.5 / LN2

LANES = 128
SUBLANES = 8
ATT_HALF = 64
ATT_QB = 2 * ATT_HALF
ATT_WIN = ATT_QB + 2 * ATT_HALF
CONV_HALO = 16
GLU_COLS = 256
VMEM_LIMIT = 56 * 1024 * 1024


def _dot(a, b):
    return jnp.dot(a, b, preferred_element_type=F32)


def _dot_nt(a, b):
    return lax.dot_general(a, b, (((1,), (1,)), ((), ())), preferred_element_type=F32)


def _dot_tn(a, b):
    return lax.dot_general(a, b, (((0,), (0,)), ((), ())), preferred_element_type=F32)


def _rms(x, g):
    var = jnp.mean(x * x, axis=-1, keepdims=True)
    return x * lax.rsqrt(var + EPS) * g


def _cast_once(pairs, chunk=512):
    @pl.when(pl.program_id(0) == 0)
    def _():
        for src, dst in pairs:
            for c in range(0, src.shape[1], chunk):
                dst[:, c:c + chunk] = src[:, c:c + chunk].astype(BF16)


def _in_proj_kernel(x_ref, g_ref, wf_ref, cos_ref, sa_ref, sb_ref, *refs, tm):
    n_pat = len(DILATED_PATTERNS)
    qkv_refs = [refs[a * n_pat:(a + 1) * n_pat] for a in range(3)]
    u_ref = refs[3 * n_pat]
    w_ref = refs[3 * n_pat + 1]
    stage = refs[3 * n_pat + 2:]
    _cast_once([(wf_ref, w_ref)])
    xn = _rms(x_ref[...], g_ref[...]).astype(BF16)
    cos, sa, sb = cos_ref[...], sa_ref[...], sb_ref[...]
    n_slab = D_ATT // LANES

    def plain(y, j):
        return y[:, LANES * j:LANES * (j + 1)]

    def rot(y, j):
        yj = plain(y, j)
        return (yj * cos + pltpu.roll(yj, LANES - ROT_DIM // 2, 1) * sa
                + pltpu.roll(yj, ROT_DIM // 2, 1) * sb)

    def emit(y, outs, transform):
        for j in range(n_slab):
            piece = transform(y, j)
            stage[0][0, j] = piece
            outs[0][:, j * LANES:(j + 1) * LANES] = piece.astype(BF16)
        for k in range(1, n_pat):
            d_prev, d = DILATED_PATTERNS[k - 1][1], DILATED_PATTERNS[k][1]
            for r_prev in range(d_prev):
                for t in range(DIL_STEP):
                    r = t * d_prev + r_prev
                    for j in range(n_slab):
                        piece = stage[k - 1][r_prev, j, pl.ds(t, tm // d, stride=DIL_STEP), :]
                        if k + 1 < n_pat:
                            stage[k][r, j] = piece
                        c0 = r * D_ATT + j * LANES
                        outs[k][:, c0:c0 + LANES] = piece.astype(BF16)

    emit(_dot(xn, w_ref[:, 0:D_ATT]), qkv_refs[0], lambda y, j: rot(y, j) * Q_SCALE)
    emit(_dot(xn, w_ref[:, D_ATT:2 * D_ATT]), qkv_refs[1], rot)
    emit(_dot(xn, w_ref[:, 2 * D_ATT:3 * D_ATT]), qkv_refs[2], plain)
    c0 = 3 * D_ATT
    for cc in range(0, D_CONV, GLU_COLS):
        a = _dot(xn, w_ref[:, c0 + cc:c0 + cc + GLU_COLS])
        gt = _dot(xn, w_ref[:, c0 + D_CONV + cc:c0 + D_CONV + cc + GLU_COLS])
        u_ref[:, cc:cc + GLU_COLS] = (a * (1.0 / (1.0 + jnp.exp(-gt)))).astype(u_ref.dtype)


def _in_proj(x2, g, w_in, cos_t, sa_t, sb_t, seq, tm=1024):
    T = x2.shape[0]
    d_in = w_in.shape[1]
    n_s = seq // tm
    n_pat = len(DILATED_PATTERNS)
    row = lambda i: (i, 0)
    tab = lambda i: (i % n_s, 0)
    const = lambda i: (0, 0)
    qkv_specs, qkv_shapes = [], []
    for _ in range(3):
        for _, d in DILATED_PATTERNS:
            qkv_specs.append(pl.BlockSpec((tm // d, d * D_ATT), row))
            qkv_shapes.append(jax.ShapeDtypeStruct((T // d, d * D_ATT), BF16))
    outs = pl.pallas_call(
        functools.partial(_in_proj_kernel, tm=tm),
        grid=(T // tm,),
        in_specs=[pl.BlockSpec((tm, D_MODEL), row),
                  pl.BlockSpec((1, D_MODEL), const),
                  pl.BlockSpec((D_MODEL, d_in), const, pipeline_mode=pl.Buffered(1)),
                  pl.BlockSpec((tm, LANES), tab),
                  pl.BlockSpec((tm, LANES), tab),
                  pl.BlockSpec((tm, LANES), tab)],
        out_specs=qkv_specs + [pl.BlockSpec((tm, D_CONV), row)],
        out_shape=qkv_shapes + [jax.ShapeDtypeStruct((T, D_CONV), BF16)],
        scratch_shapes=[pltpu.VMEM((D_MODEL, d_in), BF16)]
                       + [pltpu.VMEM((d, D_ATT // LANES, tm // d, LANES), F32)
                          for _, d in DILATED_PATTERNS[:-1]],
        compiler_params=pltpu.CompilerParams(dimension_semantics=("arbitrary",),
                                             vmem_limit_bytes=VMEM_LIMIT),
        name="in_proj",
    )(x2, g, w_in, cos_t, sa_t, sb_t)
    return outs[0:n_pat], outs[n_pat:2 * n_pat], outs[2 * n_pat:3 * n_pat], outs[3 * n_pat]


def _attn_kernel(q_ref, kc_ref, kp_ref, kn_ref, vc_ref, vp_ref, vn_ref, bias_ref, hm_ref,
                 o_ref, lse_ref, *, lb, nblk_total):
    i = pl.program_id(2)
    mask_a = hm_ref[0:1, :]
    mask_b = hm_ref[1:2, :]
    nblk = lb // ATT_QB

    def window(cur, before, after, start, lanes):
        if start < 0:
            return jnp.concatenate([before[0, ATT_HALF + start:, lanes],
                                    cur[0, 0:ATT_QB + start, lanes]], axis=0)
        if start + ATT_QB > lb:
            return jnp.concatenate([cur[0, start:lb, lanes],
                                    after[0, 0:start + ATT_QB - lb, lanes]], axis=0)
        return cur[0, start:start + ATT_QB, lanes]

    for n in range(nblk):
        r0 = n * ATT_QB
        gblk = i * nblk + n
        bidx = jnp.where(gblk == 0, 0, jnp.where(gblk == nblk_total - 1, 2, 1))
        bias = bias_ref[bidx]
        lse_rows = []
        for hp in range(q_ref.shape[2] // LANES):
            c0 = LANES * hp
            lanes = slice(c0, c0 + LANES)
            q2 = q_ref[0, r0:r0 + ATT_QB, lanes]
            qs = jnp.concatenate([q2 * mask_a, q2 * mask_b], axis=0)
            parts = []
            for kh in range(ATT_WIN // ATT_QB):
                k0 = r0 - ATT_HALF + kh * ATT_QB
                kw = window(kc_ref, kp_ref, kn_ref, k0, lanes)
                vw = window(vc_ref, vp_ref, vn_ref, k0, lanes)
                s = _dot_nt(kw, qs) + bias[kh * ATT_QB:(kh + 1) * ATT_QB]
                m = jnp.max(s, axis=0, keepdims=True)
                p = jnp.exp2(s - m)
                l = jnp.sum(p, axis=0, keepdims=True)
                parts.append((m, l, _dot_tn(vw, p.astype(BF16))))
            (m1, l1, o1), (m2, l2, o2) = parts
            m = jnp.maximum(m1, m2)
            a1 = jnp.exp2(m1 - m)
            a2 = jnp.exp2(m2 - m)
            l = a1 * l1 + a2 * l2
            inv = 1.0 / l
            ot = o1 * (a1 * inv) + o2 * (a2 * inv)
            own = jnp.concatenate([ot[0:HEAD_DIM, 0:ATT_QB], ot[HEAD_DIM:, ATT_QB:]], axis=0)
            o_ref[0, r0:r0 + ATT_QB, lanes] = jnp.transpose(own).astype(o_ref.dtype)
            lse = (m + jnp.log2(l)) * LN2
            lse_rows += [lse[:, 0:ATT_QB], lse[:, ATT_QB:]]
        for rc in range(len(lse_rows) // ATT_HEADS):
            rows = lse_rows[rc * ATT_HEADS:(rc + 1) * ATT_HEADS]
            pad = jnp.zeros((LANES - ATT_HEADS, ATT_QB), F32)
            lse_ref[0, r0:r0 + ATT_QB, rc * LANES:(rc + 1) * LANES] = jnp.transpose(
                jnp.concatenate(rows + [pad], axis=0))


def _attn_bias():
    i = np.arange(ATT_QB)[:, None]
    j = np.arange(ATT_WIN)[None, :]
    band = (j >= i) & (j <= i + 2 * ATT_HALF)
    first = band & (j >= ATT_HALF)
    last = band & (j < ATT_HALF + ATT_QB)
    tabs = np.stack([first, band, last]).astype(np.float32)
    tabs = np.concatenate([tabs.transpose(0, 2, 1)] * 2, axis=2)
    return jnp.asarray((1.0 - tabs) * NEG_INF, dtype=F32)


def _head_masks():
    lane = np.arange(LANES)
    m = np.zeros((16, LANES), np.float32)
    m[0] = lane < HEAD_DIM
    m[1] = lane >= HEAD_DIM
    return jnp.asarray(m, dtype=BF16)


def _attn(q, k, v, bias, hmask, batch, seq, d, blocks_per_step=16, lb_max=2048):
    L = seq // d
    lb = min(lb_max, L)
    rpb = min(d, max(1, blocks_per_step * ATT_QB // lb))
    width = rpb * D_ATT
    nblk_total = L // ATT_QB
    assert nblk_total >= 2 and L % lb == 0 and lb % ATT_QB == 0 and d % rpb == 0
    view = lambda t: t.reshape(batch, L, d * D_ATT)
    hb = lb // ATT_HALF
    cur = lambda b, r, i: (b, i, r)
    prev = lambda b, r, i: (b, jnp.maximum(i * hb - 1, 0), r)
    nxt = lambda b, r, i: (b, jnp.minimum((i + 1) * hb, L // ATT_HALF - 1), r)
    blk = pl.BlockSpec((1, lb, width), cur)
    halo_p = pl.BlockSpec((1, ATT_HALF, width), prev)
    halo_n = pl.BlockSpec((1, ATT_HALF, width), nxt)
    o, lse = pl.pallas_call(
        functools.partial(_attn_kernel, lb=lb, nblk_total=nblk_total),
        grid=(batch, d // rpb, L // lb),
        in_specs=[blk, blk, halo_p, halo_n, blk, halo_p, halo_n,
                  pl.BlockSpec((3, ATT_WIN, 2 * ATT_QB), lambda b, r, i: (0, 0, 0)),
                  pl.BlockSpec((16, LANES), lambda b, r, i: (0, 0))],
        out_specs=[blk, pl.BlockSpec((1, lb, rpb * LANES), cur)],
        out_shape=[jax.ShapeDtypeStruct((batch, L, d * D_ATT), BF16),
                   jax.ShapeDtypeStruct((batch, L, d * LANES), F32)],
        compiler_params=pltpu.CompilerParams(
            dimension_semantics=("arbitrary", "arbitrary", "arbitrary"),
            vmem_limit_bytes=VMEM_LIMIT),
        name=f"attn_d{d}",
    )(view(q), view(k), view(k), view(k), view(v), view(v), view(v), bias, hmask)
    return o.reshape(batch * L, d * D_ATT), lse.reshape(batch * L, d * LANES)


def _conv_kernel(uc_ref, up_ref, un_ref, w_ref, b_ref, lg_ref, lb_ref, *refs, ts, rows, ln_rows,
                 n_cast):
    cast_in, c_ref = refs[:n_cast], refs[n_cast]
    cast_out, (ext, ybuf) = refs[n_cast + 1:2 * n_cast + 1], refs[2 * n_cast + 1:]
    for src, dst in zip(cast_in, cast_out):
        dst[...] = src[...].astype(BF16)
    i = pl.program_id(1)
    n = pl.num_programs(1)
    n_slab = D_CONV // LANES
    for g in range(n_slab):
        ls = slice(g * LANES, (g + 1) * LANES)
        ext[g, 0:CONV_HALO] = jnp.where(i > 0, up_ref[0, :, ls].astype(F32), 0.0)
        ext[g, CONV_HALO:CONV_HALO + ts] = uc_ref[0, :, ls].astype(F32)
        ext[g, CONV_HALO + ts:] = jnp.where(i < n - 1, un_ref[0, :, ls].astype(F32), 0.0)
    lg = lg_ref[...]
    lb = lb_ref[...]
    off = CONV_HALO - CONV_PAD
    n_blk = ts // rows

    def taps(idx, carry):
        g = idx // n_blk
        base = pl.multiple_of((idx % n_blk) * rows, rows)
        acc = jnp.broadcast_to(b_ref[g], (rows, LANES))
        for j in range(SUBLANES):
            steps = [m for m in range((CONV_WIDTH + off) // SUBLANES + 1)
                     if 0 <= SUBLANES * m + j - off < CONV_WIDTH]
            win = ext[g, pl.ds(base + j, rows + SUBLANES * max(steps)), :]
            for m in steps:
                k = SUBLANES * m + j - off
                acc = acc + win[SUBLANES * m:SUBLANES * m + rows] * w_ref[g, k:k + 1, :]
        ybuf[g, pl.ds(base, rows), :] = acc
        return carry

    lax.fori_loop(0, n_slab * n_blk, taps, 0, unroll=4)
    for r0 in range(0, ts, ln_rows):
        acc = jnp.concatenate([ybuf[g, r0:r0 + ln_rows, :] for g in range(n_slab)], axis=1)
        mu = jnp.mean(acc, axis=-1, keepdims=True)
        cen = acc - mu
        var = jnp.mean(cen * cen, axis=-1, keepdims=True)
        y = cen * lax.rsqrt(var + EPS) * lg + lb
        c_ref[0, r0:r0 + ln_rows, :] = (y * (1.0 / (1.0 + jnp.exp(-y)))).astype(c_ref.dtype)


def _conv(u3, conv_w, conv_b, ln_g, ln_b, weights, ts=512, rows=64, ln_rows=32):
    B, S, C = u3.shape
    n_i = S // ts
    sl = lambda b, i: (b * n_i + i, 0)
    w_rows = [w.shape[0] // (B * n_i) for w in weights]
    w_specs = [pl.BlockSpec((r, w.shape[1]), sl) for r, w in zip(w_rows, weights)]
    hb = ts // CONV_HALO
    cur = lambda b, i: (b, i, 0)
    prev = lambda b, i: (b, jnp.maximum(i * hb - 1, 0), 0)
    nxt = lambda b, i: (b, jnp.minimum((i + 1) * hb, S // CONV_HALO - 1), 0)
    const = lambda b, i: (0, 0)
    const3 = lambda b, i: (0, 0, 0)
    n_slab = C // LANES
    w_slab = jnp.transpose(conv_w.reshape(CONV_WIDTH, n_slab, LANES), (1, 0, 2))
    b_slab = conv_b.reshape(n_slab, 1, LANES)
    outs = pl.pallas_call(
        functools.partial(_conv_kernel, ts=ts, rows=rows, ln_rows=ln_rows, n_cast=len(weights)),
        grid=(B, S // ts),
        in_specs=[pl.BlockSpec((1, ts, C), cur),
                  pl.BlockSpec((1, CONV_HALO, C), prev),
                  pl.BlockSpec((1, CONV_HALO, C), nxt),
                  pl.BlockSpec((n_slab, CONV_WIDTH, LANES), const3),
                  pl.BlockSpec((n_slab, 1, LANES), const3),
                  pl.BlockSpec((1, C), const),
                  pl.BlockSpec((1, C), const)] + w_specs,
        out_specs=[pl.BlockSpec((1, ts, C), cur)] + w_specs,
        out_shape=[jax.ShapeDtypeStruct((B, S, C), BF16)]
                  + [jax.ShapeDtypeStruct(w.shape, BF16) for w in weights],
        scratch_shapes=[pltpu.VMEM((C // LANES, ts + 2 * CONV_HALO, LANES), F32),
                        pltpu.VMEM((C // LANES, ts, LANES), F32)],
        compiler_params=pltpu.CompilerParams(dimension_semantics=("arbitrary", "arbitrary"),
                                             vmem_limit_bytes=VMEM_LIMIT),
        name="conv",
    )(u3, u3, u3, w_slab, b_slab, ln_g, ln_b, *weights)
    return outs[0], outs[1:]


def _mem_kv_kernel(mem_ref, g_ref, wk_ref, wv_ref, k_ref, v_ref):
    mn = _rms(mem_ref[...], g_ref[...]).astype(BF16)
    k_ref[...] = _dot(mn, wk_ref[...].astype(BF16)).astype(BF16)
    v_ref[...] = _dot(mn, wv_ref[...].astype(BF16)).astype(BF16)


def _mem_kv(mem2, g, wk, wv):
    R = mem2.shape[0]
    full = lambda shape: pl.BlockSpec(shape, lambda i: (0, 0))
    return pl.pallas_call(
        _mem_kv_kernel,
        grid=(1,),
        in_specs=[full((R, D_MODEL)), full((1, D_MODEL)),
                  full((D_MODEL, D_MODEL)), full((D_MODEL, D_MODEL))],
        out_specs=[full((R, D_MODEL))] * 2,
        out_shape=[jax.ShapeDtypeStruct((R, D_MODEL), BF16)] * 2,
        compiler_params=pltpu.CompilerParams(dimension_semantics=("arbitrary",),
                                             vmem_limit_bytes=VMEM_LIMIT),
        name="mem_kv",
    )(mem2, g, wk, wv)


def _post_kernel(x_ref, o1_ref, o2_ref, o3_ref, l1_ref, l2_ref, l3_ref, c_ref, e_ref,
                 wo_ref, gx_ref, wq_ref, xk_ref, xv_ref, wxo_ref, gm_ref, wu_ref, wd_ref, gf_ref,
                 out_ref, *bufs, tm, sub, chunk, final_norm):
    n_slab = D_ATT // LANES
    n_pat = len(DILATED_PATTERNS)
    o_refs = (o1_ref, o2_ref, o3_ref)
    l_refs = (l1_ref, l2_ref, l3_ref)
    n_grp = tm // sub

    def mix(g):
        t0 = g * sub
        obuf, lbuf = bufs[2 * g], bufs[2 * g + 1]
        otmp, ltmp = bufs[2 * n_grp + 2 * g], bufs[2 * n_grp + 2 * g + 1]
        for p, (_, d) in enumerate(DILATED_PATTERNS):
            if d == 1:
                continue
            src = slice(t0 // d, (t0 + sub) // d)
            jobs = [(o_refs[p], D_ATT, j, obuf, otmp, True) for j in range(n_slab)]
            jobs.append((l_refs[p], LANES, 0, lbuf, ltmp, False))
            for ref, pitch, j, dst, tmp, widen in jobs:
                pieces = {}
                for r in range(d):
                    t = ref[src, r * pitch + j * LANES:r * pitch + (j + 1) * LANES]
                    pieces[r] = t.astype(F32) if widen else t
                dd = d
                while dd > DIL_STEP:
                    lower = dd // DIL_STEP
                    merged = {}
                    for r_low in range(lower):
                        for t in range(DIL_STEP):
                            tmp[r_low, j, pl.ds(t, sub // dd, stride=DIL_STEP), :] = (
                                pieces[t * lower + r_low])
                        merged[r_low] = tmp[r_low, j, 0:sub // lower, :]
                    pieces, dd = merged, lower
                for r in range(dd):
                    dst[p, j, pl.ds(r, sub // dd, stride=DIL_STEP), :] = pieces[r]

        ls = [l_refs[p][t0:t0 + sub, :] if d == 1 else lbuf[p, 0]
              for p, (_, d) in enumerate(DILATED_PATTERNS)]
        m = jnp.maximum(jnp.maximum(ls[0], ls[1]), ls[2])
        es = [jnp.exp(l - m) for l in ls]
        inv = 1.0 / (es[0] + es[1] + es[2])
        ws = [_dot((e * inv).astype(BF16), e_ref[...]) for e in es]
        att = []
        for j in range(n_slab):
            lanes = slice(j * LANES, (j + 1) * LANES)
            os_ = [(o_refs[p][t0:t0 + sub, lanes] if d == 1 else obuf[p, j]).astype(F32)
                   for p, (_, d) in enumerate(DILATED_PATTERNS)]
            att.append((ws[0][:, lanes] * os_[0] + ws[1][:, lanes] * os_[1]
                        + ws[2][:, lanes] * os_[2]).astype(BF16))
        return jnp.concatenate(att, axis=1)

    def project(g, att):
        tr = slice(g * sub, (g + 1) * sub)
        h1 = (x_ref[tr, :] + _dot(att, wo_ref[0:D_ATT, :])
              + _dot(c_ref[tr, :], wo_ref[D_ATT:, :]))
        xq = (_dot(_rms(h1, gx_ref[...]).astype(BF16), wq_ref[...])
              * (XATT_HEAD_DIM ** -0.5)).astype(BF16)
        return h1, xq

    def cross(g, h1, xq):
        heads = []
        for h in range(XATT_HEADS):
            sl = slice(h * XATT_HEAD_DIM, (h + 1) * XATT_HEAD_DIM)
            s = _dot_nt(xq[:, sl], xk_ref[0, :, sl])
            mx = jnp.max(s, axis=-1, keepdims=True)
            p = jnp.exp(s - mx)
            den = jnp.sum(p, axis=-1, keepdims=True)
            heads.append((_dot(p.astype(BF16), xv_ref[0, :, sl]) * (1.0 / den)).astype(BF16))
        xo = jnp.concatenate(heads, axis=1)
        return h1 + _dot(xo, wxo_ref[...])

    def mlp(g, h):
        hn = _rms(h, gm_ref[...]).astype(BF16)
        acc = h
        for j in range(D_FF // chunk):
            u = jnp.maximum(_dot(hn, wu_ref[:, j * chunk:(j + 1) * chunk]), 0.0)
            acc = acc + _dot((u * u).astype(BF16), wd_ref[j * chunk:(j + 1) * chunk, :])
        out_ref[g * sub:(g + 1) * sub, :] = _rms(acc, gf_ref[...]) if final_norm else acc

    for g in range(n_grp):
        h1, xq = project(g, mix(g))
        mlp(g, cross(g, h1, xq))


def _post(x2, os_, lses, c2, w_out, gx, w_xq, xk, xv, w_xo, gm, w_up, w_down, gf, final_norm,
          seq, tm=512, sub=512, chunk=1024):
    T = x2.shape[0]
    n_mem = xk.shape[1]
    per_b = seq // tm
    n_slab = D_ATT // LANES
    d_max = DILATED_PATTERNS[-1][1]
    row = lambda i: (i, 0)
    const = lambda i: (0, 0)
    memb = lambda i: (i // per_b, 0, 0)
    whole = lambda w: pl.BlockSpec(w.shape, const, pipeline_mode=pl.Buffered(1))
    vec = pl.BlockSpec((1, D_MODEL), const)
    n_pat = len(DILATED_PATTERNS)
    return pl.pallas_call(
        functools.partial(_post_kernel, tm=tm, sub=sub, chunk=chunk, final_norm=final_norm),
        grid=(T // tm,),
        in_specs=[pl.BlockSpec((tm, D_MODEL), row)]
                 + [pl.BlockSpec((tm // d, d * D_ATT), row) for _, d in DILATED_PATTERNS]
                 + [pl.BlockSpec((tm // d, d * LANES), row) for _, d in DILATED_PATTERNS]
                 + [pl.BlockSpec((tm, D_CONV), row),
                    pl.BlockSpec((LANES, D_ATT), const),
                    whole(w_out), vec, whole(w_xq),
                    pl.BlockSpec((1, n_mem, D_MODEL), memb),
                    pl.BlockSpec((1, n_mem, D_MODEL), memb),
                    whole(w_xo), vec, whole(w_up), whole(w_down), vec],
        out_specs=pl.BlockSpec((tm, D_MODEL), row),
        out_shape=jax.ShapeDtypeStruct((T, D_MODEL), F32),
        scratch_shapes=[pltpu.VMEM((n_pat, n_slab, sub, LANES), F32),
                          pltpu.VMEM((n_pat, 1, sub, LANES), F32)] * (tm // sub)
                       + [pltpu.VMEM((d_max // DIL_STEP, n_slab, sub // DIL_STEP, LANES), F32),
                          pltpu.VMEM((d_max // DIL_STEP, 1, sub // DIL_STEP, LANES), F32)
                          ] * (tm // sub),
        compiler_params=pltpu.CompilerParams(dimension_semantics=("arbitrary",),
                                             vmem_limit_bytes=VMEM_LIMIT),
        name="post",
    )(x2, *os_, *lses, c2, _head_expand(), w_out, gx, w_xq, xk, xv, w_xo, gm, w_up, w_down, gf)


def _head_expand():
    e = np.zeros((LANES, D_ATT), np.float32)
    for h in range(ATT_HEADS):
        e[h, h * HEAD_DIM:(h + 1) * HEAD_DIM] = 1.0
    return jnp.asarray(e, dtype=BF16)


def _rotary_tables(seq):
    half = ROT_DIM // 2
    freqs = ROPE_THETA ** (-np.arange(0, ROT_DIM, 2, dtype=np.float64) / ROT_DIM)
    ang = np.arange(seq, dtype=np.float64)[:, None] * freqs[None, :]
    cos, sin = np.cos(ang), np.sin(ang)
    zeros = np.zeros((seq, HEAD_DIM - ROT_DIM))
    z8 = np.zeros((seq, half))
    rep = LANES // HEAD_DIM
    tabs = ([cos, cos, zeros + 1.0], [-sin, z8, zeros], [z8, sin, zeros])
    return tuple(jnp.asarray(np.concatenate(t * rep, axis=1), dtype=F32) for t in tabs)


def kernel(x, mem, norm_mix_g, w_in, conv_w, conv_b, conv_ln_g, conv_ln_b, w_out, norm_x_g,
           norm_mem_g, w_xq, w_xk, w_xv, w_xo, norm_mlp_g, w_up, w_down, norm_final_g):
    B, S, D = x.shape
    n_mem = mem.shape[1]
    depth = w_in.shape[0]
    T = B * S
    cos_t, sa_t, sb_t = _rotary_tables(S)
    bias = _attn_bias()
    hmask = _head_masks()
    row = lambda g: g.reshape(1, -1)

    h = x.reshape(T, D)
    for l in range(depth):
        q, k, v, u = _in_proj(h, row(norm_mix_g[l]), w_in[l], cos_t, sa_t, sb_t, S)
        os_, lses = [], []
        for p, (_, d) in enumerate(DILATED_PATTERNS):
            o, lse = _attn(q[p], k[p], v[p], bias, hmask, B, S, d)
            os_.append(o)
            lses.append(lse)
        c, (w_out_b, w_xq_b, w_xo_b, w_up_b, w_down_b) = _conv(
            u.reshape(B, S, D_CONV), conv_w[l], row(conv_b[l]), row(conv_ln_g[l]),
            row(conv_ln_b[l]), (w_out[l], w_xq[l], w_xo[l], w_up[l], w_down[l]))
        xk, xv = _mem_kv(mem.reshape(B * n_mem, D), row(norm_mem_g[l]), w_xk[l], w_xv[l])
        h = _post(h, os_, lses, c.reshape(T, D_CONV), w_out_b, row(norm_x_g[l]), w_xq_b,
                  xk.reshape(B, n_mem, D), xv.reshape(B, n_mem, D), w_xo_b, row(norm_mlp_g[l]),
                  w_up_b, w_down_b, row(norm_final_g), final_norm=(l == depth - 1), seq=S)
    return h.reshape(B, S, D)
```

```python
import functools
import math

import numpy as np
import jax
import jax.numpy as jnp
from jax import lax
from jax.experimental import pallas as pl
from jax.experimental.pallas import tpu as pltpu

F32 = jnp.float32
BF16 = jnp.bfloat16

D_MODEL = 1024
ATT_HEADS = 8
HEAD_DIM = 64
D_ATT = ATT_HEADS * HEAD_DIM
D_CONV = D_MODEL - D_ATT
DILATED_PATTERNS = ((128, 1), (512, 4), (2048, 16))
DIL_STEP = 4
assert all(d == DIL_STEP ** k for k, (_, d) in enumerate(DILATED_PATTERNS))
KV_LAYOUT = (1, 1, 2)
ROPE_THETA = 500000.0
ROT_DIM = HEAD_DIM // 4
CONV_WIDTH = 31
CONV_PAD = (CONV_WIDTH - 1) // 2
XATT_HEADS = 4
XATT_HEAD_DIM = D_MODEL // XATT_HEADS
D_FF = 4 * D_MODEL
EPS = 1e-6
NEG_INF = -1e30
LN2 = math.log(2.0)
Q_SCALE = HEAD_DIM ** -0.5 / LN2

LANES = 128
SUBLANES = 8
ATT_HALF = 64
ATT_QB = 2 * ATT_HALF
ATT_WIN = ATT_QB + 2 * ATT_HALF
CONV_HALO = 16
GLU_COLS = 256
VMEM_LIMIT = 56 * 1024 * 1024


def _dot(a, b):
    return jnp.dot(a, b, preferred_element_type=F32)


def _dot_nt(a, b):
    return lax.dot_general(a, b, (((1,), (1,)), ((), ())), preferred_element_type=F32)


def _dot_tn(a, b):
    return lax.dot_general(a, b, (((0,), (0,)), ((), ())), preferred_element_type=F32)


def _rms(x, g):
    var = jnp.mean(x * x, axis=-1, keepdims=True)
    return x * lax.rsqrt(var + EPS) * g


def _cast_once(pairs, chunk=512):
    @pl.when(pl.program_id(0) == 0)
    def _():
        for src, dst in pairs:
            for c in range(0, src.shape[1], chunk):
                dst[:, c:c + chunk] = src[:, c:c + chunk].astype(BF16)


def _in_proj_kernel(x_ref, g_ref, wf_ref, cos_ref, sa_ref, sb_ref, *refs, tm):
    n_pat = len(DILATED_PATTERNS)
    kv_pats = sorted(set(KV_LAYOUT))
    refs = list(refs)
    q_refs = [refs.pop(0) for _ in range(n_pat)]
    k_refs = {p: refs.pop(0) for p in kv_pats}
    v_refs = {p: refs.pop(0) for p in kv_pats}
    qkv_refs = [q_refs, [k_refs.get(p) for p in range(n_pat)], [v_refs.get(p) for p in range(n_pat)]]
    u_ref, w_ref = refs[0], refs[1]
    stage = refs[2:]
    _cast_once([(wf_ref, w_ref)])
    xn = _rms(x_ref[...], g_ref[...]).astype(BF16)
    cos, sa, sb = cos_ref[...], sa_ref[...], sb_ref[...]
    n_slab = D_ATT // LANES

    def plain(y, j):
        return y[:, LANES * j:LANES * (j + 1)]

    def rot(y, j):
        yj = plain(y, j)
        return (yj * cos + pltpu.roll(yj, LANES - ROT_DIM // 2, 1) * sa
                + pltpu.roll(yj, ROT_DIM // 2, 1) * sb)

    def emit(y, outs, transform):
        for j in range(n_slab):
            piece = transform(y, j)
            stage[0][0, j] = piece
            if outs[0] is not None:
                outs[0][:, j * LANES:(j + 1) * LANES] = piece.astype(BF16)
        for k in range(1, n_pat):
            d_prev, d = DILATED_PATTERNS[k - 1][1], DILATED_PATTERNS[k][1]
            for r_prev in range(d_prev):
                for t in range(DIL_STEP):
                    r = t * d_prev + r_prev
                    for j in range(n_slab):
                        piece = stage[k - 1][r_prev, j, pl.ds(t, tm // d, stride=DIL_STEP), :]
                        if k + 1 < n_pat:
                            stage[k][r, j] = piece
                        c0 = r * D_ATT + j * LANES
                        if outs[k] is not None:
                            outs[k][:, c0:c0 + LANES] = piece.astype(BF16)

    emit(_dot(xn, w_ref[:, 0:D_ATT]), qkv_refs[0], lambda y, j: rot(y, j) * Q_SCALE)
    emit(_dot(xn, w_ref[:, D_ATT:2 * D_ATT]), qkv_refs[1], rot)
    emit(_dot(xn, w_ref[:, 2 * D_ATT:3 * D_ATT]), qkv_refs[2], plain)
    c0 = 3 * D_ATT
    for cc in range(0, D_CONV, GLU_COLS):
        a = _dot(xn, w_ref[:, c0 + cc:c0 + cc + GLU_COLS])
        gt = _dot(xn, w_ref[:, c0 + D_CONV + cc:c0 + D_CONV + cc + GLU_COLS])
        u_ref[:, cc:cc + GLU_COLS] = (a * (1.0 / (1.0 + jnp.exp(-gt)))).astype(u_ref.dtype)


def _in_proj(x2, g, w_in, cos_t, sa_t, sb_t, seq, tm=1024):
    T = x2.shape[0]
    d_in = w_in.shape[1]
    n_s = seq // tm
    n_pat = len(DILATED_PATTERNS)
    row = lambda i: (i, 0)
    tab = lambda i: (i % n_s, 0)
    const = lambda i: (0, 0)
    kv_pats = sorted(set(KV_LAYOUT))
    qkv_specs, qkv_shapes = [], []
    for pats in (range(n_pat), kv_pats, kv_pats):
        for p in pats:
            d = DILATED_PATTERNS[p][1]
            qkv_specs.append(pl.BlockSpec((tm // d, d * D_ATT), row))
            qkv_shapes.append(jax.ShapeDtypeStruct((T // d, d * D_ATT), BF16))
    outs = pl.pallas_call(
        functools.partial(_in_proj_kernel, tm=tm),
        grid=(T // tm,),
        in_specs=[pl.BlockSpec((tm, D_MODEL), row),
                  pl.BlockSpec((1, D_MODEL), const),
                  pl.BlockSpec((D_MODEL, d_in), const, pipeline_mode=pl.Buffered(1)),
                  pl.BlockSpec((tm, LANES), tab),
                  pl.BlockSpec((tm, LANES), tab),
                  pl.BlockSpec((tm, LANES), tab)],
        out_specs=qkv_specs + [pl.BlockSpec((tm, D_CONV), row)],
        out_shape=qkv_shapes + [jax.ShapeDtypeStruct((T, D_CONV), BF16)],
        scratch_shapes=[pltpu.VMEM((D_MODEL, d_in), BF16)]
                       + [pltpu.VMEM((d, D_ATT // LANES, tm // d, LANES), F32)
                          for _, d in DILATED_PATTERNS[:-1]],
        compiler_params=pltpu.CompilerParams(dimension_semantics=("arbitrary",),
                                             vmem_limit_bytes=VMEM_LIMIT),
        name="in_proj",
    )(x2, g, w_in, cos_t, sa_t, sb_t)
    n_kv = len(kv_pats)
    return (outs[0:n_pat], dict(zip(kv_pats, outs[n_pat:n_pat + n_kv])),
            dict(zip(kv_pats, outs[n_pat + n_kv:n_pat + 2 * n_kv])), outs[n_pat + 2 * n_kv])


def _attn_kernel(q_ref, kc_ref, kp_ref, kn_ref, vc_ref, vp_ref, vn_ref, bias_ref, hm_ref,
                 o_ref, lse_ref, *, lb, nblk_total, kf):
    i = pl.program_id(2)
    mask_a = hm_ref[0:1, :]
    mask_b = hm_ref[1:2, :]
    nblk = lb // ATT_QB

    def window(cur, before, after, start, lanes):
        halo, qb, n, s0 = ATT_HALF // kf, ATT_QB // kf, lb // kf, start // kf
        width = cur.shape[2] // kf
        pieces = []
        for t in range(kf):
            ln = slice(t * width + lanes.start, t * width + lanes.stop)
            if s0 < 0:
                pieces += [before[0, halo + s0:, ln], cur[0, 0:qb + s0, ln]]
            elif s0 + qb > n:
                pieces += [cur[0, s0:n, ln], after[0, 0:s0 + qb - n, ln]]
            else:
                pieces.append(cur[0, s0:s0 + qb, ln])
        return pieces[0] if len(pieces) == 1 else jnp.concatenate(pieces, axis=0)

    for n in range(nblk):
        r0 = n * ATT_QB
        gblk = i * nblk + n
        bidx = jnp.where(gblk == 0, 0, jnp.where(gblk == nblk_total - 1, 2, 1))
        bias = bias_ref[bidx]
        lse_rows = []
        for hp in range(q_ref.shape[2] // LANES):
            c0 = LANES * hp
            lanes = slice(c0, c0 + LANES)
            q2 = q_ref[0, r0:r0 + ATT_QB, lanes]
            qs = jnp.concatenate([q2 * mask_a, q2 * mask_b], axis=0)
            parts = []
            for kh in range(ATT_WIN // ATT_QB):
                k0 = r0 - ATT_HALF + kh * ATT_QB
                kw = window(kc_ref, kp_ref, kn_ref, k0, lanes)
                vw = window(vc_ref, vp_ref, vn_ref, k0, lanes)
                s = _dot_nt(kw, qs) + bias[kh * ATT_QB:(kh + 1) * ATT_QB]
                m = jnp.max(s, axis=0, keepdims=True)
                p = jnp.exp2(s - m)
                l = jnp.sum(p, axis=0, keepdims=True)
                parts.append((m, l, _dot_tn(vw, p.astype(BF16))))
            (m1, l1, o1), (m2, l2, o2) = parts
            m = jnp.maximum(m1, m2)
            a1 = jnp.exp2(m1 - m)
            a2 = jnp.exp2(m2 - m)
            l = a1 * l1 + a2 * l2
            inv = 1.0 / l
            ot = o1 * (a1 * inv) + o2 * (a2 * inv)
            own = jnp.concatenate([ot[0:HEAD_DIM, 0:ATT_QB], ot[HEAD_DIM:, ATT_QB:]], axis=0)
            o_ref[0, r0:r0 + ATT_QB, lanes] = jnp.transpose(own).astype(o_ref.dtype)
            lse = (m + jnp.log2(l)) * LN2
            lse_rows += [lse[:, 0:ATT_QB], lse[:, ATT_QB:]]
        for rc in range(len(lse_rows) // ATT_HEADS):
            rows = lse_rows[rc * ATT_HEADS:(rc + 1) * ATT_HEADS]
            pad = jnp.zeros((LANES - ATT_HEADS, ATT_QB), F32)
            lse_ref[0, r0:r0 + ATT_QB, rc * LANES:(rc + 1) * LANES] = jnp.transpose(
                jnp.concatenate(rows + [pad], axis=0))


def _attn_bias(kf=1):
    i = np.arange(ATT_QB)[:, None]
    j = np.arange(ATT_WIN)[None, :]
    band = (j >= i) & (j <= i + 2 * ATT_HALF)
    first = band & (j >= ATT_HALF)
    last = band & (j < ATT_HALF + ATT_QB)
    tabs = np.stack([first, band, last]).astype(np.float32)
    tabs = np.concatenate([tabs.transpose(0, 2, 1)] * 2, axis=2)
    order = np.arange(ATT_QB).reshape(ATT_QB // kf, kf).T.reshape(-1)
    order = np.concatenate([h * ATT_QB + order for h in range(ATT_WIN // ATT_QB)])
    return jnp.asarray((1.0 - tabs[:, order, :]) * NEG_INF, dtype=F32)


def _head_masks():
    lane = np.arange(LANES)
    m = np.zeros((16, LANES), np.float32)
    m[0] = lane < HEAD_DIM
    m[1] = lane >= HEAD_DIM
    return jnp.asarray(m, dtype=BF16)


def _attn(q, k, v, hmask, batch, seq, d, kf, blocks_per_step=16, lb_max=2048):
    L = seq // d
    lb = min(lb_max, L)
    rpb = min(d, max(1, blocks_per_step * ATT_QB // lb))
    width = rpb * D_ATT
    nblk_total = L // ATT_QB
    assert nblk_total >= 2 and L % lb == 0 and lb % ATT_QB == 0 and d % rpb == 0
    assert kf == 1 or rpb == d, "a dilated K/V layout needs all residue classes in one block"
    view = lambda t: t.reshape(batch, L, d * D_ATT)
    view_kv = lambda t: t.reshape(batch, L // kf, kf * d * D_ATT)
    hb = lb // ATT_HALF
    cur = lambda b, r, i: (b, i, r)
    prev = lambda b, r, i: (b, jnp.maximum(i * hb - 1, 0), r)
    nxt = lambda b, r, i: (b, jnp.minimum((i + 1) * hb, L // ATT_HALF - 1), r)
    blk = pl.BlockSpec((1, lb, width), cur)
    blk_kv = pl.BlockSpec((1, lb // kf, kf * width), cur)
    halo_p = pl.BlockSpec((1, ATT_HALF // kf, kf * width), prev)
    halo_n = pl.BlockSpec((1, ATT_HALF // kf, kf * width), nxt)
    o, lse = pl.pallas_call(
        functools.partial(_attn_kernel, lb=lb, nblk_total=nblk_total, kf=kf),
        grid=(batch, d // rpb, L // lb),
        in_specs=[blk, blk_kv, halo_p, halo_n, blk_kv, halo_p, halo_n,
                  pl.BlockSpec((3, ATT_WIN, 2 * ATT_QB), lambda b, r, i: (0, 0, 0)),
                  pl.BlockSpec((16, LANES), lambda b, r, i: (0, 0))],
        out_specs=[blk, pl.BlockSpec((1, lb, rpb * LANES), cur)],
        out_shape=[jax.ShapeDtypeStruct((batch, L, d * D_ATT), BF16),
                   jax.ShapeDtypeStruct((batch, L, d * LANES), F32)],
        compiler_params=pltpu.CompilerParams(
            dimension_semantics=("arbitrary", "arbitrary", "arbitrary"),
            vmem_limit_bytes=VMEM_LIMIT),
        name=f"attn_d{d}",
    )(view(q), view_kv(k), view_kv(k), view_kv(k), view_kv(v), view_kv(v), view_kv(v),
      _attn_bias(kf), hmask)
    return o.reshape(batch * L, d * D_ATT), lse.reshape(batch * L, d * LANES)


def _conv_kernel(uc_ref, up_ref, un_ref, w_ref, b_ref, lg_ref, lb_ref, *refs, ts, rows, ln_rows,
                 n_cast):
    cast_in, c_ref = refs[:n_cast], refs[n_cast]
    cast_out, (ext, ybuf) = refs[n_cast + 1:2 * n_cast + 1], refs[2 * n_cast + 1:]
    for src, dst in zip(cast_in, cast_out):
        dst[...] = src[...].astype(BF16)
    i = pl.program_id(1)
    n = pl.num_programs(1)
    n_slab = D_CONV // LANES
    for g in range(n_slab):
        ls = slice(g * LANES, (g + 1) * LANES)
        ext[g, 0:CONV_HALO] = jnp.where(i > 0, up_ref[0, :, ls].astype(F32), 0.0)
        ext[g, CONV_HALO:CONV_HALO + ts] = uc_ref[0, :, ls].astype(F32)
        ext[g, CONV_HALO + ts:] = jnp.where(i < n - 1, un_ref[0, :, ls].astype(F32), 0.0)
    lg = lg_ref[...]
    lb = lb_ref[...]
    off = CONV_HALO - CONV_PAD
    n_blk = ts // rows

    def taps(idx, carry):
        g = idx // n_blk
        base = pl.multiple_of((idx % n_blk) * rows, rows)
        acc = jnp.broadcast_to(b_ref[g], (rows, LANES))
        for j in range(SUBLANES):
            steps = [m for m in range((CONV_WIDTH + off) // SUBLANES + 1)
                     if 0 <= SUBLANES * m + j - off < CONV_WIDTH]
            win = ext[g, pl.ds(base + j, rows + SUBLANES * max(steps)), :]
            for m in steps:
                k = SUBLANES * m + j - off
                acc = acc + win[SUBLANES * m:SUBLANES * m + rows] * w_ref[g, k:k + 1, :]
        ybuf[g, pl.ds(base, rows), :] = acc
        return carry

    lax.fori_loop(0, n_slab * n_blk, taps, 0, unroll=4)
    for r0 in range(0, ts, ln_rows):
        acc = jnp.concatenate([ybuf[g, r0:r0 + ln_rows, :] for g in range(n_slab)], axis=1)
        mu = jnp.mean(acc, axis=-1, keepdims=True)
        cen = acc - mu
        var = jnp.mean(cen * cen, axis=-1, keepdims=True)
        y = cen * lax.rsqrt(var + EPS) * lg + lb
        c_ref[0, r0:r0 + ln_rows, :] = (y * (1.0 / (1.0 + jnp.exp(-y)))).astype(c_ref.dtype)


def _conv(u3, conv_w, conv_b, ln_g, ln_b, weights, ts=512, rows=64, ln_rows=32):
    B, S, C = u3.shape
    n_i = S // ts
    sl = lambda b, i: (b * n_i + i, 0)
    w_rows = [w.shape[0] // (B * n_i) for w in weights]
    w_specs = [pl.BlockSpec((r, w.shape[1]), sl) for r, w in zip(w_rows, weights)]
    hb = ts // CONV_HALO
    cur = lambda b, i: (b, i, 0)
    prev = lambda b, i: (b, jnp.maximum(i * hb - 1, 0), 0)
    nxt = lambda b, i: (b, jnp.minimum((i + 1) * hb, S // CONV_HALO - 1), 0)
    const = lambda b, i: (0, 0)
    const3 = lambda b, i: (0, 0, 0)
    n_slab = C // LANES
    w_slab = jnp.transpose(conv_w.reshape(CONV_WIDTH, n_slab, LANES), (1, 0, 2))
    b_slab = conv_b.reshape(n_slab, 1, LANES)
    outs = pl.pallas_call(
        functools.partial(_conv_kernel, ts=ts, rows=rows, ln_rows=ln_rows, n_cast=len(weights)),
        grid=(B, S // ts),
        in_specs=[pl.BlockSpec((1, ts, C), cur),
                  pl.BlockSpec((1, CONV_HALO, C), prev),
                  pl.BlockSpec((1, CONV_HALO, C), nxt),
                  pl.BlockSpec((n_slab, CONV_WIDTH, LANES), const3),
                  pl.BlockSpec((n_slab, 1, LANES), const3),
                  pl.BlockSpec((1, C), const),
                  pl.BlockSpec((1, C), const)] + w_specs,
        out_specs=[pl.BlockSpec((1, ts, C), cur)] + w_specs,
        out_shape=[jax.ShapeDtypeStruct((B, S, C), BF16)]
                  + [jax.ShapeDtypeStruct(w.shape, BF16) for w in weights],
        scratch_shapes=[pltpu.VMEM((C // LANES, ts + 2 * CONV_HALO, LANES), F32),
                        pltpu.VMEM((C // LANES, ts, LANES), F32)],
        compiler_params=pltpu.CompilerParams(dimension_semantics=("arbitrary", "arbitrary"),
                                             vmem_limit_bytes=VMEM_LIMIT),
        name="conv",
    )(u3, u3, u3, w_slab, b_slab, ln_g, ln_b, *weights)
    return outs[0], outs[1:]


def _mem_kv_kernel(mem_ref, g_ref, wk_ref, wv_ref, k_ref, v_ref):
    mn = _rms(mem_ref[...], g_ref[...]).astype(BF16)
    k_ref[...] = _dot(mn, wk_ref[...].astype(BF16)).astype(BF16)
    v_ref[...] = _dot(mn, wv_ref[...].astype(BF16)).astype(BF16)


def _mem_kv(mem2, g, wk, wv):
    R = mem2.shape[0]
    full = lambda shape: pl.BlockSpec(shape, lambda i: (0, 0))
    return pl.pallas_call(
        _mem_kv_kernel,
        grid=(1,),
        in_specs=[full((R, D_MODEL)), full((1, D_MODEL)),
                  full((D_MODEL, D_MODEL)), full((D_MODEL, D_MODEL))],
        out_specs=[full((R, D_MODEL))] * 2,
        out_shape=[jax.ShapeDtypeStruct((R, D_MODEL), BF16)] * 2,
        compiler_params=pltpu.CompilerParams(dimension_semantics=("arbitrary",),
                                             vmem_limit_bytes=VMEM_LIMIT),
        name="mem_kv",
    )(mem2, g, wk, wv)


def _post_kernel(x_ref, o1_ref, o2_ref, o3_ref, l1_ref, l2_ref, l3_ref, c_ref, e_ref,
                 wo_ref, gx_ref, wq_ref, xk_ref, xv_ref, wxo_ref, gm_ref, wu_ref, wd_ref, gf_ref,
                 out_ref, *bufs, tm, sub, chunk, final_norm):
    n_slab = D_ATT // LANES
    n_pat = len(DILATED_PATTERNS)
    o_refs = (o1_ref, o2_ref, o3_ref)
    l_refs = (l1_ref, l2_ref, l3_ref)
    n_grp = tm // sub

    def mix(g):
        t0 = g * sub
        obuf, lbuf = bufs[2 * g], bufs[2 * g + 1]
        otmp, ltmp = bufs[2 * n_grp + 2 * g], bufs[2 * n_grp + 2 * g + 1]
        for p, (_, d) in enumerate(DILATED_PATTERNS):
            if d == 1:
                continue
            src = slice(t0 // d, (t0 + sub) // d)
            jobs = [(o_refs[p], D_ATT, j, obuf, otmp, True) for j in range(n_slab)]
            jobs.append((l_refs[p], LANES, 0, lbuf, ltmp, False))
            for ref, pitch, j, dst, tmp, widen in jobs:
                pieces = {}
                for r in range(d):
                    t = ref[src, r * pitch + j * LANES:r * pitch + (j + 1) * LANES]
                    pieces[r] = t.astype(F32) if widen else t
                dd = d
                while dd > DIL_STEP:
                    lower = dd // DIL_STEP
                    merged = {}
                    for r_low in range(lower):
                        for t in range(DIL_STEP):
                            tmp[r_low, j, pl.ds(t, sub // dd, stride=DIL_STEP), :] = (
                                pieces[t * lower + r_low])
                        merged[r_low] = tmp[r_low, j, 0:sub // lower, :]
                    pieces, dd = merged, lower
                for r in range(dd):
                    dst[p, j, pl.ds(r, sub // dd, stride=DIL_STEP), :] = pieces[r]

        ls = [l_refs[p][t0:t0 + sub, :] if d == 1 else lbuf[p, 0]
              for p, (_, d) in enumerate(DILATED_PATTERNS)]
        m = jnp.maximum(jnp.maximum(ls[0], ls[1]), ls[2])
        es = [jnp.exp(l - m) for l in ls]
        inv = 1.0 / (es[0] + es[1] + es[2])
        ws = [_dot((e * inv).astype(BF16), e_ref[...]) for e in es]
        att = []
        for j in range(n_slab):
            lanes = slice(j * LANES, (j + 1) * LANES)
            os_ = [(o_refs[p][t0:t0 + sub, lanes] if d == 1 else obuf[p, j]).astype(F32)
                   for p, (_, d) in enumerate(DILATED_PATTERNS)]
            att.append((ws[0][:, lanes] * os_[0] + ws[1][:, lanes] * os_[1]
                        + ws[2][:, lanes] * os_[2]).astype(BF16))
        return jnp.concatenate(att, axis=1)

    def project(g, att):
        tr = slice(g * sub, (g + 1) * sub)
        h1 = (x_ref[tr, :] + _dot(att, wo_ref[0:D_ATT, :])
              + _dot(c_ref[tr, :], wo_ref[D_ATT:, :]))
        xq = (_dot(_rms(h1, gx_ref[...]).astype(BF16), wq_ref[...])
              * (XATT_HEAD_DIM ** -0.5)).astype(BF16)
        return h1, xq

    def cross(g, h1, xq):
        heads = []
        for h in range(XATT_HEADS):
            sl = slice(h * XATT_HEAD_DIM, (h + 1) * XATT_HEAD_DIM)
            s = _dot_nt(xq[:, sl], xk_ref[0, :, sl])
            mx = jnp.max(s, axis=-1, keepdims=True)
            p = jnp.exp(s - mx)
            den = jnp.sum(p, axis=-1, keepdims=True)
            heads.append((_dot(p.astype(BF16), xv_ref[0, :, sl]) * (1.0 / den)).astype(BF16))
        xo = jnp.concatenate(heads, axis=1)
        return h1 + _dot(xo, wxo_ref[...])

    def mlp(g, h):
        hn = _rms(h, gm_ref[...]).astype(BF16)
        acc = h
        for j in range(D_FF // chunk):
            u = jnp.maximum(_dot(hn, wu_ref[:, j * chunk:(j + 1) * chunk]), 0.0)
            acc = acc + _dot((u * u).astype(BF16), wd_ref[j * chunk:(j + 1) * chunk, :])
        out_ref[g * sub:(g + 1) * sub, :] = _rms(acc, gf_ref[...]) if final_norm else acc

    for g in range(n_grp):
        h1, xq = project(g, mix(g))
        mlp(g, cross(g, h1, xq))


def _post(x2, os_, lses, c2, w_out, gx, w_xq, xk, xv, w_xo, gm, w_up, w_down, gf, final_norm,
          seq, tm=512, sub=512, chunk=1024):
    T = x2.shape[0]
    n_mem = xk.shape[1]
    per_b = seq // tm
    n_slab = D_ATT // LANES
    d_max = DILATED_PATTERNS[-1][1]
    row = lambda i: (i, 0)
    const = lambda i: (0, 0)
    memb = lambda i: (i // per_b, 0, 0)
    whole = lambda w: pl.BlockSpec(w.shape, const, pipeline_mode=pl.Buffered(1))
    vec = pl.BlockSpec((1, D_MODEL), const)
    n_pat = len(DILATED_PATTERNS)
    return pl.pallas_call(
        functools.partial(_post_kernel, tm=tm, sub=sub, chunk=chunk, final_norm=final_norm),
        grid=(T // tm,),
        in_specs=[pl.BlockSpec((tm, D_MODEL), row)]
                 + [pl.BlockSpec((tm // d, d * D_ATT), row) for _, d in DILATED_PATTERNS]
                 + [pl.BlockSpec((tm // d, d * LANES), row) for _, d in DILATED_PATTERNS]
                 + [pl.BlockSpec((tm, D_CONV), row),
                    pl.BlockSpec((LANES, D_ATT), const),
                    whole(w_out), vec, whole(w_xq),
                    pl.BlockSpec((1, n_mem, D_MODEL), memb),
                    pl.BlockSpec((1, n_mem, D_MODEL), memb),
                    whole(w_xo), vec, whole(w_up), whole(w_down), vec],
        out_specs=pl.BlockSpec((tm, D_MODEL), row),
        out_shape=jax.ShapeDtypeStruct((T, D_MODEL), F32),
        scratch_shapes=[pltpu.VMEM((n_pat, n_slab, sub, LANES), F32),
                          pltpu.VMEM((n_pat, 1, sub, LANES), F32)] * (tm // sub)
                       + [pltpu.VMEM((d_max // DIL_STEP, n_slab, sub // DIL_STEP, LANES), F32),
                          pltpu.VMEM((d_max // DIL_STEP, 1, sub // DIL_STEP, LANES), F32)
                          ] * (tm // sub),
        compiler_params=pltpu.CompilerParams(dimension_semantics=("arbitrary",),
                                             vmem_limit_bytes=VMEM_LIMIT),
        name="post",
    )(x2, *os_, *lses, c2, _head_expand(), w_out, gx, w_xq, xk, xv, w_xo, gm, w_up, w_down, gf)


def _head_expand():
    e = np.zeros((LANES, D_ATT), np.float32)
    for h in range(ATT_HEADS):
        e[h, h * HEAD_DIM:(h + 1) * HEAD_DIM] = 1.0
    return jnp.asarray(e, dtype=BF16)


def _rotary_tables(seq):
    half = ROT_DIM // 2
    freqs = ROPE_THETA ** (-np.arange(0, ROT_DIM, 2, dtype=np.float64) / ROT_DIM)
    ang = np.arange(seq, dtype=np.float64)[:, None] * freqs[None, :]
    cos, sin = np.cos(ang), np.sin(ang)
    zeros = np.zeros((seq, HEAD_DIM - ROT_DIM))
    z8 = np.zeros((seq, half))
    rep = LANES // HEAD_DIM
    tabs = ([cos, cos, zeros + 1.0], [-sin, z8, zeros], [z8, sin, zeros])
    return tuple(jnp.asarray(np.concatenate(t * rep, axis=1), dtype=F32) for t in tabs)


def kernel(x, mem, norm_mix_g, w_in, conv_w, conv_b, conv_ln_g, conv_ln_b, w_out, norm_x_g,
           norm_mem_g, w_xq, w_xk, w_xv, w_xo, norm_mlp_g, w_up, w_down, norm_final_g):
    B, S, D = x.shape
    n_mem = mem.shape[1]
    depth = w_in.shape[0]
    T = B * S
    cos_t, sa_t, sb_t = _rotary_tables(S)
    hmask = _head_masks()
    row = lambda g: g.reshape(1, -1)

    h = x.reshape(T, D)
    for l in range(depth):
        q, k, v, u = _in_proj(h, row(norm_mix_g[l]), w_in[l], cos_t, sa_t, sb_t, S)
        os_, lses = [], []
        for p, (_, d) in enumerate(DILATED_PATTERNS):
            src = KV_LAYOUT[p]
            o, lse = _attn(q[p], k[src], v[src], hmask, B, S, d, DILATED_PATTERNS[src][1] // d)
            os_.append(o)
            lses.append(lse)
        c, (w_out_b, w_xq_b, w_xo_b, w_up_b, w_down_b) = _conv(
            u.reshape(B, S, D_CONV), conv_w[l], row(conv_b[l]), row(conv_ln_g[l]),
            row(conv_ln_b[l]), (w_out[l], w_xq[l], w_xo[l], w_up[l], w_down[l]))
        xk, xv = _mem_kv(mem.reshape(B * n_mem, D), row(norm_mem_g[l]), w_xk[l], w_xv[l])
        h = _post(h, os_, lses, c.reshape(T, D_CONV), w_out_b, row(norm_x_g[l]), w_xq_b,
                  xk.reshape(B, n_mem, D), xv.reshape(B, n_mem, D), w_xo_b, row(norm_mlp_g[l]),
                  w_up_b, w_down_b, row(norm_final_g), final_norm=(l == depth - 1), seq=S)
    return h.reshape(B, S, D)
```

```python
import functools
import math

import numpy as np
import jax
import jax.numpy as jnp
from jax import lax
from jax.experimental import pallas as pl
from jax.experimental.pallas import tpu as pltpu

F32 = jnp.float32
BF16 = jnp.bfloat16

D_MODEL = 1024
ATT_HEADS = 8
HEAD_DIM = 64
D_ATT = ATT_HEADS * HEAD_DIM
D_CONV = D_MODEL - D_ATT
DILATED_PATTERNS = ((128, 1), (512, 4), (2048, 16))
DIL_STEP = 4
assert all(d == DIL_STEP ** k for k, (_, d) in enumerate(DILATED_PATTERNS))
ROPE_THETA = 500000.0
ROT_DIM = HEAD_DIM // 4
CONV_WIDTH = 31
CONV_PAD = (CONV_WIDTH - 1) // 2
XATT_HEADS = 4
XATT_HEAD_DIM = D_MODEL // XATT_HEADS
D_FF = 4 * D_MODEL
EPS = 1e-6
NEG_INF = -1e30
LN2 = math.log(2.0)
Q_SCALE = HEAD_DIM ** -0.5 / LN2

LANES = 128
SUBLANES = 8
ATT_HALF = 64
ATT_QB = 2 * ATT_HALF
ATT_WIN = ATT_QB + 2 * ATT_HALF
CONV_HALO = 16
GLU_COLS = 256
VMEM_LIMIT = 56 * 1024 * 1024


def _dot(a, b):
    return jnp.dot(a, b, preferred_element_type=F32)


def _dot_nt(a, b):
    return lax.dot_general(a, b, (((1,), (1,)), ((), ())), preferred_element_type=F32)


def _dot_tn(a, b):
    return lax.dot_general(a, b, (((0,), (0,)), ((), ())), preferred_element_type=F32)


def _rms(x, g):
    var = jnp.mean(x * x, axis=-1, keepdims=True)
    return x * lax.rsqrt(var + EPS) * g


def _cast_once(pairs, chunk=512):
    @pl.when(pl.program_id(0) == 0)
    def _():
        for src, dst in pairs:
            for c in range(0, src.shape[1], chunk):
                dst[:, c:c + chunk] = src[:, c:c + chunk].astype(BF16)


def _in_proj_kernel(x_ref, g_ref, wf_ref, cos_ref, sa_ref, sb_ref, *refs, tm):
    n_pat = len(DILATED_PATTERNS)
    qkv_refs = [refs[a * n_pat:(a + 1) * n_pat] for a in range(3)]
    u_ref = refs[3 * n_pat]
    w_ref = refs[3 * n_pat + 1]
    stage = refs[3 * n_pat + 2:]
    _cast_once([(wf_ref, w_ref)])
    xn = _rms(x_ref[...], g_ref[...]).astype(BF16)
    cos, sa, sb = cos_ref[...], sa_ref[...], sb_ref[...]
    n_slab = D_ATT // LANES

    def plain(y, j):
        return y[:, LANES * j:LANES * (j + 1)]

    def rot(y, j):
        yj = plain(y, j)
        return (yj * cos + pltpu.roll(yj, LANES - ROT_DIM // 2, 1) * sa
                + pltpu.roll(yj, ROT_DIM // 2, 1) * sb)

    def emit(y, outs, transform):
        for j in range(n_slab):
            piece = transform(y, j)
            stage[0][0, j] = piece
            outs[0][:, j * LANES:(j + 1) * LANES] = piece.astype(BF16)
        for k in range(1, n_pat):
            d_prev, d = DILATED_PATTERNS[k - 1][1], DILATED_PATTERNS[k][1]
            for r_prev in range(d_prev):
                for t in range(DIL_STEP):
                    r = t * d_prev + r_prev
                    for j in range(n_slab):
                        piece = stage[k - 1][r_prev, j, pl.ds(t, tm // d, stride=DIL_STEP), :]
                        if k + 1 < n_pat:
                            stage[k][r, j] = piece
                        c0 = r * D_ATT + j * LANES
                        outs[k][:, c0:c0 + LANES] = piece.astype(BF16)

    emit(_dot(xn, w_ref[:, 0:D_ATT]), qkv_refs[0], lambda y, j: rot(y, j) * Q_SCALE)
    emit(_dot(xn, w_ref[:, D_ATT:2 * D_ATT]), qkv_refs[1], rot)
    emit(_dot(xn, w_ref[:, 2 * D_ATT:3 * D_ATT]), qkv_refs[2], plain)
    c0 = 3 * D_ATT
    for cc in range(0, D_CONV, GLU_COLS):
        a = _dot(xn, w_ref[:, c0 + cc:c0 + cc + GLU_COLS])
        gt = _dot(xn, w_ref[:, c0 + D_CONV + cc:c0 + D_CONV + cc + GLU_COLS])
        u_ref[:, cc:cc + GLU_COLS] = (a * (1.0 / (1.0 + jnp.exp(-gt)))).astype(u_ref.dtype)


def _in_proj(x2, g, w_in, cos_t, sa_t, sb_t, seq, tm=1024):
    T = x2.shape[0]
    d_in = w_in.shape[1]
    n_s = seq // tm
    n_pat = len(DILATED_PATTERNS)
    row = lambda i: (i, 0)
    tab = lambda i: (i % n_s, 0)
    const = lambda i: (0, 0)
    qkv_specs, qkv_shapes = [], []
    for _ in range(3):
        for _, d in DILATED_PATTERNS:
            qkv_specs.append(pl.BlockSpec((tm // d, d * D_ATT), row))
            qkv_shapes.append(jax.ShapeDtypeStruct((T // d, d * D_ATT), BF16))
    outs = pl.pallas_call(
        functools.partial(_in_proj_kernel, tm=tm),
        grid=(T // tm,),
        in_specs=[pl.BlockSpec((tm, D_MODEL), row),
                  pl.BlockSpec((1, D_MODEL), const),
                  pl.BlockSpec((D_MODEL, d_in), const, pipeline_mode=pl.Buffered(1)),
                  pl.BlockSpec((tm, LANES), tab),
                  pl.BlockSpec((tm, LANES), tab),
                  pl.BlockSpec((tm, LANES), tab)],
        out_specs=qkv_specs + [pl.BlockSpec((tm, D_CONV), row)],
        out_shape=qkv_shapes + [jax.ShapeDtypeStruct((T, D_CONV), BF16)],
        scratch_shapes=[pltpu.VMEM((D_MODEL, d_in), BF16)]
                       + [pltpu.VMEM((d, D_ATT // LANES, tm // d, LANES), F32)
                          for _, d in DILATED_PATTERNS[:-1]],
        compiler_params=pltpu.CompilerParams(dimension_semantics=("arbitrary",),
                                             vmem_limit_bytes=VMEM_LIMIT),
        name="in_proj",
    )(x2, g, w_in, cos_t, sa_t, sb_t)
    return outs[0:n_pat], outs[n_pat:2 * n_pat], outs[2 * n_pat:3 * n_pat], outs[3 * n_pat]


def _attn_kernel(q_ref, kc_ref, kp_ref, kn_ref, vc_ref, vp_ref, vn_ref, bias_ref, hm_ref,
                 o_ref, lse_ref, *, lb, nblk_total):
    i = pl.program_id(2)
    mask_a = hm_ref[0:1, :]
    mask_b = hm_ref[1:2, :]
    nblk = lb // ATT_QB

    def window(cur, before, after, start, lanes):
        if start < 0:
            return jnp.concatenate([before[0, ATT_HALF + start:, lanes],
                                    cur[0, 0:ATT_QB + start, lanes]], axis=0)
        if start + ATT_QB > lb:
            return jnp.concatenate([cur[0, start:lb, lanes],
                                    after[0, 0:start + ATT_QB - lb, lanes]], axis=0)
        return cur[0, start:start + ATT_QB, lanes]

    for n in range(nblk):
        r0 = n * ATT_QB
        gblk = i * nblk + n
        bidx = jnp.where(gblk == 0, 0, jnp.where(gblk == nblk_total - 1, 2, 1))
        bias = bias_ref[bidx]
        lse_rows = []
        for hp in range(q_ref.shape[2] // LANES):
            c0 = LANES * hp
            lanes = slice(c0, c0 + LANES)
            q2 = q_ref[0, r0:r0 + ATT_QB, lanes]
            qs = jnp.concatenate([q2 * mask_a, q2 * mask_b], axis=0)
            parts = []
            for kh in range(ATT_WIN // ATT_QB):
                k0 = r0 - ATT_HALF + kh * ATT_QB
                kw = window(kc_ref, kp_ref, kn_ref, k0, lanes)
                vw = window(vc_ref, vp_ref, vn_ref, k0, lanes)
                s = _dot_nt(kw, qs) + bias[kh * ATT_QB:(kh + 1) * ATT_QB]
                m = jnp.max(s, axis=0, keepdims=True)
                p = jnp.exp2(s - m)
                l = jnp.sum(p, axis=0, keepdims=True)
                parts.append((m, l, _dot_tn(vw, p.astype(BF16))))
            (m1, l1, o1), (m2, l2, o2) = parts
            m = jnp.maximum(m1, m2)
            a1 = jnp.exp2(m1 - m)
            a2 = jnp.exp2(m2 - m)
            l = a1 * l1 + a2 * l2
            inv = 1.0 / l
            ot = o1 * (a1 * inv) + o2 * (a2 * inv)
            own = jnp.concatenate([ot[0:HEAD_DIM, 0:ATT_QB], ot[HEAD_DIM:, ATT_QB:]], axis=0)
            o_ref[0, r0:r0 + ATT_QB, lanes] = jnp.transpose(own.astype(o_ref.dtype))
            lse = (m + jnp.log2(l)) * LN2
            lse_rows += [lse[:, 0:ATT_QB], lse[:, ATT_QB:]]
        for rc in range(len(lse_rows) // ATT_HEADS):
            rows = lse_rows[rc * ATT_HEADS:(rc + 1) * ATT_HEADS]
            pad = jnp.zeros((LANES - ATT_HEADS, ATT_QB), F32)
            lse_ref[0, r0:r0 + ATT_QB, rc * LANES:(rc + 1) * LANES] = jnp.transpose(
                jnp.concatenate(rows + [pad], axis=0))


def _attn_bias():
    i = np.arange(ATT_QB)[:, None]
    j = np.arange(ATT_WIN)[None, :]
    band = (j >= i) & (j <= i + 2 * ATT_HALF)
    first = band & (j >= ATT_HALF)
    last = band & (j < ATT_HALF + ATT_QB)
    tabs = np.stack([first, band, last]).astype(np.float32)
    tabs = np.concatenate([tabs.transpose(0, 2, 1)] * 2, axis=2)
    return jnp.asarray((1.0 - tabs) * NEG_INF, dtype=F32)


def _head_masks():
    lane = np.arange(LANES)
    m = np.zeros((16, LANES), np.float32)
    m[0] = lane < HEAD_DIM
    m[1] = lane >= HEAD_DIM
    return jnp.asarray(m, dtype=BF16)


def _attn(q, k, v, bias, hmask, batch, seq, d, blocks_per_step=16, lb_max=2048):
    L = seq // d
    lb = min(lb_max, L)
    rpb = min(d, max(1, blocks_per_step * ATT_QB // lb))
    width = rpb * D_ATT
    nblk_total = L // ATT_QB
    assert nblk_total >= 2 and L % lb == 0 and lb % ATT_QB == 0 and d % rpb == 0
    view = lambda t: t.reshape(batch, L, d * D_ATT)
    hb = lb // ATT_HALF
    cur = lambda b, r, i: (b, i, r)
    prev = lambda b, r, i: (b, jnp.maximum(i * hb - 1, 0), r)
    nxt = lambda b, r, i: (b, jnp.minimum((i + 1) * hb, L // ATT_HALF - 1), r)
    blk = pl.BlockSpec((1, lb, width), cur)
    halo_p = pl.BlockSpec((1, ATT_HALF, width), prev)
    halo_n = pl.BlockSpec((1, ATT_HALF, width), nxt)
    o, lse = pl.pallas_call(
        functools.partial(_attn_kernel, lb=lb, nblk_total=nblk_total),
        grid=(batch, d // rpb, L // lb),
        in_specs=[blk, blk, halo_p, halo_n, blk, halo_p, halo_n,
                  pl.BlockSpec((3, ATT_WIN, 2 * ATT_QB), lambda b, r, i: (0, 0, 0)),
                  pl.BlockSpec((16, LANES), lambda b, r, i: (0, 0))],
        out_specs=[blk, pl.BlockSpec((1, lb, rpb * LANES), cur)],
        out_shape=[jax.ShapeDtypeStruct((batch, L, d * D_ATT), BF16),
                   jax.ShapeDtypeStruct((batch, L, d * LANES), F32)],
        compiler_params=pltpu.CompilerParams(
            dimension_semantics=("arbitrary", "arbitrary", "arbitrary"),
            vmem_limit_bytes=VMEM_LIMIT),
        name=f"attn_d{d}",
    )(view(q), view(k), view(k), view(k), view(v), view(v), view(v), bias, hmask)
    return o.reshape(batch * L, d * D_ATT), lse.reshape(batch * L, d * LANES)


def _conv_kernel(uc_ref, up_ref, un_ref, w_ref, b_ref, lg_ref, lb_ref, *refs, ts, rows, ln_rows,
                 n_cast):
    cast_in, c_ref = refs[:n_cast], refs[n_cast]
    cast_out, (ext, ybuf) = refs[n_cast + 1:2 * n_cast + 1], refs[2 * n_cast + 1:]
    for src, dst in zip(cast_in, cast_out):
        dst[...] = src[...].astype(BF16)
    i = pl.program_id(1)
    n = pl.num_programs(1)
    n_slab = D_CONV // LANES
    for g in range(n_slab):
        ls = slice(g * LANES, (g + 1) * LANES)
        ext[g, 0:CONV_HALO] = jnp.where(i > 0, up_ref[0, :, ls].astype(F32), 0.0)
        ext[g, CONV_HALO:CONV_HALO + ts] = uc_ref[0, :, ls].astype(F32)
        ext[g, CONV_HALO + ts:] = jnp.where(i < n - 1, un_ref[0, :, ls].astype(F32), 0.0)
    lg = lg_ref[...]
    lb = lb_ref[...]
    off = CONV_HALO - CONV_PAD
    n_blk = ts // rows

    def taps(idx, carry):
        g = idx // n_blk
        base = pl.multiple_of((idx % n_blk) * rows, rows)
        acc = jnp.broadcast_to(b_ref[g], (rows, LANES))
        for j in range(SUBLANES):
            steps = [m for m in range((CONV_WIDTH + off) // SUBLANES + 1)
                     if 0 <= SUBLANES * m + j - off < CONV_WIDTH]
            win = ext[g, pl.ds(base + j, rows + SUBLANES * max(steps)), :]
            for m in steps:
                k = SUBLANES * m + j - off
                acc = acc + win[SUBLANES * m:SUBLANES * m + rows] * w_ref[g, k:k + 1, :]
        ybuf[g, pl.ds(base, rows), :] = acc
        return carry

    lax.fori_loop(0, n_slab * n_blk, taps, 0, unroll=4)
    for r0 in range(0, ts, ln_rows):
        acc = jnp.concatenate([ybuf[g, r0:r0 + ln_rows, :] for g in range(n_slab)], axis=1)
        mu = jnp.mean(acc, axis=-1, keepdims=True)
        cen = acc - mu
        var = jnp.mean(cen * cen, axis=-1, keepdims=True)
        y = cen * lax.rsqrt(var + EPS) * lg + lb
        c_ref[0, r0:r0 + ln_rows, :] = (y * (1.0 / (1.0 + jnp.exp(-y)))).astype(c_ref.dtype)


def _conv(u3, conv_w, conv_b, ln_g, ln_b, weights, ts=512, rows=64, ln_rows=32):
    B, S, C = u3.shape
    n_i = S // ts
    sl = lambda b, i: (b * n_i + i, 0)
    w_rows = [w.shape[0] // (B * n_i) for w in weights]
    w_specs = [pl.BlockSpec((r, w.shape[1]), sl) for r, w in zip(w_rows, weights)]
    hb = ts // CONV_HALO
    cur = lambda b, i: (b, i, 0)
    prev = lambda b, i: (b, jnp.maximum(i * hb - 1, 0), 0)
    nxt = lambda b, i: (b, jnp.minimum((i + 1) * hb, S // CONV_HALO - 1), 0)
    const = lambda b, i: (0, 0)
    const3 = lambda b, i: (0, 0, 0)
    n_slab = C // LANES
    w_slab = jnp.transpose(conv_w.reshape(CONV_WIDTH, n_slab, LANES), (1, 0, 2))
    b_slab = conv_b.reshape(n_slab, 1, LANES)
    outs = pl.pallas_call(
        functools.partial(_conv_kernel, ts=ts, rows=rows, ln_rows=ln_rows, n_cast=len(weights)),
        grid=(B, S // ts),
        in_specs=[pl.BlockSpec((1, ts, C), cur),
                  pl.BlockSpec((1, CONV_HALO, C), prev),
                  pl.BlockSpec((1, CONV_HALO, C), nxt),
                  pl.BlockSpec((n_slab, CONV_WIDTH, LANES), const3),
                  pl.BlockSpec((n_slab, 1, LANES), const3),
                  pl.BlockSpec((1, C), const),
                  pl.BlockSpec((1, C), const)] + w_specs,
        out_specs=[pl.BlockSpec((1, ts, C), cur)] + w_specs,
        out_shape=[jax.ShapeDtypeStruct((B, S, C), BF16)]
                  + [jax.ShapeDtypeStruct(w.shape, BF16) for w in weights],
        scratch_shapes=[pltpu.VMEM((C // LANES, ts + 2 * CONV_HALO, LANES), F32),
                        pltpu.VMEM((C // LANES, ts, LANES), F32)],
        compiler_params=pltpu.CompilerParams(dimension_semantics=("arbitrary", "arbitrary"),
                                             vmem_limit_bytes=VMEM_LIMIT),
        name="conv",
    )(u3, u3, u3, w_slab, b_slab, ln_g, ln_b, *weights)
    return outs[0], outs[1:]


def _mem_kv_kernel(mem_ref, g_ref, wk_ref, wv_ref, k_ref, v_ref):
    mn = _rms(mem_ref[...], g_ref[...]).astype(BF16)
    k_ref[...] = _dot(mn, wk_ref[...].astype(BF16)).astype(BF16)
    v_ref[...] = _dot(mn, wv_ref[...].astype(BF16)).astype(BF16)


def _mem_kv(mem2, g, wk, wv):
    R = mem2.shape[0]
    full = lambda shape: pl.BlockSpec(shape, lambda i: (0, 0))
    return pl.pallas_call(
        _mem_kv_kernel,
        grid=(1,),
        in_specs=[full((R, D_MODEL)), full((1, D_MODEL)),
                  full((D_MODEL, D_MODEL)), full((D_MODEL, D_MODEL))],
        out_specs=[full((R, D_MODEL))] * 2,
        out_shape=[jax.ShapeDtypeStruct((R, D_MODEL), BF16)] * 2,
        compiler_params=pltpu.CompilerParams(dimension_semantics=("arbitrary",),
                                             vmem_limit_bytes=VMEM_LIMIT),
        name="mem_kv",
    )(mem2, g, wk, wv)


def _post_kernel(x_ref, o1_ref, o2_ref, o3_ref, l1_ref, l2_ref, l3_ref, c_ref, e_ref,
                 wo_ref, gx_ref, wq_ref, xk_ref, xv_ref, wxo_ref, gm_ref, wu_ref, wd_ref, gf_ref,
                 out_ref, *bufs, tm, sub, chunk, final_norm):
    n_slab = D_ATT // LANES
    n_pat = len(DILATED_PATTERNS)
    o_refs = (o1_ref, o2_ref, o3_ref)
    l_refs = (l1_ref, l2_ref, l3_ref)
    n_grp = tm // sub

    def mix(g):
        t0 = g * sub
        obuf, lbuf = bufs[2 * g], bufs[2 * g + 1]
        otmp, ltmp = bufs[2 * n_grp + 2 * g], bufs[2 * n_grp + 2 * g + 1]
        for p, (_, d) in enumerate(DILATED_PATTERNS):
            if d == 1:
                continue
            src = slice(t0 // d, (t0 + sub) // d)
            jobs = [(o_refs[p], D_ATT, j, obuf, otmp, True) for j in range(n_slab)]
            jobs.append((l_refs[p], LANES, 0, lbuf, ltmp, False))
            for ref, pitch, j, dst, tmp, widen in jobs:
                pieces = {}
                for r in range(d):
                    t = ref[src, r * pitch + j * LANES:r * pitch + (j + 1) * LANES]
                    pieces[r] = t.astype(F32) if widen else t
                dd = d
                while dd > DIL_STEP:
                    lower = dd // DIL_STEP
                    merged = {}
                    for r_low in range(lower):
                        for t in range(DIL_STEP):
                            tmp[r_low, j, pl.ds(t, sub // dd, stride=DIL_STEP), :] = (
                                pieces[t * lower + r_low])
                        merged[r_low] = tmp[r_low, j, 0:sub // lower, :]
                    pieces, dd = merged, lower
                for r in range(dd):
                    dst[p, j, pl.ds(r, sub // dd, stride=DIL_STEP), :] = pieces[r]

        ls = [l_refs[p][t0:t0 + sub, :] if d == 1 else lbuf[p, 0]
              for p, (_, d) in enumerate(DILATED_PATTERNS)]
        m = jnp.maximum(jnp.maximum(ls[0], ls[1]), ls[2])
        es = [jnp.exp(l - m) for l in ls]
        inv = 1.0 / (es[0] + es[1] + es[2])
        ws = [_dot((e * inv).astype(BF16), e_ref[...]) for e in es]
        att = []
        for j in range(n_slab):
            lanes = slice(j * LANES, (j + 1) * LANES)
            os_ = [(o_refs[p][t0:t0 + sub, lanes] if d == 1 else obuf[p, j]).astype(F32)
                   for p, (_, d) in enumerate(DILATED_PATTERNS)]
            att.append((ws[0][:, lanes] * os_[0] + ws[1][:, lanes] * os_[1]
                        + ws[2][:, lanes] * os_[2]).astype(BF16))
        return jnp.concatenate(att, axis=1)

    def project(g, att):
        tr = slice(g * sub, (g + 1) * sub)
        h1 = (x_ref[tr, :] + _dot(att, wo_ref[0:D_ATT, :])
              + _dot(c_ref[tr, :], wo_ref[D_ATT:, :]))
        xq = (_dot(_rms(h1, gx_ref[...]).astype(BF16), wq_ref[...])
              * (XATT_HEAD_DIM ** -0.5)).astype(BF16)
        return h1, xq

    def cross(g, h1, xq):
        heads = []
        for h in range(XATT_HEADS):
            sl = slice(h * XATT_HEAD_DIM, (h + 1) * XATT_HEAD_DIM)
            s = _dot_nt(xq[:, sl], xk_ref[0, :, sl])
            mx = jnp.max(s, axis=-1, keepdims=True)
            p = jnp.exp(s - mx)
            den = jnp.sum(p, axis=-1, keepdims=True)
            heads.append((_dot(p.astype(BF16), xv_ref[0, :, sl]) * (1.0 / den)).astype(BF16))
        xo = jnp.concatenate(heads, axis=1)
        return h1 + _dot(xo, wxo_ref[...])

    def mlp(g, h):
        hn = _rms(h, gm_ref[...]).astype(BF16)
        acc = h
        for j in range(D_FF // chunk):
            u = jnp.maximum(_dot(hn, wu_ref[:, j * chunk:(j + 1) * chunk]), 0.0)
            acc = acc + _dot((u * u).astype(BF16), wd_ref[j * chunk:(j + 1) * chunk, :])
        out_ref[g * sub:(g + 1) * sub, :] = _rms(acc, gf_ref[...]) if final_norm else acc

    for g in range(n_grp):
        h1, xq = project(g, mix(g))
        mlp(g, cross(g, h1, xq))


def _post(x2, os_, lses, c2, w_out, gx, w_xq, xk, xv, w_xo, gm, w_up, w_down, gf, final_norm,
          seq, tm=512, sub=512, chunk=1024):
    T = x2.shape[0]
    n_mem = xk.shape[1]
    per_b = seq // tm
    n_slab = D_ATT // LANES
    d_max = DILATED_PATTERNS[-1][1]
    row = lambda i: (i, 0)
    const = lambda i: (0, 0)
    memb = lambda i: (i // per_b, 0, 0)
    whole = lambda w: pl.BlockSpec(w.shape, const, pipeline_mode=pl.Buffered(1))
    vec = pl.BlockSpec((1, D_MODEL), const)
    n_pat = len(DILATED_PATTERNS)
    return pl.pallas_call(
        functools.partial(_post_kernel, tm=tm, sub=sub, chunk=chunk, final_norm=final_norm),
        grid=(T // tm,),
        in_specs=[pl.BlockSpec((tm, D_MODEL), row)]
                 + [pl.BlockSpec((tm // d, d * D_ATT), row) for _, d in DILATED_PATTERNS]
                 + [pl.BlockSpec((tm // d, d * LANES), row) for _, d in DILATED_PATTERNS]
                 + [pl.BlockSpec((tm, D_CONV), row),
                    pl.BlockSpec((LANES, D_ATT), const),
                    whole(w_out), vec, whole(w_xq),
                    pl.BlockSpec((1, n_mem, D_MODEL), memb),
                    pl.BlockSpec((1, n_mem, D_MODEL), memb),
                    whole(w_xo), vec, whole(w_up), whole(w_down), vec],
        out_specs=pl.BlockSpec((tm, D_MODEL), row),
        out_shape=jax.ShapeDtypeStruct((T, D_MODEL), F32),
        scratch_shapes=[pltpu.VMEM((n_pat, n_slab, sub, LANES), F32),
                          pltpu.VMEM((n_pat, 1, sub, LANES), F32)] * (tm // sub)
                       + [pltpu.VMEM((d_max // DIL_STEP, n_slab, sub // DIL_STEP, LANES), F32),
                          pltpu.VMEM((d_max // DIL_STEP, 1, sub // DIL_STEP, LANES), F32)
                          ] * (tm // sub),
        compiler_params=pltpu.CompilerParams(dimension_semantics=("arbitrary",),
                                             vmem_limit_bytes=VMEM_LIMIT),
        name="post",
    )(x2, *os_, *lses, c2, _head_expand(), w_out, gx, w_xq, xk, xv, w_xo, gm, w_up, w_down, gf)


def _head_expand():
    e = np.zeros((LANES, D_ATT), np.float32)
    for h in range(ATT_HEADS):
        e[h, h * HEAD_DIM:(h + 1) * HEAD_DIM] = 1.0
    return jnp.asarray(e, dtype=BF16)


def _rotary_tables(seq):
    half = ROT_DIM // 2
    freqs = ROPE_THETA ** (-np.arange(0, ROT_DIM, 2, dtype=np.float64) / ROT_DIM)
    ang = np.arange(seq, dtype=np.float64)[:, None] * freqs[None, :]
    cos, sin = np.cos(ang), np.sin(ang)
    zeros = np.zeros((seq, HEAD_DIM - ROT_DIM))
    z8 = np.zeros((seq, half))
    rep = LANES // HEAD_DIM
    tabs = ([cos, cos, zeros + 1.0], [-sin, z8, zeros], [z8, sin, zeros])
    return tuple(jnp.asarray(np.concatenate(t * rep, axis=1), dtype=F32) for t in tabs)


def kernel(x, mem, norm_mix_g, w_in, conv_w, conv_b, conv_ln_g, conv_ln_b, w_out, norm_x_g,
           norm_mem_g, w_xq, w_xk, w_xv, w_xo, norm_mlp_g, w_up, w_down, norm_final_g):
    B, S, D = x.shape
    n_mem = mem.shape[1]
    depth = w_in.shape[0]
    T = B * S
    cos_t, sa_t, sb_t = _rotary_tables(S)
    bias = _attn_bias()
    hmask = _head_masks()
    row = lambda g: g.reshape(1, -1)

    h = x.reshape(T, D)
    for l in range(depth):
        q, k, v, u = _in_proj(h, row(norm_mix_g[l]), w_in[l], cos_t, sa_t, sb_t, S)
        os_, lses = [], []
        for p, (_, d) in enumerate(DILATED_PATTERNS):
            o, lse = _attn(q[p], k[p], v[p], bias, hmask, B, S, d)
            os_.append(o)
            lses.append(lse)
        c, (w_out_b, w_xq_b, w_xo_b, w_up_b, w_down_b) = _conv(
            u.reshape(B, S, D_CONV), conv_w[l], row(conv_b[l]), row(conv_ln_g[l]),
            row(conv_ln_b[l]), (w_out[l], w_xq[l], w_xo[l], w_up[l], w_down[l]))
        xk, xv = _mem_kv(mem.reshape(B * n_mem, D), row(norm_mem_g[l]), w_xk[l], w_xv[l])
        h = _post(h, os_, lses, c.reshape(T, D_CONV), w_out_b, row(norm_x_g[l]), w_xq_b,
                  xk.reshape(B, n_mem, D), xv.reshape(B, n_mem, D), w_xo_b, row(norm_mlp_g[l]),
                  w_up_b, w_down_b, row(norm_final_g), final_norm=(l == depth - 1), seq=S)
    return h.reshape(B, S, D)
```

```python
import functools
import math

import numpy as np
import jax
import jax.numpy as jnp
from jax import lax
from jax.experimental import pallas as pl
from jax.experimental.pallas import tpu as pltpu

F32 = jnp.float32
BF16 = jnp.bfloat16

D_MODEL = 1024
ATT_HEADS = 8
HEAD_DIM = 64
D_ATT = ATT_HEADS * HEAD_DIM
D_CONV = D_MODEL - D_ATT
DILATED_PATTERNS = ((128, 1), (512, 4), (2048, 16))
DIL_STEP = 4
assert all(d == DIL_STEP ** k for k, (_, d) in enumerate(DILATED_PATTERNS))
ROPE_THETA = 500000.0
ROT_DIM = HEAD_DIM // 4
CONV_WIDTH = 31
CONV_PAD = (CONV_WIDTH - 1) // 2
XATT_HEADS = 4
XATT_HEAD_DIM = D_MODEL // XATT_HEADS
D_FF = 4 * D_MODEL
EPS = 1e-6
NEG_INF = -1e30
LN2 = math.log(2.0)
Q_SCALE = HEAD_DIM ** -0.5 / LN2

LANES = 128


def _qk_lane_order():
    half = ROT_DIM // 2
    order = np.arange(LANES)
    order[half:ROT_DIM] = HEAD_DIM + np.arange(half)
    order[HEAD_DIM:HEAD_DIM + half] = half + np.arange(half)
    return order


SUBLANES = 8
ATT_HALF = 64
ATT_QB = 2 * ATT_HALF
ATT_WIN = ATT_QB + 2 * ATT_HALF
CONV_HALO = 16
GLU_COLS = 256
VMEM_LIMIT = 56 * 1024 * 1024


def _dot(a, b):
    return jnp.dot(a, b, preferred_element_type=F32)


def _dot_nt(a, b):
    return lax.dot_general(a, b, (((1,), (1,)), ((), ())), preferred_element_type=F32)


def _dot_tn(a, b):
    return lax.dot_general(a, b, (((0,), (0,)), ((), ())), preferred_element_type=F32)


def _rms(x, g):
    var = jnp.mean(x * x, axis=-1, keepdims=True)
    return x * lax.rsqrt(var + EPS) * g


def _cast_once(pairs, chunk=512):
    @pl.when(pl.program_id(0) == 0)
    def _():
        for src, dst in pairs:
            for c in range(0, src.shape[1], chunk):
                dst[:, c:c + chunk] = src[:, c:c + chunk].astype(BF16)


def _in_proj_kernel(x_ref, g_ref, wf_ref, perm_ref, cos_ref, sin_ref, *refs, tm):
    n_pat = len(DILATED_PATTERNS)
    qkv_refs = [refs[a * n_pat:(a + 1) * n_pat] for a in range(3)]
    u_ref = refs[3 * n_pat]
    w_ref = refs[3 * n_pat + 1]
    stage = refs[3 * n_pat + 2:]
    _cast_once([(wf_ref, w_ref)])

    @pl.when(pl.program_id(0) == 0)
    def _():
        for c in range(0, 2 * D_ATT, LANES):
            w_ref[:, c:c + LANES] = _dot(w_ref[:, c:c + LANES], perm_ref[...]).astype(BF16)

    xn = _rms(x_ref[...], g_ref[...]).astype(BF16)
    cos, sin = cos_ref[...], sin_ref[...]
    n_slab = D_ATT // LANES

    def plain(y, j):
        return y[:, LANES * j:LANES * (j + 1)]

    def rot(y, j):
        yj = plain(y, j)
        return yj * cos + pltpu.roll(yj, HEAD_DIM, 1) * sin

    def emit(y, outs, transform):
        for j in range(n_slab):
            piece = transform(y, j)
            stage[0][0, j] = piece
            outs[0][:, j * LANES:(j + 1) * LANES] = piece.astype(BF16)
        for k in range(1, n_pat):
            d_prev, d = DILATED_PATTERNS[k - 1][1], DILATED_PATTERNS[k][1]
            for r_prev in range(d_prev):
                for t in range(DIL_STEP):
                    r = t * d_prev + r_prev
                    for j in range(n_slab):
                        piece = stage[k - 1][r_prev, j, pl.ds(t, tm // d, stride=DIL_STEP), :]
                        if k + 1 < n_pat:
                            stage[k][r, j] = piece
                        c0 = r * D_ATT + j * LANES
                        outs[k][:, c0:c0 + LANES] = piece.astype(BF16)

    emit(_dot(xn, w_ref[:, 0:D_ATT]), qkv_refs[0], lambda y, j: rot(y, j) * Q_SCALE)
    emit(_dot(xn, w_ref[:, D_ATT:2 * D_ATT]), qkv_refs[1], rot)
    emit(_dot(xn, w_ref[:, 2 * D_ATT:3 * D_ATT]), qkv_refs[2], plain)
    c0 = 3 * D_ATT
    for cc in range(0, D_CONV, GLU_COLS):
        a = _dot(xn, w_ref[:, c0 + cc:c0 + cc + GLU_COLS])
        gt = _dot(xn, w_ref[:, c0 + D_CONV + cc:c0 + D_CONV + cc + GLU_COLS])
        u_ref[:, cc:cc + GLU_COLS] = (a * (1.0 / (1.0 + jnp.exp(-gt)))).astype(u_ref.dtype)


def _in_proj(x2, g, w_in, cos_t, sin_t, seq, tm=1024):
    T = x2.shape[0]
    d_in = w_in.shape[1]
    n_s = seq // tm
    n_pat = len(DILATED_PATTERNS)
    row = lambda i: (i, 0)
    tab = lambda i: (i % n_s, 0)
    const = lambda i: (0, 0)
    qkv_specs, qkv_shapes = [], []
    for _ in range(3):
        for _, d in DILATED_PATTERNS:
            qkv_specs.append(pl.BlockSpec((tm // d, d * D_ATT), row))
            qkv_shapes.append(jax.ShapeDtypeStruct((T // d, d * D_ATT), BF16))
    outs = pl.pallas_call(
        functools.partial(_in_proj_kernel, tm=tm),
        grid=(T // tm,),
        in_specs=[pl.BlockSpec((tm, D_MODEL), row),
                  pl.BlockSpec((1, D_MODEL), const),
                  pl.BlockSpec((D_MODEL, d_in), const, pipeline_mode=pl.Buffered(1)),
                  pl.BlockSpec((LANES, LANES), const),
                  pl.BlockSpec((tm, LANES), tab),
                  pl.BlockSpec((tm, LANES), tab)],
        out_specs=qkv_specs + [pl.BlockSpec((tm, D_CONV), row)],
        out_shape=qkv_shapes + [jax.ShapeDtypeStruct((T, D_CONV), BF16)],
        scratch_shapes=[pltpu.VMEM((D_MODEL, d_in), BF16)]
                       + [pltpu.VMEM((d, D_ATT // LANES, tm // d, LANES), F32)
                          for _, d in DILATED_PATTERNS[:-1]],
        compiler_params=pltpu.CompilerParams(dimension_semantics=("arbitrary",),
                                             vmem_limit_bytes=VMEM_LIMIT),
        name="in_proj",
    )(x2, g, w_in, _qk_lane_perm(), cos_t, sin_t)
    return outs[0:n_pat], outs[n_pat:2 * n_pat], outs[2 * n_pat:3 * n_pat], outs[3 * n_pat]


def _attn_kernel(q_ref, kc_ref, kp_ref, kn_ref, vc_ref, vp_ref, vn_ref, bias_ref, hm_ref,
                 o_ref, lse_ref, *, lb, nblk_total):
    i = pl.program_id(2)
    mask_a = hm_ref[0:1, :]
    mask_b = hm_ref[1:2, :]
    nblk = lb // ATT_QB

    def window(cur, before, after, start, lanes):
        if start < 0:
            return jnp.concatenate([before[0, ATT_HALF + start:, lanes],
                                    cur[0, 0:ATT_QB + start, lanes]], axis=0)
        if start + ATT_QB > lb:
            return jnp.concatenate([cur[0, start:lb, lanes],
                                    after[0, 0:start + ATT_QB - lb, lanes]], axis=0)
        return cur[0, start:start + ATT_QB, lanes]

    for n in range(nblk):
        r0 = n * ATT_QB
        gblk = i * nblk + n
        bidx = jnp.where(gblk == 0, 0, jnp.where(gblk == nblk_total - 1, 2, 1))
        bias = bias_ref[bidx]
        lse_rows = []
        for hp in range(q_ref.shape[2] // LANES):
            c0 = LANES * hp
            lanes = slice(c0, c0 + LANES)
            q2 = q_ref[0, r0:r0 + ATT_QB, lanes]
            qs = jnp.concatenate([q2 * mask_a, q2 * mask_b], axis=0)
            parts = []
            for kh in range(ATT_WIN // ATT_QB):
                k0 = r0 - ATT_HALF + kh * ATT_QB
                kw = window(kc_ref, kp_ref, kn_ref, k0, lanes)
                vw = window(vc_ref, vp_ref, vn_ref, k0, lanes)
                s = _dot_nt(kw, qs) + bias[kh * ATT_QB:(kh + 1) * ATT_QB]
                m = jnp.max(s, axis=0, keepdims=True)
                p = jnp.exp2(s - m)
                l = jnp.sum(p, axis=0, keepdims=True)
                parts.append((m, l, _dot_tn(vw, p.astype(BF16))))
            (m1, l1, o1), (m2, l2, o2) = parts
            m = jnp.maximum(m1, m2)
            a1 = jnp.exp2(m1 - m)
            a2 = jnp.exp2(m2 - m)
            l = a1 * l1 + a2 * l2
            inv = 1.0 / l
            ot = o1 * (a1 * inv) + o2 * (a2 * inv)
            own = jnp.concatenate([ot[0:HEAD_DIM, 0:ATT_QB], ot[HEAD_DIM:, ATT_QB:]], axis=0)
            o_ref[0, r0:r0 + ATT_QB, lanes] = jnp.transpose(own.astype(o_ref.dtype))
            lse = (m + jnp.log2(l)) * LN2
            lse_rows += [lse[:, 0:ATT_QB], lse[:, ATT_QB:]]
        for rc in range(len(lse_rows) // ATT_HEADS):
            rows = lse_rows[rc * ATT_HEADS:(rc + 1) * ATT_HEADS]
            pad = jnp.zeros((LANES - ATT_HEADS, ATT_QB), F32)
            lse_ref[0, r0:r0 + ATT_QB, rc * LANES:(rc + 1) * LANES] = jnp.transpose(
                jnp.concatenate(rows + [pad], axis=0))


def _attn_bias():
    i = np.arange(ATT_QB)[:, None]
    j = np.arange(ATT_WIN)[None, :]
    band = (j >= i) & (j <= i + 2 * ATT_HALF)
    first = band & (j >= ATT_HALF)
    last = band & (j < ATT_HALF + ATT_QB)
    tabs = np.stack([first, band, last]).astype(np.float32)
    tabs = np.concatenate([tabs.transpose(0, 2, 1)] * 2, axis=2)
    return jnp.asarray((1.0 - tabs) * NEG_INF, dtype=F32)


def _head_masks():
    first = _qk_lane_order() < HEAD_DIM
    m = np.zeros((16, LANES), np.float32)
    m[0] = first
    m[1] = ~first
    return jnp.asarray(m, dtype=BF16)


def _qk_lane_perm():
    p = np.zeros((LANES, LANES), np.float32)
    p[_qk_lane_order(), np.arange(LANES)] = 1.0
    return jnp.asarray(p, dtype=BF16)


def _attn(q, k, v, bias, hmask, batch, seq, d, blocks_per_step=16, lb_max=2048):
    L = seq // d
    lb = min(lb_max, L)
    rpb = min(d, max(1, blocks_per_step * ATT_QB // lb))
    width = rpb * D_ATT
    nblk_total = L // ATT_QB
    assert nblk_total >= 2 and L % lb == 0 and lb % ATT_QB == 0 and d % rpb == 0
    view = lambda t: t.reshape(batch, L, d * D_ATT)
    hb = lb // ATT_HALF
    cur = lambda b, r, i: (b, i, r)
    prev = lambda b, r, i: (b, jnp.maximum(i * hb - 1, 0), r)
    nxt = lambda b, r, i: (b, jnp.minimum((i + 1) * hb, L // ATT_HALF - 1), r)
    blk = pl.BlockSpec((1, lb, width), cur)
    halo_p = pl.BlockSpec((1, ATT_HALF, width), prev)
    halo_n = pl.BlockSpec((1, ATT_HALF, width), nxt)
    o, lse = pl.pallas_call(
        functools.partial(_attn_kernel, lb=lb, nblk_total=nblk_total),
        grid=(batch, d // rpb, L // lb),
        in_specs=[blk, blk, halo_p, halo_n, blk, halo_p, halo_n,
                  pl.BlockSpec((3, ATT_WIN, 2 * ATT_QB), lambda b, r, i: (0, 0, 0)),
                  pl.BlockSpec((16, LANES), lambda b, r, i: (0, 0))],
        out_specs=[blk, pl.BlockSpec((1, lb, rpb * LANES), cur)],
        out_shape=[jax.ShapeDtypeStruct((batch, L, d * D_ATT), BF16),
                   jax.ShapeDtypeStruct((batch, L, d * LANES), F32)],
        compiler_params=pltpu.CompilerParams(
            dimension_semantics=("arbitrary", "arbitrary", "arbitrary"),
            vmem_limit_bytes=VMEM_LIMIT),
        name=f"attn_d{d}",
    )(view(q), view(k), view(k), view(k), view(v), view(v), view(v), bias, hmask)
    return o.reshape(batch * L, d * D_ATT), lse.reshape(batch * L, d * LANES)


def _conv_kernel(uc_ref, up_ref, un_ref, w_ref, b_ref, lg_ref, lb_ref, *refs, ts, rows, ln_rows,
                 n_cast):
    cast_in, c_ref = refs[:n_cast], refs[n_cast]
    cast_out, (ext, ybuf) = refs[n_cast + 1:2 * n_cast + 1], refs[2 * n_cast + 1:]
    for src, dst in zip(cast_in, cast_out):
        dst[...] = src[...].astype(BF16)
    i = pl.program_id(1)
    n = pl.num_programs(1)
    n_slab = D_CONV // LANES
    for g in range(n_slab):
        ls = slice(g * LANES, (g + 1) * LANES)
        ext[g, 0:CONV_HALO] = jnp.where(i > 0, up_ref[0, :, ls].astype(F32), 0.0)
        ext[g, CONV_HALO:CONV_HALO + ts] = uc_ref[0, :, ls].astype(F32)
        ext[g, CONV_HALO + ts:] = jnp.where(i < n - 1, un_ref[0, :, ls].astype(F32), 0.0)
    lg = lg_ref[...]
    lb = lb_ref[...]
    off = CONV_HALO - CONV_PAD
    n_blk = ts // rows

    def taps(idx, carry):
        g = idx // n_blk
        base = pl.multiple_of((idx % n_blk) * rows, rows)
        acc = jnp.broadcast_to(b_ref[g], (rows, LANES))
        for j in range(SUBLANES):
            steps = [m for m in range((CONV_WIDTH + off) // SUBLANES + 1)
                     if 0 <= SUBLANES * m + j - off < CONV_WIDTH]
            win = ext[g, pl.ds(base + j, rows + SUBLANES * max(steps)), :]
            for m in steps:
                k = SUBLANES * m + j - off
                acc = acc + win[SUBLANES * m:SUBLANES * m + rows] * w_ref[g, k:k + 1, :]
        ybuf[g, pl.ds(base, rows), :] = acc
        return carry

    lax.fori_loop(0, n_slab * n_blk, taps, 0, unroll=4)
    for r0 in range(0, ts, ln_rows):
        acc = jnp.concatenate([ybuf[g, r0:r0 + ln_rows, :] for g in range(n_slab)], axis=1)
        mu = jnp.mean(acc, axis=-1, keepdims=True)
        cen = acc - mu
        var = jnp.mean(cen * cen, axis=-1, keepdims=True)
        y = cen * lax.rsqrt(var + EPS) * lg + lb
        c_ref[0, r0:r0 + ln_rows, :] = (y * (1.0 / (1.0 + jnp.exp(-y)))).astype(c_ref.dtype)


def _conv(u3, conv_w, conv_b, ln_g, ln_b, weights, ts=512, rows=64, ln_rows=32):
    B, S, C = u3.shape
    n_i = S // ts
    sl = lambda b, i: (b * n_i + i, 0)
    w_rows = [w.shape[0] // (B * n_i) for w in weights]
    w_specs = [pl.BlockSpec((r, w.shape[1]), sl) for r, w in zip(w_rows, weights)]
    hb = ts // CONV_HALO
    cur = lambda b, i: (b, i, 0)
    prev = lambda b, i: (b, jnp.maximum(i * hb - 1, 0), 0)
    nxt = lambda b, i: (b, jnp.minimum((i + 1) * hb, S // CONV_HALO - 1), 0)
    const = lambda b, i: (0, 0)
    const3 = lambda b, i: (0, 0, 0)
    n_slab = C // LANES
    w_slab = jnp.transpose(conv_w.reshape(CONV_WIDTH, n_slab, LANES), (1, 0, 2))
    b_slab = conv_b.reshape(n_slab, 1, LANES)
    outs = pl.pallas_call(
        functools.partial(_conv_kernel, ts=ts, rows=rows, ln_rows=ln_rows, n_cast=len(weights)),
        grid=(B, S // ts),
        in_specs=[pl.BlockSpec((1, ts, C), cur),
                  pl.BlockSpec((1, CONV_HALO, C), prev),
                  pl.BlockSpec((1, CONV_HALO, C), nxt),
                  pl.BlockSpec((n_slab, CONV_WIDTH, LANES), const3),
                  pl.BlockSpec((n_slab, 1, LANES), const3),
                  pl.BlockSpec((1, C), const),
                  pl.BlockSpec((1, C), const)] + w_specs,
        out_specs=[pl.BlockSpec((1, ts, C), cur)] + w_specs,
        out_shape=[jax.ShapeDtypeStruct((B, S, C), BF16)]
                  + [jax.ShapeDtypeStruct(w.shape, BF16) for w in weights],
        scratch_shapes=[pltpu.VMEM((C // LANES, ts + 2 * CONV_HALO, LANES), F32),
                        pltpu.VMEM((C // LANES, ts, LANES), F32)],
        compiler_params=pltpu.CompilerParams(dimension_semantics=("arbitrary", "arbitrary"),
                                             vmem_limit_bytes=VMEM_LIMIT),
        name="conv",
    )(u3, u3, u3, w_slab, b_slab, ln_g, ln_b, *weights)
    return outs[0], outs[1:]


def _mem_kv_kernel(mem_ref, g_ref, wk_ref, wv_ref, k_ref, v_ref):
    mn = _rms(mem_ref[...], g_ref[...]).astype(BF16)
    k_ref[...] = _dot(mn, wk_ref[...].astype(BF16)).astype(BF16)
    v_ref[...] = _dot(mn, wv_ref[...].astype(BF16)).astype(BF16)


def _mem_kv(mem2, g, wk, wv):
    R = mem2.shape[0]
    full = lambda shape: pl.BlockSpec(shape, lambda i: (0, 0))
    return pl.pallas_call(
        _mem_kv_kernel,
        grid=(1,),
        in_specs=[full((R, D_MODEL)), full((1, D_MODEL)),
                  full((D_MODEL, D_MODEL)), full((D_MODEL, D_MODEL))],
        out_specs=[full((R, D_MODEL))] * 2,
        out_shape=[jax.ShapeDtypeStruct((R, D_MODEL), BF16)] * 2,
        compiler_params=pltpu.CompilerParams(dimension_semantics=("arbitrary",),
                                             vmem_limit_bytes=VMEM_LIMIT),
        name="mem_kv",
    )(mem2, g, wk, wv)


def _post_kernel(x_ref, o1_ref, o2_ref, o3_ref, l1_ref, l2_ref, l3_ref, c_ref, e_ref,
                 wo_ref, gx_ref, wq_ref, xk_ref, xv_ref, wxo_ref, gm_ref, wu_ref, wd_ref, gf_ref,
                 out_ref, *bufs, tm, sub, chunk, final_norm):
    n_slab = D_ATT // LANES
    n_pat = len(DILATED_PATTERNS)
    o_refs = (o1_ref, o2_ref, o3_ref)
    l_refs = (l1_ref, l2_ref, l3_ref)
    n_grp = tm // sub

    def mix(g):
        t0 = g * sub
        obuf, lbuf = bufs[2 * g], bufs[2 * g + 1]
        otmp, ltmp = bufs[2 * n_grp + 2 * g], bufs[2 * n_grp + 2 * g + 1]
        for p, (_, d) in enumerate(DILATED_PATTERNS):
            if d == 1:
                continue
            src = slice(t0 // d, (t0 + sub) // d)
            jobs = [(o_refs[p], D_ATT, j, obuf, otmp, True) for j in range(n_slab)]
            jobs.append((l_refs[p], LANES, 0, lbuf, ltmp, False))
            for ref, pitch, j, dst, tmp, widen in jobs:
                pieces = {}
                for r in range(d):
                    t = ref[src, r * pitch + j * LANES:r * pitch + (j + 1) * LANES]
                    pieces[r] = t.astype(F32) if widen else t
                dd = d
                while dd > DIL_STEP:
                    lower = dd // DIL_STEP
                    merged = {}
                    for r_low in range(lower):
                        for t in range(DIL_STEP):
                            tmp[r_low, j, pl.ds(t, sub // dd, stride=DIL_STEP), :] = (
                                pieces[t * lower + r_low])
                        merged[r_low] = tmp[r_low, j, 0:sub // lower, :]
                    pieces, dd = merged, lower
                for r in range(dd):
                    dst[p, j, pl.ds(r, sub // dd, stride=DIL_STEP), :] = pieces[r]

        ls = [l_refs[p][t0:t0 + sub, :] if d == 1 else lbuf[p, 0]
              for p, (_, d) in enumerate(DILATED_PATTERNS)]
        m = jnp.maximum(jnp.maximum(ls[0], ls[1]), ls[2])
        es = [jnp.exp(l - m) for l in ls]
        inv = 1.0 / (es[0] + es[1] + es[2])
        ws = [_dot((e * inv).astype(BF16), e_ref[...]) for e in es]
        att = []
        for j in range(n_slab):
            lanes = slice(j * LANES, (j + 1) * LANES)
            os_ = [(o_refs[p][t0:t0 + sub, lanes] if d == 1 else obuf[p, j]).astype(F32)
                   for p, (_, d) in enumerate(DILATED_PATTERNS)]
            att.append((ws[0][:, lanes] * os_[0] + ws[1][:, lanes] * os_[1]
                        + ws[2][:, lanes] * os_[2]).astype(BF16))
        return jnp.concatenate(att, axis=1)

    def project(g, att):
        tr = slice(g * sub, (g + 1) * sub)
        h1 = (x_ref[tr, :] + _dot(att, wo_ref[0:D_ATT, :])
              + _dot(c_ref[tr, :], wo_ref[D_ATT:, :]))
        xq = (_dot(_rms(h1, gx_ref[...]).astype(BF16), wq_ref[...])
              * (XATT_HEAD_DIM ** -0.5)).astype(BF16)
        return h1, xq

    def cross(g, h1, xq):
        heads = []
        for h in range(XATT_HEADS):
            sl = slice(h * XATT_HEAD_DIM, (h + 1) * XATT_HEAD_DIM)
            s = _dot_nt(xq[:, sl], xk_ref[0, :, sl])
            mx = jnp.max(s, axis=-1, keepdims=True)
            p = jnp.exp(s - mx)
            den = jnp.sum(p, axis=-1, keepdims=True)
            heads.append((_dot(p.astype(BF16), xv_ref[0, :, sl]) * (1.0 / den)).astype(BF16))
        xo = jnp.concatenate(heads, axis=1)
        return h1 + _dot(xo, wxo_ref[...])

    def mlp(g, h):
        hn = _rms(h, gm_ref[...]).astype(BF16)
        acc = h
        for j in range(D_FF // chunk):
            u = jnp.maximum(_dot(hn, wu_ref[:, j * chunk:(j + 1) * chunk]), 0.0)
            acc = acc + _dot((u * u).astype(BF16), wd_ref[j * chunk:(j + 1) * chunk, :])
        out_ref[g * sub:(g + 1) * sub, :] = _rms(acc, gf_ref[...]) if final_norm else acc

    for g in range(n_grp):
        h1, xq = project(g, mix(g))
        mlp(g, cross(g, h1, xq))


def _post(x2, os_, lses, c2, w_out, gx, w_xq, xk, xv, w_xo, gm, w_up, w_down, gf, final_norm,
          seq, tm=512, sub=512, chunk=1024):
    T = x2.shape[0]
    n_mem = xk.shape[1]
    per_b = seq // tm
    n_slab = D_ATT // LANES
    d_max = DILATED_PATTERNS[-1][1]
    row = lambda i: (i, 0)
    const = lambda i: (0, 0)
    memb = lambda i: (i // per_b, 0, 0)
    whole = lambda w: pl.BlockSpec(w.shape, const, pipeline_mode=pl.Buffered(1))
    vec = pl.BlockSpec((1, D_MODEL), const)
    n_pat = len(DILATED_PATTERNS)
    return pl.pallas_call(
        functools.partial(_post_kernel, tm=tm, sub=sub, chunk=chunk, final_norm=final_norm),
        grid=(T // tm,),
        in_specs=[pl.BlockSpec((tm, D_MODEL), row)]
                 + [pl.BlockSpec((tm // d, d * D_ATT), row) for _, d in DILATED_PATTERNS]
                 + [pl.BlockSpec((tm // d, d * LANES), row) for _, d in DILATED_PATTERNS]
                 + [pl.BlockSpec((tm, D_CONV), row),
                    pl.BlockSpec((LANES, D_ATT), const),
                    whole(w_out), vec, whole(w_xq),
                    pl.BlockSpec((1, n_mem, D_MODEL), memb),
                    pl.BlockSpec((1, n_mem, D_MODEL), memb),
                    whole(w_xo), vec, whole(w_up), whole(w_down), vec],
        out_specs=pl.BlockSpec((tm, D_MODEL), row),
        out_shape=jax.ShapeDtypeStruct((T, D_MODEL), F32),
        scratch_shapes=[pltpu.VMEM((n_pat, n_slab, sub, LANES), F32),
                          pltpu.VMEM((n_pat, 1, sub, LANES), F32)] * (tm // sub)
                       + [pltpu.VMEM((d_max // DIL_STEP, n_slab, sub // DIL_STEP, LANES), F32),
                          pltpu.VMEM((d_max // DIL_STEP, 1, sub // DIL_STEP, LANES), F32)
                          ] * (tm // sub),
        compiler_params=pltpu.CompilerParams(dimension_semantics=("arbitrary",),
                                             vmem_limit_bytes=VMEM_LIMIT),
        name="post",
    )(x2, *os_, *lses, c2, _head_expand(), w_out, gx, w_xq, xk, xv, w_xo, gm, w_up, w_down, gf)


def _head_expand():
    e = np.zeros((LANES, D_ATT), np.float32)
    for h in range(ATT_HEADS):
        e[h, h * HEAD_DIM:(h + 1) * HEAD_DIM] = 1.0
    return jnp.asarray(e, dtype=BF16)


def _rotary_tables(seq):
    half = ROT_DIM // 2
    freqs = ROPE_THETA ** (-np.arange(0, ROT_DIM, 2, dtype=np.float64) / ROT_DIM)
    ang = np.arange(seq, dtype=np.float64)[:, None] * freqs[None, :]
    old = _qk_lane_order() % HEAD_DIM
    rot = old < ROT_DIM
    cos = np.where(rot[None, :], np.cos(ang)[:, old % half], 1.0)
    sign = np.where(old < half, -1.0, 1.0) * rot
    sin = np.sin(ang)[:, old % half] * sign[None, :]
    return jnp.asarray(cos, dtype=F32), jnp.asarray(sin, dtype=F32)


def kernel(x, mem, norm_mix_g, w_in, conv_w, conv_b, conv_ln_g, conv_ln_b, w_out, norm_x_g,
           norm_mem_g, w_xq, w_xk, w_xv, w_xo, norm_mlp_g, w_up, w_down, norm_final_g):
    B, S, D = x.shape
    n_mem = mem.shape[1]
    depth = w_in.shape[0]
    T = B * S
    cos_t, sin_t = _rotary_tables(S)
    bias = _attn_bias()
    hmask = _head_masks()
    row = lambda g: g.reshape(1, -1)

    h = x.reshape(T, D)
    for l in range(depth):
        q, k, v, u = _in_proj(h, row(norm_mix_g[l]), w_in[l], cos_t, sin_t, S)
        os_, lses = [], []
        for p, (_, d) in enumerate(DILATED_PATTERNS):
            o, lse = _attn(q[p], k[p], v[p], bias, hmask, B, S, d)
            os_.append(o)
            lses.append(lse)
        c, (w_out_b, w_xq_b, w_xo_b, w_up_b, w_down_b) = _conv(
            u.reshape(B, S, D_CONV), conv_w[l], row(conv_b[l]), row(conv_ln_g[l]),
            row(conv_ln_b[l]), (w_out[l], w_xq[l], w_xo[l], w_up[l], w_down[l]))
        xk, xv = _mem_kv(mem.reshape(B * n_mem, D), row(norm_mem_g[l]), w_xk[l], w_xv[l])
        h = _post(h, os_, lses, c.reshape(T, D_CONV), w_out_b, row(norm_x_g[l]), w_xq_b,
                  xk.reshape(B, n_mem, D), xv.reshape(B, n_mem, D), w_xo_b, row(norm_mlp_g[l]),
                  w_up_b, w_down_b, row(norm_final_g), final_norm=(l == depth - 1), seq=S)
    return h.reshape(B, S, D)
```

```python
import functools
import math

import numpy as np
import jax
import jax.numpy as jnp
from jax import lax
from jax.experimental import pallas as pl
from jax.experimental.pallas import tpu as pltpu

F32 = jnp.float32
BF16 = jnp.bfloat16

D_MODEL = 1024
ATT_HEADS = 8
HEAD_DIM = 64
D_ATT = ATT_HEADS * HEAD_DIM
D_CONV = D_MODEL - D_ATT
DILATED_PATTERNS = ((128, 1), (512, 4), (2048, 16))
DIL_STEP = 4
assert all(d == DIL_STEP ** k for k, (_, d) in enumerate(DILATED_PATTERNS))
ROPE_THETA = 500000.0
ROT_DIM = HEAD_DIM // 4
CONV_WIDTH = 31
CONV_PAD = (CONV_WIDTH - 1) // 2
XATT_HEADS = 4
XATT_HEAD_DIM = D_MODEL // XATT_HEADS
D_FF = 4 * D_MODEL
EPS = 1e-6
NEG_INF = -1e30
LN2 = math.log(2.0)
Q_SCALE = HEAD_DIM ** -0.5 / LN2

LANES = 128
SUBLANES = 8
ATT_HALF = 64
ATT_QB = 2 * ATT_HALF
ATT_WIN = ATT_QB + 2 * ATT_HALF
CONV_HALO = 16
GLU_COLS = 256
VMEM_LIMIT = 56 * 1024 * 1024


def _dot(a, b):
    return jnp.dot(a, b, preferred_element_type=F32)


def _dot_nt(a, b):
    return lax.dot_general(a, b, (((1,), (1,)), ((), ())), preferred_element_type=F32)


def _dot_tn(a, b):
    return lax.dot_general(a, b, (((0,), (0,)), ((), ())), preferred_element_type=F32)


def _rms(x, g):
    var = jnp.mean(x * x, axis=-1, keepdims=True)
    return x * lax.rsqrt(var + EPS) * g


def _cast_once(pairs, chunk=512):
    @pl.when(pl.program_id(0) == 0)
    def _():
        for src, dst in pairs:
            for c in range(0, src.shape[1], chunk):
                dst[:, c:c + chunk] = src[:, c:c + chunk].astype(BF16)


def _in_proj_kernel(x_ref, g_ref, wf_ref, cos_ref, sa_ref, sb_ref, *refs, tm):
    n_pat = len(DILATED_PATTERNS)
    qkv_refs = [refs[a * n_pat:(a + 1) * n_pat] for a in range(3)]
    u_ref = refs[3 * n_pat]
    w_ref = refs[3 * n_pat + 1]
    stage = refs[3 * n_pat + 2:]
    _cast_once([(wf_ref, w_ref)])
    xn = _rms(x_ref[...], g_ref[...]).astype(BF16)
    cos, sa, sb = cos_ref[...], sa_ref[...], sb_ref[...]
    n_slab = D_ATT // LANES

    def plain(y, j):
        return y[:, LANES * j:LANES * (j + 1)]

    def rot(y, j):
        yj = plain(y, j)
        return (yj * cos + pltpu.roll(yj, LANES - ROT_DIM // 2, 1) * sa
                + pltpu.roll(yj, ROT_DIM // 2, 1) * sb)

    def emit(y, outs, transform):
        for j in range(n_slab):
            piece = transform(y, j)
            stage[0][0, j] = piece
            outs[0][:, j * LANES:(j + 1) * LANES] = piece.astype(BF16)
        for k in range(1, n_pat):
            d_prev, d = DILATED_PATTERNS[k - 1][1], DILATED_PATTERNS[k][1]
            for r_prev in range(d_prev):
                for t in range(DIL_STEP):
                    r = t * d_prev + r_prev
                    for j in range(n_slab):
                        piece = stage[k - 1][r_prev, j, pl.ds(t, tm // d, stride=DIL_STEP), :]
                        if k + 1 < n_pat:
                            stage[k][r, j] = piece
                        c0 = r * D_ATT + j * LANES
                        outs[k][:, c0:c0 + LANES] = piece.astype(BF16)

    emit(_dot(xn, w_ref[:, 0:D_ATT]), qkv_refs[0], lambda y, j: rot(y, j) * Q_SCALE)
    emit(_dot(xn, w_ref[:, D_ATT:2 * D_ATT]), qkv_refs[1], rot)
    emit(_dot(xn, w_ref[:, 2 * D_ATT:3 * D_ATT]), qkv_refs[2], plain)
    c0 = 3 * D_ATT
    for cc in range(0, D_CONV, GLU_COLS):
        a = _dot(xn, w_ref[:, c0 + cc:c0 + cc + GLU_COLS])
        gt = _dot(xn, w_ref[:, c0 + D_CONV + cc:c0 + D_CONV + cc + GLU_COLS])
        u_ref[:, cc:cc + GLU_COLS] = (a * (1.0 / (1.0 + jnp.exp(-gt)))).astype(u_ref.dtype)


def _in_proj(x2, g, w_in, cos_t, sa_t, sb_t, seq, tm=1024):
    T = x2.shape[0]
    d_in = w_in.shape[1]
    n_s = seq // tm
    n_pat = len(DILATED_PATTERNS)
    row = lambda i: (i, 0)
    tab = lambda i: (i % n_s, 0)
    const = lambda i: (0, 0)
    qkv_specs, qkv_shapes = [], []
    for _ in range(3):
        for _, d in DILATED_PATTERNS:
            qkv_specs.append(pl.BlockSpec((tm // d, d * D_ATT), row))
            qkv_shapes.append(jax.ShapeDtypeStruct((T // d, d * D_ATT), BF16))
    outs = pl.pallas_call(
        functools.partial(_in_proj_kernel, tm=tm),
        grid=(T // tm,),
        in_specs=[pl.BlockSpec((tm, D_MODEL), row),
                  pl.BlockSpec((1, D_MODEL), const),
                  pl.BlockSpec((D_MODEL, d_in), const, pipeline_mode=pl.Buffered(1)),
                  pl.BlockSpec((tm, LANES), tab),
                  pl.BlockSpec((tm, LANES), tab),
                  pl.BlockSpec((tm, LANES), tab)],
        out_specs=qkv_specs + [pl.BlockSpec((tm, D_CONV), row)],
        out_shape=qkv_shapes + [jax.ShapeDtypeStruct((T, D_CONV), BF16)],
        scratch_shapes=[pltpu.VMEM((D_MODEL, d_in), BF16)]
                       + [pltpu.VMEM((d, D_ATT // LANES, tm // d, LANES), F32)
                          for _, d in DILATED_PATTERNS[:-1]],
        compiler_params=pltpu.CompilerParams(dimension_semantics=("arbitrary",),
                                             vmem_limit_bytes=VMEM_LIMIT),
        name="in_proj",
    )(x2, g, w_in, cos_t, sa_t, sb_t)
    return outs[0:n_pat], outs[n_pat:2 * n_pat], outs[2 * n_pat:3 * n_pat], outs[3 * n_pat]


def _attn_kernel(q_ref, kc_ref, kp_ref, kn_ref, vc_ref, vp_ref, vn_ref, bias_ref, hm_ref,
                 o_ref, lse_ref, *, lb, nblk_total):
    i = pl.program_id(2)
    mask_a = hm_ref[0:1, :]
    mask_b = hm_ref[1:2, :]
    nblk = lb // ATT_QB

    def window(cur, before, after, start, lanes):
        if start < 0:
            return jnp.concatenate([before[0, ATT_HALF + start:, lanes],
                                    cur[0, 0:ATT_QB + start, lanes]], axis=0)
        if start + ATT_QB > lb:
            return jnp.concatenate([cur[0, start:lb, lanes],
                                    after[0, 0:start + ATT_QB - lb, lanes]], axis=0)
        return cur[0, start:start + ATT_QB, lanes]

    for n in range(nblk):
        r0 = n * ATT_QB
        gblk = i * nblk + n
        bidx = jnp.where(gblk == 0, 0, jnp.where(gblk == nblk_total - 1, 2, 1))
        bias = bias_ref[bidx]
        lse_rows = []
        for hp in range(q_ref.shape[2] // LANES):
            c0 = LANES * hp
            lanes = slice(c0, c0 + LANES)
            q2 = q_ref[0, r0:r0 + ATT_QB, lanes]
            qs = jnp.concatenate([q2 * mask_a, q2 * mask_b], axis=0)
            parts = []
            for kh in range(ATT_WIN // ATT_QB):
                k0 = r0 - ATT_HALF + kh * ATT_QB
                kw = window(kc_ref, kp_ref, kn_ref, k0, lanes)
                vw = window(vc_ref, vp_ref, vn_ref, k0, lanes)
                s = _dot_nt(kw, qs) + bias[kh * ATT_QB:(kh + 1) * ATT_QB]
                m = jnp.max(s, axis=0, keepdims=True)
                p = jnp.exp2(s - m)
                l = jnp.sum(p, axis=0, keepdims=True)
                parts.append((m, l, _dot_tn(vw, p.astype(BF16))))
            (m1, l1, o1), (m2, l2, o2) = parts
            m = jnp.maximum(m1, m2)
            a1 = jnp.exp2(m1 - m)
            a2 = jnp.exp2(m2 - m)
            l = a1 * l1 + a2 * l2
            inv = 1.0 / l
            ot = o1 * (a1 * inv) + o2 * (a2 * inv)
            own = jnp.concatenate([ot[0:HEAD_DIM, 0:ATT_QB], ot[HEAD_DIM:, ATT_QB:]], axis=0)
            o_ref[0, r0:r0 + ATT_QB, lanes] = jnp.transpose(own.astype(o_ref.dtype))
            lse = (m + jnp.log2(l)) * LN2
            lse_rows += [lse[:, 0:ATT_QB], lse[:, ATT_QB:]]
        for rc in range(len(lse_rows) // ATT_HEADS):
            rows = lse_rows[rc * ATT_HEADS:(rc + 1) * ATT_HEADS]
            pad = jnp.zeros((LANES - ATT_HEADS, ATT_QB), F32)
            lse_ref[0, r0:r0 + ATT_QB, rc * LANES:(rc + 1) * LANES] = jnp.transpose(
                jnp.concatenate(rows + [pad], axis=0))


def _attn_bias():
    i = np.arange(ATT_QB)[:, None]
    j = np.arange(ATT_WIN)[None, :]
    band = (j >= i) & (j <= i + 2 * ATT_HALF)
    first = band & (j >= ATT_HALF)
    last = band & (j < ATT_HALF + ATT_QB)
    tabs = np.stack([first, band, last]).astype(np.float32)
    tabs = np.concatenate([tabs.transpose(0, 2, 1)] * 2, axis=2)
    return jnp.asarray((1.0 - tabs) * NEG_INF, dtype=F32)


def _head_masks():
    lane = np.arange(LANES)
    m = np.zeros((16, LANES), np.float32)
    m[0] = lane < HEAD_DIM
    m[1] = lane >= HEAD_DIM
    return jnp.asarray(m, dtype=BF16)


def _attn(q, k, v, bias, hmask, batch, seq, d, blocks_per_step=32, lb_max=4096):
    L = seq // d
    lb = min(lb_max, L)
    rpb = min(d, max(1, blocks_per_step * ATT_QB // lb))
    width = rpb * D_ATT
    nblk_total = L // ATT_QB
    assert nblk_total >= 2 and L % lb == 0 and lb % ATT_QB == 0 and d % rpb == 0
    view = lambda t: t.reshape(batch, L, d * D_ATT)
    hb = lb // ATT_HALF
    cur = lambda b, r, i: (b, i, r)
    prev = lambda b, r, i: (b, jnp.maximum(i * hb - 1, 0), r)
    nxt = lambda b, r, i: (b, jnp.minimum((i + 1) * hb, L // ATT_HALF - 1), r)
    blk = pl.BlockSpec((1, lb, width), cur)
    halo_p = pl.BlockSpec((1, ATT_HALF, width), prev)
    halo_n = pl.BlockSpec((1, ATT_HALF, width), nxt)
    o, lse = pl.pallas_call(
        functools.partial(_attn_kernel, lb=lb, nblk_total=nblk_total),
        grid=(batch, d // rpb, L // lb),
        in_specs=[blk, blk, halo_p, halo_n, blk, halo_p, halo_n,
                  pl.BlockSpec((3, ATT_WIN, 2 * ATT_QB), lambda b, r, i: (0, 0, 0)),
                  pl.BlockSpec((16, LANES), lambda b, r, i: (0, 0))],
        out_specs=[blk, pl.BlockSpec((1, lb, rpb * LANES), cur)],
        out_shape=[jax.ShapeDtypeStruct((batch, L, d * D_ATT), BF16),
                   jax.ShapeDtypeStruct((batch, L, d * LANES), F32)],
        compiler_params=pltpu.CompilerParams(
            dimension_semantics=("arbitrary", "arbitrary", "arbitrary"),
            vmem_limit_bytes=VMEM_LIMIT),
        name=f"attn_d{d}",
    )(view(q), view(k), view(k), view(k), view(v), view(v), view(v), bias, hmask)
    return o.reshape(batch * L, d * D_ATT), lse.reshape(batch * L, d * LANES)


def _conv_kernel(uc_ref, up_ref, un_ref, w_ref, b_ref, lg_ref, lb_ref, *refs, ts, rows, ln_rows,
                 n_cast):
    cast_in, c_ref = refs[:n_cast], refs[n_cast]
    cast_out, (ext, ybuf) = refs[n_cast + 1:2 * n_cast + 1], refs[2 * n_cast + 1:]
    for src, dst in zip(cast_in, cast_out):
        dst[...] = src[...].astype(BF16)
    i = pl.program_id(1)
    n = pl.num_programs(1)
    n_slab = D_CONV // LANES
    for g in range(n_slab):
        ls = slice(g * LANES, (g + 1) * LANES)
        ext[g, 0:CONV_HALO] = jnp.where(i > 0, up_ref[0, :, ls].astype(F32), 0.0)
        ext[g, CONV_HALO:CONV_HALO + ts] = uc_ref[0, :, ls].astype(F32)
        ext[g, CONV_HALO + ts:] = jnp.where(i < n - 1, un_ref[0, :, ls].astype(F32), 0.0)
    lg = lg_ref[...]
    lb = lb_ref[...]
    off = CONV_HALO - CONV_PAD
    n_blk = ts // rows

    def taps(idx, carry):
        g = idx // n_blk
        base = pl.multiple_of((idx % n_blk) * rows, rows)
        acc = jnp.broadcast_to(b_ref[g], (rows, LANES))
        for j in range(SUBLANES):
            steps = [m for m in range((CONV_WIDTH + off) // SUBLANES + 1)
                     if 0 <= SUBLANES * m + j - off < CONV_WIDTH]
            win = ext[g, pl.ds(base + j, rows + SUBLANES * max(steps)), :]
            for m in steps:
                k = SUBLANES * m + j - off
                acc = acc + win[SUBLANES * m:SUBLANES * m + rows] * w_ref[g, k:k + 1, :]
        ybuf[g, pl.ds(base, rows), :] = acc
        return carry

    lax.fori_loop(0, n_slab * n_blk, taps, 0, unroll=4)
    for r0 in range(0, ts, ln_rows):
        acc = jnp.concatenate([ybuf[g, r0:r0 + ln_rows, :] for g in range(n_slab)], axis=1)
        mu = jnp.mean(acc, axis=-1, keepdims=True)
        cen = acc - mu
        var = jnp.mean(cen * cen, axis=-1, keepdims=True)
        y = cen * lax.rsqrt(var + EPS) * lg + lb
        c_ref[0, r0:r0 + ln_rows, :] = (y * (1.0 / (1.0 + jnp.exp(-y)))).astype(c_ref.dtype)


def _conv(u3, conv_w, conv_b, ln_g, ln_b, weights, ts=512, rows=64, ln_rows=32):
    B, S, C = u3.shape
    n_i = S // ts
    sl = lambda b, i: (b * n_i + i, 0)
    w_rows = [w.shape[0] // (B * n_i) for w in weights]
    w_specs = [pl.BlockSpec((r, w.shape[1]), sl) for r, w in zip(w_rows, weights)]
    hb = ts // CONV_HALO
    cur = lambda b, i: (b, i, 0)
    prev = lambda b, i: (b, jnp.maximum(i * hb - 1, 0), 0)
    nxt = lambda b, i: (b, jnp.minimum((i + 1) * hb, S // CONV_HALO - 1), 0)
    const = lambda b, i: (0, 0)
    const3 = lambda b, i: (0, 0, 0)
    n_slab = C // LANES
    w_slab = jnp.transpose(conv_w.reshape(CONV_WIDTH, n_slab, LANES), (1, 0, 2))
    b_slab = conv_b.reshape(n_slab, 1, LANES)
    outs = pl.pallas_call(
        functools.partial(_conv_kernel, ts=ts, rows=rows, ln_rows=ln_rows, n_cast=len(weights)),
        grid=(B, S // ts),
        in_specs=[pl.BlockSpec((1, ts, C), cur),
                  pl.BlockSpec((1, CONV_HALO, C), prev),
                  pl.BlockSpec((1, CONV_HALO, C), nxt),
                  pl.BlockSpec((n_slab, CONV_WIDTH, LANES), const3),
                  pl.BlockSpec((n_slab, 1, LANES), const3),
                  pl.BlockSpec((1, C), const),
                  pl.BlockSpec((1, C), const)] + w_specs,
        out_specs=[pl.BlockSpec((1, ts, C), cur)] + w_specs,
        out_shape=[jax.ShapeDtypeStruct((B, S, C), BF16)]
                  + [jax.ShapeDtypeStruct(w.shape, BF16) for w in weights],
        scratch_shapes=[pltpu.VMEM((C // LANES, ts + 2 * CONV_HALO, LANES), F32),
                        pltpu.VMEM((C // LANES, ts, LANES), F32)],
        compiler_params=pltpu.CompilerParams(dimension_semantics=("arbitrary", "arbitrary"),
                                             vmem_limit_bytes=VMEM_LIMIT),
        name="conv",
    )(u3, u3, u3, w_slab, b_slab, ln_g, ln_b, *weights)
    return outs[0], outs[1:]


def _mem_kv_kernel(mem_ref, g_ref, wk_ref, wv_ref, k_ref, v_ref):
    mn = _rms(mem_ref[...], g_ref[...]).astype(BF16)
    k_ref[...] = _dot(mn, wk_ref[...].astype(BF16)).astype(BF16)
    v_ref[...] = _dot(mn, wv_ref[...].astype(BF16)).astype(BF16)


def _mem_kv(mem2, g, wk, wv):
    R = mem2.shape[0]
    full = lambda shape: pl.BlockSpec(shape, lambda i: (0, 0))
    return pl.pallas_call(
        _mem_kv_kernel,
        grid=(1,),
        in_specs=[full((R, D_MODEL)), full((1, D_MODEL)),
                  full((D_MODEL, D_MODEL)), full((D_MODEL, D_MODEL))],
        out_specs=[full((R, D_MODEL))] * 2,
        out_shape=[jax.ShapeDtypeStruct((R, D_MODEL), BF16)] * 2,
        compiler_params=pltpu.CompilerParams(dimension_semantics=("arbitrary",),
                                             vmem_limit_bytes=VMEM_LIMIT),
        name="mem_kv",
    )(mem2, g, wk, wv)


def _post_kernel(x_ref, o1_ref, o2_ref, o3_ref, l1_ref, l2_ref, l3_ref, c_ref, e_ref,
                 wo_ref, gx_ref, wq_ref, xk_ref, xv_ref, wxo_ref, gm_ref, wu_ref, wd_ref, gf_ref,
                 out_ref, *bufs, tm, sub, chunk, final_norm):
    n_slab = D_ATT // LANES
    n_pat = len(DILATED_PATTERNS)
    o_refs = (o1_ref, o2_ref, o3_ref)
    l_refs = (l1_ref, l2_ref, l3_ref)
    n_grp = tm // sub

    def mix(g):
        t0 = g * sub
        obuf, lbuf = bufs[2 * g], bufs[2 * g + 1]
        otmp, ltmp = bufs[2 * n_grp + 2 * g], bufs[2 * n_grp + 2 * g + 1]
        for p, (_, d) in enumerate(DILATED_PATTERNS):
            if d == 1:
                continue
            src = slice(t0 // d, (t0 + sub) // d)
            jobs = [(o_refs[p], D_ATT, j, obuf, otmp, True) for j in range(n_slab)]
            jobs.append((l_refs[p], LANES, 0, lbuf, ltmp, False))
            for ref, pitch, j, dst, tmp, widen in jobs:
                pieces = {}
                for r in range(d):
                    t = ref[src, r * pitch + j * LANES:r * pitch + (j + 1) * LANES]
                    pieces[r] = t.astype(F32) if widen else t
                dd = d
                while dd > DIL_STEP:
                    lower = dd // DIL_STEP
                    merged = {}
                    for r_low in range(lower):
                        for t in range(DIL_STEP):
                            tmp[r_low, j, pl.ds(t, sub // dd, stride=DIL_STEP), :] = (
                                pieces[t * lower + r_low])
                        merged[r_low] = tmp[r_low, j, 0:sub // lower, :]
                    pieces, dd = merged, lower
                for r in range(dd):
                    dst[p, j, pl.ds(r, sub // dd, stride=DIL_STEP), :] = pieces[r]

        ls = [l_refs[p][t0:t0 + sub, :] if d == 1 else lbuf[p, 0]
              for p, (_, d) in enumerate(DILATED_PATTERNS)]
        m = jnp.maximum(jnp.maximum(ls[0], ls[1]), ls[2])
        es = [jnp.exp(l - m) for l in ls]
        inv = 1.0 / (es[0] + es[1] + es[2])
        ws = [_dot((e * inv).astype(BF16), e_ref[...]) for e in es]
        att = []
        for j in range(n_slab):
            lanes = slice(j * LANES, (j + 1) * LANES)
            os_ = [(o_refs[p][t0:t0 + sub, lanes] if d == 1 else obuf[p, j]).astype(F32)
                   for p, (_, d) in enumerate(DILATED_PATTERNS)]
            att.append((ws[0][:, lanes] * os_[0] + ws[1][:, lanes] * os_[1]
                        + ws[2][:, lanes] * os_[2]).astype(BF16))
        return jnp.concatenate(att, axis=1)

    def project(g, att):
        tr = slice(g * sub, (g + 1) * sub)
        h1 = (x_ref[tr, :] + _dot(att, wo_ref[0:D_ATT, :])
              + _dot(c_ref[tr, :], wo_ref[D_ATT:, :]))
        xq = (_dot(_rms(h1, gx_ref[...]).astype(BF16), wq_ref[...])
              * (XATT_HEAD_DIM ** -0.5)).astype(BF16)
        return h1, xq

    def cross(g, h1, xq):
        heads = []
        for h in range(XATT_HEADS):
            sl = slice(h * XATT_HEAD_DIM, (h + 1) * XATT_HEAD_DIM)
            s = _dot_nt(xq[:, sl], xk_ref[0, :, sl])
            mx = jnp.max(s, axis=-1, keepdims=True)
            p = jnp.exp(s - mx)
            den = jnp.sum(p, axis=-1, keepdims=True)
            heads.append((_dot(p.astype(BF16), xv_ref[0, :, sl]) * (1.0 / den)).astype(BF16))
        xo = jnp.concatenate(heads, axis=1)
        return h1 + _dot(xo, wxo_ref[...])

    def mlp(g, h):
        hn = _rms(h, gm_ref[...]).astype(BF16)
        acc = h
        for j in range(D_FF // chunk):
            u = jnp.maximum(_dot(hn, wu_ref[:, j * chunk:(j + 1) * chunk]), 0.0)
            acc = acc + _dot((u * u).astype(BF16), wd_ref[j * chunk:(j + 1) * chunk, :])
        out_ref[g * sub:(g + 1) * sub, :] = _rms(acc, gf_ref[...]) if final_norm else acc

    for g in range(n_grp):
        h1, xq = project(g, mix(g))
        mlp(g, cross(g, h1, xq))


def _post(x2, os_, lses, c2, w_out, gx, w_xq, xk, xv, w_xo, gm, w_up, w_down, gf, final_norm,
          seq, tm=512, sub=512, chunk=1024):
    T = x2.shape[0]
    n_mem = xk.shape[1]
    per_b = seq // tm
    n_slab = D_ATT // LANES
    d_max = DILATED_PATTERNS[-1][1]
    row = lambda i: (i, 0)
    const = lambda i: (0, 0)
    memb = lambda i: (i // per_b, 0, 0)
    whole = lambda w: pl.BlockSpec(w.shape, const, pipeline_mode=pl.Buffered(1))
    vec = pl.BlockSpec((1, D_MODEL), const)
    n_pat = len(DILATED_PATTERNS)
    return pl.pallas_call(
        functools.partial(_post_kernel, tm=tm, sub=sub, chunk=chunk, final_norm=final_norm),
        grid=(T // tm,),
        in_specs=[pl.BlockSpec((tm, D_MODEL), row)]
                 + [pl.BlockSpec((tm // d, d * D_ATT), row) for _, d in DILATED_PATTERNS]
                 + [pl.BlockSpec((tm // d, d * LANES), row) for _, d in DILATED_PATTERNS]
                 + [pl.BlockSpec((tm, D_CONV), row),
                    pl.BlockSpec((LANES, D_ATT), const),
                    whole(w_out), vec, whole(w_xq),
                    pl.BlockSpec((1, n_mem, D_MODEL), memb),
                    pl.BlockSpec((1, n_mem, D_MODEL), memb),
                    whole(w_xo), vec, whole(w_up), whole(w_down), vec],
        out_specs=pl.BlockSpec((tm, D_MODEL), row),
        out_shape=jax.ShapeDtypeStruct((T, D_MODEL), F32),
        scratch_shapes=[pltpu.VMEM((n_pat, n_slab, sub, LANES), F32),
                          pltpu.VMEM((n_pat, 1, sub, LANES), F32)] * (tm // sub)
                       + [pltpu.VMEM((d_max // DIL_STEP, n_slab, sub // DIL_STEP, LANES), F32),
                          pltpu.VMEM((d_max // DIL_STEP, 1, sub // DIL_STEP, LANES), F32)
                          ] * (tm // sub),
        compiler_params=pltpu.CompilerParams(dimension_semantics=("arbitrary",),
                                             vmem_limit_bytes=VMEM_LIMIT),
        name="post",
    )(x2, *os_, *lses, c2, _head_expand(), w_out, gx, w_xq, xk, xv, w_xo, gm, w_up, w_down, gf)


def _head_expand():
    e = np.zeros((LANES, D_ATT), np.float32)
    for h in range(ATT_HEADS):
        e[h, h * HEAD_DIM:(h + 1) * HEAD_DIM] = 1.0
    return jnp.asarray(e, dtype=BF16)


def _rotary_tables(seq):
    half = ROT_DIM // 2
    freqs = ROPE_THETA ** (-np.arange(0, ROT_DIM, 2, dtype=np.float64) / ROT_DIM)
    ang = np.arange(seq, dtype=np.float64)[:, None] * freqs[None, :]
    cos, sin = np.cos(ang), np.sin(ang)
    zeros = np.zeros((seq, HEAD_DIM - ROT_DIM))
    z8 = np.zeros((seq, half))
    rep = LANES // HEAD_DIM
    tabs = ([cos, cos, zeros + 1.0], [-sin, z8, zeros], [z8, sin, zeros])
    return tuple(jnp.asarray(np.concatenate(t * rep, axis=1), dtype=F32) for t in tabs)


def kernel(x, mem, norm_mix_g, w_in, conv_w, conv_b, conv_ln_g, conv_ln_b, w_out, norm_x_g,
           norm_mem_g, w_xq, w_xk, w_xv, w_xo, norm_mlp_g, w_up, w_down, norm_final_g):
    B, S, D = x.shape
    n_mem = mem.shape[1]
    depth = w_in.shape[0]
    T = B * S
    cos_t, sa_t, sb_t = _rotary_tables(S)
    bias = _attn_bias()
    hmask = _head_masks()
    row = lambda g: g.reshape(1, -1)

    h = x.reshape(T, D)
    for l in range(depth):
        q, k, v, u = _in_proj(h, row(norm_mix_g[l]), w_in[l], cos_t, sa_t, sb_t, S)
        os_, lses = [], []
        for p, (_, d) in enumerate(DILATED_PATTERNS):
            o, lse = _attn(q[p], k[p], v[p], bias, hmask, B, S, d)
            os_.append(o)
            lses.append(lse)
        c, (w_out_b, w_xq_b, w_xo_b, w_up_b, w_down_b) = _conv(
            u.reshape(B, S, D_CONV), conv_w[l], row(conv_b[l]), row(conv_ln_g[l]),
            row(conv_ln_b[l]), (w_out[l], w_xq[l], w_xo[l], w_up[l], w_down[l]))
        xk, xv = _mem_kv(mem.reshape(B * n_mem, D), row(norm_mem_g[l]), w_xk[l], w_xv[l])
        h = _post(h, os_, lses, c.reshape(T, D_CONV), w_out_b, row(norm_x_g[l]), w_xq_b,
                  xk.reshape(B, n_mem, D), xv.reshape(B, n_mem, D), w_xo_b, row(norm_mlp_g[l]),
                  w_up_b, w_down_b, row(norm_final_g), final_norm=(l == depth - 1), seq=S)
    return h.reshape(B, S, D)
```

```python
import functools
import math

import numpy as np
import jax
import jax.numpy as jnp
from jax import lax
from jax.experimental import pallas as pl
from jax.experimental.pallas import tpu as pltpu

F32 = jnp.float32
BF16 = jnp.bfloat16

D_MODEL = 1024
ATT_HEADS = 8
HEAD_DIM = 64
D_ATT = ATT_HEADS * HEAD_DIM
D_CONV = D_MODEL - D_ATT
DILATED_PATTERNS = ((128, 1), (512, 4), (2048, 16))
DIL_STEP = 4
assert all(d == DIL_STEP ** k for k, (_, d) in enumerate(DILATED_PATTERNS))
ROPE_THETA = 500000.0
ROT_DIM = HEAD_DIM // 4
CONV_WIDTH = 31
CONV_PAD = (CONV_WIDTH - 1) // 2
XATT_HEADS = 4
XATT_HEAD_DIM = D_MODEL // XATT_HEADS
D_FF = 4 * D_MODEL
EPS = 1e-6
NEG_INF = -1e30
LN2 = math.log(2.0)
Q_SCALE = HEAD_DIM ** -0.5 / LN2

LANES = 128
SUBLANES = 8
ATT_HALF = 64
ATT_QB = 2 * ATT_HALF
ATT_WIN = ATT_QB + 2 * ATT_HALF
CONV_HALO = 16
GLU_COLS = 256
VMEM_LIMIT = 56 * 1024 * 1024


def _dot(a, b):
    return jnp.dot(a, b, preferred_element_type=F32)


def _dot_nt(a, b):
    return lax.dot_general(a, b, (((1,), (1,)), ((), ())), preferred_element_type=F32)


def _dot_tn(a, b):
    return lax.dot_general(a, b, (((0,), (0,)), ((), ())), preferred_element_type=F32)


def _rms(x, g):
    var = jnp.mean(x * x, axis=-1, keepdims=True)
    return x * lax.rsqrt(var + EPS) * g


def _cast_once(pairs, chunk=512):
    @pl.when(pl.program_id(0) == 0)
    def _():
        for src, dst in pairs:
            for c in range(0, src.shape[1], chunk):
                dst[:, c:c + chunk] = src[:, c:c + chunk].astype(BF16)


def _in_proj_kernel(x_ref, g_ref, wf_ref, cos_ref, sa_ref, sb_ref, *refs, tm):
    n_pat = len(DILATED_PATTERNS)
    qkv_refs = [refs[a * n_pat:(a + 1) * n_pat] for a in range(3)]
    u_ref = refs[3 * n_pat]
    w_ref = refs[3 * n_pat + 1]
    stage = refs[3 * n_pat + 2:]
    _cast_once([(wf_ref, w_ref)])
    xn = _rms(x_ref[...], g_ref[...]).astype(BF16)
    cos, sa, sb = cos_ref[...], sa_ref[...], sb_ref[...]
    n_slab = D_ATT // LANES

    def plain(y, j):
        return y[:, LANES * j:LANES * (j + 1)]

    def rot(y, j):
        yj = plain(y, j)
        return (yj * cos + pltpu.roll(yj, LANES - ROT_DIM // 2, 1) * sa
                + pltpu.roll(yj, ROT_DIM // 2, 1) * sb)

    def emit(y, outs, transform):
        for j in range(n_slab):
            piece = transform(y, j)
            stage[0][0, j] = piece
            outs[0][:, j * LANES:(j + 1) * LANES] = piece.astype(BF16)
        for k in range(1, n_pat):
            d_prev, d = DILATED_PATTERNS[k - 1][1], DILATED_PATTERNS[k][1]
            for r_prev in range(d_prev):
                for t in range(DIL_STEP):
                    r = t * d_prev + r_prev
                    for j in range(n_slab):
                        piece = stage[k - 1][r_prev, j, pl.ds(t, tm // d, stride=DIL_STEP), :]
                        if k + 1 < n_pat:
                            stage[k][r, j] = piece
                        c0 = r * D_ATT + j * LANES
                        outs[k][:, c0:c0 + LANES] = piece.astype(BF16)

    emit(_dot(xn, w_ref[:, 0:D_ATT]), qkv_refs[0], lambda y, j: rot(y, j) * Q_SCALE)
    emit(_dot(xn, w_ref[:, D_ATT:2 * D_ATT]), qkv_refs[1], rot)
    emit(_dot(xn, w_ref[:, 2 * D_ATT:3 * D_ATT]), qkv_refs[2], plain)
    c0 = 3 * D_ATT
    for cc in range(0, D_CONV, GLU_COLS):
        a = _dot(xn, w_ref[:, c0 + cc:c0 + cc + GLU_COLS])
        gt = _dot(xn, w_ref[:, c0 + D_CONV + cc:c0 + D_CONV + cc + GLU_COLS])
        u_ref[:, cc:cc + GLU_COLS] = (a * (1.0 / (1.0 + jnp.exp(-gt)))).astype(u_ref.dtype)


def _in_proj(x2, g, w_in, cos_t, sa_t, sb_t, seq, tm=1024):
    T = x2.shape[0]
    d_in = w_in.shape[1]
    n_s = seq // tm
    n_pat = len(DILATED_PATTERNS)
    row = lambda i: (i, 0)
    tab = lambda i: (i % n_s, 0)
    const = lambda i: (0, 0)
    qkv_specs, qkv_shapes = [], []
    for _ in range(3):
        for _, d in DILATED_PATTERNS:
            qkv_specs.append(pl.BlockSpec((tm // d, d * D_ATT), row))
            qkv_shapes.append(jax.ShapeDtypeStruct((T // d, d * D_ATT), BF16))
    outs = pl.pallas_call(
        functools.partial(_in_proj_kernel, tm=tm),
        grid=(T // tm,),
        in_specs=[pl.BlockSpec((tm, D_MODEL), row),
                  pl.BlockSpec((1, D_MODEL), const),
                  pl.BlockSpec((D_MODEL, d_in), const, pipeline_mode=pl.Buffered(1)),
                  pl.BlockSpec((tm, LANES), tab),
                  pl.BlockSpec((tm, LANES), tab),
                  pl.BlockSpec((tm, LANES), tab)],
        out_specs=qkv_specs + [pl.BlockSpec((tm, D_CONV), row)],
        out_shape=qkv_shapes + [jax.ShapeDtypeStruct((T, D_CONV), BF16)],
        scratch_shapes=[pltpu.VMEM((D_MODEL, d_in), BF16)]
                       + [pltpu.VMEM((d, D_ATT // LANES, tm // d, LANES), F32)
                          for _, d in DILATED_PATTERNS[:-1]],
        compiler_params=pltpu.CompilerParams(dimension_semantics=("arbitrary",),
                                             vmem_limit_bytes=VMEM_LIMIT),
        name="in_proj",
    )(x2, g, w_in, cos_t, sa_t, sb_t)
    return outs[0:n_pat], outs[n_pat:2 * n_pat], outs[2 * n_pat:3 * n_pat], outs[3 * n_pat]


def _attn_kernel(q_ref, kc_ref, kp_ref, kn_ref, vc_ref, vp_ref, vn_ref, bias_ref, hm_ref,
                 o_ref, lse_ref, *, lb, nblk_total):
    i = pl.program_id(2)
    mask_a = hm_ref[0:1, :]
    mask_b = hm_ref[1:2, :]
    nblk = lb // ATT_QB

    def window(cur, before, after, start, lanes):
        if start < 0:
            return jnp.concatenate([before[0, ATT_HALF + start:, lanes],
                                    cur[0, 0:ATT_QB + start, lanes]], axis=0)
        if start + ATT_QB > lb:
            return jnp.concatenate([cur[0, start:lb, lanes],
                                    after[0, 0:start + ATT_QB - lb, lanes]], axis=0)
        return cur[0, start:start + ATT_QB, lanes]

    for n in range(nblk):
        r0 = n * ATT_QB
        gblk = i * nblk + n
        bidx = jnp.where(gblk == 0, 0, jnp.where(gblk == nblk_total - 1, 2, 1))
        bias = bias_ref[bidx]
        lse_rows = []
        for hp in range(q_ref.shape[2] // LANES):
            c0 = LANES * hp
            lanes = slice(c0, c0 + LANES)
            q2 = q_ref[0, r0:r0 + ATT_QB, lanes]
            qs = jnp.concatenate([q2 * mask_a, q2 * mask_b], axis=0)
            parts = []
            for kh in range(ATT_WIN // ATT_QB):
                k0 = r0 - ATT_HALF + kh * ATT_QB
                kw = window(kc_ref, kp_ref, kn_ref, k0, lanes)
                vw = window(vc_ref, vp_ref, vn_ref, k0, lanes)
                s = _dot_nt(kw, qs) + bias[kh * ATT_QB:(kh + 1) * ATT_QB]
                m = jnp.max(s, axis=0, keepdims=True)
                p = jnp.exp2(s - m)
                l = jnp.sum(p, axis=0, keepdims=True)
                parts.append((m, l, _dot_tn(vw, p.astype(BF16))))
            (m1, l1, o1), (m2, l2, o2) = parts
            m = jnp.maximum(m1, m2)
            a1 = jnp.exp2(m1 - m)
            a2 = jnp.exp2(m2 - m)
            l = a1 * l1 + a2 * l2
            inv = 1.0 / l
            ot = o1 * (a1 * inv) + o2 * (a2 * inv)
            own = jnp.concatenate([ot[0:HEAD_DIM, 0:ATT_QB], ot[HEAD_DIM:, ATT_QB:]], axis=0)
            o_ref[0, r0:r0 + ATT_QB, lanes] = jnp.transpose(own.astype(o_ref.dtype))
            lse = (m + jnp.log2(l)) * LN2
            lse_rows += [lse[:, 0:ATT_QB], lse[:, ATT_QB:]]
        for rc in range(len(lse_rows) // ATT_HEADS):
            rows = lse_rows[rc * ATT_HEADS:(rc + 1) * ATT_HEADS]
            pad = jnp.zeros((LANES - ATT_HEADS, ATT_QB), F32)
            lse_ref[0, r0:r0 + ATT_QB, rc * LANES:(rc + 1) * LANES] = jnp.transpose(
                jnp.concatenate(rows + [pad], axis=0))


def _attn_bias():
    i = np.arange(ATT_QB)[:, None]
    j = np.arange(ATT_WIN)[None, :]
    band = (j >= i) & (j <= i + 2 * ATT_HALF)
    first = band & (j >= ATT_HALF)
    last = band & (j < ATT_HALF + ATT_QB)
    tabs = np.stack([first, band, last]).astype(np.float32)
    tabs = np.concatenate([tabs.transpose(0, 2, 1)] * 2, axis=2)
    return jnp.asarray((1.0 - tabs) * NEG_INF, dtype=F32)


def _head_masks():
    lane = np.arange(LANES)
    m = np.zeros((16, LANES), np.float32)
    m[0] = lane < HEAD_DIM
    m[1] = lane >= HEAD_DIM
    return jnp.asarray(m, dtype=BF16)


def _attn(q, k, v, bias, hmask, batch, seq, d, blocks_per_step=16, lb_max=2048):
    L = seq // d
    lb = min(lb_max, L)
    rpb = min(d, max(1, blocks_per_step * ATT_QB // lb))
    width = rpb * D_ATT
    nblk_total = L // ATT_QB
    assert nblk_total >= 2 and L % lb == 0 and lb % ATT_QB == 0 and d % rpb == 0
    view = lambda t: t.reshape(batch, L, d * D_ATT)
    hb = lb // ATT_HALF
    cur = lambda b, r, i: (b, i, r)
    prev = lambda b, r, i: (b, jnp.maximum(i * hb - 1, 0), r)
    nxt = lambda b, r, i: (b, jnp.minimum((i + 1) * hb, L // ATT_HALF - 1), r)
    blk = pl.BlockSpec((1, lb, width), cur)
    halo_p = pl.BlockSpec((1, ATT_HALF, width), prev)
    halo_n = pl.BlockSpec((1, ATT_HALF, width), nxt)
    o, lse = pl.pallas_call(
        functools.partial(_attn_kernel, lb=lb, nblk_total=nblk_total),
        grid=(batch, d // rpb, L // lb),
        in_specs=[blk, blk, halo_p, halo_n, blk, halo_p, halo_n,
                  pl.BlockSpec((3, ATT_WIN, 2 * ATT_QB), lambda b, r, i: (0, 0, 0)),
                  pl.BlockSpec((16, LANES), lambda b, r, i: (0, 0))],
        out_specs=[blk, pl.BlockSpec((1, lb, rpb * LANES), cur)],
        out_shape=[jax.ShapeDtypeStruct((batch, L, d * D_ATT), BF16),
                   jax.ShapeDtypeStruct((batch, L, d * LANES), F32)],
        compiler_params=pltpu.CompilerParams(
            dimension_semantics=("arbitrary", "arbitrary", "arbitrary"),
            vmem_limit_bytes=VMEM_LIMIT),
        name=f"attn_d{d}",
    )(view(q), view(k), view(k), view(k), view(v), view(v), view(v), bias, hmask)
    return o.reshape(batch * L, d * D_ATT), lse.reshape(batch * L, d * LANES)


def _conv_kernel(uc_ref, up_ref, un_ref, w_ref, b_ref, lg_ref, lb_ref, *refs, ts, rows, ln_rows,
                 n_cast):
    cast_in, c_ref = refs[:n_cast], refs[n_cast]
    cast_out, (ext, ybuf) = refs[n_cast + 1:2 * n_cast + 1], refs[2 * n_cast + 1:]
    for src, dst in zip(cast_in, cast_out):
        dst[...] = src[...].astype(BF16)
    i = pl.program_id(1)
    n = pl.num_programs(1)
    n_slab = D_CONV // LANES
    for g in range(n_slab):
        ls = slice(g * LANES, (g + 1) * LANES)
        ext[g, 0:CONV_HALO] = jnp.where(i > 0, up_ref[0, :, ls].astype(F32), 0.0)
        ext[g, CONV_HALO:CONV_HALO + ts] = uc_ref[0, :, ls].astype(F32)
        ext[g, CONV_HALO + ts:] = jnp.where(i < n - 1, un_ref[0, :, ls].astype(F32), 0.0)
    lg = lg_ref[...]
    lb = lb_ref[...]
    off = CONV_HALO - CONV_PAD
    n_blk = ts // rows

    def taps(idx, carry):
        g = idx // n_blk
        base = pl.multiple_of((idx % n_blk) * rows, rows)
        acc = jnp.broadcast_to(b_ref[g], (rows, LANES))
        for j in range(SUBLANES):
            steps = [m for m in range((CONV_WIDTH + off) // SUBLANES + 1)
                     if 0 <= SUBLANES * m + j - off < CONV_WIDTH]
            win = ext[g, pl.ds(base + j, rows + SUBLANES * max(steps)), :]
            for m in steps:
                k = SUBLANES * m + j - off
                acc = acc + win[SUBLANES * m:SUBLANES * m + rows] * w_ref[g, k:k + 1, :]
        ybuf[g, pl.ds(base, rows), :] = acc
        return carry

    lax.fori_loop(0, n_slab * n_blk, taps, 0, unroll=4)
    for r0 in range(0, ts, ln_rows):
        acc = jnp.concatenate([ybuf[g, r0:r0 + ln_rows, :] for g in range(n_slab)], axis=1)
        mu = jnp.mean(acc, axis=-1, keepdims=True)
        cen = acc - mu
        var = jnp.mean(cen * cen, axis=-1, keepdims=True)
        y = (cen * lax.rsqrt(var + EPS) * lg + lb).astype(c_ref.dtype)
        c_ref[0, r0:r0 + ln_rows, :] = y * (1.0 / (1.0 + jnp.exp(-y)))


def _conv(u3, conv_w, conv_b, ln_g, ln_b, weights, ts=512, rows=64, ln_rows=32):
    B, S, C = u3.shape
    n_i = S // ts
    sl = lambda b, i: (b * n_i + i, 0)
    w_rows = [w.shape[0] // (B * n_i) for w in weights]
    w_specs = [pl.BlockSpec((r, w.shape[1]), sl) for r, w in zip(w_rows, weights)]
    hb = ts // CONV_HALO
    cur = lambda b, i: (b, i, 0)
    prev = lambda b, i: (b, jnp.maximum(i * hb - 1, 0), 0)
    nxt = lambda b, i: (b, jnp.minimum((i + 1) * hb, S // CONV_HALO - 1), 0)
    const = lambda b, i: (0, 0)
    const3 = lambda b, i: (0, 0, 0)
    n_slab = C // LANES
    w_slab = jnp.transpose(conv_w.reshape(CONV_WIDTH, n_slab, LANES), (1, 0, 2))
    b_slab = conv_b.reshape(n_slab, 1, LANES)
    outs = pl.pallas_call(
        functools.partial(_conv_kernel, ts=ts, rows=rows, ln_rows=ln_rows, n_cast=len(weights)),
        grid=(B, S // ts),
        in_specs=[pl.BlockSpec((1, ts, C), cur),
                  pl.BlockSpec((1, CONV_HALO, C), prev),
                  pl.BlockSpec((1, CONV_HALO, C), nxt),
                  pl.BlockSpec((n_slab, CONV_WIDTH, LANES), const3),
                  pl.BlockSpec((n_slab, 1, LANES), const3),
                  pl.BlockSpec((1, C), const),
                  pl.BlockSpec((1, C), const)] + w_specs,
        out_specs=[pl.BlockSpec((1, ts, C), cur)] + w_specs,
        out_shape=[jax.ShapeDtypeStruct((B, S, C), BF16)]
                  + [jax.ShapeDtypeStruct(w.shape, BF16) for w in weights],
        scratch_shapes=[pltpu.VMEM((C // LANES, ts + 2 * CONV_HALO, LANES), F32),
                        pltpu.VMEM((C // LANES, ts, LANES), F32)],
        compiler_params=pltpu.CompilerParams(dimension_semantics=("arbitrary", "arbitrary"),
                                             vmem_limit_bytes=VMEM_LIMIT),
        name="conv",
    )(u3, u3, u3, w_slab, b_slab, ln_g, ln_b, *weights)
    return outs[0], outs[1:]


def _mem_kv_kernel(mem_ref, g_ref, wk_ref, wv_ref, k_ref, v_ref):
    mn = _rms(mem_ref[...], g_ref[...]).astype(BF16)
    k_ref[...] = _dot(mn, wk_ref[...].astype(BF16)).astype(BF16)
    v_ref[...] = _dot(mn, wv_ref[...].astype(BF16)).astype(BF16)


def _mem_kv(mem2, g, wk, wv):
    R = mem2.shape[0]
    full = lambda shape: pl.BlockSpec(shape, lambda i: (0, 0))
    return pl.pallas_call(
        _mem_kv_kernel,
        grid=(1,),
        in_specs=[full((R, D_MODEL)), full((1, D_MODEL)),
                  full((D_MODEL, D_MODEL)), full((D_MODEL, D_MODEL))],
        out_specs=[full((R, D_MODEL))] * 2,
        out_shape=[jax.ShapeDtypeStruct((R, D_MODEL), BF16)] * 2,
        compiler_params=pltpu.CompilerParams(dimension_semantics=("arbitrary",),
                                             vmem_limit_bytes=VMEM_LIMIT),
        name="mem_kv",
    )(mem2, g, wk, wv)


def _post_kernel(x_ref, o1_ref, o2_ref, o3_ref, l1_ref, l2_ref, l3_ref, c_ref, e_ref,
                 wo_ref, gx_ref, wq_ref, xk_ref, xv_ref, wxo_ref, gm_ref, wu_ref, wd_ref, gf_ref,
                 out_ref, *bufs, tm, sub, chunk, final_norm):
    n_slab = D_ATT // LANES
    n_pat = len(DILATED_PATTERNS)
    o_refs = (o1_ref, o2_ref, o3_ref)
    l_refs = (l1_ref, l2_ref, l3_ref)
    n_grp = tm // sub

    def mix(g):
        t0 = g * sub
        obuf, lbuf = bufs[2 * g], bufs[2 * g + 1]
        otmp, ltmp = bufs[2 * n_grp + 2 * g], bufs[2 * n_grp + 2 * g + 1]
        for p, (_, d) in enumerate(DILATED_PATTERNS):
            if d == 1:
                continue
            src = slice(t0 // d, (t0 + sub) // d)
            jobs = [(o_refs[p], D_ATT, j, obuf, otmp, True) for j in range(n_slab)]
            jobs.append((l_refs[p], LANES, 0, lbuf, ltmp, False))
            for ref, pitch, j, dst, tmp, widen in jobs:
                pieces = {}
                for r in range(d):
                    t = ref[src, r * pitch + j * LANES:r * pitch + (j + 1) * LANES]
                    pieces[r] = t.astype(F32) if widen else t
                dd = d
                while dd > DIL_STEP:
                    lower = dd // DIL_STEP
                    merged = {}
                    for r_low in range(lower):
                        for t in range(DIL_STEP):
                            tmp[r_low, j, pl.ds(t, sub // dd, stride=DIL_STEP), :] = (
                                pieces[t * lower + r_low])
                        merged[r_low] = tmp[r_low, j, 0:sub // lower, :]
                    pieces, dd = merged, lower
                for r in range(dd):
                    dst[p, j, pl.ds(r, sub // dd, stride=DIL_STEP), :] = pieces[r]

        ls = [l_refs[p][t0:t0 + sub, :] if d == 1 else lbuf[p, 0]
              for p, (_, d) in enumerate(DILATED_PATTERNS)]
        m = jnp.maximum(jnp.maximum(ls[0], ls[1]), ls[2])
        es = [jnp.exp(l - m) for l in ls]
        inv = 1.0 / (es[0] + es[1] + es[2])
        ws = [_dot((e * inv).astype(BF16), e_ref[...]) for e in es]
        att = []
        for j in range(n_slab):
            lanes = slice(j * LANES, (j + 1) * LANES)
            os_ = [(o_refs[p][t0:t0 + sub, lanes] if d == 1 else obuf[p, j]).astype(F32)
                   for p, (_, d) in enumerate(DILATED_PATTERNS)]
            att.append((ws[0][:, lanes] * os_[0] + ws[1][:, lanes] * os_[1]
                        + ws[2][:, lanes] * os_[2]).astype(BF16))
        return jnp.concatenate(att, axis=1)

    def project(g, att):
        tr = slice(g * sub, (g + 1) * sub)
        h1 = (x_ref[tr, :] + _dot(att, wo_ref[0:D_ATT, :])
              + _dot(c_ref[tr, :], wo_ref[D_ATT:, :]))
        xq = (_dot(_rms(h1, gx_ref[...]).astype(BF16), wq_ref[...])
              * (XATT_HEAD_DIM ** -0.5)).astype(BF16)
        return h1, xq

    def cross(g, h1, xq):
        heads = []
        for h in range(XATT_HEADS):
            sl = slice(h * XATT_HEAD_DIM, (h + 1) * XATT_HEAD_DIM)
            s = _dot_nt(xq[:, sl], xk_ref[0, :, sl])
            mx = jnp.max(s, axis=-1, keepdims=True)
            p = jnp.exp(s - mx)
            den = jnp.sum(p, axis=-1, keepdims=True)
            heads.append((_dot(p.astype(BF16), xv_ref[0, :, sl]) * (1.0 / den)).astype(BF16))
        xo = jnp.concatenate(heads, axis=1)
        return h1 + _dot(xo, wxo_ref[...])

    def mlp(g, h):
        hn = _rms(h, gm_ref[...]).astype(BF16)
        acc = h
        for j in range(D_FF // chunk):
            u = jnp.maximum(_dot(hn, wu_ref[:, j * chunk:(j + 1) * chunk]), 0.0)
            acc = acc + _dot((u * u).astype(BF16), wd_ref[j * chunk:(j + 1) * chunk, :])
        out_ref[g * sub:(g + 1) * sub, :] = _rms(acc, gf_ref[...]) if final_norm else acc

    for g in range(n_grp):
        h1, xq = project(g, mix(g))
        mlp(g, cross(g, h1, xq))


def _post(x2, os_, lses, c2, w_out, gx, w_xq, xk, xv, w_xo, gm, w_up, w_down, gf, final_norm,
          seq, tm=512, sub=512, chunk=1024):
    T = x2.shape[0]
    n_mem = xk.shape[1]
    per_b = seq // tm
    n_slab = D_ATT // LANES
    d_max = DILATED_PATTERNS[-1][1]
    row = lambda i: (i, 0)
    const = lambda i: (0, 0)
    memb = lambda i: (i // per_b, 0, 0)
    whole = lambda w: pl.BlockSpec(w.shape, const, pipeline_mode=pl.Buffered(1))
    vec = pl.BlockSpec((1, D_MODEL), const)
    n_pat = len(DILATED_PATTERNS)
    return pl.pallas_call(
        functools.partial(_post_kernel, tm=tm, sub=sub, chunk=chunk, final_norm=final_norm),
        grid=(T // tm,),
        in_specs=[pl.BlockSpec((tm, D_MODEL), row)]
                 + [pl.BlockSpec((tm // d, d * D_ATT), row) for _, d in DILATED_PATTERNS]
                 + [pl.BlockSpec((tm // d, d * LANES), row) for _, d in DILATED_PATTERNS]
                 + [pl.BlockSpec((tm, D_CONV), row),
                    pl.BlockSpec((LANES, D_ATT), const),
                    whole(w_out), vec, whole(w_xq),
                    pl.BlockSpec((1, n_mem, D_MODEL), memb),
                    pl.BlockSpec((1, n_mem, D_MODEL), memb),
                    whole(w_xo), vec, whole(w_up), whole(w_down), vec],
        out_specs=pl.BlockSpec((tm, D_MODEL), row),
        out_shape=jax.ShapeDtypeStruct((T, D_MODEL), F32),
        scratch_shapes=[pltpu.VMEM((n_pat, n_slab, sub, LANES), F32),
                          pltpu.VMEM((n_pat, 1, sub, LANES), F32)] * (tm // sub)
                       + [pltpu.VMEM((d_max // DIL_STEP, n_slab, sub // DIL_STEP, LANES), F32),
                          pltpu.VMEM((d_max // DIL_STEP, 1, sub // DIL_STEP, LANES), F32)
                          ] * (tm // sub),
        compiler_params=pltpu.CompilerParams(dimension_semantics=("arbitrary",),
                                             vmem_limit_bytes=VMEM_LIMIT),
        name="post",
    )(x2, *os_, *lses, c2, _head_expand(), w_out, gx, w_xq, xk, xv, w_xo, gm, w_up, w_down, gf)


def _head_expand():
    e = np.zeros((LANES, D_ATT), np.float32)
    for h in range(ATT_HEADS):
        e[h, h * HEAD_DIM:(h + 1) * HEAD_DIM] = 1.0
    return jnp.asarray(e, dtype=BF16)


def _rotary_tables(seq):
    half = ROT_DIM // 2
    freqs = ROPE_THETA ** (-np.arange(0, ROT_DIM, 2, dtype=np.float64) / ROT_DIM)
    ang = np.arange(seq, dtype=np.float64)[:, None] * freqs[None, :]
    cos, sin = np.cos(ang), np.sin(ang)
    zeros = np.zeros((seq, HEAD_DIM - ROT_DIM))
    z8 = np.zeros((seq, half))
    rep = LANES // HEAD_DIM
    tabs = ([cos, cos, zeros + 1.0], [-sin, z8, zeros], [z8, sin, zeros])
    return tuple(jnp.asarray(np.concatenate(t * rep, axis=1), dtype=F32) for t in tabs)


def kernel(x, mem, norm_mix_g, w_in, conv_w, conv_b, conv_ln_g, conv_ln_b, w_out, norm_x_g,
           norm_mem_g, w_xq, w_xk, w_xv, w_xo, norm_mlp_g, w_up, w_down, norm_final_g):
    B, S, D = x.shape
    n_mem = mem.shape[1]
    depth = w_in.shape[0]
    T = B * S
    cos_t, sa_t, sb_t = _rotary_tables(S)
    bias = _attn_bias()
    hmask = _head_masks()
    row = lambda g: g.reshape(1, -1)

    h = x.reshape(T, D)
    for l in range(depth):
        q, k, v, u = _in_proj(h, row(norm_mix_g[l]), w_in[l], cos_t, sa_t, sb_t, S)
        os_, lses = [], []
        for p, (_, d) in enumerate(DILATED_PATTERNS):
            o, lse = _attn(q[p], k[p], v[p], bias, hmask, B, S, d)
            os_.append(o)
            lses.append(lse)
        c, (w_out_b, w_xq_b, w_xo_b, w_up_b, w_down_b) = _conv(
            u.reshape(B, S, D_CONV), conv_w[l], row(conv_b[l]), row(conv_ln_g[l]),
            row(conv_ln_b[l]), (w_out[l], w_xq[l], w_xo[l], w_up[l], w_down[l]))
        xk, xv = _mem_kv(mem.reshape(B * n_mem, D), row(norm_mem_g[l]), w_xk[l], w_xv[l])
        h = _post(h, os_, lses, c.reshape(T, D_CONV), w_out_b, row(norm_x_g[l]), w_xq_b,
                  xk.reshape(B, n_mem, D), xv.reshape(B, n_mem, D), w_xo_b, row(norm_mlp_g[l]),
                  w_up_b, w_down_b, row(norm_final_g), final_norm=(l == depth - 1), seq=S)
    return h.reshape(B, S, D)
```

```python
import functools
import math

import numpy as np
import jax
import jax.numpy as jnp
from jax import lax
from jax.experimental import pallas as pl
from jax.experimental.pallas import tpu as pltpu

F32 = jnp.float32
BF16 = jnp.bfloat16

D_MODEL = 1024
ATT_HEADS = 8
HEAD_DIM = 64
D_ATT = ATT_HEADS * HEAD_DIM
D_CONV = D_MODEL - D_ATT
DILATED_PATTERNS = ((128, 1), (512, 4), (2048, 16))
DIL_STEP = 4
assert all(d == DIL_STEP ** k for k, (_, d) in enumerate(DILATED_PATTERNS))
ROPE_THETA = 500000.0
ROT_DIM = HEAD_DIM // 4
CONV_WIDTH = 31
CONV_PAD = (CONV_WIDTH - 1) // 2
XATT_HEADS = 4
XATT_HEAD_DIM = D_MODEL // XATT_HEADS
D_FF = 4 * D_MODEL
EPS = 1e-6
NEG_INF = -1e30
LN2 = math.log(2.0)
Q_SCALE = HEAD_DIM ** -0.5 / LN2

LANES = 128
SUBLANES = 8
ATT_HALF = 64
ATT_QB = 2 * ATT_HALF
ATT_WIN = ATT_QB + 2 * ATT_HALF
CONV_HALO = 16
GLU_COLS = 256
VMEM_LIMIT = 56 * 1024 * 1024


def _dot(a, b):
    return jnp.dot(a, b, preferred_element_type=F32)


def _dot_nt(a, b):
    return lax.dot_general(a, b, (((1,), (1,)), ((), ())), preferred_element_type=F32)


def _dot_tn(a, b):
    return lax.dot_general(a, b, (((0,), (0,)), ((), ())), preferred_element_type=F32)


def _rms(x, g):
    var = jnp.mean(x * x, axis=-1, keepdims=True)
    return x * lax.rsqrt(var + EPS) * g


def _cast_once(pairs, chunk=512):
    @pl.when(pl.program_id(0) == 0)
    def _():
        for src, dst in pairs:
            for c in range(0, src.shape[1], chunk):
                dst[:, c:c + chunk] = src[:, c:c + chunk].astype(BF16)


def _in_proj_kernel(x_ref, g_ref, wf_ref, cos_ref, sa_ref, sb_ref, *refs, tm):
    n_pat = len(DILATED_PATTERNS)
    qkv_refs = [refs[a * n_pat:(a + 1) * n_pat] for a in range(3)]
    u_ref = refs[3 * n_pat]
    w_ref = refs[3 * n_pat + 1]
    stage = refs[3 * n_pat + 2:]
    _cast_once([(wf_ref, w_ref)])
    xn = _rms(x_ref[...], g_ref[...]).astype(BF16)
    cos, sa, sb = cos_ref[...], sa_ref[...], sb_ref[...]
    n_slab = D_ATT // LANES

    def plain(y, j):
        return y[:, LANES * j:LANES * (j + 1)]

    def rot(y, j):
        yj = plain(y, j)
        return (yj * cos + pltpu.roll(yj, LANES - ROT_DIM // 2, 1) * sa
                + pltpu.roll(yj, ROT_DIM // 2, 1) * sb)

    def emit(y, outs, transform):
        for j in range(n_slab):
            piece = transform(y, j)
            stage[0][0, j] = piece
            outs[0][:, j * LANES:(j + 1) * LANES] = piece.astype(BF16)
        for k in range(1, n_pat):
            d_prev, d = DILATED_PATTERNS[k - 1][1], DILATED_PATTERNS[k][1]
            for r_prev in range(d_prev):
                for t in range(DIL_STEP):
                    r = t * d_prev + r_prev
                    for j in range(n_slab):
                        piece = stage[k - 1][r_prev, j, pl.ds(t, tm // d, stride=DIL_STEP), :]
                        if k + 1 < n_pat:
                            stage[k][r, j] = piece
                        c0 = r * D_ATT + j * LANES
                        outs[k][:, c0:c0 + LANES] = piece.astype(BF16)

    emit(_dot(xn, w_ref[:, 0:D_ATT]), qkv_refs[0], lambda y, j: rot(y, j) * Q_SCALE)
    emit(_dot(xn, w_ref[:, D_ATT:2 * D_ATT]), qkv_refs[1], rot)
    emit(_dot(xn, w_ref[:, 2 * D_ATT:3 * D_ATT]), qkv_refs[2], plain)
    c0 = 3 * D_ATT
    for cc in range(0, D_CONV, GLU_COLS):
        a = _dot(xn, w_ref[:, c0 + cc:c0 + cc + GLU_COLS])
        gt = _dot(xn, w_ref[:, c0 + D_CONV + cc:c0 + D_CONV + cc + GLU_COLS])
        u_ref[:, cc:cc + GLU_COLS] = (a * (1.0 / (1.0 + jnp.exp(-gt)))).astype(u_ref.dtype)


def _in_proj(x2, g, w_in, cos_t, sa_t, sb_t, seq, tm=1024):
    T = x2.shape[0]
    d_in = w_in.shape[1]
    n_s = seq // tm
    n_pat = len(DILATED_PATTERNS)
    row = lambda i: (i, 0)
    tab = lambda i: (i % n_s, 0)
    const = lambda i: (0, 0)
    qkv_specs, qkv_shapes = [], []
    for _ in range(3):
        for _, d in DILATED_PATTERNS:
            qkv_specs.append(pl.BlockSpec((tm // d, d * D_ATT), row))
            qkv_shapes.append(jax.ShapeDtypeStruct((T // d, d * D_ATT), BF16))
    outs = pl.pallas_call(
        functools.partial(_in_proj_kernel, tm=tm),
        grid=(T // tm,),
        in_specs=[pl.BlockSpec((tm, D_MODEL), row),
                  pl.BlockSpec((1, D_MODEL), const),
                  pl.BlockSpec((D_MODEL, d_in), const, pipeline_mode=pl.Buffered(1)),
                  pl.BlockSpec((tm, LANES), tab),
                  pl.BlockSpec((tm, LANES), tab),
                  pl.BlockSpec((tm, LANES), tab)],
        out_specs=qkv_specs + [pl.BlockSpec((tm, D_CONV), row)],
        out_shape=qkv_shapes + [jax.ShapeDtypeStruct((T, D_CONV), BF16)],
        scratch_shapes=[pltpu.VMEM((D_MODEL, d_in), BF16)]
                       + [pltpu.VMEM((d, D_ATT // LANES, tm // d, LANES), F32)
                          for _, d in DILATED_PATTERNS[:-1]],
        compiler_params=pltpu.CompilerParams(dimension_semantics=("arbitrary",),
                                             vmem_limit_bytes=VMEM_LIMIT),
        name="in_proj",
    )(x2, g, w_in, cos_t, sa_t, sb_t)
    return outs[0:n_pat], outs[n_pat:2 * n_pat], outs[2 * n_pat:3 * n_pat], outs[3 * n_pat]


def _attn_kernel(q_ref, kc_ref, kp_ref, kn_ref, vc_ref, vp_ref, vn_ref, bias_ref, hm_ref,
                 o_ref, lse_ref, *, lb, nblk_total):
    i = pl.program_id(2)
    mask_a = hm_ref[0:1, :]
    mask_b = hm_ref[1:2, :]
    nblk = lb // ATT_QB

    def window(cur, before, after, start, lanes):
        if start < 0:
            return jnp.concatenate([before[0, ATT_HALF + start:, lanes],
                                    cur[0, 0:ATT_QB + start, lanes]], axis=0)
        if start + ATT_QB > lb:
            return jnp.concatenate([cur[0, start:lb, lanes],
                                    after[0, 0:start + ATT_QB - lb, lanes]], axis=0)
        return cur[0, start:start + ATT_QB, lanes]

    for n in range(nblk):
        r0 = n * ATT_QB
        gblk = i * nblk + n
        bidx = jnp.where(gblk == 0, 0, jnp.where(gblk == nblk_total - 1, 2, 1))
        bias = bias_ref[bidx]
        lse_rows = []
        for hp in range(q_ref.shape[2] // LANES):
            c0 = LANES * hp
            lanes = slice(c0, c0 + LANES)
            q2 = q_ref[0, r0:r0 + ATT_QB, lanes]
            qs = jnp.concatenate([q2 * mask_a, q2 * mask_b], axis=0)
            parts = []
            for kh in range(ATT_WIN // ATT_QB):
                k0 = r0 - ATT_HALF + kh * ATT_QB
                kw = window(kc_ref, kp_ref, kn_ref, k0, lanes)
                vw = window(vc_ref, vp_ref, vn_ref, k0, lanes)
                s = _dot_nt(kw, qs) + bias[kh * ATT_QB:(kh + 1) * ATT_QB]
                m = jnp.max(s, axis=0, keepdims=True)
                p = jnp.exp2(s - m)
                l = jnp.sum(p, axis=0, keepdims=True)
                parts.append((m, l, _dot_tn(vw, p.astype(BF16))))
            (m1, l1, o1), (m2, l2, o2) = parts
            m = jnp.maximum(m1, m2)
            a1 = jnp.exp2(m1 - m)
            a2 = jnp.exp2(m2 - m)
            l = a1 * l1 + a2 * l2
            inv = 1.0 / l
            ot = o1 * (a1 * inv) + o2 * (a2 * inv)
            own = jnp.concatenate([ot[0:HEAD_DIM, 0:ATT_QB], ot[HEAD_DIM:, ATT_QB:]], axis=0)
            o_ref[0, r0:r0 + ATT_QB, lanes] = jnp.transpose(own.astype(o_ref.dtype))
            lse = (m + jnp.log2(l)) * LN2
            lse_rows += [lse[:, 0:ATT_QB], lse[:, ATT_QB:]]
        for rc in range(len(lse_rows) // ATT_HEADS):
            rows = lse_rows[rc * ATT_HEADS:(rc + 1) * ATT_HEADS]
            pad = jnp.zeros((LANES - ATT_HEADS, ATT_QB), F32)
            lse_ref[0, r0:r0 + ATT_QB, rc * LANES:(rc + 1) * LANES] = jnp.transpose(
                jnp.concatenate(rows + [pad], axis=0))


def _attn_bias():
    i = np.arange(ATT_QB)[:, None]
    j = np.arange(ATT_WIN)[None, :]
    band = (j >= i) & (j <= i + 2 * ATT_HALF)
    first = band & (j >= ATT_HALF)
    last = band & (j < ATT_HALF + ATT_QB)
    tabs = np.stack([first, band, last]).astype(np.float32)
    tabs = np.concatenate([tabs.transpose(0, 2, 1)] * 2, axis=2)
    return jnp.asarray((1.0 - tabs) * NEG_INF, dtype=F32)


def _head_masks():
    lane = np.arange(LANES)
    m = np.zeros((16, LANES), np.float32)
    m[0] = lane < HEAD_DIM
    m[1] = lane >= HEAD_DIM
    return jnp.asarray(m, dtype=BF16)


def _attn(q, k, v, bias, hmask, batch, seq, d, blocks_per_step=16, lb_max=2048):
    L = seq // d
    lb = min(lb_max, L)
    rpb = min(d, max(1, blocks_per_step * ATT_QB // lb))
    width = rpb * D_ATT
    nblk_total = L // ATT_QB
    assert nblk_total >= 2 and L % lb == 0 and lb % ATT_QB == 0 and d % rpb == 0
    view = lambda t: t.reshape(batch, L, d * D_ATT)
    hb = lb // ATT_HALF
    cur = lambda b, r, i: (b, i, r)
    prev = lambda b, r, i: (b, jnp.maximum(i * hb - 1, 0), r)
    nxt = lambda b, r, i: (b, jnp.minimum((i + 1) * hb, L // ATT_HALF - 1), r)
    blk = pl.BlockSpec((1, lb, width), cur)
    halo_p = pl.BlockSpec((1, ATT_HALF, width), prev)
    halo_n = pl.BlockSpec((1, ATT_HALF, width), nxt)
    o, lse = pl.pallas_call(
        functools.partial(_attn_kernel, lb=lb, nblk_total=nblk_total),
        grid=(batch, d // rpb, L // lb),
        in_specs=[blk, blk, halo_p, halo_n, blk, halo_p, halo_n,
                  pl.BlockSpec((3, ATT_WIN, 2 * ATT_QB), lambda b, r, i: (0, 0, 0)),
                  pl.BlockSpec((16, LANES), lambda b, r, i: (0, 0))],
        out_specs=[blk, pl.BlockSpec((1, lb, rpb * LANES), cur)],
        out_shape=[jax.ShapeDtypeStruct((batch, L, d * D_ATT), BF16),
                   jax.ShapeDtypeStruct((batch, L, d * LANES), F32)],
        compiler_params=pltpu.CompilerParams(
            dimension_semantics=("arbitrary", "arbitrary", "arbitrary"),
            vmem_limit_bytes=VMEM_LIMIT),
        name=f"attn_d{d}",
    )(view(q), view(k), view(k), view(k), view(v), view(v), view(v), bias, hmask)
    return o.reshape(batch * L, d * D_ATT), lse.reshape(batch * L, d * LANES)


def _conv_kernel(uc_ref, up_ref, un_ref, w_ref, b_ref, lg_ref, lb_ref, *refs, ts, rows, ln_rows,
                 n_cast):
    cast_in, c_ref = refs[:n_cast], refs[n_cast]
    cast_out, (ext, ybuf) = refs[n_cast + 1:2 * n_cast + 1], refs[2 * n_cast + 1:]
    for src, dst in zip(cast_in, cast_out):
        dst[...] = src[...].astype(BF16)
    i = pl.program_id(1)
    n = pl.num_programs(1)
    n_slab = D_CONV // LANES
    for g in range(n_slab):
        ls = slice(g * LANES, (g + 1) * LANES)
        ext[g, 0:CONV_HALO] = jnp.where(i > 0, up_ref[0, :, ls].astype(F32), 0.0)
        ext[g, CONV_HALO:CONV_HALO + ts] = uc_ref[0, :, ls].astype(F32)
        ext[g, CONV_HALO + ts:] = jnp.where(i < n - 1, un_ref[0, :, ls].astype(F32), 0.0)
    lg = lg_ref[...]
    lb = lb_ref[...]
    off = CONV_HALO - CONV_PAD
    n_blk = ts // rows

    def taps(idx, carry):
        g = idx // n_blk
        base = pl.multiple_of((idx % n_blk) * rows, rows)
        acc = jnp.broadcast_to(b_ref[g], (rows, LANES))
        for j in range(SUBLANES):
            steps = [m for m in range((CONV_WIDTH + off) // SUBLANES + 1)
                     if 0 <= SUBLANES * m + j - off < CONV_WIDTH]
            win = ext[g, pl.ds(base + j, rows + SUBLANES * max(steps)), :]
            for m in steps:
                k = SUBLANES * m + j - off
                acc = acc + win[SUBLANES * m:SUBLANES * m + rows] * w_ref[g, k:k + 1, :]
        ybuf[g, pl.ds(base, rows), :] = acc
        return carry

    lax.fori_loop(0, n_slab * n_blk, taps, 0, unroll=4)
    for r0 in range(0, ts, ln_rows):
        acc = jnp.concatenate([ybuf[g, r0:r0 + ln_rows, :] for g in range(n_slab)], axis=1)
        mu = jnp.mean(acc, axis=-1, keepdims=True)
        cen = acc - mu
        var = jnp.mean(cen * cen, axis=-1, keepdims=True)
        y = (cen * lax.rsqrt(var + EPS) * lg + lb).astype(c_ref.dtype)
        c_ref[0, r0:r0 + ln_rows, :] = y * (1.0 / (1.0 + jnp.exp(-y)))


def _conv(u3, conv_w, conv_b, ln_g, ln_b, weights, ts=512, rows=64, ln_rows=32):
    B, S, C = u3.shape
    n_i = S // ts
    sl = lambda b, i: (b * n_i + i, 0)
    w_rows = [w.shape[0] // (B * n_i) for w in weights]
    w_specs = [pl.BlockSpec((r, w.shape[1]), sl) for r, w in zip(w_rows, weights)]
    hb = ts // CONV_HALO
    cur = lambda b, i: (b, i, 0)
    prev = lambda b, i: (b, jnp.maximum(i * hb - 1, 0), 0)
    nxt = lambda b, i: (b, jnp.minimum((i + 1) * hb, S // CONV_HALO - 1), 0)
    const = lambda b, i: (0, 0)
    const3 = lambda b, i: (0, 0, 0)
    n_slab = C // LANES
    w_slab = jnp.transpose(conv_w.reshape(CONV_WIDTH, n_slab, LANES), (1, 0, 2))
    b_slab = conv_b.reshape(n_slab, 1, LANES)
    outs = pl.pallas_call(
        functools.partial(_conv_kernel, ts=ts, rows=rows, ln_rows=ln_rows, n_cast=len(weights)),
        grid=(B, S // ts),
        in_specs=[pl.BlockSpec((1, ts, C), cur),
                  pl.BlockSpec((1, CONV_HALO, C), prev),
                  pl.BlockSpec((1, CONV_HALO, C), nxt),
                  pl.BlockSpec((n_slab, CONV_WIDTH, LANES), const3),
                  pl.BlockSpec((n_slab, 1, LANES), const3),
                  pl.BlockSpec((1, C), const),
                  pl.BlockSpec((1, C), const)] + w_specs,
        out_specs=[pl.BlockSpec((1, ts, C), cur)] + w_specs,
        out_shape=[jax.ShapeDtypeStruct((B, S, C), BF16)]
                  + [jax.ShapeDtypeStruct(w.shape, BF16) for w in weights],
        scratch_shapes=[pltpu.VMEM((C // LANES, ts + 2 * CONV_HALO, LANES), F32),
                        pltpu.VMEM((C // LANES, ts, LANES), F32)],
        compiler_params=pltpu.CompilerParams(dimension_semantics=("arbitrary", "arbitrary"),
                                             vmem_limit_bytes=VMEM_LIMIT),
        name="conv",
    )(u3, u3, u3, w_slab, b_slab, ln_g, ln_b, *weights)
    return outs[0], outs[1:]


def _mem_kv_kernel(mem_ref, g_ref, wk_ref, wv_ref, k_ref, v_ref):
    mn = _rms(mem_ref[...], g_ref[...]).astype(BF16)
    k_ref[...] = _dot(mn, wk_ref[...].astype(BF16)).astype(BF16)
    v_ref[...] = _dot(mn, wv_ref[...].astype(BF16)).astype(BF16)


def _mem_kv(mem2, g, wk, wv):
    R = mem2.shape[0]
    full = lambda shape: pl.BlockSpec(shape, lambda i: (0, 0))
    return pl.pallas_call(
        _mem_kv_kernel,
        grid=(1,),
        in_specs=[full((R, D_MODEL)), full((1, D_MODEL)),
                  full((D_MODEL, D_MODEL)), full((D_MODEL, D_MODEL))],
        out_specs=[full((R, D_MODEL))] * 2,
        out_shape=[jax.ShapeDtypeStruct((R, D_MODEL), BF16)] * 2,
        compiler_params=pltpu.CompilerParams(dimension_semantics=("arbitrary",),
                                             vmem_limit_bytes=VMEM_LIMIT),
        name="mem_kv",
    )(mem2, g, wk, wv)


def _post_kernel(x_ref, o1_ref, o2_ref, o3_ref, l1_ref, l2_ref, l3_ref, c_ref, e_ref,
                 wo_ref, gx_ref, wq_ref, xk_ref, xv_ref, wxo_ref, gm_ref, wu_ref, wd_ref, gf_ref,
                 out_ref, *bufs, tm, sub, chunk, final_norm):
    n_slab = D_ATT // LANES
    n_pat = len(DILATED_PATTERNS)
    o_refs = (o1_ref, o2_ref, o3_ref)
    l_refs = (l1_ref, l2_ref, l3_ref)
    n_grp = tm // sub

    def mix(g):
        t0 = g * sub
        obuf, lbuf = bufs[2 * g], bufs[2 * g + 1]
        otmp, ltmp = bufs[2 * n_grp + 2 * g], bufs[2 * n_grp + 2 * g + 1]
        for p, (_, d) in enumerate(DILATED_PATTERNS):
            if d == 1:
                continue
            src = slice(t0 // d, (t0 + sub) // d)
            jobs = [(o_refs[p], D_ATT, j, obuf, otmp, True) for j in range(n_slab)]
            jobs.append((l_refs[p], LANES, 0, lbuf, ltmp, False))
            for ref, pitch, j, dst, tmp, widen in jobs:
                pieces = {}
                for r in range(d):
                    t = ref[src, r * pitch + j * LANES:r * pitch + (j + 1) * LANES]
                    pieces[r] = t.astype(F32) if widen else t
                dd = d
                while dd > DIL_STEP:
                    lower = dd // DIL_STEP
                    merged = {}
                    for r_low in range(lower):
                        for t in range(DIL_STEP):
                            tmp[r_low, j, pl.ds(t, sub // dd, stride=DIL_STEP), :] = (
                                pieces[t * lower + r_low])
                        merged[r_low] = tmp[r_low, j, 0:sub // lower, :]
                    pieces, dd = merged, lower
                for r in range(dd):
                    dst[p, j, pl.ds(r, sub // dd, stride=DIL_STEP), :] = pieces[r]

        ls = [l_refs[p][t0:t0 + sub, :] if d == 1 else lbuf[p, 0]
              for p, (_, d) in enumerate(DILATED_PATTERNS)]
        m = jnp.maximum(jnp.maximum(ls[0], ls[1]), ls[2])
        es = [jnp.exp(l - m) for l in ls]
        inv = 1.0 / (es[0] + es[1] + es[2])
        ws = [_dot((e * inv).astype(BF16), e_ref[...]) for e in es]
        att = []
        for j in range(n_slab):
            lanes = slice(j * LANES, (j + 1) * LANES)
            os_ = [(o_refs[p][t0:t0 + sub, lanes] if d == 1 else obuf[p, j]).astype(F32)
                   for p, (_, d) in enumerate(DILATED_PATTERNS)]
            att.append((ws[0][:, lanes] * os_[0] + ws[1][:, lanes] * os_[1]
                        + ws[2][:, lanes] * os_[2]).astype(BF16))
        return jnp.concatenate(att, axis=1)

    def project(g, att):
        tr = slice(g * sub, (g + 1) * sub)
        h1 = (x_ref[tr, :] + _dot(att, wo_ref[0:D_ATT, :])
              + _dot(c_ref[tr, :], wo_ref[D_ATT:, :]))
        xq = (_dot(_rms(h1, gx_ref[...]).astype(BF16), wq_ref[...])
              * (XATT_HEAD_DIM ** -0.5)).astype(BF16)
        return h1, xq

    def cross(g, h1, xq):
        heads = []
        for h in range(XATT_HEADS):
            sl = slice(h * XATT_HEAD_DIM, (h + 1) * XATT_HEAD_DIM)
            s = _dot_nt(xq[:, sl], xk_ref[0, :, sl])
            mx = jnp.max(s, axis=-1, keepdims=True)
            p = jnp.exp(s - mx)
            den = jnp.sum(p, axis=-1, keepdims=True)
            heads.append((_dot(p.astype(BF16), xv_ref[0, :, sl]) * (1.0 / den)).astype(BF16))
        xo = jnp.concatenate(heads, axis=1)
        return h1 + _dot(xo, wxo_ref[...])

    def mlp(g, h):
        hn = _rms(h, gm_ref[...]).astype(BF16)
        acc = h
        for j in range(D_FF // chunk):
            u = jnp.maximum(_dot(hn, wu_ref[:, j * chunk:(j + 1) * chunk]), 0.0)
            acc = acc + _dot((u * u).astype(BF16), wd_ref[j * chunk:(j + 1) * chunk, :])
        out_ref[g * sub:(g + 1) * sub, :] = _rms(acc, gf_ref[...]) if final_norm else acc

    for g in range(n_grp):
        h1, xq = project(g, mix(g))
        mlp(g, cross(g, h1, xq))


def _post(x2, os_, lses, c2, w_out, gx, w_xq, xk, xv, w_xo, gm, w_up, w_down, gf, final_norm,
          seq, tm=512, sub=512, chunk=512):
    T = x2.shape[0]
    n_mem = xk.shape[1]
    per_b = seq // tm
    n_slab = D_ATT // LANES
    d_max = DILATED_PATTERNS[-1][1]
    row = lambda i: (i, 0)
    const = lambda i: (0, 0)
    memb = lambda i: (i // per_b, 0, 0)
    whole = lambda w: pl.BlockSpec(w.shape, const, pipeline_mode=pl.Buffered(1))
    vec = pl.BlockSpec((1, D_MODEL), const)
    n_pat = len(DILATED_PATTERNS)
    return pl.pallas_call(
        functools.partial(_post_kernel, tm=tm, sub=sub, chunk=chunk, final_norm=final_norm),
        grid=(T // tm,),
        in_specs=[pl.BlockSpec((tm, D_MODEL), row)]
                 + [pl.BlockSpec((tm // d, d * D_ATT), row) for _, d in DILATED_PATTERNS]
                 + [pl.BlockSpec((tm // d, d * LANES), row) for _, d in DILATED_PATTERNS]
                 + [pl.BlockSpec((tm, D_CONV), row),
                    pl.BlockSpec((LANES, D_ATT), const),
                    whole(w_out), vec, whole(w_xq),
                    pl.BlockSpec((1, n_mem, D_MODEL), memb),
                    pl.BlockSpec((1, n_mem, D_MODEL), memb),
                    whole(w_xo), vec, whole(w_up), whole(w_down), vec],
        out_specs=pl.BlockSpec((tm, D_MODEL), row),
        out_shape=jax.ShapeDtypeStruct((T, D_MODEL), F32),
        scratch_shapes=[pltpu.VMEM((n_pat, n_slab, sub, LANES), F32),
                          pltpu.VMEM((n_pat, 1, sub, LANES), F32)] * (tm // sub)
                       + [pltpu.VMEM((d_max // DIL_STEP, n_slab, sub // DIL_STEP, LANES), F32),
                          pltpu.VMEM((d_max // DIL_STEP, 1, sub // DIL_STEP, LANES), F32)
                          ] * (tm // sub),
        compiler_params=pltpu.CompilerParams(dimension_semantics=("arbitrary",),
                                             vmem_limit_bytes=VMEM_LIMIT),
        name="post",
    )(x2, *os_, *lses, c2, _head_expand(), w_out, gx, w_xq, xk, xv, w_xo, gm, w_up, w_down, gf)


def _head_expand():
    e = np.zeros((LANES, D_ATT), np.float32)
    for h in range(ATT_HEADS):
        e[h, h * HEAD_DIM:(h + 1) * HEAD_DIM] = 1.0
    return jnp.asarray(e, dtype=BF16)


def _rotary_tables(seq):
    half = ROT_DIM // 2
    freqs = ROPE_THETA ** (-np.arange(0, ROT_DIM, 2, dtype=np.float64) / ROT_DIM)
    ang = np.arange(seq, dtype=np.float64)[:, None] * freqs[None, :]
    cos, sin = np.cos(ang), np.sin(ang)
    zeros = np.zeros((seq, HEAD_DIM - ROT_DIM))
    z8 = np.zeros((seq, half))
    rep = LANES // HEAD_DIM
    tabs = ([cos, cos, zeros + 1.0], [-sin, z8, zeros], [z8, sin, zeros])
    return tuple(jnp.asarray(np.concatenate(t * rep, axis=1), dtype=F32) for t in tabs)


def kernel(x, mem, norm_mix_g, w_in, conv_w, conv_b, conv_ln_g, conv_ln_b, w_out, norm_x_g,
           norm_mem_g, w_xq, w_xk, w_xv, w_xo, norm_mlp_g, w_up, w_down, norm_final_g):
    B, S, D = x.shape
    n_mem = mem.shape[1]
    depth = w_in.shape[0]
    T = B * S
    cos_t, sa_t, sb_t = _rotary_tables(S)
    bias = _attn_bias()
    hmask = _head_masks()
    row = lambda g: g.reshape(1, -1)

    h = x.reshape(T, D)
    for l in range(depth):
        q, k, v, u = _in_proj(h, row(norm_mix_g[l]), w_in[l], cos_t, sa_t, sb_t, S)
        os_, lses = [], []
        for p, (_, d) in enumerate(DILATED_PATTERNS):
            o, lse = _attn(q[p], k[p], v[p], bias, hmask, B, S, d)
            os_.append(o)
            lses.append(lse)
        c, (w_out_b, w_xq_b, w_xo_b, w_up_b, w_down_b) = _conv(
            u.reshape(B, S, D_CONV), conv_w[l], row(conv_b[l]), row(conv_ln_g[l]),
            row(conv_ln_b[l]), (w_out[l], w_xq[l], w_xo[l], w_up[l], w_down[l]))
        xk, xv = _mem_kv(mem.reshape(B * n_mem, D), row(norm_mem_g[l]), w_xk[l], w_xv[l])
        h = _post(h, os_, lses, c.reshape(T, D_CONV), w_out_b, row(norm_x_g[l]), w_xq_b,
                  xk.reshape(B, n_mem, D), xv.reshape(B, n_mem, D), w_xo_b, row(norm_mlp_g[l]),
                  w_up_b, w_down_b, row(norm_final_g), final_norm=(l == depth - 1), seq=S)
    return h.reshape(B, S, D)
```
